```python
import math
import jax
import jax.numpy as jnp
from jax import lax
import numpy as np

D_MODEL = 1024
BATCH = 4
SEQ = 8192
DEPTH = 2

GRID_W = 64
CTX_LEN = 256
HEAD_DIM = 64
D_A = D_MODEL // 2
D_B = D_MODEL - D_A
D_C = D_MODEL // 2
D_D = D_MODEL - D_C
A_HEADS = D_A // HEAD_DIM
B_HEADS = D_B // HEAD_DIM
C_HEADS = D_C // HEAD_DIM
DECAY_LORA = 64
AAA_LORA = 64
GATE_LORA = 128
LN_X_EPS = 1e-5 * HEAD_DIM
A_SPLITS = [D_A, 2 * D_A, 3 * D_A, 3 * D_A + 2 * DECAY_LORA, 3 * D_A + 2 * DECAY_LORA + 2 * AAA_LORA]
A_COLS = 3 * D_A + 2 * DECAY_LORA + 2 * AAA_LORA + GATE_LORA
B_COLS = 5 * D_B
AB_COLS = A_COLS + B_COLS
GLA_CHUNK = 64
NA_ROWS = 8
NA_COLS = 16
NA_QROWS = 2
ROPE_THETA = 10000.0
C_COLS = 3 * D_C
CD_COLS = C_COLS + 3 * D_D
HYENA_ORDER = 2
HYENA_EMB = 33
HYENA_WIDTH = 64
HYENA_FAST_DECAY = 0.3
HYENA_SLOW_DECAY = 1.5
HYENA_TARGET = 1e-2
N_EXPERTS = 32
TOP_K = 4
D_EXPERT = D_MODEL
SWIGLU_ALPHA = 1.702
SWIGLU_LIMIT = 7.0
MOE_BLOCK = 256
RMS_EPS = 1e-6

kernel_name = 'hybrid_rwkv7_hgrn2_natten_hyena_moe_dit'


def _rms(x):
    xf = x.astype(jnp.float32)
    return (xf * lax.rsqrt(jnp.mean(xf * xf, axis=-1, keepdims=True) + RMS_EPS)).astype(x.dtype)


def _modulate(x, shift, scale):
    return _rms(x) * (1 + scale) + shift


def _conv3(u, w):
    up = jnp.pad(u, ((0, 0), (1, 1), (0, 0)))
    return up[:, :-2] * w[0] + up[:, 1:-1] * w[1] + up[:, 2:] * w[2]


def _rwkv7_streams(u, shift_w, w0, w2, a0, a2, g2, k_k, k_a):
    bsz, L, _ = u.shape
    u = _conv3(u, shift_w).astype(jnp.float32)
    r, k, v, wd, ad, gd = jnp.split(u, A_SPLITS, axis=-1)
    heads = lambda t: t.reshape(t.shape[:-1] + (A_HEADS, HEAD_DIM))
    lora_w = jnp.tanh(wd.reshape(bsz, L, 2, DECAY_LORA))
    w_log = -jax.nn.softplus(-(w0 + jnp.einsum('bldr,drc->bldc', lora_w, w2))) - 0.5
    decay = jnp.exp(-jnp.exp(w_log))
    a = jax.nn.sigmoid(a0 + jnp.einsum('bldr,drc->bldc', ad.reshape(bsz, L, 2, AAA_LORA), a2))
    kk = heads(k * k_k)
    kk = kk / jnp.maximum(jnp.linalg.norm(kk, axis=-1, keepdims=True), 1e-12)
    k_dir = k[:, :, None, :] * (1.0 + (a - 1.0) * k_a)
    g = jax.nn.sigmoid(gd) @ g2
    return heads(r), heads(v), kk, heads(decay), heads(a), heads(k_dir), g


def _rwkv7_scan(s0, r, w, k, v, kk, a, reverse):
    def step(s, inp):
        r_t, w_t, k_t, v_t, kk_t, a_t = inp
        sa = jnp.einsum('bhvk,bhk->bhv', s, kk_t)
        s = (s * w_t[:, :, None, :] - sa[..., None] * (kk_t * a_t)[:, :, None, :]
             + v_t[..., None] * k_t[:, :, None, :])
        return s, jnp.einsum('bhvk,bhk->bhv', s, r_t)
    xs = tuple(jnp.swapaxes(t, 0, 1) for t in (r, w, k, v, kk, a))
    s_fin, y = lax.scan(step, s0, xs, reverse=reverse)
    return jnp.swapaxes(y, 0, 1), s_fin


def _rwkv7_dir_args(streams, d):
    r, v, kk, decay, a, k_dir, _ = streams
    return r, decay[:, :, d], k_dir[:, :, d], v, kk, a[:, :, d]


def _rwkv7_readout(y, streams, r_k, ln_w, ln_b):
    r, v, _, _, _, k_dir, g = streams
    bsz, L = y.shape[:2]
    mu = jnp.mean(y, axis=-1, keepdims=True)
    var = jnp.mean(jnp.square(y - mu), axis=-1, keepdims=True)
    y = ((y - mu) * lax.rsqrt(var + LN_X_EPS)).reshape(bsz, L, D_A) * ln_w + ln_b
    bonus = jnp.sum(jnp.sum(r[:, :, None] * k_dir * r_k, axis=-1, keepdims=True) * v[:, :, None], axis=2)
    return (y + bonus.reshape(bsz, L, D_A)) * g


def _rwkv7_mix(u, uc, need_ctx, shift_w, w0, w2, a0, a2, g2, k_k, k_a, r_k, ln_w, ln_b):
    lat = _rwkv7_streams(u, shift_w, w0, w2, a0, a2, g2, k_k, k_a)
    ctx = _rwkv7_streams(uc, shift_w, w0, w2, a0, a2, g2, k_k, k_a)
    s0 = jnp.zeros((u.shape[0], A_HEADS, HEAD_DIM, HEAD_DIM), jnp.float32)
    y_dirs, yc_dirs = [], []
    for d in range(2):
        yc_d, s_ctx = _rwkv7_scan(s0, *_rwkv7_dir_args(ctx, d), reverse=(d == 1))
        y_d, _ = _rwkv7_scan(s_ctx, *_rwkv7_dir_args(lat, d), reverse=(d == 1))
        y_dirs.append(y_d)
        yc_dirs.append(yc_d)
    y = _rwkv7_readout(y_dirs[0] + y_dirs[1], lat, r_k, ln_w, ln_b)
    yc = _rwkv7_readout(yc_dirs[0] + yc_dirs[1], ctx, r_k, ln_w, ln_b) if need_ctx else None
    return y, yc


def _gla_chunked(q, k, v, logf, s0):
    bsz, L, H, _ = q.shape
    n = L // GLA_CHUNK
    def chunks(t):
        return t.reshape(bsz, n, GLA_CHUNK, H, t.shape[-1]).transpose(1, 0, 3, 2, 4)
    qc, kc, vc, gc = chunks(q), chunks(k), chunks(v), chunks(logf)
    b = jnp.cumsum(gc, axis=3)
    b_mid = b[:, :, :, GLA_CHUNK // 2 - 1:GLA_CHUNK // 2]
    att = jnp.einsum('nbhtd,nbhsd->nbhts', qc * jnp.exp(b - b_mid), kc * jnp.exp(b_mid - b))
    within = jnp.tril(jnp.ones((GLA_CHUNK, GLA_CHUNK), dtype=bool))
    o_intra = jnp.einsum('nbhts,nbhse->nbhte', jnp.where(within, att, 0.0), vc)
    q_in = qc * jnp.exp(b)
    k_out = kc * jnp.exp(b[:, :, :, -1:] - b)
    dec = jnp.exp(b[:, :, :, -1])
    def step(s, inp):
        q_t, k_t, v_t, d_t = inp
        o = jnp.einsum('bhtd,bhde->bhte', q_t, s)
        s = s * d_t[..., None] + jnp.einsum('bhsd,bhse->bhde', k_t, v_t)
        return s, o
    s_fin, o_inter = lax.scan(step, s0, (q_in, k_out, vc, dec))
    o = (o_intra + o_inter).transpose(1, 0, 3, 2, 4).reshape(bsz, L, H, v.shape[-1])
    return o, s_fin


def _gla_dir(q, k, v, logf, s0, reverse):
    if not reverse:
        return _gla_chunked(q, k, v, logf, s0)
    flip = lambda t: jnp.flip(t, axis=1)
    o, s_fin = _gla_chunked(flip(q), flip(k), flip(v), flip(logf), s0)
    return flip(o), s_fin


def _hgrn2_streams(u, lb):
    bsz, L, _ = u.shape
    q, f_fwd, f_bwd, i, og = jnp.split(u.astype(jnp.float32), 5, axis=-1)
    heads = lambda t: t.reshape(bsz, L, B_HEADS, HEAD_DIM)
    dirs = []
    for f in (f_fwd, f_bwd):
        fg = lb + (1.0 - lb) * jax.nn.sigmoid(f)
        dirs.append((heads(1.0 - fg), heads(jnp.log(fg))))
    return heads(jax.nn.silu(q)), heads(i), og, dirs


def _hgrn2_readout(o, og, norm_w):
    return (_rms(o) * norm_w).reshape(o.shape[:2] + (D_B,)) * jax.nn.silu(og)


def _hgrn2_mix(u, uc, need_ctx, lb, norm_w):
    q, i, og, dirs = _hgrn2_streams(u, lb)
    qc, ic, ogc, dirs_c = _hgrn2_streams(uc, lb)
    s0 = jnp.zeros((u.shape[0], B_HEADS, HEAD_DIM, HEAD_DIM), jnp.float32)
    o_dirs, oc_dirs = [], []
    for d in range(2):
        kc_d, lfc_d = dirs_c[d]
        k_d, lf_d = dirs[d]
        oc_d, s_ctx = _gla_dir(qc, kc_d, ic, lfc_d, s0, reverse=(d == 1))
        o_d, _ = _gla_dir(q, k_d, i, lf_d, s_ctx, reverse=(d == 1))
        o_dirs.append(o_d)
        oc_dirs.append(oc_d)
    y = _hgrn2_readout(o_dirs[0] + o_dirs[1], og, norm_w)
    yc = _hgrn2_readout(oc_dirs[0] + oc_dirs[1], ogc, norm_w) if need_ctx else None
    return y, yc


def _even_mixer(h, hc, need_ctx, w_in, w_out, shift_w, w0, w2, a0, a2, g2, k_k, k_a, r_k, ln_w, ln_b,
                lb, norm_w):
    u, uc = h @ w_in, hc @ w_in
    ya, yac = _rwkv7_mix(u[..., :A_COLS], uc[..., :A_COLS], need_ctx, shift_w, w0, w2, a0, a2, g2,
                         k_k, k_a, r_k, ln_w, ln_b)
    yb, ybc = _hgrn2_mix(u[..., A_COLS:], uc[..., A_COLS:], need_ctx, lb, norm_w)
    y = jnp.concatenate([ya, yb], axis=-1).astype(h.dtype) @ w_out
    if not need_ctx:
        return y, None
    yc = jnp.concatenate([yac, ybc], axis=-1).astype(h.dtype) @ w_out
    return y, yc


def _axial_rope(x):
    bsz, L, H, dh = x.shape
    t = jnp.arange(L)
    pos = jnp.stack([t // GRID_W, t % GRID_W], axis=-1).astype(jnp.float32)
    nf = dh // 4
    inv = ROPE_THETA ** (-jnp.arange(nf, dtype=jnp.float32) / nf)
    ang = pos[:, None, :, None] * inv
    cos, sin = jnp.cos(ang), jnp.sin(ang)
    xr = x.astype(jnp.float32).reshape(bsz, L, H, 2, 2, nf)
    x1, x2 = xr[..., 0, :], xr[..., 1, :]
    out = jnp.stack([x1 * cos - x2 * sin, x1 * sin + x2 * cos], axis=-2)
    return out.reshape(bsz, L, H, dh).astype(x.dtype)


def _na_latent(q, k, v, kc, vc, rpb):
    bsz, S, H, dh = q.shape
    rows = S // GRID_W
    wr = min(NA_ROWS, rows)
    r = np.arange(rows)
    row_idx = np.clip(r - wr // 2, 0, rows - wr)[:, None] + np.arange(wr)
    row_off = row_idx - r[:, None] + NA_ROWS - 1
    cc = np.arange(GRID_W)
    col_idx = np.clip(cc - NA_COLS // 2, 0, GRID_W - NA_COLS)[:, None] + np.arange(NA_COLS)
    col_off = col_idx - cc[:, None] + NA_COLS - 1
    n_blk = rows // NA_QROWS
    qg = q.reshape(bsz, n_blk, NA_QROWS, GRID_W, H, dh).transpose(1, 0, 2, 3, 4, 5)
    kg = k.reshape(bsz, rows, GRID_W, H, dh)
    vg = v.reshape(bsz, rows, GRID_W, H, dh)
    ridx = jnp.asarray(row_idx.reshape(n_blk, NA_QROWS, wr))
    roff = jnp.asarray(row_off.reshape(n_blk, NA_QROWS, wr))
    n_win = wr * NA_COLS

    def block(args):
        qb, ri, ro = args
        kw = kg[:, ri][:, :, :, col_idx]
        vw = vg[:, ri][:, :, :, col_idx]
        s_win = jnp.einsum('bqwhd,bqrwchd->bhqwrc', qb, kw)
        bias = rpb[:, ro[:, None, :, None], col_off[None, :, None, :]]
        s_win = (s_win + bias[None]).reshape(bsz, H, NA_QROWS, GRID_W, n_win)
        s_ctx = jnp.einsum('bqwhd,bkhd->bhqwk', qb, kc)
        p = jax.nn.softmax(jnp.concatenate([s_win, s_ctx], axis=-1).astype(jnp.float32), axis=-1)
        p_win = p[..., :n_win].reshape(bsz, H, NA_QROWS, GRID_W, wr, NA_COLS).astype(v.dtype)
        p_ctx = p[..., n_win:].astype(v.dtype)
        return (jnp.einsum('bhqwrc,bqrwchd->bqwhd', p_win, vw)
                + jnp.einsum('bhqwk,bkhd->bqwhd', p_ctx, vc))

    o = lax.map(block, (qg, ridx, roff))
    return o.transpose(1, 0, 2, 3, 4, 5).reshape(bsz, S, H * dh)


def _ctx_attention(q, k, v):
    s = jnp.einsum('bqhd,bkhd->bhqk', q, k).astype(jnp.float32)
    p = jax.nn.softmax(s, axis=-1).astype(v.dtype)
    o = jnp.einsum('bhqk,bkhd->bqhd', p, v)
    return o.reshape(o.shape[:2] + (-1,))


def _hyena_filters(L, w1, b1, fr1, w2, b2, fr2, w3):
    t = jnp.linspace(0.0, 1.0, L, dtype=jnp.float32)[:, None]
    bands = (HYENA_EMB - 1) // 2
    f = jnp.linspace(1e-4, bands - 1, bands, dtype=jnp.float32)
    ang = (2.0 * math.pi / L) * jnp.arange(L, dtype=jnp.float32)[:, None] * f
    z = jnp.concatenate([t, jnp.cos(ang), -jnp.sin(ang)], axis=-1)
    hid = jnp.sin(fr1 * (z @ w1 + b1))
    hid = jnp.sin(fr2 * (hid @ w2 + b2))
    h = (hid @ w3).reshape(L, HYENA_ORDER, 2, D_D)
    deltas = jnp.abs(jnp.linspace(math.log(HYENA_TARGET) / HYENA_SLOW_DECAY,
                                  math.log(HYENA_TARGET) / HYENA_FAST_DECAY, D_D, dtype=jnp.float32))
    return h * jnp.exp(-t * deltas)[:, None, None, :]


def _bidir_longconv(z, h_fwd, h_bwd, bias):
    L = z.shape[1]
    k2 = jnp.concatenate([h_fwd.at[0].add(h_bwd[0]), jnp.zeros_like(h_fwd[:1]), h_bwd[:0:-1]], axis=0)
    zf = jnp.fft.rfft(z.astype(jnp.float32), n=2 * L, axis=1)
    kf = jnp.fft.rfft(k2.astype(jnp.float32), n=2 * L, axis=0)
    y = jnp.fft.irfft(zf * kf, n=2 * L, axis=1)[:, :L]
    return (y + z * bias).astype(z.dtype)


def _hyena(u, short_w, w1, b1, fr1, w2, b2, fr2, w3, bias):
    L = u.shape[1]
    u = _conv3(u, short_w)
    v, x1, x2 = jnp.split(u, 3, axis=-1)
    h = _hyena_filters(L, w1, b1, fr1, w2, b2, fr2, w3)
    z = x1 * _bidir_longconv(v, h[:, 0, 0], h[:, 0, 1], bias[0])
    return x2 * _bidir_longconv(z, h[:, 1, 0], h[:, 1, 1], bias[1])


def _odd_mixer(h, hc, need_ctx, w_in, w_out, q_norm, k_norm, rpb, short_w, w1, b1, fr1, w2, b2, fr2, w3,
               bias):
    u, uc = h @ w_in, hc @ w_in

    def qkv(t):
        bsz, L, _ = t.shape
        q, k, v = jnp.split(t[..., :C_COLS], 3, axis=-1)
        heads = lambda a: a.reshape(bsz, L, C_HEADS, HEAD_DIM)
        return _rms(heads(q)) * q_norm, _rms(heads(k)) * k_norm, heads(v)

    q, k, v = qkv(u)
    qc, kc, vc = qkv(uc)
    scale = HEAD_DIM ** -0.5
    y_na = _na_latent(_axial_rope(q) * scale, _axial_rope(k), v, kc, vc, rpb)
    y_hy = _hyena(u[..., C_COLS:], short_w, w1, b1, fr1, w2, b2, fr2, w3, bias)
    y = jnp.concatenate([y_na, y_hy.astype(y_na.dtype)], axis=-1) @ w_out
    if not need_ctx:
        return y, None
    yc_na = _ctx_attention(qc * scale, kc, vc)
    yc_hy = _hyena(uc[..., C_COLS:], short_w, w1, b1, fr1, w2, b2, fr2, w3, bias)
    yc = jnp.concatenate([yc_na, yc_hy.astype(yc_na.dtype)], axis=-1) @ w_out
    return y, yc


def _clamped_swiglu(y):
    glu, lin = y[..., ::2], y[..., 1::2]
    glu = jnp.minimum(glu, SWIGLU_LIMIT)
    lin = jnp.clip(lin, -SWIGLU_LIMIT, SWIGLU_LIMIT)
    return glu * jax.nn.sigmoid(SWIGLU_ALPHA * glu) * (lin + 1.0)


def _moe(h, router_w, router_b, w1, b1, w2, b2):
    T, D = h.shape
    logits = (h @ router_w + router_b).astype(jnp.float32)
    top_logit, top_e = lax.top_k(logits, TOP_K)
    gate = jax.nn.softmax(top_logit, axis=-1)
    n_assign = T * TOP_K
    flat_e = top_e.reshape(-1)
    order = jnp.argsort(flat_e)
    e_sorted = flat_e[order]
    counts = jnp.bincount(flat_e, length=N_EXPERTS)
    padded = (counts + MOE_BLOCK - 1) // MOE_BLOCK * MOE_BLOCK
    start = jnp.cumsum(counts) - counts
    p_end = jnp.cumsum(padded)
    p_start = p_end - padded
    dest = p_start[e_sorted] + jnp.arange(n_assign) - start[e_sorted]
    n_blocks = -(-n_assign // MOE_BLOCK) + N_EXPERTS
    n_slots = n_blocks * MOE_BLOCK
    slot_tok = jnp.full((n_slots,), T, jnp.int32).at[dest].set((order // TOP_K).astype(jnp.int32))
    slot_gate = jnp.zeros((n_slots,), jnp.float32).at[dest].set(gate.reshape(-1)[order])
    blk_e = jnp.minimum(jnp.searchsorted(p_end, jnp.arange(n_blocks) * MOE_BLOCK, side='right'),
                        N_EXPERTS - 1)
    x_slots = jnp.concatenate([h, jnp.zeros((1, D), h.dtype)], axis=0)[slot_tok]
    x_slots = x_slots.reshape(n_blocks, MOE_BLOCK, D)

    def expert_block(args):
        xb, e = args
        y = _clamped_swiglu(xb @ w1[e] + b1[e])
        return y @ w2[e] + b2[e]

    y = lax.map(expert_block, (x_slots, blk_e)).reshape(n_slots, D)
    out = jnp.zeros((T + 1, D), jnp.float32).at[slot_tok].add(y.astype(jnp.float32) * slot_gate[:, None])
    return out[:T].astype(h.dtype)


def setup_inputs(seed: int = 0) -> dict:
    key = jax.random.key(seed)
    ks = iter(jax.random.split(key, 48))
    def nrm(shape, scale=1.0):
        return scale * jax.random.normal(next(ks), shape, jnp.float32)
    ne, no = (DEPTH + 1) // 2, DEPTH // 2
    taps = jnp.array([0.25, 0.5, 0.25], jnp.float32)[:, None]
    return {
        'x': nrm((BATCH, SEQ, D_MODEL)),
        'c': nrm((BATCH, D_MODEL)),
        'ctx': nrm((BATCH, CTX_LEN, D_MODEL)),
        'c_ctx': nrm((D_MODEL,)),
        'mod_w': nrm((DEPTH, D_MODEL, 6 * D_MODEL), D_MODEL ** -0.5),
        'mod_b': nrm((DEPTH, 6 * D_MODEL), 0.02),
        'router_w': nrm((DEPTH, D_MODEL, N_EXPERTS), D_MODEL ** -0.5),
        'router_b': nrm((DEPTH, N_EXPERTS), 0.01),
        'moe_w1': nrm((DEPTH, N_EXPERTS, D_MODEL, 2 * D_EXPERT), D_MODEL ** -0.5),
        'moe_b1': nrm((DEPTH, N_EXPERTS, 2 * D_EXPERT), 0.02),
        'moe_w2': nrm((DEPTH, N_EXPERTS, D_EXPERT, D_MODEL), D_EXPERT ** -0.5),
        'moe_b2': nrm((DEPTH, N_EXPERTS, D_MODEL), 0.02),
        'ab_w_in': nrm((ne, D_MODEL, AB_COLS), D_MODEL ** -0.5),
        'ab_w_out': nrm((ne, D_A + D_B, D_MODEL), (D_A + D_B) ** -0.5),
        'rwkv_shift': taps + nrm((ne, 3, A_COLS), 0.05),
        'rwkv_w0': jnp.linspace(-6.0, -1.0, D_A, dtype=jnp.float32) + nrm((ne, 2, D_A), 0.1),
        'rwkv_w2': nrm((ne, 2, DECAY_LORA, D_A), 0.1),
        'rwkv_a0': nrm((ne, 2, D_A), 0.1),
        'rwkv_a2': nrm((ne, 2, AAA_LORA, D_A), 0.1),
        'rwkv_g2': nrm((ne, GATE_LORA, D_A), GATE_LORA ** -0.5),
        'rwkv_k_k': 0.85 + nrm((ne, D_A), 0.05),
        'rwkv_k_a': 1.0 + nrm((ne, D_A), 0.05),
        'rwkv_r_k': nrm((ne, A_HEADS, HEAD_DIM), 0.1),
        'rwkv_ln_w': 1.0 + nrm((ne, D_A), 0.05),
        'rwkv_ln_b': nrm((ne, D_A), 0.02),
        'hgrn_lb_logits': nrm((ne + 1, D_B), 0.1),
        'hgrn_norm_w': 1.0 + nrm((ne, HEAD_DIM), 0.05),
        'cd_w_in': nrm((no, D_MODEL, CD_COLS), D_MODEL ** -0.5),
        'cd_w_out': nrm((no, D_C + D_D, D_MODEL), (D_C + D_D) ** -0.5),
        'na_q_norm': 1.0 + nrm((no, HEAD_DIM), 0.05),
        'na_k_norm': 1.0 + nrm((no, HEAD_DIM), 0.05),
        'na_rpb': nrm((no, C_HEADS, 2 * NA_ROWS - 1, 2 * NA_COLS - 1), 0.1),
        'hy_short': taps + nrm((no, 3, 3 * D_D), 0.05),
        'hy_w1': nrm((no, HYENA_EMB, HYENA_WIDTH), HYENA_EMB ** -0.5),
        'hy_b1': nrm((no, HYENA_WIDTH), 0.1),
        'hy_freq1': 1.0 + nrm((no, HYENA_WIDTH), 0.1),
        'hy_w2': nrm((no, HYENA_WIDTH, HYENA_WIDTH), HYENA_WIDTH ** -0.5),
        'hy_b2': nrm((no, HYENA_WIDTH), 0.1),
        'hy_freq2': 1.0 + nrm((no, HYENA_WIDTH), 0.1),
        'hy_w3': nrm((no, HYENA_WIDTH, HYENA_ORDER * 2 * D_D), 0.004),
        'hy_bias': nrm((no, HYENA_ORDER, D_D)),
    }


def reference(x, c, ctx, c_ctx, mod_w, mod_b, router_w, router_b, moe_w1, moe_b1, moe_w2, moe_b2,
              ab_w_in, ab_w_out, rwkv_shift, rwkv_w0, rwkv_w2, rwkv_a0, rwkv_a2, rwkv_g2, rwkv_k_k,
              rwkv_k_a, rwkv_r_k, rwkv_ln_w, rwkv_ln_b, hgrn_lb_logits, hgrn_norm_w, cd_w_in, cd_w_out,
              na_q_norm, na_k_norm, na_rpb, hy_short, hy_w1, hy_b1, hy_freq1, hy_w2, hy_b2, hy_freq2,
              hy_w3, hy_bias):
    xc = ctx
    lb_all = jnp.cumsum(jax.nn.softmax(hgrn_lb_logits.astype(jnp.float32), axis=0), axis=0)
    for l in range(DEPTH):
        need_ctx = l < DEPTH - 1
        j = l // 2
        mod = jax.nn.silu(c) @ mod_w[l] + mod_b[l]
        mod_c = jax.nn.silu(c_ctx) @ mod_w[l] + mod_b[l]
        sh1, sc1, g1, sh2, sc2, g2 = jnp.split(mod[:, None, :], 6, axis=-1)
        sh1c, sc1c, g1c, sh2c, sc2c, g2c = jnp.split(mod_c, 6)
        h = _modulate(x, sh1, sc1)
        hc = _modulate(xc, sh1c, sc1c)
        if l % 2 == 0:
            y, yc = _even_mixer(h, hc, need_ctx, ab_w_in[j], ab_w_out[j], rwkv_shift[j], rwkv_w0[j],
                                rwkv_w2[j], rwkv_a0[j], rwkv_a2[j], rwkv_g2[j], rwkv_k_k[j], rwkv_k_a[j],
                                rwkv_r_k[j], rwkv_ln_w[j], rwkv_ln_b[j], lb_all[j], hgrn_norm_w[j])
        else:
            y, yc = _odd_mixer(h, hc, need_ctx, cd_w_in[j], cd_w_out[j], na_q_norm[j], na_k_norm[j],
                               na_rpb[j], hy_short[j], hy_w1[j], hy_b1[j], hy_freq1[j], hy_w2[j], hy_b2[j],
                               hy_freq2[j], hy_w3[j], hy_bias[j])
        x = x + g1 * y.astype(x.dtype)
        h = _modulate(x, sh2, sc2)
        n_lat = h.shape[0] * h.shape[1]
        if need_ctx:
            xc = xc + g1c * yc.astype(xc.dtype)
            hc = _modulate(xc, sh2c, sc2c)
            tokens = jnp.concatenate([h.reshape(n_lat, -1), hc.reshape(-1, hc.shape[-1])], axis=0)
            out = _moe(tokens, router_w[l], router_b[l], moe_w1[l], moe_b1[l], moe_w2[l], moe_b2[l])
            x = x + g2 * out[:n_lat].reshape(x.shape)
            xc = xc + g2c * out[n_lat:].reshape(xc.shape)
        else:
            out = _moe(h.reshape(n_lat, -1), router_w[l], router_b[l], moe_w1[l], moe_b1[l], moe_w2[l],
                       moe_b2[l])
            x = x + g2 * out.reshape(x.shape)
    return x
```

```python
import functools
import math

import numpy as np
import jax
import jax.numpy as jnp
from jax import lax
from jax.experimental import pallas as pl
from jax.experimental.pallas import tpu as pltpu

F32 = jnp.float32
BF16 = jnp.bfloat16

HEAD_DIM = 64
GRID_W = 64
DECAY_LORA = 64
AAA_LORA = 64
GATE_LORA = 128
LN_X_EPS = 1e-5 * HEAD_DIM
NA_ROWS = 8
NA_COLS = 16
ROPE_THETA = 10000.0
HYENA_ORDER = 2
HYENA_EMB = 33
HYENA_FAST_DECAY = 0.3
HYENA_SLOW_DECAY = 1.5
HYENA_TARGET = 1e-2
N_EXPERTS = 32
TOP_K = 4
SWIGLU_ALPHA = 1.702
SWIGLU_LIMIT = 7.0
MOE_BLOCK = 256
RMS_EPS = 1e-6

ROW_TILE = 256
REC_CHUNK = 64
VMEM_LIMIT = 56 * 1024 * 1024

_HI = lax.Precision.HIGHEST


def _dot(a, b):
    return jnp.dot(a, b, precision=_HI, preferred_element_type=F32)


def _dot_nt(a, b):
    return lax.dot_general(a, b, (((1,), (1,)), ((), ())), precision=_HI, preferred_element_type=F32)


def _dot_tn(a, b):
    return lax.dot_general(a, b, (((0,), (0,)), ((), ())), precision=_HI, preferred_element_type=F32)


def _mod_linear_kernel(x_ref, sh_ref, sc_ref, w_ref, o_ref):
    x = x_ref[...]
    ms = jnp.mean(x * x, axis=-1, keepdims=True)
    h = x * lax.rsqrt(ms + RMS_EPS) * (1.0 + sc_ref[0]) + sh_ref[0]
    o_ref[...] = jnp.dot(h.astype(BF16), w_ref[...], preferred_element_type=F32)


def _mod_linear(x, shift, scale, w, rows_per_mod):
    m, d = x.shape
    n = w.shape[1]
    tiles_per_mod = rows_per_mod // ROW_TILE
    mod_spec = pl.BlockSpec((1, 1, d), lambda i: (i // tiles_per_mod, 0, 0))
    return pl.pallas_call(
        _mod_linear_kernel,
        grid=(m // ROW_TILE,),
        in_specs=[pl.BlockSpec((ROW_TILE, d), lambda i: (i, 0)), mod_spec, mod_spec,
                  pl.BlockSpec((d, n), lambda i: (0, 0))],
        out_specs=pl.BlockSpec((ROW_TILE, n), lambda i: (i, 0)),
        out_shape=jax.ShapeDtypeStruct((m, n), F32),
        compiler_params=pltpu.CompilerParams(dimension_semantics=("arbitrary",),
                                             vmem_limit_bytes=VMEM_LIMIT),
        name="mod_linear",
    )(x, shift[:, None, :], scale[:, None, :], w.astype(BF16))


def _res_linear_kernel(y_ref, w_ref, res_ref, g_ref, o_ref):
    o_ref[...] = res_ref[...] + g_ref[0] * jnp.dot(y_ref[...].astype(BF16), w_ref[...],
                                                   preferred_element_type=F32)


def _res_linear(y, w, res, gate, rows_per_mod):
    m, k = y.shape
    n = w.shape[1]
    tiles_per_mod = rows_per_mod // ROW_TILE
    return pl.pallas_call(
        _res_linear_kernel,
        grid=(m // ROW_TILE,),
        in_specs=[pl.BlockSpec((ROW_TILE, k), lambda i: (i, 0)),
                  pl.BlockSpec((k, n), lambda i: (0, 0)),
                  pl.BlockSpec((ROW_TILE, n), lambda i: (i, 0)),
                  pl.BlockSpec((1, 1, n), lambda i: (i // tiles_per_mod, 0, 0))],
        out_specs=pl.BlockSpec((ROW_TILE, n), lambda i: (i, 0)),
        out_shape=jax.ShapeDtypeStruct((m, n), F32),
        compiler_params=pltpu.CompilerParams(dimension_semantics=("arbitrary",),
                                             vmem_limit_bytes=VMEM_LIMIT),
        name="res_linear",
    )(y, w.astype(BF16), res, gate[:, None, :])


def _small_linear_kernel(x_ref, w_ref, b_ref, o_ref):
    o_ref[...] = _dot(x_ref[...], w_ref[...]) + b_ref[...]


def _small_linear(x, w, b):
    m, k = x.shape
    n = w.shape[1]
    tn = 1024 if n % 1024 == 0 else n
    return pl.pallas_call(
        _small_linear_kernel,
        grid=(n // tn,),
        in_specs=[pl.BlockSpec((m, k), lambda j: (0, 0)), pl.BlockSpec((k, tn), lambda j: (0, j)),
                  pl.BlockSpec((1, tn), lambda j: (0, j))],
        out_specs=pl.BlockSpec((m, tn), lambda j: (0, j)),
        out_shape=jax.ShapeDtypeStruct((m, n), F32),
        name="small_linear",
    )(x, w, b[None, :])


def _rec_kernel(*refs, delta, n_heads, nc_ctx):
    if delta:
        r_ref, v_ref, kap_ref, lw_ref, k_ref, al_ref, y_ref, s_ref = refs
    else:
        r_ref, v_ref, lw_ref, k_ref, y_ref, s_ref = refs
    c = r_ref.shape[1]
    d = pl.program_id(0)
    i = pl.program_id(2)

    @pl.when(i == 0)
    def _():
        s_ref[...] = jnp.zeros_like(s_ref)

    fwd = d == 0
    row = lax.broadcasted_iota(jnp.int32, (c, c), 0)
    col = lax.broadcasted_iota(jnp.int32, (c, c), 1)
    ahead = jnp.where(fwd, row - col, col - row)
    incl = ahead >= 0
    strict = ahead > 0
    tri = incl.astype(F32)
    mid = c // 2
    outs = []
    for h in range(n_heads):
        sl = slice(h * HEAD_DIM, (h + 1) * HEAD_DIM)
        lw = lw_ref[0, 0, :, sl]
        r = r_ref[0, :, sl]
        v = v_ref[0, :, sl]
        k = k_ref[0, 0, :, sl]
        b = _dot(tri, lw)
        bm = b[mid:mid + 1, :]
        tot = jnp.where(fwd, b[c - 1:c, :], b[0:1, :])
        e_neg = jnp.exp(bm - b)
        e_end = jnp.exp(tot - b)
        s0 = s_ref[h]
        s0m = s0 * jnp.exp(bm)
        rq = r * jnp.exp(b - bm)
        kd = k * e_neg
        if delta:
            kap = kap_ref[0, :, sl]
            al = al_ref[0, 0, :, sl]
            kq = kap * jnp.exp(b - lw - bm)
            q2 = jnp.concatenate([kq, rq], axis=0)
            k2 = jnp.concatenate([kd, al * e_neg], axis=0)
            a = _dot_nt(q2, k2)
            p = _dot_nt(q2, s0m)
            n_kk = jnp.where(strict, a[:c, :c], 0.0)
            x = -jnp.where(strict, a[:c, c:], 0.0)
            z = -(p[:c] + _dot(n_kk, v))
            z = z + _dot(x, z)
            for _ in range(int(math.log2(c)) - 1):
                x = _dot(x, x)
                z = z + _dot(x, z)
            y = (p[c:] + _dot(jnp.where(incl, a[c:, :c], 0.0), v)
                 + _dot(jnp.where(incl, a[c:, c:], 0.0), z))
            s_new = s0 * jnp.exp(tot) + _dot_tn(v, k * e_end) + _dot_tn(z, al * e_end)
        else:
            a = _dot_nt(rq, kd)
            y = _dot_nt(rq, s0m) + _dot(jnp.where(incl, a, 0.0), v)
            s_new = s0 * jnp.exp(tot) + _dot_tn(v, k * e_end)
        s_ref[h] = s_new
        outs.append(y)
    y_ref[0, 0] = jnp.concatenate(outs, axis=-1)


def _chunk_recurrence(shared, per_dir, n_ctx, delta):
    bsz, length, width = shared[0].shape
    c = REC_CHUNK
    n_tot = length // c
    nc_ctx = n_ctx // c

    def chunk(d, i):
        back = jnp.where(i < nc_ctx, nc_ctx - 1 - i, n_tot + nc_ctx - 1 - i)
        return jnp.where(d == 0, i, back)

    shared_spec = pl.BlockSpec((1, c, width), lambda d, b, i: (b, chunk(d, i), 0))
    dir_spec = pl.BlockSpec((1, 1, c, width), lambda d, b, i: (d, b, chunk(d, i), 0))
    if delta:
        args = [shared[0], shared[1], shared[2], per_dir[0], per_dir[1], per_dir[2]]
        specs = [shared_spec] * 3 + [dir_spec] * 3
    else:
        args = [shared[0], shared[1], per_dir[0], per_dir[1]]
        specs = [shared_spec] * 2 + [dir_spec] * 2
    n_heads = width // HEAD_DIM
    return pl.pallas_call(
        functools.partial(_rec_kernel, delta=delta, n_heads=n_heads, nc_ctx=nc_ctx),
        grid=(2, bsz, n_tot),
        in_specs=specs,
        out_specs=dir_spec,
        out_shape=jax.ShapeDtypeStruct((2, bsz, length, width), F32),
        scratch_shapes=[pltpu.VMEM((n_heads, HEAD_DIM, HEAD_DIM), F32)],
        compiler_params=pltpu.CompilerParams(
            dimension_semantics=("arbitrary", "arbitrary", "arbitrary")),
        name="rwkv7_rec" if delta else "hgrn2_rec",
    )(*args)


def _na_kernel(q_ref, k_ref, v_ref, kc_ref, vc_ref, bias_ref, o_ref, *, rows_per_step, n_rows):
    g = pl.program_id(2)
    kc = kc_ref[0, 0]
    vc = vc_ref[0, 0]
    band = NA_ROWS * GRID_W
    for j in range(rows_per_step):
        qr = g * rows_per_step + j
        start = jnp.clip(qr - NA_ROWS // 2, 0, n_rows - NA_ROWS)
        off = pl.multiple_of(start * GRID_W, GRID_W)
        q = q_ref[0, 0, j * GRID_W:(j + 1) * GRID_W, :].astype(BF16)
        kw = k_ref[0, 0, pl.ds(off, band), :]
        vw = v_ref[0, 0, pl.ds(off, band), :]
        s_win = lax.dot_general(q, kw, (((1,), (1,)), ((), ())), preferred_element_type=F32)
        s_win = s_win + bias_ref[0, start - qr + NA_ROWS - 1]
        s_ctx = lax.dot_general(q, kc, (((1,), (1,)), ((), ())), preferred_element_type=F32)
        m = jnp.maximum(jnp.max(s_win, axis=-1, keepdims=True), jnp.max(s_ctx, axis=-1, keepdims=True))
        p_win = jnp.exp(s_win - m)
        p_ctx = jnp.exp(s_ctx - m)
        den = jnp.sum(p_win, axis=-1, keepdims=True) + jnp.sum(p_ctx, axis=-1, keepdims=True)
        o = (jnp.dot(p_win.astype(BF16), vw, preferred_element_type=F32)
             + jnp.dot(p_ctx.astype(BF16), vc, preferred_element_type=F32))
        o_ref[0, 0, j * GRID_W:(j + 1) * GRID_W, :] = o / den


def _na_bias_table(rpb):
    cc = np.arange(GRID_W)
    col_start = np.clip(cc - NA_COLS // 2, 0, GRID_W - NA_COLS)
    kc = np.arange(GRID_W)
    inside = (kc[None, :] >= col_start[:, None]) & (kc[None, :] < col_start[:, None] + NA_COLS)
    col_off = np.clip(kc[None, :] - cc[:, None] + NA_COLS - 1, 0, 2 * NA_COLS - 2)
    tab = rpb[:, :, col_off]
    tab = jnp.where(jnp.asarray(inside)[None, None], tab, -jnp.inf)
    d0 = np.arange(NA_ROWS)[:, None] + np.arange(NA_ROWS)[None, :]
    band = tab[:, d0]
    return band.transpose(0, 1, 3, 2, 4).reshape(rpb.shape[0], NA_ROWS, GRID_W, NA_ROWS * GRID_W)


def _neighbourhood_attention(q, k, v, kc, vc, rpb):
    bsz, n_heads, s, dh = q.shape
    n_rows = s // GRID_W
    rows_per_step = 8
    lc = kc.shape[2]
    bias = _na_bias_table(rpb)
    full = pl.BlockSpec((1, 1, s, dh), lambda b, h, g: (b, h, 0, 0))
    ctx = pl.BlockSpec((1, 1, lc, dh), lambda b, h, g: (b, h, 0, 0))
    tile = pl.BlockSpec((1, 1, rows_per_step * GRID_W, dh), lambda b, h, g: (b, h, g, 0))
    return pl.pallas_call(
        functools.partial(_na_kernel, rows_per_step=rows_per_step, n_rows=n_rows),
        grid=(bsz, n_heads, n_rows // rows_per_step),
        in_specs=[tile, full, full, ctx, ctx,
                  pl.BlockSpec((1, NA_ROWS, GRID_W, NA_ROWS * GRID_W), lambda b, h, g: (h, 0, 0, 0))],
        out_specs=tile,
        out_shape=jax.ShapeDtypeStruct((bsz, n_heads, s, dh), F32),
        compiler_params=pltpu.CompilerParams(
            dimension_semantics=("arbitrary", "arbitrary", "arbitrary")),
        name="neighbourhood_attention",
    )(q, k.astype(BF16), v.astype(BF16), kc.astype(BF16), vc.astype(BF16), bias)


def _moe_kernel(be_ref, nb_ref, x_ref, w1_ref, b1_ref, w2_ref, b2_ref, o_ref):
    i = pl.program_id(0)

    @pl.when(i < nb_ref[0])
    def _():
        y = jnp.dot(x_ref[...], w1_ref[0], preferred_element_type=F32) + b1_ref[0]
        half = y.shape[1] // 2
        glu = jnp.minimum(y[:, :half], SWIGLU_LIMIT)
        lin = jnp.clip(y[:, half:], -SWIGLU_LIMIT, SWIGLU_LIMIT)
        act = glu * jax.nn.sigmoid(SWIGLU_ALPHA * glu) * (lin + 1.0)
        o_ref[...] = jnp.dot(act.astype(BF16), w2_ref[0], preferred_element_type=F32) + b2_ref[0]

    @pl.when(i >= nb_ref[0])
    def _():
        o_ref[...] = jnp.zeros_like(o_ref)


def _moe_experts(x_slots, blk_e, n_used, w1, b1, w2, b2):
    n_slots, d = x_slots.shape
    n_blocks = n_slots // MOE_BLOCK
    de2 = w1.shape[2]
    grid_spec = pltpu.PrefetchScalarGridSpec(
        num_scalar_prefetch=2,
        grid=(n_blocks,),
        in_specs=[pl.BlockSpec((MOE_BLOCK, d), lambda i, be, nb: (i, 0)),
                  pl.BlockSpec((1, d, de2), lambda i, be, nb: (be[i], 0, 0)),
                  pl.BlockSpec((1, 1, de2), lambda i, be, nb: (be[i], 0, 0)),
                  pl.BlockSpec((1, de2 // 2, d), lambda i, be, nb: (be[i], 0, 0)),
                  pl.BlockSpec((1, 1, d), lambda i, be, nb: (be[i], 0, 0))],
        out_specs=pl.BlockSpec((MOE_BLOCK, d), lambda i, be, nb: (i, 0)),
    )
    return pl.pallas_call(
        _moe_kernel,
        grid_spec=grid_spec,
        out_shape=jax.ShapeDtypeStruct((n_slots, d), F32),
        compiler_params=pltpu.CompilerParams(dimension_semantics=("arbitrary",),
                                             vmem_limit_bytes=VMEM_LIMIT),
        name="moe_experts",
    )(blk_e, n_used, x_slots, w1, b1[:, None, :], w2, b2[:, None, :])


def _moe(h, router_w, router_b, w1, b1, w2, b2):
    t, d = h.shape
    logits = _small_rows_linear(h, router_w, router_b)
    top_logit, top_e = lax.top_k(logits, TOP_K)
    gate = jax.nn.softmax(top_logit, axis=-1)
    n_assign = t * TOP_K
    flat_e = top_e.reshape(-1)
    order = jnp.argsort(flat_e)
    e_sorted = flat_e[order]
    counts = jnp.bincount(flat_e, length=N_EXPERTS)
    padded = (counts + MOE_BLOCK - 1) // MOE_BLOCK * MOE_BLOCK
    start = jnp.cumsum(counts) - counts
    p_end = jnp.cumsum(padded)
    p_start = p_end - padded
    dest = (p_start[e_sorted] + jnp.arange(n_assign) - start[e_sorted]).astype(jnp.int32)
    n_blocks = -(-n_assign // MOE_BLOCK) + N_EXPERTS
    n_slots = n_blocks * MOE_BLOCK
    slot_tok = jnp.full((n_slots,), t, jnp.int32).at[dest].set((order // TOP_K).astype(jnp.int32))
    blk_e = jnp.minimum(jnp.searchsorted(p_end, jnp.arange(n_blocks) * MOE_BLOCK, side='right'),
                        N_EXPERTS - 1).astype(jnp.int32)
    n_used = (p_end[-1] // MOE_BLOCK).astype(jnp.int32).reshape(1)
    x_slots = jnp.concatenate([h.astype(BF16), jnp.zeros((1, d), BF16)], axis=0)[slot_tok]
    w1p = jnp.concatenate([w1[..., ::2], w1[..., 1::2]], axis=-1).astype(BF16)
    b1p = jnp.concatenate([b1[..., ::2], b1[..., 1::2]], axis=-1)
    y = _moe_experts(x_slots, blk_e, n_used, w1p, b1p, w2.astype(BF16), b2)
    pos = jnp.zeros((n_assign,), jnp.int32).at[order].set(dest)
    yk = y[pos].reshape(t, TOP_K, d)
    return jnp.sum(yk * gate[:, :, None], axis=1)


def _rows_linear_kernel(x_ref, w_ref, b_ref, o_ref):
    o_ref[...] = _dot(x_ref[...], w_ref[...]) + b_ref[...]


def _small_rows_linear(x, w, b):
    m, k = x.shape
    n = w.shape[1]
    tm = 1024
    return pl.pallas_call(
        _rows_linear_kernel,
        grid=(m // tm,),
        in_specs=[pl.BlockSpec((tm, k), lambda i: (i, 0)), pl.BlockSpec((k, n), lambda i: (0, 0)),
                  pl.BlockSpec((1, n), lambda i: (0, 0))],
        out_specs=pl.BlockSpec((tm, n), lambda i: (i, 0)),
        out_shape=jax.ShapeDtypeStruct((m, n), F32),
        name="router_logits",
    )(x, w, b[None, :])


def _rms(x):
    return x * lax.rsqrt(jnp.mean(x * x, axis=-1, keepdims=True) + RMS_EPS)


def _conv3(u, w):
    up = jnp.pad(u, ((0, 0), (1, 1), (0, 0)))
    return up[:, :-2] * w[0] + up[:, 1:-1] * w[1] + up[:, 2:] * w[2]


def _rwkv7_streams(u, shift_w, w0, w2, a0, a2, g2, k_k, k_a):
    bsz, length, _ = u.shape
    d_a = w0.shape[-1]
    n_heads = d_a // HEAD_DIM
    u = _conv3(u, shift_w)
    splits = [d_a, 2 * d_a, 3 * d_a, 3 * d_a + 2 * DECAY_LORA, 3 * d_a + 2 * DECAY_LORA + 2 * AAA_LORA]
    r, k, v, wd, ad, gd = jnp.split(u, splits, axis=-1)
    lora_w = jnp.tanh(wd.reshape(bsz, length, 2, DECAY_LORA))
    w_log = -jax.nn.softplus(-(w0 + jnp.einsum('bldr,drc->bldc', lora_w, w2, precision=_HI))) - 0.5
    log_decay = -jnp.exp(w_log)
    a = jax.nn.sigmoid(a0 + jnp.einsum('bldr,drc->bldc', ad.reshape(bsz, length, 2, AAA_LORA), a2,
                                       precision=_HI))
    kk = (k * k_k).reshape(bsz, length, n_heads, HEAD_DIM)
    kk = kk / jnp.maximum(jnp.linalg.norm(kk, axis=-1, keepdims=True), 1e-12)
    kk = kk.reshape(bsz, length, d_a)
    k_dir = k[:, :, None, :] * (1.0 + (a - 1.0) * k_a)
    al = kk[:, :, None, :] * a
    g = jnp.dot(jax.nn.sigmoid(gd), g2, precision=_HI)
    to_dir = lambda t: jnp.moveaxis(t, 2, 0)
    return r, v, kk, to_dir(log_decay), to_dir(k_dir), to_dir(al), g


def _rwkv7_readout(y, r, v, k_dir, g, r_k, ln_w, ln_b):
    bsz, length, d_a = y.shape
    n_heads = d_a // HEAD_DIM
    heads = lambda t: t.reshape(t.shape[:-1] + (n_heads, HEAD_DIM))
    yh = heads(y)
    mu = jnp.mean(yh, axis=-1, keepdims=True)
    var = jnp.mean(jnp.square(yh - mu), axis=-1, keepdims=True)
    yn = ((yh - mu) * lax.rsqrt(var + LN_X_EPS)).reshape(bsz, length, d_a) * ln_w + ln_b
    rk = jnp.sum(heads(r)[None] * heads(k_dir) * r_k, axis=-1, keepdims=True)
    bonus = jnp.sum(rk * heads(v)[None], axis=0).reshape(bsz, length, d_a)
    return (yn + bonus) * g


def _hgrn2_streams(u, lb):
    q, f_fwd, f_bwd, i, og = jnp.split(u, 5, axis=-1)
    ks, lfs = [], []
    for f in (f_fwd, f_bwd):
        fg = lb + (1.0 - lb) * jax.nn.sigmoid(f)
        ks.append(1.0 - fg)
        lfs.append(jnp.log(fg))
    return jax.nn.silu(q), i, og, jnp.stack(lfs), jnp.stack(ks)


def _hgrn2_readout(o, og, norm_w):
    bsz, length, d_b = o.shape
    oh = o.reshape(bsz, length, d_b // HEAD_DIM, HEAD_DIM)
    return (_rms(oh) * norm_w).reshape(bsz, length, d_b) * jax.nn.silu(og)


def _even_mixer(u_lat, u_ctx, shift_w, w0, w2, a0, a2, g2, k_k, k_a, r_k, ln_w, ln_b, lb, norm_w):
    n_ctx = u_ctx.shape[1]
    a_cols = 3 * w0.shape[-1] + 2 * DECAY_LORA + 2 * AAA_LORA + GATE_LORA
    cat = lambda tc, tl: jnp.concatenate([tc, tl], axis=-2)
    sa_l = _rwkv7_streams(u_lat[..., :a_cols], shift_w, w0, w2, a0, a2, g2, k_k, k_a)
    sa_c = _rwkv7_streams(u_ctx[..., :a_cols], shift_w, w0, w2, a0, a2, g2, k_k, k_a)
    r, v, kap, lw, kd, al, g = (cat(c_, l_) for c_, l_ in zip(sa_c, sa_l))
    ya = _chunk_recurrence([r, v, kap], [lw, kd, al], n_ctx, delta=True)
    ya = _rwkv7_readout(ya[0] + ya[1], r, v, kd, g, r_k, ln_w, ln_b)
    sb_l = _hgrn2_streams(u_lat[..., a_cols:], lb)
    sb_c = _hgrn2_streams(u_ctx[..., a_cols:], lb)
    q, i, og, lf, kb = (cat(c_, l_) for c_, l_ in zip(sb_c, sb_l))
    yb = _chunk_recurrence([q, i], [lf, kb], n_ctx, delta=False)
    yb = _hgrn2_readout(yb[0] + yb[1], og, norm_w)
    y = jnp.concatenate([ya, yb], axis=-1)
    return y[:, n_ctx:], y[:, :n_ctx]


def _axial_rope(x):
    bsz, length, n_heads, dh = x.shape
    t = jnp.arange(length)
    pos = jnp.stack([t // GRID_W, t % GRID_W], axis=-1).astype(F32)
    nf = dh // 4
    inv = ROPE_THETA ** (-jnp.arange(nf, dtype=F32) / nf)
    ang = pos[:, None, :, None] * inv
    cos, sin = jnp.cos(ang), jnp.sin(ang)
    xr = x.reshape(bsz, length, n_heads, 2, 2, nf)
    x1, x2 = xr[..., 0, :], xr[..., 1, :]
    out = jnp.stack([x1 * cos - x2 * sin, x1 * sin + x2 * cos], axis=-2)
    return out.reshape(bsz, length, n_heads, dh)


def _hyena_filters(length, d_d, w1, b1, fr1, w2, b2, fr2, w3):
    t = jnp.linspace(0.0, 1.0, length, dtype=F32)[:, None]
    bands = (HYENA_EMB - 1) // 2
    f = jnp.linspace(1e-4, bands - 1, bands, dtype=F32)
    ang = (2.0 * math.pi / length) * jnp.arange(length, dtype=F32)[:, None] * f
    z = jnp.concatenate([t, jnp.cos(ang), -jnp.sin(ang)], axis=-1)
    hid = jnp.sin(fr1 * (jnp.dot(z, w1, precision=_HI) + b1))
    hid = jnp.sin(fr2 * (jnp.dot(hid, w2, precision=_HI) + b2))
    h = jnp.dot(hid, w3, precision=_HI).reshape(length, HYENA_ORDER, 2, d_d)
    deltas = jnp.abs(jnp.linspace(math.log(HYENA_TARGET) / HYENA_SLOW_DECAY,
                                  math.log(HYENA_TARGET) / HYENA_FAST_DECAY, d_d, dtype=F32))
    return h * jnp.exp(-t * deltas)[:, None, None, :]


def _bidir_longconv(z, h_fwd, h_bwd, bias):
    length = z.shape[1]
    k2 = jnp.concatenate([h_fwd.at[0].add(h_bwd[0]), jnp.zeros_like(h_fwd[:1]), h_bwd[:0:-1]], axis=0)
    zf = jnp.fft.rfft(z, n=2 * length, axis=1)
    kf = jnp.fft.rfft(k2, n=2 * length, axis=0)
    y = jnp.fft.irfft(zf * kf, n=2 * length, axis=1)[:, :length]
    return y + z * bias


def _hyena(u, short_w, w1, b1, fr1, w2, b2, fr2, w3, bias):
    length = u.shape[1]
    u = _conv3(u, short_w)
    v, x1, x2 = jnp.split(u, 3, axis=-1)
    h = _hyena_filters(length, v.shape[-1], w1, b1, fr1, w2, b2, fr2, w3)
    z = x1 * _bidir_longconv(v, h[:, 0, 0], h[:, 0, 1], bias[0])
    return x2 * _bidir_longconv(z, h[:, 1, 0], h[:, 1, 1], bias[1])


def _odd_mixer_latent(u_lat, u_ctx, q_norm, k_norm, rpb, short_w, w1, b1, fr1, w2, b2, fr2, w3, bias):
    n_heads = rpb.shape[0]
    d_c = n_heads * HEAD_DIM
    bsz, s, _ = u_lat.shape

    def qkv(t):
        length = t.shape[1]
        q, k, v = jnp.split(t[..., :3 * d_c], 3, axis=-1)
        heads = lambda a_: a_.reshape(bsz, length, n_heads, HEAD_DIM)
        return _rms(heads(q)) * q_norm, _rms(heads(k)) * k_norm, heads(v)

    q, k, v = qkv(u_lat)
    _, kc, vc = qkv(u_ctx)
    hm = lambda t: t.transpose(0, 2, 1, 3)
    o = _neighbourhood_attention(hm(_axial_rope(q) * HEAD_DIM ** -0.5), hm(_axial_rope(k)), hm(v),
                                 hm(kc), hm(vc), rpb)
    y_na = o.transpose(0, 2, 1, 3).reshape(bsz, s, d_c)
    y_hy = _hyena(u_lat[..., 3 * d_c:], short_w, w1, b1, fr1, w2, b2, fr2, w3, bias)
    return jnp.concatenate([y_na, y_hy], axis=-1)


def kernel(x, c, ctx, c_ctx, mod_w, mod_b, router_w, router_b, moe_w1, moe_b1, moe_w2, moe_b2, ab_w_in, ab_w_out, rwkv_shift, rwkv_w0, rwkv_w2, rwkv_a0, rwkv_a2, rwkv_g2, rwkv_k_k, rwkv_k_a, rwkv_r_k, rwkv_ln_w, rwkv_ln_b, hgrn_lb_logits, hgrn_norm_w, cd_w_in, cd_w_out, na_q_norm, na_k_norm, na_rpb, hy_short, hy_w1, hy_b1, hy_freq1, hy_w2, hy_b2, hy_freq2, hy_w3, hy_bias):
    bsz, s, d = x.shape
    n_ctx = ctx.shape[1]
    depth = mod_w.shape[0]
    n_lat = bsz * s
    lb_all = jnp.cumsum(jax.nn.softmax(hgrn_lb_logits, axis=0), axis=0)
    tok = jnp.concatenate([x.reshape(n_lat, d), ctx.reshape(bsz * n_ctx, d)], axis=0)
    cond = jnp.concatenate([jax.nn.silu(c), jax.nn.silu(c_ctx)[None, :],
                            jnp.zeros((8 - bsz - 1, d), F32)], axis=0)
    for l in range(depth):
        last = l == depth - 1
        j = l // 2
        mod = _small_linear(cond, mod_w[l], mod_b[l])[:bsz + 1]
        sh1, sc1, g1, sh2, sc2, g2 = jnp.split(mod, 6, axis=-1)
        w_in = ab_w_in[j] if l % 2 == 0 else cd_w_in[j]
        u = _mod_linear(tok, sh1, sc1, w_in, s)
        u_lat = u[:n_lat].reshape(bsz, s, -1)
        u_ctx = u[n_lat:].reshape(bsz, n_ctx, -1)
        if l % 2 == 0:
            y_lat, y_ctx = _even_mixer(u_lat, u_ctx, rwkv_shift[j], rwkv_w0[j], rwkv_w2[j], rwkv_a0[j],
                                       rwkv_a2[j], rwkv_g2[j], rwkv_k_k[j], rwkv_k_a[j], rwkv_r_k[j],
                                       rwkv_ln_w[j], rwkv_ln_b[j], lb_all[j], hgrn_norm_w[j])
            w_out = ab_w_out[j]
        else:
            y_lat = _odd_mixer_latent(u_lat, u_ctx, na_q_norm[j], na_k_norm[j], na_rpb[j], hy_short[j],
                                      hy_w1[j], hy_b1[j], hy_freq1[j], hy_w2[j], hy_b2[j], hy_freq2[j],
                                      hy_w3[j], hy_bias[j])
            y_ctx = jnp.zeros((bsz, n_ctx, d), F32)
            w_out = cd_w_out[j]
        if last:
            tok = tok[:n_lat]
            y = y_lat.reshape(n_lat, d)
        else:
            y = jnp.concatenate([y_lat.reshape(n_lat, d), y_ctx.reshape(bsz * n_ctx, d)], axis=0)
        tok = _res_linear(y, w_out, tok, g1, s)
        h2 = _modulate_rows(tok, sh2, sc2, s)
        out = _moe(h2, router_w[l], router_b[l], moe_w1[l], moe_b1[l], moe_w2[l], moe_b2[l])
        tok = _gated_add(tok, out, g2, s)
    return tok[:n_lat].reshape(bsz, s, d)


def _modulate_kernel(x_ref, sh_ref, sc_ref, o_ref):
    x = x_ref[...]
    ms = jnp.mean(x * x, axis=-1, keepdims=True)
    o_ref[...] = x * lax.rsqrt(ms + RMS_EPS) * (1.0 + sc_ref[0]) + sh_ref[0]


def _modulate_rows(x, shift, scale, rows_per_mod):
    m, d = x.shape
    tiles_per_mod = rows_per_mod // ROW_TILE
    mod_spec = pl.BlockSpec((1, 1, d), lambda i: (i // tiles_per_mod, 0, 0))
    return pl.pallas_call(
        _modulate_kernel,
        grid=(m // ROW_TILE,),
        in_specs=[pl.BlockSpec((ROW_TILE, d), lambda i: (i, 0)), mod_spec, mod_spec],
        out_specs=pl.BlockSpec((ROW_TILE, d), lambda i: (i, 0)),
        out_shape=jax.ShapeDtypeStruct((m, d), F32),
        name="modulate",
    )(x, shift[:, None, :], scale[:, None, :])


def _gated_add_kernel(x_ref, y_ref, g_ref, o_ref):
    o_ref[...] = x_ref[...] + g_ref[0] * y_ref[...]


def _gated_add(x, y, gate, rows_per_mod):
    m, d = x.shape
    tiles_per_mod = rows_per_mod // ROW_TILE
    tile = pl.BlockSpec((ROW_TILE, d), lambda i: (i, 0))
    return pl.pallas_call(
        _gated_add_kernel,
        grid=(m // ROW_TILE,),
        in_specs=[tile, tile, pl.BlockSpec((1, 1, d), lambda i: (i // tiles_per_mod, 0, 0))],
        out_specs=tile,
        out_shape=jax.ShapeDtypeStruct((m, d), F32),
        name="gated_add",
    )(x, y, gate[:, None, :])
```

```python
import functools
import math

import numpy as np
import jax
import jax.numpy as jnp
from jax import lax
from jax.experimental import pallas as pl
from jax.experimental.pallas import tpu as pltpu

F32 = jnp.float32
BF16 = jnp.bfloat16

HEAD_DIM = 64
GRID_W = 64
DECAY_LORA = 64
AAA_LORA = 64
GATE_LORA = 128
LN_X_EPS = 1e-5 * HEAD_DIM
NA_ROWS = 8
NA_COLS = 16
ROPE_THETA = 10000.0
HYENA_ORDER = 2
HYENA_EMB = 33
HYENA_FAST_DECAY = 0.3
HYENA_SLOW_DECAY = 1.5
HYENA_TARGET = 1e-2
N_EXPERTS = 32
TOP_K = 4
SWIGLU_ALPHA = 1.702
SWIGLU_LIMIT = 7.0
MOE_BLOCK = 256
RMS_EPS = 1e-6

ROW_TILE = 256
REC_CHUNK = 64
REC_PASSES = 1
REC_CHAIN_PASSES = 3
VMEM_LIMIT = 56 * 1024 * 1024

_HI = lax.Precision.HIGHEST


_DIMS = {"nn": (((1,), (0,)), ((), ())), "nt": (((1,), (1,)), ((), ())), "tn": (((0,), (0,)), ((), ()))}


def _mm(a, b, form, passes):
    dims = _DIMS[form]
    if passes == 6:
        return lax.dot_general(a, b, dims, precision=_HI, preferred_element_type=F32)
    a_hi = a.astype(BF16)
    b_hi = b.astype(BF16)
    out = lax.dot_general(a_hi, b_hi, dims, preferred_element_type=F32)
    if passes == 3:
        a_lo = (a - a_hi.astype(F32)).astype(BF16)
        b_lo = (b - b_hi.astype(F32)).astype(BF16)
        out = out + (lax.dot_general(a_hi, b_lo, dims, preferred_element_type=F32)
                     + lax.dot_general(a_lo, b_hi, dims, preferred_element_type=F32))
    return out


def _dot(a, b):
    return _mm(a, b, "nn", 6)


def _mod_linear_kernel(x_ref, sh_ref, sc_ref, w_ref, o_ref):
    x = x_ref[...]
    ms = jnp.mean(x * x, axis=-1, keepdims=True)
    h = x * lax.rsqrt(ms + RMS_EPS) * (1.0 + sc_ref[0]) + sh_ref[0]
    o_ref[...] = jnp.dot(h.astype(BF16), w_ref[...], preferred_element_type=F32)


def _mod_linear(x, shift, scale, w, rows_per_mod):
    m, d = x.shape
    n = w.shape[1]
    tiles_per_mod = rows_per_mod // ROW_TILE
    mod_spec = pl.BlockSpec((1, 1, d), lambda i: (i // tiles_per_mod, 0, 0))
    return pl.pallas_call(
        _mod_linear_kernel,
        grid=(m // ROW_TILE,),
        in_specs=[pl.BlockSpec((ROW_TILE, d), lambda i: (i, 0)), mod_spec, mod_spec,
                  pl.BlockSpec((d, n), lambda i: (0, 0))],
        out_specs=pl.BlockSpec((ROW_TILE, n), lambda i: (i, 0)),
        out_shape=jax.ShapeDtypeStruct((m, n), F32),
        compiler_params=pltpu.CompilerParams(dimension_semantics=("arbitrary",),
                                             vmem_limit_bytes=VMEM_LIMIT),
        name="mod_linear",
    )(x, shift[:, None, :], scale[:, None, :], w.astype(BF16))


def _res_linear_kernel(y_ref, w_ref, res_ref, g_ref, o_ref):
    o_ref[...] = res_ref[...] + g_ref[0] * jnp.dot(y_ref[...].astype(BF16), w_ref[...],
                                                   preferred_element_type=F32)


def _res_linear(y, w, res, gate, rows_per_mod):
    m, k = y.shape
    n = w.shape[1]
    tiles_per_mod = rows_per_mod // ROW_TILE
    return pl.pallas_call(
        _res_linear_kernel,
        grid=(m // ROW_TILE,),
        in_specs=[pl.BlockSpec((ROW_TILE, k), lambda i: (i, 0)),
                  pl.BlockSpec((k, n), lambda i: (0, 0)),
                  pl.BlockSpec((ROW_TILE, n), lambda i: (i, 0)),
                  pl.BlockSpec((1, 1, n), lambda i: (i // tiles_per_mod, 0, 0))],
        out_specs=pl.BlockSpec((ROW_TILE, n), lambda i: (i, 0)),
        out_shape=jax.ShapeDtypeStruct((m, n), F32),
        compiler_params=pltpu.CompilerParams(dimension_semantics=("arbitrary",),
                                             vmem_limit_bytes=VMEM_LIMIT),
        name="res_linear",
    )(y, w.astype(BF16), res, gate[:, None, :])


def _small_linear_kernel(x_ref, w_ref, b_ref, o_ref):
    o_ref[...] = _dot(x_ref[...], w_ref[...]) + b_ref[...]


def _small_linear(x, w, b):
    m, k = x.shape
    n = w.shape[1]
    tn = 1024 if n % 1024 == 0 else n
    return pl.pallas_call(
        _small_linear_kernel,
        grid=(n // tn,),
        in_specs=[pl.BlockSpec((m, k), lambda j: (0, 0)), pl.BlockSpec((k, tn), lambda j: (0, j)),
                  pl.BlockSpec((1, tn), lambda j: (0, j))],
        out_specs=pl.BlockSpec((m, tn), lambda j: (0, j)),
        out_shape=jax.ShapeDtypeStruct((m, n), F32),
        name="small_linear",
    )(x, w, b[None, :])


def _rec_kernel(*refs, delta, n_heads):
    if delta:
        r_ref, v_ref, kap_ref, lw_ref, k_ref, al_ref, y_ref, s_ref = refs
    else:
        r_ref, v_ref, lw_ref, k_ref, y_ref, s_ref = refs
    c = r_ref.shape[1]
    d = pl.program_id(0)
    i = pl.program_id(2)

    @pl.when(i == 0)
    def _():
        s_ref[...] = jnp.zeros_like(s_ref)

    fwd = d == 0
    row = lax.broadcasted_iota(jnp.int32, (c, c), 0)
    col = lax.broadcasted_iota(jnp.int32, (c, c), 1)
    ahead = jnp.where(fwd, row - col, col - row)
    incl = ahead >= 0
    strict = ahead > 0
    mid = c // 2
    heads = range(n_heads)
    hs = lambda t: [t[:, h * HEAD_DIM:(h + 1) * HEAD_DIM] for h in heads]
    mm = functools.partial(_mm, passes=REC_PASSES)

    lw = lw_ref[0, 0]
    b = _mm(incl.astype(F32), lw, "nn", 6)
    bm = b[mid:mid + 1, :]
    tot = jnp.where(fwd, b[c - 1:c, :], b[0:1, :])
    e_neg = jnp.exp(bm - b)
    e_end = jnp.exp(tot - b)
    k = k_ref[0, 0]
    v = hs(v_ref[0])
    rq = hs(r_ref[0] * jnp.exp(b - bm))
    kd = hs(k * e_neg)
    k_end = hs(k * e_end)
    g_mid = hs(jnp.exp(bm))
    g_tot = hs(jnp.exp(tot))
    s0 = [s_ref[h] for h in heads]
    s0m = [s0[h] * g_mid[h] for h in heads]
    if delta:
        al = al_ref[0, 0]
        kq = hs(kap_ref[0] * jnp.exp(b - lw - bm))
        ad = hs(al * e_neg)
        al_end = hs(al * e_end)
        q2 = [jnp.concatenate([kq[h], rq[h]], axis=0) for h in heads]
        k2 = [jnp.concatenate([kd[h], ad[h]], axis=0) for h in heads]
        a = [mm(q2[h], k2[h], "nt") for h in heads]
        p = [mm(q2[h], s0m[h], "nt") for h in heads]
        z = [-(p[h][:c] + mm(jnp.where(strict, a[h][:c, :c], 0.0), v[h], "nn")) for h in heads]
        mc = functools.partial(_mm, passes=REC_CHAIN_PASSES)
        m = [jnp.where(strict, a[h][:c, c:], 0.0) for h in heads]
        pair = (row >> 1) == (col >> 1)
        eye = (row == col).astype(F32)
        t = [eye - jnp.where(pair, m[h], 0.0) for h in heads]
        for lvl in range(1, int(math.log2(c))):
            off = ((row >> (lvl + 1)) == (col >> (lvl + 1))) & ((row >> lvl) != (col >> lvl))
            tc = [mc(t[h], jnp.where(off, m[h], 0.0), "nn") for h in heads]
            t = [t[h] - mc(tc[h], t[h], "nn") for h in heads]
        z = [mc(t[h], z[h], "nn") for h in heads]
        y = [p[h][c:] + mm(jnp.where(incl, a[h][c:, :c], 0.0), v[h], "nn")
             + mm(jnp.where(incl, a[h][c:, c:], 0.0), z[h], "nn") for h in heads]
        s_new = [s0[h] * g_tot[h] + mm(v[h], k_end[h], "tn") + mm(z[h], al_end[h], "tn") for h in heads]
    else:
        a = [mm(rq[h], kd[h], "nt") for h in heads]
        y = [mm(rq[h], s0m[h], "nt") + mm(jnp.where(incl, a[h], 0.0), v[h], "nn") for h in heads]
        s_new = [s0[h] * g_tot[h] + mm(v[h], k_end[h], "tn") for h in heads]
    for h in heads:
        s_ref[h] = s_new[h]
    y_ref[0, 0] = jnp.concatenate(y, axis=-1)


def _chunk_recurrence(shared, per_dir, n_ctx, delta):
    bsz, length, width = shared[0].shape
    c = REC_CHUNK
    n_tot = length // c
    nc_ctx = n_ctx // c

    def chunk(d, i):
        back = jnp.where(i < nc_ctx, nc_ctx - 1 - i, n_tot + nc_ctx - 1 - i)
        return jnp.where(d == 0, i, back)

    shared_spec = pl.BlockSpec((1, c, width), lambda d, b, i: (b, chunk(d, i), 0))
    dir_spec = pl.BlockSpec((1, 1, c, width), lambda d, b, i: (d, b, chunk(d, i), 0))
    if delta:
        args = [shared[0], shared[1], shared[2], per_dir[0], per_dir[1], per_dir[2]]
        specs = [shared_spec] * 3 + [dir_spec] * 3
    else:
        args = [shared[0], shared[1], per_dir[0], per_dir[1]]
        specs = [shared_spec] * 2 + [dir_spec] * 2
    n_heads = width // HEAD_DIM
    return pl.pallas_call(
        functools.partial(_rec_kernel, delta=delta, n_heads=n_heads),
        grid=(2, bsz, n_tot),
        in_specs=specs,
        out_specs=dir_spec,
        out_shape=jax.ShapeDtypeStruct((2, bsz, length, width), F32),
        scratch_shapes=[pltpu.VMEM((n_heads, HEAD_DIM, HEAD_DIM), F32)],
        compiler_params=pltpu.CompilerParams(
            dimension_semantics=("arbitrary", "arbitrary", "arbitrary")),
        name="rwkv7_rec" if delta else "hgrn2_rec",
    )(*args)


def _na_kernel(q_ref, k_ref, v_ref, kc_ref, vc_ref, bias_ref, o_ref, *, rows_per_step, n_rows):
    g = pl.program_id(2)
    kc = kc_ref[0, 0]
    vc = vc_ref[0, 0]
    band = NA_ROWS * GRID_W
    for j in range(rows_per_step):
        qr = g * rows_per_step + j
        start = jnp.clip(qr - NA_ROWS // 2, 0, n_rows - NA_ROWS)
        off = pl.multiple_of(start * GRID_W, GRID_W)
        q = q_ref[0, 0, j * GRID_W:(j + 1) * GRID_W, :].astype(BF16)
        kw = k_ref[0, 0, pl.ds(off, band), :]
        vw = v_ref[0, 0, pl.ds(off, band), :]
        s_win = lax.dot_general(q, kw, (((1,), (1,)), ((), ())), preferred_element_type=F32)
        s_win = s_win + bias_ref[0, start - qr + NA_ROWS - 1]
        s_ctx = lax.dot_general(q, kc, (((1,), (1,)), ((), ())), preferred_element_type=F32)
        m = jnp.maximum(jnp.max(s_win, axis=-1, keepdims=True), jnp.max(s_ctx, axis=-1, keepdims=True))
        p_win = jnp.exp(s_win - m)
        p_ctx = jnp.exp(s_ctx - m)
        den = jnp.sum(p_win, axis=-1, keepdims=True) + jnp.sum(p_ctx, axis=-1, keepdims=True)
        o = (jnp.dot(p_win.astype(BF16), vw, preferred_element_type=F32)
             + jnp.dot(p_ctx.astype(BF16), vc, preferred_element_type=F32))
        o_ref[0, 0, j * GRID_W:(j + 1) * GRID_W, :] = o / den


def _na_bias_table(rpb):
    cc = np.arange(GRID_W)
    col_start = np.clip(cc - NA_COLS // 2, 0, GRID_W - NA_COLS)
    kc = np.arange(GRID_W)
    inside = (kc[None, :] >= col_start[:, None]) & (kc[None, :] < col_start[:, None] + NA_COLS)
    col_off = np.clip(kc[None, :] - cc[:, None] + NA_COLS - 1, 0, 2 * NA_COLS - 2)
    tab = rpb[:, :, col_off]
    tab = jnp.where(jnp.asarray(inside)[None, None], tab, -jnp.inf)
    d0 = np.arange(NA_ROWS)[:, None] + np.arange(NA_ROWS)[None, :]
    band = tab[:, d0]
    return band.transpose(0, 1, 3, 2, 4).reshape(rpb.shape[0], NA_ROWS, GRID_W, NA_ROWS * GRID_W)


def _neighbourhood_attention(q, k, v, kc, vc, rpb):
    bsz, n_heads, s, dh = q.shape
    n_rows = s // GRID_W
    rows_per_step = 8
    lc = kc.shape[2]
    bias = _na_bias_table(rpb)
    full = pl.BlockSpec((1, 1, s, dh), lambda b, h, g: (b, h, 0, 0))
    ctx = pl.BlockSpec((1, 1, lc, dh), lambda b, h, g: (b, h, 0, 0))
    tile = pl.BlockSpec((1, 1, rows_per_step * GRID_W, dh), lambda b, h, g: (b, h, g, 0))
    return pl.pallas_call(
        functools.partial(_na_kernel, rows_per_step=rows_per_step, n_rows=n_rows),
        grid=(bsz, n_heads, n_rows // rows_per_step),
        in_specs=[tile, full, full, ctx, ctx,
                  pl.BlockSpec((1, NA_ROWS, GRID_W, NA_ROWS * GRID_W), lambda b, h, g: (h, 0, 0, 0))],
        out_specs=tile,
        out_shape=jax.ShapeDtypeStruct((bsz, n_heads, s, dh), F32),
        compiler_params=pltpu.CompilerParams(
            dimension_semantics=("arbitrary", "arbitrary", "arbitrary")),
        name="neighbourhood_attention",
    )(q, k.astype(BF16), v.astype(BF16), kc.astype(BF16), vc.astype(BF16), bias)


def _moe_kernel(be_ref, nb_ref, x_ref, w1_ref, b1_ref, w2_ref, b2_ref, o_ref):
    i = pl.program_id(0)

    @pl.when(i < nb_ref[0])
    def _():
        y = jnp.dot(x_ref[...], w1_ref[0], preferred_element_type=F32) + b1_ref[0]
        half = y.shape[1] // 2
        glu = jnp.minimum(y[:, :half], SWIGLU_LIMIT)
        lin = jnp.clip(y[:, half:], -SWIGLU_LIMIT, SWIGLU_LIMIT)
        act = glu * jax.nn.sigmoid(SWIGLU_ALPHA * glu) * (lin + 1.0)
        o_ref[...] = jnp.dot(act.astype(BF16), w2_ref[0], preferred_element_type=F32) + b2_ref[0]

    @pl.when(i >= nb_ref[0])
    def _():
        o_ref[...] = jnp.zeros_like(o_ref)


def _moe_experts(x_slots, blk_e, n_used, w1, b1, w2, b2):
    n_slots, d = x_slots.shape
    n_blocks = n_slots // MOE_BLOCK
    de2 = w1.shape[2]
    grid_spec = pltpu.PrefetchScalarGridSpec(
        num_scalar_prefetch=2,
        grid=(n_blocks,),
        in_specs=[pl.BlockSpec((MOE_BLOCK, d), lambda i, be, nb: (i, 0)),
                  pl.BlockSpec((1, d, de2), lambda i, be, nb: (be[i], 0, 0)),
                  pl.BlockSpec((1, 1, de2), lambda i, be, nb: (be[i], 0, 0)),
                  pl.BlockSpec((1, de2 // 2, d), lambda i, be, nb: (be[i], 0, 0)),
                  pl.BlockSpec((1, 1, d), lambda i, be, nb: (be[i], 0, 0))],
        out_specs=pl.BlockSpec((MOE_BLOCK, d), lambda i, be, nb: (i, 0)),
    )
    return pl.pallas_call(
        _moe_kernel,
        grid_spec=grid_spec,
        out_shape=jax.ShapeDtypeStruct((n_slots, d), F32),
        compiler_params=pltpu.CompilerParams(dimension_semantics=("arbitrary",),
                                             vmem_limit_bytes=VMEM_LIMIT),
        name="moe_experts",
    )(blk_e, n_used, x_slots, w1, b1[:, None, :], w2, b2[:, None, :])


DEINT_GROUP = 256


def _deinterleave_kernel(w_ref, p_ref, o_ref):
    half = o_ref.shape[1] // 2
    g_out = DEINT_GROUP // 2
    for g in range(w_ref.shape[1] // DEINT_GROUP):
        t = jnp.dot(w_ref[:, g * DEINT_GROUP:(g + 1) * DEINT_GROUP].astype(BF16), p_ref[...],
                    preferred_element_type=F32).astype(BF16)
        o_ref[:, g * g_out:(g + 1) * g_out] = t[:, :g_out]
        o_ref[:, half + g * g_out:half + (g + 1) * g_out] = t[:, g_out:]


def _deinterleave_columns(w):
    m, n = w.shape
    tm = 512
    perm = np.zeros((DEINT_GROUP, DEINT_GROUP), np.float32)
    idx = np.arange(DEINT_GROUP // 2)
    perm[2 * idx, idx] = 1.0
    perm[2 * idx + 1, DEINT_GROUP // 2 + idx] = 1.0
    return pl.pallas_call(
        _deinterleave_kernel,
        grid=(m // tm,),
        in_specs=[pl.BlockSpec((tm, n), lambda i: (i, 0)),
                  pl.BlockSpec((DEINT_GROUP, DEINT_GROUP), lambda i: (0, 0))],
        out_specs=pl.BlockSpec((tm, n), lambda i: (i, 0)),
        out_shape=jax.ShapeDtypeStruct((m, n), BF16),
        name="deinterleave_columns",
    )(w, jnp.asarray(perm, BF16))


def _moe(h, router_w, router_b, w1, b1, w2, b2):
    t, d = h.shape
    logits = _small_rows_linear(h, router_w, router_b)
    top_logit, top_e = lax.top_k(logits, TOP_K)
    gate = jax.nn.softmax(top_logit, axis=-1)
    n_assign = t * TOP_K
    flat_e = top_e.reshape(-1)
    order = jnp.argsort(flat_e)
    e_sorted = flat_e[order]
    counts = jnp.bincount(flat_e, length=N_EXPERTS)
    padded = (counts + MOE_BLOCK - 1) // MOE_BLOCK * MOE_BLOCK
    start = jnp.cumsum(counts) - counts
    p_end = jnp.cumsum(padded)
    p_start = p_end - padded
    dest = (p_start[e_sorted] + jnp.arange(n_assign) - start[e_sorted]).astype(jnp.int32)
    n_blocks = -(-n_assign // MOE_BLOCK) + N_EXPERTS
    n_slots = n_blocks * MOE_BLOCK
    slot_tok = jnp.full((n_slots,), t, jnp.int32).at[dest].set((order // TOP_K).astype(jnp.int32))
    blk_e = jnp.minimum(jnp.searchsorted(p_end, jnp.arange(n_blocks) * MOE_BLOCK, side='right'),
                        N_EXPERTS - 1).astype(jnp.int32)
    n_used = (p_end[-1] // MOE_BLOCK).astype(jnp.int32).reshape(1)
    x_slots = jnp.concatenate([h.astype(BF16), jnp.zeros((1, d), BF16)], axis=0)[slot_tok]
    w1p = _deinterleave_columns(w1.reshape(N_EXPERTS * d, -1)).reshape(w1.shape)
    b1p = jnp.concatenate([b1[..., ::2], b1[..., 1::2]], axis=-1)
    y = _moe_experts(x_slots, blk_e, n_used, w1p, b1p, w2.astype(BF16), b2)
    pos = jnp.zeros((n_assign,), jnp.int32).at[order].set(dest)
    yk = y[pos].reshape(t, TOP_K, d)
    return jnp.sum(yk * gate[:, :, None], axis=1)


def _rows_linear_kernel(x_ref, w_ref, b_ref, o_ref):
    o_ref[...] = _dot(x_ref[...], w_ref[...]) + b_ref[...]


def _small_rows_linear(x, w, b):
    m, k = x.shape
    n = w.shape[1]
    tm = 1024
    return pl.pallas_call(
        _rows_linear_kernel,
        grid=(m // tm,),
        in_specs=[pl.BlockSpec((tm, k), lambda i: (i, 0)), pl.BlockSpec((k, n), lambda i: (0, 0)),
                  pl.BlockSpec((1, n), lambda i: (0, 0))],
        out_specs=pl.BlockSpec((tm, n), lambda i: (i, 0)),
        out_shape=jax.ShapeDtypeStruct((m, n), F32),
        name="router_logits",
    )(x, w, b[None, :])


def _rms(x):
    return x * lax.rsqrt(jnp.mean(x * x, axis=-1, keepdims=True) + RMS_EPS)


def _conv3(u, w):
    up = jnp.pad(u, ((0, 0), (1, 1), (0, 0)))
    return up[:, :-2] * w[0] + up[:, 1:-1] * w[1] + up[:, 2:] * w[2]


def _rwkv7_streams(u, shift_w, w0, w2, a0, a2, g2, k_k, k_a):
    bsz, length, _ = u.shape
    d_a = w0.shape[-1]
    n_heads = d_a // HEAD_DIM
    u = _conv3(u, shift_w)
    splits = [d_a, 2 * d_a, 3 * d_a, 3 * d_a + 2 * DECAY_LORA, 3 * d_a + 2 * DECAY_LORA + 2 * AAA_LORA]
    r, k, v, wd, ad, gd = jnp.split(u, splits, axis=-1)
    lora_w = jnp.tanh(wd.reshape(bsz, length, 2, DECAY_LORA))
    w_log = -jax.nn.softplus(-(w0 + jnp.einsum('bldr,drc->bldc', lora_w, w2, precision=_HI))) - 0.5
    log_decay = -jnp.exp(w_log)
    a = jax.nn.sigmoid(a0 + jnp.einsum('bldr,drc->bldc', ad.reshape(bsz, length, 2, AAA_LORA), a2,
                                       precision=_HI))
    kk = (k * k_k).reshape(bsz, length, n_heads, HEAD_DIM)
    kk = kk / jnp.maximum(jnp.linalg.norm(kk, axis=-1, keepdims=True), 1e-12)
    kk = kk.reshape(bsz, length, d_a)
    k_dir = k[:, :, None, :] * (1.0 + (a - 1.0) * k_a)
    al = kk[:, :, None, :] * a
    g = jnp.dot(jax.nn.sigmoid(gd), g2, precision=_HI)
    to_dir = lambda t: jnp.moveaxis(t, 2, 0)
    return r, v, kk, to_dir(log_decay), to_dir(k_dir), to_dir(al), g


def _rwkv7_readout(y, r, v, k_dir, g, r_k, ln_w, ln_b):
    bsz, length, d_a = y.shape
    n_heads = d_a // HEAD_DIM
    heads = lambda t: t.reshape(t.shape[:-1] + (n_heads, HEAD_DIM))
    yh = heads(y)
    mu = jnp.mean(yh, axis=-1, keepdims=True)
    var = jnp.mean(jnp.square(yh - mu), axis=-1, keepdims=True)
    yn = ((yh - mu) * lax.rsqrt(var + LN_X_EPS)).reshape(bsz, length, d_a) * ln_w + ln_b
    rk = jnp.sum(heads(r)[None] * heads(k_dir) * r_k, axis=-1, keepdims=True)
    bonus = jnp.sum(rk * heads(v)[None], axis=0).reshape(bsz, length, d_a)
    return (yn + bonus) * g


def _hgrn2_streams(u, lb):
    q, f_fwd, f_bwd, i, og = jnp.split(u, 5, axis=-1)
    ks, lfs = [], []
    for f in (f_fwd, f_bwd):
        fg = lb + (1.0 - lb) * jax.nn.sigmoid(f)
        ks.append(1.0 - fg)
        lfs.append(jnp.log(fg))
    return jax.nn.silu(q), i, og, jnp.stack(lfs), jnp.stack(ks)


def _hgrn2_readout(o, og, norm_w):
    bsz, length, d_b = o.shape
    oh = o.reshape(bsz, length, d_b // HEAD_DIM, HEAD_DIM)
    return (_rms(oh) * norm_w).reshape(bsz, length, d_b) * jax.nn.silu(og)


def _even_mixer(u_lat, u_ctx, shift_w, w0, w2, a0, a2, g2, k_k, k_a, r_k, ln_w, ln_b, lb, norm_w):
    n_ctx = u_ctx.shape[1]
    a_cols = 3 * w0.shape[-1] + 2 * DECAY_LORA + 2 * AAA_LORA + GATE_LORA
    cat = lambda tc, tl: jnp.concatenate([tc, tl], axis=-2)
    sa_l = _rwkv7_streams(u_lat[..., :a_cols], shift_w, w0, w2, a0, a2, g2, k_k, k_a)
    sa_c = _rwkv7_streams(u_ctx[..., :a_cols], shift_w, w0, w2, a0, a2, g2, k_k, k_a)
    r, v, kap, lw, kd, al, g = (cat(c_, l_) for c_, l_ in zip(sa_c, sa_l))
    ya = _chunk_recurrence([r, v, kap], [lw, kd, al], n_ctx, delta=True)
    ya = _rwkv7_readout(ya[0] + ya[1], r, v, kd, g, r_k, ln_w, ln_b)
    sb_l = _hgrn2_streams(u_lat[..., a_cols:], lb)
    sb_c = _hgrn2_streams(u_ctx[..., a_cols:], lb)
    q, i, og, lf, kb = (cat(c_, l_) for c_, l_ in zip(sb_c, sb_l))
    yb = _chunk_recurrence([q, i], [lf, kb], n_ctx, delta=False)
    yb = _hgrn2_readout(yb[0] + yb[1], og, norm_w)
    y = jnp.concatenate([ya, yb], axis=-1)
    return y[:, n_ctx:], y[:, :n_ctx]


def _axial_rope(x):
    bsz, length, n_heads, dh = x.shape
    t = jnp.arange(length)
    pos = jnp.stack([t // GRID_W, t % GRID_W], axis=-1).astype(F32)
    nf = dh // 4
    inv = ROPE_THETA ** (-jnp.arange(nf, dtype=F32) / nf)
    ang = pos[:, None, :, None] * inv
    cos, sin = jnp.cos(ang), jnp.sin(ang)
    xr = x.reshape(bsz, length, n_heads, 2, 2, nf)
    x1, x2 = xr[..., 0, :], xr[..., 1, :]
    out = jnp.stack([x1 * cos - x2 * sin, x1 * sin + x2 * cos], axis=-2)
    return out.reshape(bsz, length, n_heads, dh)


def _hyena_filters(length, d_d, w1, b1, fr1, w2, b2, fr2, w3):
    t = jnp.linspace(0.0, 1.0, length, dtype=F32)[:, None]
    bands = (HYENA_EMB - 1) // 2
    f = jnp.linspace(1e-4, bands - 1, bands, dtype=F32)
    ang = (2.0 * math.pi / length) * jnp.arange(length, dtype=F32)[:, None] * f
    z = jnp.concatenate([t, jnp.cos(ang), -jnp.sin(ang)], axis=-1)
    hid = jnp.sin(fr1 * (jnp.dot(z, w1, precision=_HI) + b1))
    hid = jnp.sin(fr2 * (jnp.dot(hid, w2, precision=_HI) + b2))
    h = jnp.dot(hid, w3, precision=_HI).reshape(length, HYENA_ORDER, 2, d_d)
    deltas = jnp.abs(jnp.linspace(math.log(HYENA_TARGET) / HYENA_SLOW_DECAY,
                                  math.log(HYENA_TARGET) / HYENA_FAST_DECAY, d_d, dtype=F32))
    return h * jnp.exp(-t * deltas)[:, None, None, :]


def _bidir_longconv(z, h_fwd, h_bwd, bias):
    length = z.shape[1]
    k2 = jnp.concatenate([h_fwd.at[0].add(h_bwd[0]), jnp.zeros_like(h_fwd[:1]), h_bwd[:0:-1]], axis=0)
    zf = jnp.fft.rfft(z, n=2 * length, axis=1)
    kf = jnp.fft.rfft(k2, n=2 * length, axis=0)
    y = jnp.fft.irfft(zf * kf, n=2 * length, axis=1)[:, :length]
    return y + z * bias


def _hyena(u, short_w, w1, b1, fr1, w2, b2, fr2, w3, bias):
    length = u.shape[1]
    u = _conv3(u, short_w)
    v, x1, x2 = jnp.split(u, 3, axis=-1)
    h = _hyena_filters(length, v.shape[-1], w1, b1, fr1, w2, b2, fr2, w3)
    z = x1 * _bidir_longconv(v, h[:, 0, 0], h[:, 0, 1], bias[0])
    return x2 * _bidir_longconv(z, h[:, 1, 0], h[:, 1, 1], bias[1])


def _odd_mixer_latent(u_lat, u_ctx, q_norm, k_norm, rpb, short_w, w1, b1, fr1, w2, b2, fr2, w3, bias):
    n_heads = rpb.shape[0]
    d_c = n_heads * HEAD_DIM
    bsz, s, _ = u_lat.shape

    def qkv(t):
        length = t.shape[1]
        q, k, v = jnp.split(t[..., :3 * d_c], 3, axis=-1)
        heads = lambda a_: a_.reshape(bsz, length, n_heads, HEAD_DIM)
        return _rms(heads(q)) * q_norm, _rms(heads(k)) * k_norm, heads(v)

    q, k, v = qkv(u_lat)
    _, kc, vc = qkv(u_ctx)
    hm = lambda t: t.transpose(0, 2, 1, 3)
    o = _neighbourhood_attention(hm(_axial_rope(q) * HEAD_DIM ** -0.5), hm(_axial_rope(k)), hm(v),
                                 hm(kc), hm(vc), rpb)
    y_na = o.transpose(0, 2, 1, 3).reshape(bsz, s, d_c)
    y_hy = _hyena(u_lat[..., 3 * d_c:], short_w, w1, b1, fr1, w2, b2, fr2, w3, bias)
    return jnp.concatenate([y_na, y_hy], axis=-1)


def kernel(x, c, ctx, c_ctx, mod_w, mod_b, router_w, router_b, moe_w1, moe_b1, moe_w2, moe_b2, ab_w_in, ab_w_out, rwkv_shift, rwkv_w0, rwkv_w2, rwkv_a0, rwkv_a2, rwkv_g2, rwkv_k_k, rwkv_k_a, rwkv_r_k, rwkv_ln_w, rwkv_ln_b, hgrn_lb_logits, hgrn_norm_w, cd_w_in, cd_w_out, na_q_norm, na_k_norm, na_rpb, hy_short, hy_w1, hy_b1, hy_freq1, hy_w2, hy_b2, hy_freq2, hy_w3, hy_bias):
    bsz, s, d = x.shape
    n_ctx = ctx.shape[1]
    depth = mod_w.shape[0]
    n_lat = bsz * s
    lb_all = jnp.cumsum(jax.nn.softmax(hgrn_lb_logits, axis=0), axis=0)
    tok = jnp.concatenate([x.reshape(n_lat, d), ctx.reshape(bsz * n_ctx, d)], axis=0)
    cond = jnp.concatenate([jax.nn.silu(c), jax.nn.silu(c_ctx)[None, :],
                            jnp.zeros((8 - bsz - 1, d), F32)], axis=0)
    for l in range(depth):
        last = l == depth - 1
        j = l // 2
        mod = _small_linear(cond, mod_w[l], mod_b[l])[:bsz + 1]
        sh1, sc1, g1, sh2, sc2, g2 = jnp.split(mod, 6, axis=-1)
        w_in = ab_w_in[j] if l % 2 == 0 else cd_w_in[j]
        u = _mod_linear(tok, sh1, sc1, w_in, s)
        u_lat = u[:n_lat].reshape(bsz, s, -1)
        u_ctx = u[n_lat:].reshape(bsz, n_ctx, -1)
        if l % 2 == 0:
            y_lat, y_ctx = _even_mixer(u_lat, u_ctx, rwkv_shift[j], rwkv_w0[j], rwkv_w2[j], rwkv_a0[j],
                                       rwkv_a2[j], rwkv_g2[j], rwkv_k_k[j], rwkv_k_a[j], rwkv_r_k[j],
                                       rwkv_ln_w[j], rwkv_ln_b[j], lb_all[j], hgrn_norm_w[j])
            w_out = ab_w_out[j]
        else:
            y_lat = _odd_mixer_latent(u_lat, u_ctx, na_q_norm[j], na_k_norm[j], na_rpb[j], hy_short[j],
                                      hy_w1[j], hy_b1[j], hy_freq1[j], hy_w2[j], hy_b2[j], hy_freq2[j],
                                      hy_w3[j], hy_bias[j])
            y_ctx = jnp.zeros((bsz, n_ctx, d), F32)
            w_out = cd_w_out[j]
        if last:
            tok = tok[:n_lat]
            y = y_lat.reshape(n_lat, d)
        else:
            y = jnp.concatenate([y_lat.reshape(n_lat, d), y_ctx.reshape(bsz * n_ctx, d)], axis=0)
        tok = _res_linear(y, w_out, tok, g1, s)
        h2 = _modulate_rows(tok, sh2, sc2, s)
        out = _moe(h2, router_w[l], router_b[l], moe_w1[l], moe_b1[l], moe_w2[l], moe_b2[l])
        tok = _gated_add(tok, out, g2, s)
    return tok[:n_lat].reshape(bsz, s, d)


def _modulate_kernel(x_ref, sh_ref, sc_ref, o_ref):
    x = x_ref[...]
    ms = jnp.mean(x * x, axis=-1, keepdims=True)
    o_ref[...] = x * lax.rsqrt(ms + RMS_EPS) * (1.0 + sc_ref[0]) + sh_ref[0]


def _modulate_rows(x, shift, scale, rows_per_mod):
    m, d = x.shape
    tiles_per_mod = rows_per_mod // ROW_TILE
    mod_spec = pl.BlockSpec((1, 1, d), lambda i: (i // tiles_per_mod, 0, 0))
    return pl.pallas_call(
        _modulate_kernel,
        grid=(m // ROW_TILE,),
        in_specs=[pl.BlockSpec((ROW_TILE, d), lambda i: (i, 0)), mod_spec, mod_spec],
        out_specs=pl.BlockSpec((ROW_TILE, d), lambda i: (i, 0)),
        out_shape=jax.ShapeDtypeStruct((m, d), F32),
        name="modulate",
    )(x, shift[:, None, :], scale[:, None, :])


def _gated_add_kernel(x_ref, y_ref, g_ref, o_ref):
    o_ref[...] = x_ref[...] + g_ref[0] * y_ref[...]


def _gated_add(x, y, gate, rows_per_mod):
    m, d = x.shape
    tiles_per_mod = rows_per_mod // ROW_TILE
    tile = pl.BlockSpec((ROW_TILE, d), lambda i: (i, 0))
    return pl.pallas_call(
        _gated_add_kernel,
        grid=(m // ROW_TILE,),
        in_specs=[tile, tile, pl.BlockSpec((1, 1, d), lambda i: (i // tiles_per_mod, 0, 0))],
        out_specs=tile,
        out_shape=jax.ShapeDtypeStruct((m, d), F32),
        name="gated_add",
    )(x, y, gate[:, None, :])
```

```python
import functools
import math

import numpy as np
import jax
import jax.numpy as jnp
from jax import lax
from jax.experimental import pallas as pl
from jax.experimental.pallas import tpu as pltpu

F32 = jnp.float32
BF16 = jnp.bfloat16

HEAD_DIM = 64
GRID_W = 64
DECAY_LORA = 64
AAA_LORA = 64
GATE_LORA = 128
LN_X_EPS = 1e-5 * HEAD_DIM
NA_ROWS = 8
NA_COLS = 16
ROPE_THETA = 10000.0
HYENA_ORDER = 2
HYENA_EMB = 33
HYENA_FAST_DECAY = 0.3
HYENA_SLOW_DECAY = 1.5
HYENA_TARGET = 1e-2
N_EXPERTS = 32
TOP_K = 4
SWIGLU_ALPHA = 1.702
SWIGLU_LIMIT = 7.0
MOE_BLOCK = 256
RMS_EPS = 1e-6

ROW_TILE = 256
REC_CHUNK = 64
REC_PASSES = 1
REC_CHAIN_PASSES = 3
VMEM_LIMIT = 56 * 1024 * 1024

_HI = lax.Precision.HIGHEST


_DIMS = {"nn": (((1,), (0,)), ((), ())), "nt": (((1,), (1,)), ((), ())), "tn": (((0,), (0,)), ((), ()))}


def _mm(a, b, form, passes):
    dims = _DIMS[form]
    if passes == 6:
        return lax.dot_general(a, b, dims, precision=_HI, preferred_element_type=F32)
    a_hi = a.astype(BF16)
    b_hi = b.astype(BF16)
    out = lax.dot_general(a_hi, b_hi, dims, preferred_element_type=F32)
    if passes == 3:
        a_lo = (a - a_hi.astype(F32)).astype(BF16)
        b_lo = (b - b_hi.astype(F32)).astype(BF16)
        out = out + (lax.dot_general(a_hi, b_lo, dims, preferred_element_type=F32)
                     + lax.dot_general(a_lo, b_hi, dims, preferred_element_type=F32))
    return out


def _dot(a, b):
    return _mm(a, b, "nn", 6)


def _mod_linear_kernel(x_ref, sh_ref, sc_ref, w_ref, o_ref):
    x = x_ref[...]
    ms = jnp.mean(x * x, axis=-1, keepdims=True)
    h = x * lax.rsqrt(ms + RMS_EPS) * (1.0 + sc_ref[0]) + sh_ref[0]
    o_ref[...] = jnp.dot(h.astype(BF16), w_ref[...], preferred_element_type=F32)


def _mod_linear(x, shift, scale, w, rows_per_mod):
    m, d = x.shape
    n = w.shape[1]
    tiles_per_mod = rows_per_mod // ROW_TILE
    mod_spec = pl.BlockSpec((1, 1, d), lambda i: (i // tiles_per_mod, 0, 0))
    return pl.pallas_call(
        _mod_linear_kernel,
        grid=(m // ROW_TILE,),
        in_specs=[pl.BlockSpec((ROW_TILE, d), lambda i: (i, 0)), mod_spec, mod_spec,
                  pl.BlockSpec((d, n), lambda i: (0, 0))],
        out_specs=pl.BlockSpec((ROW_TILE, n), lambda i: (i, 0)),
        out_shape=jax.ShapeDtypeStruct((m, n), F32),
        compiler_params=pltpu.CompilerParams(dimension_semantics=("arbitrary",),
                                             vmem_limit_bytes=VMEM_LIMIT),
        name="mod_linear",
    )(x, shift[:, None, :], scale[:, None, :], w.astype(BF16))


def _res_linear_kernel(y_ref, w_ref, res_ref, g_ref, o_ref):
    o_ref[...] = res_ref[...] + g_ref[0] * jnp.dot(y_ref[...].astype(BF16), w_ref[...],
                                                   preferred_element_type=F32)


def _res_linear(y, w, res, gate, rows_per_mod):
    m, k = y.shape
    n = w.shape[1]
    tiles_per_mod = rows_per_mod // ROW_TILE
    return pl.pallas_call(
        _res_linear_kernel,
        grid=(m // ROW_TILE,),
        in_specs=[pl.BlockSpec((ROW_TILE, k), lambda i: (i, 0)),
                  pl.BlockSpec((k, n), lambda i: (0, 0)),
                  pl.BlockSpec((ROW_TILE, n), lambda i: (i, 0)),
                  pl.BlockSpec((1, 1, n), lambda i: (i // tiles_per_mod, 0, 0))],
        out_specs=pl.BlockSpec((ROW_TILE, n), lambda i: (i, 0)),
        out_shape=jax.ShapeDtypeStruct((m, n), F32),
        compiler_params=pltpu.CompilerParams(dimension_semantics=("arbitrary",),
                                             vmem_limit_bytes=VMEM_LIMIT),
        name="res_linear",
    )(y, w.astype(BF16), res, gate[:, None, :])


def _small_linear_kernel(x_ref, w_ref, b_ref, o_ref):
    o_ref[...] = _dot(x_ref[...], w_ref[...]) + b_ref[...]


def _small_linear(x, w, b):
    m, k = x.shape
    n = w.shape[1]
    tn = 1024 if n % 1024 == 0 else n
    return pl.pallas_call(
        _small_linear_kernel,
        grid=(n // tn,),
        in_specs=[pl.BlockSpec((m, k), lambda j: (0, 0)), pl.BlockSpec((k, tn), lambda j: (0, j)),
                  pl.BlockSpec((1, tn), lambda j: (0, j))],
        out_specs=pl.BlockSpec((m, tn), lambda j: (0, j)),
        out_shape=jax.ShapeDtypeStruct((m, n), F32),
        name="small_linear",
    )(x, w, b[None, :])


def _rec_kernel(*refs, delta, n_heads):
    if delta:
        r_ref, v_ref, kap_ref, lw_ref, k_ref, al_ref, y_ref, s_ref = refs
    else:
        r_ref, v_ref, lw_ref, k_ref, y_ref, s_ref = refs
    c = r_ref.shape[1]
    d = pl.program_id(0)
    i = pl.program_id(2)

    @pl.when(i == 0)
    def _():
        s_ref[...] = jnp.zeros_like(s_ref)

    fwd = d == 0
    row = lax.broadcasted_iota(jnp.int32, (c, c), 0)
    col = lax.broadcasted_iota(jnp.int32, (c, c), 1)
    ahead = jnp.where(fwd, row - col, col - row)
    incl = ahead >= 0
    strict = ahead > 0
    mid = c // 2
    heads = range(n_heads)
    hs = lambda t: [t[:, h * HEAD_DIM:(h + 1) * HEAD_DIM] for h in heads]
    mm = functools.partial(_mm, passes=REC_PASSES)

    lw = lw_ref[0, 0]
    b = _mm(incl.astype(F32), lw, "nn", 6)
    bm = b[mid:mid + 1, :]
    tot = jnp.where(fwd, b[c - 1:c, :], b[0:1, :])
    e_neg = jnp.exp(bm - b)
    e_end = jnp.exp(tot - b)
    k = k_ref[0, 0]
    v = hs(v_ref[0])
    rq = hs(r_ref[0] * jnp.exp(b - bm))
    kd = hs(k * e_neg)
    k_end = hs(k * e_end)
    g_mid = hs(jnp.exp(bm))
    g_tot = hs(jnp.exp(tot))
    s0 = [s_ref[h] for h in heads]
    s0m = [s0[h] * g_mid[h] for h in heads]
    if delta:
        al = al_ref[0, 0]
        kq = hs(kap_ref[0] * jnp.exp(b - lw - bm))
        ad = hs(al * e_neg)
        al_end = hs(al * e_end)
        q2 = [jnp.concatenate([kq[h], rq[h]], axis=0) for h in heads]
        k2 = [jnp.concatenate([kd[h], ad[h]], axis=0) for h in heads]
        a = [mm(q2[h], k2[h], "nt") for h in heads]
        p = [mm(q2[h], s0m[h], "nt") for h in heads]
        z = [-(p[h][:c] + mm(jnp.where(strict, a[h][:c, :c], 0.0), v[h], "nn")) for h in heads]
        mc = functools.partial(_mm, passes=REC_CHAIN_PASSES)
        m = [jnp.where(strict, a[h][:c, c:], 0.0) for h in heads]
        pair = (row >> 1) == (col >> 1)
        eye = (row == col).astype(F32)
        t = [eye - jnp.where(pair, m[h], 0.0) for h in heads]
        for lvl in range(1, int(math.log2(c))):
            off = ((row >> (lvl + 1)) == (col >> (lvl + 1))) & ((row >> lvl) != (col >> lvl))
            tc = [mc(t[h], jnp.where(off, m[h], 0.0), "nn") for h in heads]
            t = [t[h] - mc(tc[h], t[h], "nn") for h in heads]
        z = [mc(t[h], z[h], "nn") for h in heads]
        y = [p[h][c:] + mm(jnp.where(incl, a[h][c:, :c], 0.0), v[h], "nn")
             + mm(jnp.where(incl, a[h][c:, c:], 0.0), z[h], "nn") for h in heads]
        s_new = [s0[h] * g_tot[h] + mm(v[h], k_end[h], "tn") + mm(z[h], al_end[h], "tn") for h in heads]
    else:
        a = [mm(rq[h], kd[h], "nt") for h in heads]
        y = [mm(rq[h], s0m[h], "nt") + mm(jnp.where(incl, a[h], 0.0), v[h], "nn") for h in heads]
        s_new = [s0[h] * g_tot[h] + mm(v[h], k_end[h], "tn") for h in heads]
    for h in heads:
        s_ref[h] = s_new[h]
    y_ref[0, 0] = jnp.concatenate(y, axis=-1)


def _chunk_recurrence(shared, per_dir, n_ctx, delta):
    bsz, length, width = shared[0].shape
    c = REC_CHUNK
    n_tot = length // c
    nc_ctx = n_ctx // c

    def chunk(d, i):
        back = jnp.where(i < nc_ctx, nc_ctx - 1 - i, n_tot + nc_ctx - 1 - i)
        return jnp.where(d == 0, i, back)

    shared_spec = pl.BlockSpec((1, c, width), lambda d, b, i: (b, chunk(d, i), 0))
    dir_spec = pl.BlockSpec((1, 1, c, width), lambda d, b, i: (d, b, chunk(d, i), 0))
    if delta:
        args = [shared[0], shared[1], shared[2], per_dir[0], per_dir[1], per_dir[2]]
        specs = [shared_spec] * 3 + [dir_spec] * 3
    else:
        args = [shared[0], shared[1], per_dir[0], per_dir[1]]
        specs = [shared_spec] * 2 + [dir_spec] * 2
    n_heads = width // HEAD_DIM
    return pl.pallas_call(
        functools.partial(_rec_kernel, delta=delta, n_heads=n_heads),
        grid=(2, bsz, n_tot),
        in_specs=specs,
        out_specs=dir_spec,
        out_shape=jax.ShapeDtypeStruct((2, bsz, length, width), F32),
        scratch_shapes=[pltpu.VMEM((n_heads, HEAD_DIM, HEAD_DIM), F32)],
        compiler_params=pltpu.CompilerParams(
            dimension_semantics=("arbitrary", "arbitrary", "arbitrary")),
        name="rwkv7_rec" if delta else "hgrn2_rec",
    )(*args)


def _na_kernel(q_ref, k_ref, v_ref, kc_ref, vc_ref, bias_ref, o_ref, *, rows_per_step, n_rows):
    g = pl.program_id(2)
    kc = kc_ref[0, 0]
    vc = vc_ref[0, 0]
    band = NA_ROWS * GRID_W
    for j in range(rows_per_step):
        qr = g * rows_per_step + j
        start = jnp.clip(qr - NA_ROWS // 2, 0, n_rows - NA_ROWS)
        off = pl.multiple_of(start * GRID_W, GRID_W)
        q = q_ref[0, 0, j * GRID_W:(j + 1) * GRID_W, :].astype(BF16)
        kw = k_ref[0, 0, pl.ds(off, band), :]
        vw = v_ref[0, 0, pl.ds(off, band), :]
        s_win = lax.dot_general(q, kw, (((1,), (1,)), ((), ())), preferred_element_type=F32)
        s_win = s_win + bias_ref[0, start - qr + NA_ROWS - 1]
        s_ctx = lax.dot_general(q, kc, (((1,), (1,)), ((), ())), preferred_element_type=F32)
        m = jnp.maximum(jnp.max(s_win, axis=-1, keepdims=True), jnp.max(s_ctx, axis=-1, keepdims=True))
        p_win = jnp.exp(s_win - m)
        p_ctx = jnp.exp(s_ctx - m)
        den = jnp.sum(p_win, axis=-1, keepdims=True) + jnp.sum(p_ctx, axis=-1, keepdims=True)
        o = (jnp.dot(p_win.astype(BF16), vw, preferred_element_type=F32)
             + jnp.dot(p_ctx.astype(BF16), vc, preferred_element_type=F32))
        o_ref[0, 0, j * GRID_W:(j + 1) * GRID_W, :] = o / den


def _na_bias_table(rpb):
    cc = np.arange(GRID_W)
    col_start = np.clip(cc - NA_COLS // 2, 0, GRID_W - NA_COLS)
    kc = np.arange(GRID_W)
    inside = (kc[None, :] >= col_start[:, None]) & (kc[None, :] < col_start[:, None] + NA_COLS)
    col_off = np.clip(kc[None, :] - cc[:, None] + NA_COLS - 1, 0, 2 * NA_COLS - 2)
    tab = rpb[:, :, col_off]
    tab = jnp.where(jnp.asarray(inside)[None, None], tab, -jnp.inf)
    d0 = np.arange(NA_ROWS)[:, None] + np.arange(NA_ROWS)[None, :]
    band = tab[:, d0]
    return band.transpose(0, 1, 3, 2, 4).reshape(rpb.shape[0], NA_ROWS, GRID_W, NA_ROWS * GRID_W)


def _neighbourhood_attention(q, k, v, kc, vc, rpb):
    bsz, n_heads, s, dh = q.shape
    n_rows = s // GRID_W
    rows_per_step = 8
    lc = kc.shape[2]
    bias = _na_bias_table(rpb)
    full = pl.BlockSpec((1, 1, s, dh), lambda b, h, g: (b, h, 0, 0))
    ctx = pl.BlockSpec((1, 1, lc, dh), lambda b, h, g: (b, h, 0, 0))
    tile = pl.BlockSpec((1, 1, rows_per_step * GRID_W, dh), lambda b, h, g: (b, h, g, 0))
    return pl.pallas_call(
        functools.partial(_na_kernel, rows_per_step=rows_per_step, n_rows=n_rows),
        grid=(bsz, n_heads, n_rows // rows_per_step),
        in_specs=[tile, full, full, ctx, ctx,
                  pl.BlockSpec((1, NA_ROWS, GRID_W, NA_ROWS * GRID_W), lambda b, h, g: (h, 0, 0, 0))],
        out_specs=tile,
        out_shape=jax.ShapeDtypeStruct((bsz, n_heads, s, dh), F32),
        compiler_params=pltpu.CompilerParams(
            dimension_semantics=("arbitrary", "arbitrary", "arbitrary")),
        name="neighbourhood_attention",
    )(q, k.astype(BF16), v.astype(BF16), kc.astype(BF16), vc.astype(BF16), bias)


def _moe_kernel(be_ref, nb_ref, x_ref, w1_ref, b1_ref, w2_ref, b2_ref, o_ref):
    i = pl.program_id(0)

    @pl.when(i < nb_ref[0])
    def _():
        y = jnp.dot(x_ref[...], w1_ref[0], preferred_element_type=F32) + b1_ref[0]
        half = y.shape[1] // 2
        glu = jnp.minimum(y[:, :half], SWIGLU_LIMIT)
        lin = jnp.clip(y[:, half:], -SWIGLU_LIMIT, SWIGLU_LIMIT)
        act = glu * jax.nn.sigmoid(SWIGLU_ALPHA * glu) * (lin + 1.0)
        o_ref[...] = jnp.dot(act.astype(BF16), w2_ref[0], preferred_element_type=F32) + b2_ref[0]

    @pl.when(i >= nb_ref[0])
    def _():
        o_ref[...] = jnp.zeros_like(o_ref)


def _moe_experts(x_slots, blk_e, n_used, w1, b1, w2, b2):
    n_slots, d = x_slots.shape
    n_blocks = n_slots // MOE_BLOCK
    de2 = w1.shape[2]
    grid_spec = pltpu.PrefetchScalarGridSpec(
        num_scalar_prefetch=2,
        grid=(n_blocks,),
        in_specs=[pl.BlockSpec((MOE_BLOCK, d), lambda i, be, nb: (i, 0)),
                  pl.BlockSpec((1, d, de2), lambda i, be, nb: (be[i], 0, 0)),
                  pl.BlockSpec((1, 1, de2), lambda i, be, nb: (be[i], 0, 0)),
                  pl.BlockSpec((1, de2 // 2, d), lambda i, be, nb: (be[i], 0, 0)),
                  pl.BlockSpec((1, 1, d), lambda i, be, nb: (be[i], 0, 0))],
        out_specs=pl.BlockSpec((MOE_BLOCK, d), lambda i, be, nb: (i, 0)),
    )
    return pl.pallas_call(
        _moe_kernel,
        grid_spec=grid_spec,
        out_shape=jax.ShapeDtypeStruct((n_slots, d), F32),
        compiler_params=pltpu.CompilerParams(dimension_semantics=("arbitrary",),
                                             vmem_limit_bytes=VMEM_LIMIT),
        name="moe_experts",
    )(blk_e, n_used, x_slots, w1, b1[:, None, :], w2, b2[:, None, :])


DEINT_GROUP = 256


def _deinterleave_kernel(w_ref, p_ref, o_ref):
    half = o_ref.shape[1] // 2
    g_out = DEINT_GROUP // 2
    for g in range(w_ref.shape[1] // DEINT_GROUP):
        t = jnp.dot(w_ref[:, g * DEINT_GROUP:(g + 1) * DEINT_GROUP].astype(BF16), p_ref[...],
                    preferred_element_type=F32).astype(BF16)
        o_ref[:, g * g_out:(g + 1) * g_out] = t[:, :g_out]
        o_ref[:, half + g * g_out:half + (g + 1) * g_out] = t[:, g_out:]


def _deinterleave_columns(w):
    m, n = w.shape
    tm = 512
    perm = np.zeros((DEINT_GROUP, DEINT_GROUP), np.float32)
    idx = np.arange(DEINT_GROUP // 2)
    perm[2 * idx, idx] = 1.0
    perm[2 * idx + 1, DEINT_GROUP // 2 + idx] = 1.0
    return pl.pallas_call(
        _deinterleave_kernel,
        grid=(m // tm,),
        in_specs=[pl.BlockSpec((tm, n), lambda i: (i, 0)),
                  pl.BlockSpec((DEINT_GROUP, DEINT_GROUP), lambda i: (0, 0))],
        out_specs=pl.BlockSpec((tm, n), lambda i: (i, 0)),
        out_shape=jax.ShapeDtypeStruct((m, n), BF16),
        name="deinterleave_columns",
    )(w, jnp.asarray(perm, BF16))


def _moe(h, router_w, router_b, w1, b1, w2, b2):
    t, d = h.shape
    logits = _small_rows_linear(h, router_w, router_b)
    top_logit, top_e = lax.top_k(logits, TOP_K)
    gate = jax.nn.softmax(top_logit, axis=-1)
    n_assign = t * TOP_K
    flat_e = top_e.reshape(-1)
    order = jnp.argsort(flat_e)
    e_sorted = flat_e[order]
    counts = jnp.bincount(flat_e, length=N_EXPERTS)
    padded = (counts + MOE_BLOCK - 1) // MOE_BLOCK * MOE_BLOCK
    start = jnp.cumsum(counts) - counts
    p_end = jnp.cumsum(padded)
    p_start = p_end - padded
    dest = (p_start[e_sorted] + jnp.arange(n_assign) - start[e_sorted]).astype(jnp.int32)
    n_blocks = -(-n_assign // MOE_BLOCK) + N_EXPERTS
    n_slots = n_blocks * MOE_BLOCK
    slot_tok = jnp.full((n_slots,), t, jnp.int32).at[dest].set((order // TOP_K).astype(jnp.int32))
    blk_e = jnp.minimum(jnp.searchsorted(p_end, jnp.arange(n_blocks) * MOE_BLOCK, side='right'),
                        N_EXPERTS - 1).astype(jnp.int32)
    n_used = (p_end[-1] // MOE_BLOCK).astype(jnp.int32).reshape(1)
    x_slots = jnp.concatenate([h.astype(BF16), jnp.zeros((1, d), BF16)], axis=0)[slot_tok]
    w1p = _deinterleave_columns(w1.reshape(N_EXPERTS * d, -1)).reshape(w1.shape)
    b1p = jnp.concatenate([b1[..., ::2], b1[..., 1::2]], axis=-1)
    y = _moe_experts(x_slots, blk_e, n_used, w1p, b1p, w2.astype(BF16), b2)
    pos = jnp.zeros((n_assign,), jnp.int32).at[order].set(dest)
    yk = y[pos].reshape(t, TOP_K, d)
    return jnp.sum(yk * gate[:, :, None], axis=1)


def _rows_linear_kernel(x_ref, w_ref, b_ref, o_ref):
    o_ref[...] = _dot(x_ref[...], w_ref[...]) + b_ref[...]


def _small_rows_linear(x, w, b):
    m, k = x.shape
    n = w.shape[1]
    tm = 1024
    return pl.pallas_call(
        _rows_linear_kernel,
        grid=(m // tm,),
        in_specs=[pl.BlockSpec((tm, k), lambda i: (i, 0)), pl.BlockSpec((k, n), lambda i: (0, 0)),
                  pl.BlockSpec((1, n), lambda i: (0, 0))],
        out_specs=pl.BlockSpec((tm, n), lambda i: (i, 0)),
        out_shape=jax.ShapeDtypeStruct((m, n), F32),
        name="router_logits",
    )(x, w, b[None, :])


def _rms(x):
    return x * lax.rsqrt(jnp.mean(x * x, axis=-1, keepdims=True) + RMS_EPS)


def _conv3(u, w):
    up = jnp.pad(u, ((0, 0), (1, 1), (0, 0)))
    return up[:, :-2] * w[0] + up[:, 1:-1] * w[1] + up[:, 2:] * w[2]


def _rwkv7_streams(u, shift_w, w0, w2, a0, a2, g2, k_k, k_a):
    bsz, length, _ = u.shape
    d_a = w0.shape[-1]
    n_heads = d_a // HEAD_DIM
    u = _conv3(u, shift_w)
    splits = [d_a, 2 * d_a, 3 * d_a, 3 * d_a + 2 * DECAY_LORA, 3 * d_a + 2 * DECAY_LORA + 2 * AAA_LORA]
    r, k, v, wd, ad, gd = jnp.split(u, splits, axis=-1)
    lora_w = jnp.tanh(wd.reshape(bsz, length, 2, DECAY_LORA))
    w_log = -jax.nn.softplus(-(w0 + jnp.einsum('bldr,drc->bldc', lora_w, w2, precision=_HI))) - 0.5
    log_decay = -jnp.exp(w_log)
    a = jax.nn.sigmoid(a0 + jnp.einsum('bldr,drc->bldc', ad.reshape(bsz, length, 2, AAA_LORA), a2,
                                       precision=_HI))
    kk = (k * k_k).reshape(bsz, length, n_heads, HEAD_DIM)
    kk = kk / jnp.maximum(jnp.linalg.norm(kk, axis=-1, keepdims=True), 1e-12)
    kk = kk.reshape(bsz, length, d_a)
    k_dir = k[:, :, None, :] * (1.0 + (a - 1.0) * k_a)
    al = kk[:, :, None, :] * a
    g = jnp.dot(jax.nn.sigmoid(gd), g2, precision=_HI)
    to_dir = lambda t: jnp.moveaxis(t, 2, 0)
    return r, v, kk, to_dir(log_decay), to_dir(k_dir), to_dir(al), g


def _rwkv7_readout(y, r, v, k_dir, g, r_k, ln_w, ln_b):
    bsz, length, d_a = y.shape
    n_heads = d_a // HEAD_DIM
    heads = lambda t: t.reshape(t.shape[:-1] + (n_heads, HEAD_DIM))
    yh = heads(y)
    mu = jnp.mean(yh, axis=-1, keepdims=True)
    var = jnp.mean(jnp.square(yh - mu), axis=-1, keepdims=True)
    yn = ((yh - mu) * lax.rsqrt(var + LN_X_EPS)).reshape(bsz, length, d_a) * ln_w + ln_b
    rk = jnp.sum(heads(r)[None] * heads(k_dir) * r_k, axis=-1, keepdims=True)
    bonus = jnp.sum(rk * heads(v)[None], axis=0).reshape(bsz, length, d_a)
    return (yn + bonus) * g


def _hgrn2_streams(u, lb):
    q, f_fwd, f_bwd, i, og = jnp.split(u, 5, axis=-1)
    ks, lfs = [], []
    for f in (f_fwd, f_bwd):
        fg = lb + (1.0 - lb) * jax.nn.sigmoid(f)
        ks.append(1.0 - fg)
        lfs.append(jnp.log(fg))
    return jax.nn.silu(q), i, og, jnp.stack(lfs), jnp.stack(ks)


def _hgrn2_readout(o, og, norm_w):
    bsz, length, d_b = o.shape
    oh = o.reshape(bsz, length, d_b // HEAD_DIM, HEAD_DIM)
    return (_rms(oh) * norm_w).reshape(bsz, length, d_b) * jax.nn.silu(og)


def _even_mixer(u_lat, u_ctx, shift_w, w0, w2, a0, a2, g2, k_k, k_a, r_k, ln_w, ln_b, lb, norm_w):
    n_ctx = u_ctx.shape[1]
    a_cols = 3 * w0.shape[-1] + 2 * DECAY_LORA + 2 * AAA_LORA + GATE_LORA
    cat = lambda tc, tl: jnp.concatenate([tc, tl], axis=-2)
    sa_l = _rwkv7_streams(u_lat[..., :a_cols], shift_w, w0, w2, a0, a2, g2, k_k, k_a)
    sa_c = _rwkv7_streams(u_ctx[..., :a_cols], shift_w, w0, w2, a0, a2, g2, k_k, k_a)
    r, v, kap, lw, kd, al, g = (cat(c_, l_) for c_, l_ in zip(sa_c, sa_l))
    ya = _chunk_recurrence([r, v, kap], [lw, kd, al], n_ctx, delta=True)
    ya = _rwkv7_readout(ya[0] + ya[1], r, v, kd, g, r_k, ln_w, ln_b)
    sb_l = _hgrn2_streams(u_lat[..., a_cols:], lb)
    sb_c = _hgrn2_streams(u_ctx[..., a_cols:], lb)
    q, i, og, lf, kb = (cat(c_, l_) for c_, l_ in zip(sb_c, sb_l))
    yb = _chunk_recurrence([q, i], [lf, kb], n_ctx, delta=False)
    yb = _hgrn2_readout(yb[0] + yb[1], og, norm_w)
    y = jnp.concatenate([ya, yb], axis=-1)
    return y[:, n_ctx:], y[:, :n_ctx]


def _axial_rope(x):
    bsz, length, n_heads, dh = x.shape
    t = jnp.arange(length)
    pos = jnp.stack([t // GRID_W, t % GRID_W], axis=-1).astype(F32)
    nf = dh // 4
    inv = ROPE_THETA ** (-jnp.arange(nf, dtype=F32) / nf)
    ang = pos[:, None, :, None] * inv
    cos, sin = jnp.cos(ang), jnp.sin(ang)
    xr = x.reshape(bsz, length, n_heads, 2, 2, nf)
    x1, x2 = xr[..., 0, :], xr[..., 1, :]
    out = jnp.stack([x1 * cos - x2 * sin, x1 * sin + x2 * cos], axis=-2)
    return out.reshape(bsz, length, n_heads, dh)


def _hyena_filters(length, d_d, w1, b1, fr1, w2, b2, fr2, w3):
    t = jnp.linspace(0.0, 1.0, length, dtype=F32)[:, None]
    bands = (HYENA_EMB - 1) // 2
    f = jnp.linspace(1e-4, bands - 1, bands, dtype=F32)
    ang = (2.0 * math.pi / length) * jnp.arange(length, dtype=F32)[:, None] * f
    z = jnp.concatenate([t, jnp.cos(ang), -jnp.sin(ang)], axis=-1)
    hid = jnp.sin(fr1 * (jnp.dot(z, w1, precision=_HI) + b1))
    hid = jnp.sin(fr2 * (jnp.dot(hid, w2, precision=_HI) + b2))
    h = jnp.dot(hid, w3, precision=_HI).reshape(length, HYENA_ORDER, 2, d_d)
    deltas = jnp.abs(jnp.linspace(math.log(HYENA_TARGET) / HYENA_SLOW_DECAY,
                                  math.log(HYENA_TARGET) / HYENA_FAST_DECAY, d_d, dtype=F32))
    return h * jnp.exp(-t * deltas)[:, None, None, :]


DFT_R = 128
DFT_COL_TILE = 4096
DFT_K1_TILE = 4


def _hi_lo(a):
    hi = a.astype(BF16)
    return hi, (a - hi.astype(F32)).astype(BF16)


def _mm3(f_hi, f_lo, x_hi, x_lo):
    d = lambda p, q: jnp.dot(p, q, preferred_element_type=F32)
    return d(f_hi, x_hi) + (d(f_hi, x_lo) + d(f_lo, x_hi))


def _dft_matrices(n1):
    k1 = np.arange(n1, dtype=np.float64)
    k2 = np.arange(DFT_R, dtype=np.float64)
    as_f32 = lambda a: jnp.asarray(a, F32)
    cs = lambda ang: (as_f32(np.cos(ang)), as_f32(-np.sin(ang)))
    f1 = cs(2.0 * np.pi * np.outer(k1, k1) / n1)
    f2 = cs(2.0 * np.pi * np.outer(k2, k2) / DFT_R)
    tw = cs(2.0 * np.pi * np.outer(k1, k2) / (n1 * DFT_R))
    tw = tuple(jnp.broadcast_to(t[:, :, None], (n1, DFT_R, 128)) for t in tw)
    return f1, f2, tw


def _dft_rows_kernel(x_ref, frh_ref, frl_ref, fih_ref, fil_ref, or_ref, oi_ref):
    x_hi, x_lo = _hi_lo(x_ref[0])
    or_ref[0] = _mm3(frh_ref[...], frl_ref[...], x_hi, x_lo)
    oi_ref[0] = _mm3(fih_ref[...], fil_ref[...], x_hi, x_lo)


def _dft_rows(x2d, f1):
    bsz, rows, cols = x2d.shape
    n1 = f1[0].shape[0]
    mats = [m for f in f1 for m in _hi_lo(f[:, :rows])]
    mat_spec = pl.BlockSpec((n1, rows), lambda b, j: (0, 0))
    out_spec = pl.BlockSpec((1, n1, DFT_COL_TILE), lambda b, j: (b, 0, j))
    out = jax.ShapeDtypeStruct((bsz, n1, cols), F32)
    return pl.pallas_call(
        _dft_rows_kernel,
        grid=(bsz, cols // DFT_COL_TILE),
        in_specs=[pl.BlockSpec((1, rows, DFT_COL_TILE), lambda b, j: (b, 0, j))] + [mat_spec] * 4,
        out_specs=[out_spec, out_spec],
        out_shape=[out, out],
        name="dft_rows",
    )(x2d, *mats)


def _dft_mid_kernel(*refs, conv):
    if conv:
        (ar_ref, ai_ref, tr_ref, ti_ref, kr_ref, ki_ref, frh_ref, frl_ref, fih_ref, fil_ref,
         or_ref, oi_ref) = refs
    else:
        ar_ref, ai_ref, tr_ref, ti_ref, frh_ref, frl_ref, fih_ref, fil_ref, or_ref, oi_ref = refs
    frh, frl, fih, fil = frh_ref[...], frl_ref[...], fih_ref[...], fil_ref[...]
    reps = ar_ref.shape[3] // tr_ref.shape[2]
    rows = range(ar_ref.shape[1])
    wide = lambda t: jnp.concatenate([t] * reps, axis=-1)
    tr = [wide(tr_ref[j]) for j in rows]
    ti = [wide(ti_ref[j]) for j in rows]
    ar = [ar_ref[0, j] for j in rows]
    ai = [ai_ref[0, j] for j in rows]
    pr = [_hi_lo(ar[j] * tr[j] - ai[j] * ti[j]) for j in rows]
    pi = [_hi_lo(ar[j] * ti[j] + ai[j] * tr[j]) for j in rows]
    xr = [_mm3(frh, frl, *pr[j]) - _mm3(fih, fil, *pi[j]) for j in rows]
    xi = [_mm3(frh, frl, *pi[j]) + _mm3(fih, fil, *pr[j]) for j in rows]
    if not conv:
        for j in rows:
            or_ref[0, j] = xr[j]
            oi_ref[0, j] = xi[j]
        return
    yr = [_hi_lo(xr[j] * kr_ref[0, j] - xi[j] * ki_ref[0, j]) for j in rows]
    yi = [_hi_lo(xr[j] * ki_ref[0, j] + xi[j] * kr_ref[0, j]) for j in rows]
    br = [_mm3(frh, frl, *yr[j]) + _mm3(fih, fil, *yi[j]) for j in rows]
    bi = [_mm3(frh, frl, *yi[j]) - _mm3(fih, fil, *yr[j]) for j in rows]
    for j in rows:
        or_ref[0, j] = br[j] * tr[j] + bi[j] * ti[j]
        oi_ref[0, j] = bi[j] * tr[j] - br[j] * ti[j]


def _dft_mid(a_r, a_i, f2, tw, kf=None):
    (f_r, f_i), (t_r, t_i) = f2, tw
    bsz, n1, _, ch = a_r.shape
    slab = pl.BlockSpec((1, DFT_K1_TILE, DFT_R, ch), lambda b, g: (b, g, 0, 0))
    tw = pl.BlockSpec((DFT_K1_TILE, DFT_R, t_r.shape[2]), lambda b, g: (g, 0, 0))
    mat = pl.BlockSpec((DFT_R, DFT_R), lambda b, g: (0, 0))
    args = [a_r, a_i, t_r, t_i]
    specs = [slab, slab, tw, tw]
    if kf is not None:
        kf_r, kf_i, o = kf
        kspec = pl.BlockSpec((1, DFT_K1_TILE, DFT_R, ch), lambda b, g: (o, g, 0, 0))
        args += [kf_r, kf_i]
        specs += [kspec, kspec]
    args += [m for f in (f_r, f_i) for m in _hi_lo(f)]
    specs += [mat] * 4
    out = jax.ShapeDtypeStruct(a_r.shape, F32)
    return pl.pallas_call(
        functools.partial(_dft_mid_kernel, conv=kf is not None),
        grid=(bsz, n1 // DFT_K1_TILE),
        in_specs=specs,
        out_specs=[slab, slab],
        out_shape=[out, out],
        compiler_params=pltpu.CompilerParams(dimension_semantics=("arbitrary", "arbitrary"),
                                             vmem_limit_bytes=VMEM_LIMIT),
        name="dft_mid_conv" if kf is not None else "dft_mid",
    )(*args)


def _idft_rows_kernel(br_ref, bi_ref, z_ref, g_ref, bias_ref, grh_ref, grl_ref, gih_ref, gil_ref, o_ref):
    y = (_mm3(grh_ref[...], grl_ref[...], *_hi_lo(br_ref[0]))
         + _mm3(gih_ref[...], gil_ref[...], *_hi_lo(bi_ref[0])))
    o_ref[0] = g_ref[0] * (y + z_ref[0] * bias_ref[...])


def _idft_rows(b_r, b_i, z2d, gate2d, bias_cols, f1):
    bsz, rows, cols = z2d.shape
    n1 = f1[0].shape[0]
    scale = 1.0 / (n1 * DFT_R)
    mats = [m for f in f1 for m in _hi_lo(f[:rows] * scale)]
    mat_spec = pl.BlockSpec((rows, n1), lambda b, j: (0, 0))
    in_spec = pl.BlockSpec((1, n1, DFT_COL_TILE), lambda b, j: (b, 0, j))
    io_spec = pl.BlockSpec((1, rows, DFT_COL_TILE), lambda b, j: (b, 0, j))
    return pl.pallas_call(
        _idft_rows_kernel,
        grid=(bsz, cols // DFT_COL_TILE),
        in_specs=[in_spec, in_spec, io_spec, io_spec, pl.BlockSpec((1, DFT_COL_TILE), lambda b, j: (0, j))]
        + [mat_spec] * 4,
        out_specs=io_spec,
        out_shape=jax.ShapeDtypeStruct((bsz, rows, cols), F32),
        name="idft_rows",
    )(b_r, b_i, z2d, gate2d, bias_cols, *mats)


def _hyena(u, short_w, w1, b1, fr1, w2, b2, fr2, w3, bias):
    bsz, length, _ = u.shape
    u = _conv3(u, short_w)
    v, x1, x2 = jnp.split(u, 3, axis=-1)
    ch = v.shape[-1]
    rows = length // DFT_R
    h = _hyena_filters(length, ch, w1, b1, fr1, w2, b2, fr2, w3)
    h_fwd, h_bwd = h[:, :, 0], h[:, :, 1]
    k2 = jnp.concatenate([h_fwd.at[0].add(h_bwd[0]), jnp.zeros_like(h_fwd[:1]), h_bwd[:0:-1]], axis=0)
    k2 = k2.transpose(1, 0, 2).reshape(HYENA_ORDER, 2 * rows, DFT_R * ch)
    n1 = 2 * rows
    f1, f2, tw = _dft_matrices(n1)
    four_d = lambda t: t.reshape(t.shape[0], n1, DFT_R, ch)
    kf_r, kf_i = _dft_mid(*(four_d(t) for t in _dft_rows(k2, f1)), f2, tw)
    z = v
    for o, gate in enumerate((x1, x2)):
        z2d = z.reshape(bsz, rows, DFT_R * ch)
        a_r, a_i = _dft_rows(z2d, f1)
        b_r, b_i = _dft_mid(four_d(a_r), four_d(a_i), f2, tw, kf=(kf_r, kf_i, o))
        flat = lambda t: t.reshape(bsz, n1, DFT_R * ch)
        y = _idft_rows(flat(b_r), flat(b_i), z2d, gate.reshape(z2d.shape), jnp.tile(bias[o], DFT_R)[None, :], f1)
        z = y.reshape(bsz, length, ch)
    return z


def _odd_mixer_latent(u_lat, u_ctx, q_norm, k_norm, rpb, short_w, w1, b1, fr1, w2, b2, fr2, w3, bias):
    n_heads = rpb.shape[0]
    d_c = n_heads * HEAD_DIM
    bsz, s, _ = u_lat.shape

    def qkv(t):
        length = t.shape[1]
        q, k, v = jnp.split(t[..., :3 * d_c], 3, axis=-1)
        heads = lambda a_: a_.reshape(bsz, length, n_heads, HEAD_DIM)
        return _rms(heads(q)) * q_norm, _rms(heads(k)) * k_norm, heads(v)

    q, k, v = qkv(u_lat)
    _, kc, vc = qkv(u_ctx)
    hm = lambda t: t.transpose(0, 2, 1, 3)
    o = _neighbourhood_attention(hm(_axial_rope(q) * HEAD_DIM ** -0.5), hm(_axial_rope(k)), hm(v),
                                 hm(kc), hm(vc), rpb)
    y_na = o.transpose(0, 2, 1, 3).reshape(bsz, s, d_c)
    y_hy = _hyena(u_lat[..., 3 * d_c:], short_w, w1, b1, fr1, w2, b2, fr2, w3, bias)
    return jnp.concatenate([y_na, y_hy], axis=-1)


def kernel(x, c, ctx, c_ctx, mod_w, mod_b, router_w, router_b, moe_w1, moe_b1, moe_w2, moe_b2, ab_w_in, ab_w_out, rwkv_shift, rwkv_w0, rwkv_w2, rwkv_a0, rwkv_a2, rwkv_g2, rwkv_k_k, rwkv_k_a, rwkv_r_k, rwkv_ln_w, rwkv_ln_b, hgrn_lb_logits, hgrn_norm_w, cd_w_in, cd_w_out, na_q_norm, na_k_norm, na_rpb, hy_short, hy_w1, hy_b1, hy_freq1, hy_w2, hy_b2, hy_freq2, hy_w3, hy_bias):
    bsz, s, d = x.shape
    n_ctx = ctx.shape[1]
    depth = mod_w.shape[0]
    n_lat = bsz * s
    lb_all = jnp.cumsum(jax.nn.softmax(hgrn_lb_logits, axis=0), axis=0)
    tok = jnp.concatenate([x.reshape(n_lat, d), ctx.reshape(bsz * n_ctx, d)], axis=0)
    cond = jnp.concatenate([jax.nn.silu(c), jax.nn.silu(c_ctx)[None, :],
                            jnp.zeros((8 - bsz - 1, d), F32)], axis=0)
    for l in range(depth):
        last = l == depth - 1
        j = l // 2
        mod = _small_linear(cond, mod_w[l], mod_b[l])[:bsz + 1]
        sh1, sc1, g1, sh2, sc2, g2 = jnp.split(mod, 6, axis=-1)
        w_in = ab_w_in[j] if l % 2 == 0 else cd_w_in[j]
        u = _mod_linear(tok, sh1, sc1, w_in, s)
        u_lat = u[:n_lat].reshape(bsz, s, -1)
        u_ctx = u[n_lat:].reshape(bsz, n_ctx, -1)
        if l % 2 == 0:
            y_lat, y_ctx = _even_mixer(u_lat, u_ctx, rwkv_shift[j], rwkv_w0[j], rwkv_w2[j], rwkv_a0[j],
                                       rwkv_a2[j], rwkv_g2[j], rwkv_k_k[j], rwkv_k_a[j], rwkv_r_k[j],
                                       rwkv_ln_w[j], rwkv_ln_b[j], lb_all[j], hgrn_norm_w[j])
            w_out = ab_w_out[j]
        else:
            y_lat = _odd_mixer_latent(u_lat, u_ctx, na_q_norm[j], na_k_norm[j], na_rpb[j], hy_short[j],
                                      hy_w1[j], hy_b1[j], hy_freq1[j], hy_w2[j], hy_b2[j], hy_freq2[j],
                                      hy_w3[j], hy_bias[j])
            y_ctx = jnp.zeros((bsz, n_ctx, d), F32)
            w_out = cd_w_out[j]
        if last:
            tok = tok[:n_lat]
            y = y_lat.reshape(n_lat, d)
        else:
            y = jnp.concatenate([y_lat.reshape(n_lat, d), y_ctx.reshape(bsz * n_ctx, d)], axis=0)
        tok = _res_linear(y, w_out, tok, g1, s)
        h2 = _modulate_rows(tok, sh2, sc2, s)
        out = _moe(h2, router_w[l], router_b[l], moe_w1[l], moe_b1[l], moe_w2[l], moe_b2[l])
        tok = _gated_add(tok, out, g2, s)
    return tok[:n_lat].reshape(bsz, s, d)


def _modulate_kernel(x_ref, sh_ref, sc_ref, o_ref):
    x = x_ref[...]
    ms = jnp.mean(x * x, axis=-1, keepdims=True)
    o_ref[...] = x * lax.rsqrt(ms + RMS_EPS) * (1.0 + sc_ref[0]) + sh_ref[0]


def _modulate_rows(x, shift, scale, rows_per_mod):
    m, d = x.shape
    tiles_per_mod = rows_per_mod // ROW_TILE
    mod_spec = pl.BlockSpec((1, 1, d), lambda i: (i // tiles_per_mod, 0, 0))
    return pl.pallas_call(
        _modulate_kernel,
        grid=(m // ROW_TILE,),
        in_specs=[pl.BlockSpec((ROW_TILE, d), lambda i: (i, 0)), mod_spec, mod_spec],
        out_specs=pl.BlockSpec((ROW_TILE, d), lambda i: (i, 0)),
        out_shape=jax.ShapeDtypeStruct((m, d), F32),
        name="modulate",
    )(x, shift[:, None, :], scale[:, None, :])


def _gated_add_kernel(x_ref, y_ref, g_ref, o_ref):
    o_ref[...] = x_ref[...] + g_ref[0] * y_ref[...]


def _gated_add(x, y, gate, rows_per_mod):
    m, d = x.shape
    tiles_per_mod = rows_per_mod // ROW_TILE
    tile = pl.BlockSpec((ROW_TILE, d), lambda i: (i, 0))
    return pl.pallas_call(
        _gated_add_kernel,
        grid=(m // ROW_TILE,),
        in_specs=[tile, tile, pl.BlockSpec((1, 1, d), lambda i: (i // tiles_per_mod, 0, 0))],
        out_specs=tile,
        out_shape=jax.ShapeDtypeStruct((m, d), F32),
        name="gated_add",
    )(x, y, gate[:, None, :])
```

```python
import functools
import math

import numpy as np
import jax
import jax.numpy as jnp
from jax import lax
from jax.experimental import pallas as pl
from jax.experimental.pallas import tpu as pltpu

F32 = jnp.float32
BF16 = jnp.bfloat16

HEAD_DIM = 64
GRID_W = 64
DECAY_LORA = 64
AAA_LORA = 64
GATE_LORA = 128
LN_X_EPS = 1e-5 * HEAD_DIM
NA_ROWS = 8
NA_COLS = 16
ROPE_THETA = 10000.0
HYENA_ORDER = 2
HYENA_EMB = 33
HYENA_FAST_DECAY = 0.3
HYENA_SLOW_DECAY = 1.5
HYENA_TARGET = 1e-2
N_EXPERTS = 32
TOP_K = 4
SWIGLU_ALPHA = 1.702
SWIGLU_LIMIT = 7.0
MOE_BLOCK = 256
RMS_EPS = 1e-6

ROW_TILE = 256
REC_CHUNK = 64
REC_PASSES = 1
REC_CHAIN_PASSES = 3
VMEM_LIMIT = 56 * 1024 * 1024

_HI = lax.Precision.HIGHEST


_DIMS = {"nn": (((1,), (0,)), ((), ())), "nt": (((1,), (1,)), ((), ())), "tn": (((0,), (0,)), ((), ()))}


def _mm(a, b, form, passes):
    dims = _DIMS[form]
    if passes == 6:
        return lax.dot_general(a, b, dims, precision=_HI, preferred_element_type=F32)
    a_hi = a.astype(BF16)
    b_hi = b.astype(BF16)
    out = lax.dot_general(a_hi, b_hi, dims, preferred_element_type=F32)
    if passes == 3:
        a_lo = (a - a_hi.astype(F32)).astype(BF16)
        b_lo = (b - b_hi.astype(F32)).astype(BF16)
        out = out + (lax.dot_general(a_hi, b_lo, dims, preferred_element_type=F32)
                     + lax.dot_general(a_lo, b_hi, dims, preferred_element_type=F32))
    return out


def _dot(a, b):
    return _mm(a, b, "nn", 6)


def _mod_linear_kernel(x_ref, sh_ref, sc_ref, w_ref, o_ref):
    x = x_ref[...]
    ms = jnp.mean(x * x, axis=-1, keepdims=True)
    h = x * lax.rsqrt(ms + RMS_EPS) * (1.0 + sc_ref[0]) + sh_ref[0]
    o_ref[...] = jnp.dot(h.astype(BF16), w_ref[...], preferred_element_type=F32)


def _mod_linear(x, shift, scale, w, rows_per_mod):
    m, d = x.shape
    n = w.shape[1]
    tiles_per_mod = rows_per_mod // ROW_TILE
    mod_spec = pl.BlockSpec((1, 1, d), lambda i: (i // tiles_per_mod, 0, 0))
    return pl.pallas_call(
        _mod_linear_kernel,
        grid=(m // ROW_TILE,),
        in_specs=[pl.BlockSpec((ROW_TILE, d), lambda i: (i, 0)), mod_spec, mod_spec,
                  pl.BlockSpec((d, n), lambda i: (0, 0))],
        out_specs=pl.BlockSpec((ROW_TILE, n), lambda i: (i, 0)),
        out_shape=jax.ShapeDtypeStruct((m, n), F32),
        compiler_params=pltpu.CompilerParams(dimension_semantics=("arbitrary",),
                                             vmem_limit_bytes=VMEM_LIMIT),
        name="mod_linear",
    )(x, shift[:, None, :], scale[:, None, :], w.astype(BF16))


def _res_linear_kernel(y_ref, w_ref, res_ref, g_ref, o_ref):
    o_ref[...] = res_ref[...] + g_ref[0] * jnp.dot(y_ref[...].astype(BF16), w_ref[...],
                                                   preferred_element_type=F32)


def _res_linear(y, w, res, gate, rows_per_mod):
    m, k = y.shape
    n = w.shape[1]
    tiles_per_mod = rows_per_mod // ROW_TILE
    return pl.pallas_call(
        _res_linear_kernel,
        grid=(m // ROW_TILE,),
        in_specs=[pl.BlockSpec((ROW_TILE, k), lambda i: (i, 0)),
                  pl.BlockSpec((k, n), lambda i: (0, 0)),
                  pl.BlockSpec((ROW_TILE, n), lambda i: (i, 0)),
                  pl.BlockSpec((1, 1, n), lambda i: (i // tiles_per_mod, 0, 0))],
        out_specs=pl.BlockSpec((ROW_TILE, n), lambda i: (i, 0)),
        out_shape=jax.ShapeDtypeStruct((m, n), F32),
        compiler_params=pltpu.CompilerParams(dimension_semantics=("arbitrary",),
                                             vmem_limit_bytes=VMEM_LIMIT),
        name="res_linear",
    )(y, w.astype(BF16), res, gate[:, None, :])


def _small_linear_kernel(x_ref, w_ref, b_ref, o_ref):
    o_ref[...] = _dot(x_ref[...], w_ref[...]) + b_ref[...]


def _small_linear(x, w, b):
    m, k = x.shape
    n = w.shape[1]
    tn = 1024 if n % 1024 == 0 else n
    return pl.pallas_call(
        _small_linear_kernel,
        grid=(n // tn,),
        in_specs=[pl.BlockSpec((m, k), lambda j: (0, 0)), pl.BlockSpec((k, tn), lambda j: (0, j)),
                  pl.BlockSpec((1, tn), lambda j: (0, j))],
        out_specs=pl.BlockSpec((m, tn), lambda j: (0, j)),
        out_shape=jax.ShapeDtypeStruct((m, n), F32),
        name="small_linear",
    )(x, w, b[None, :])


def _rec_kernel(*refs, delta, n_heads):
    if delta:
        r_ref, v_ref, kap_ref, lw_ref, k_ref, al_ref, y_ref, s_ref = refs
    else:
        r_ref, v_ref, lw_ref, k_ref, y_ref, s_ref = refs
    c = r_ref.shape[1]
    d = pl.program_id(0)
    i = pl.program_id(2)

    @pl.when(i == 0)
    def _():
        s_ref[...] = jnp.zeros_like(s_ref)

    fwd = d == 0
    row = lax.broadcasted_iota(jnp.int32, (c, c), 0)
    col = lax.broadcasted_iota(jnp.int32, (c, c), 1)
    ahead = jnp.where(fwd, row - col, col - row)
    incl = ahead >= 0
    strict = ahead > 0
    mid = c // 2
    heads = range(n_heads)
    hs = lambda t: [t[:, h * HEAD_DIM:(h + 1) * HEAD_DIM] for h in heads]
    mm = functools.partial(_mm, passes=REC_PASSES)

    lw = lw_ref[0, 0]
    b = _mm(incl.astype(F32), lw, "nn", 6)
    bm = b[mid:mid + 1, :]
    tot = jnp.where(fwd, b[c - 1:c, :], b[0:1, :])
    e_neg = jnp.exp(bm - b)
    e_end = jnp.exp(tot - b)
    k = k_ref[0, 0]
    v = hs(v_ref[0])
    rq = hs(r_ref[0] * jnp.exp(b - bm))
    kd = hs(k * e_neg)
    k_end = hs(k * e_end)
    g_mid = hs(jnp.exp(bm))
    g_tot = hs(jnp.exp(tot))
    s0 = [s_ref[h] for h in heads]
    s0m = [s0[h] * g_mid[h] for h in heads]
    if delta:
        al = al_ref[0, 0]
        kq = hs(kap_ref[0] * jnp.exp(b - lw - bm))
        ad = hs(al * e_neg)
        al_end = hs(al * e_end)
        q2 = [jnp.concatenate([kq[h], rq[h]], axis=0) for h in heads]
        k2 = [jnp.concatenate([kd[h], ad[h]], axis=0) for h in heads]
        a = [mm(q2[h], k2[h], "nt") for h in heads]
        p = [mm(q2[h], s0m[h], "nt") for h in heads]
        z = [-(p[h][:c] + mm(jnp.where(strict, a[h][:c, :c], 0.0), v[h], "nn")) for h in heads]
        mc = functools.partial(_mm, passes=REC_CHAIN_PASSES)
        m = [jnp.where(strict, a[h][:c, c:], 0.0) for h in heads]
        pair = (row >> 1) == (col >> 1)
        eye = (row == col).astype(F32)
        t = [eye - jnp.where(pair, m[h], 0.0) for h in heads]
        for lvl in range(1, int(math.log2(c))):
            off = ((row >> (lvl + 1)) == (col >> (lvl + 1))) & ((row >> lvl) != (col >> lvl))
            tc = [mc(t[h], jnp.where(off, m[h], 0.0), "nn") for h in heads]
            t = [t[h] - mc(tc[h], t[h], "nn") for h in heads]
        z = [mc(t[h], z[h], "nn") for h in heads]
        y = [p[h][c:] + mm(jnp.where(incl, a[h][c:, :c], 0.0), v[h], "nn")
             + mm(jnp.where(incl, a[h][c:, c:], 0.0), z[h], "nn") for h in heads]
        s_new = [s0[h] * g_tot[h] + mm(v[h], k_end[h], "tn") + mm(z[h], al_end[h], "tn") for h in heads]
    else:
        a = [mm(rq[h], kd[h], "nt") for h in heads]
        y = [mm(rq[h], s0m[h], "nt") + mm(jnp.where(incl, a[h], 0.0), v[h], "nn") for h in heads]
        s_new = [s0[h] * g_tot[h] + mm(v[h], k_end[h], "tn") for h in heads]
    for h in heads:
        s_ref[h] = s_new[h]
    y_ref[0, 0] = jnp.concatenate(y, axis=-1)


def _chunk_recurrence(shared, per_dir, n_ctx, delta):
    bsz, length, width = shared[0].shape
    c = REC_CHUNK
    n_tot = length // c
    nc_ctx = n_ctx // c

    def chunk(d, i):
        back = jnp.where(i < nc_ctx, nc_ctx - 1 - i, n_tot + nc_ctx - 1 - i)
        return jnp.where(d == 0, i, back)

    shared_spec = pl.BlockSpec((1, c, width), lambda d, b, i: (b, chunk(d, i), 0))
    dir_spec = pl.BlockSpec((1, 1, c, width), lambda d, b, i: (d, b, chunk(d, i), 0))
    if delta:
        args = [shared[0], shared[1], shared[2], per_dir[0], per_dir[1], per_dir[2]]
        specs = [shared_spec] * 3 + [dir_spec] * 3
    else:
        args = [shared[0], shared[1], per_dir[0], per_dir[1]]
        specs = [shared_spec] * 2 + [dir_spec] * 2
    n_heads = width // HEAD_DIM
    return pl.pallas_call(
        functools.partial(_rec_kernel, delta=delta, n_heads=n_heads),
        grid=(2, bsz, n_tot),
        in_specs=specs,
        out_specs=dir_spec,
        out_shape=jax.ShapeDtypeStruct((2, bsz, length, width), F32),
        scratch_shapes=[pltpu.VMEM((n_heads, HEAD_DIM, HEAD_DIM), F32)],
        compiler_params=pltpu.CompilerParams(
            dimension_semantics=("arbitrary", "arbitrary", "arbitrary")),
        name="rwkv7_rec" if delta else "hgrn2_rec",
    )(*args)


def _na_kernel(q_ref, k_ref, v_ref, kc_ref, vc_ref, bias_ref, o_ref, *, rows_per_step, n_rows):
    g = pl.program_id(2)
    kc = kc_ref[0, 0]
    vc = vc_ref[0, 0]
    band = NA_ROWS * GRID_W
    for j in range(rows_per_step):
        qr = g * rows_per_step + j
        start = jnp.clip(qr - NA_ROWS // 2, 0, n_rows - NA_ROWS)
        off = pl.multiple_of(start * GRID_W, GRID_W)
        q = q_ref[0, 0, j * GRID_W:(j + 1) * GRID_W, :].astype(BF16)
        kw = k_ref[0, 0, pl.ds(off, band), :]
        vw = v_ref[0, 0, pl.ds(off, band), :]
        s_win = lax.dot_general(q, kw, (((1,), (1,)), ((), ())), preferred_element_type=F32)
        s_win = s_win + bias_ref[0, start - qr + NA_ROWS - 1]
        s_ctx = lax.dot_general(q, kc, (((1,), (1,)), ((), ())), preferred_element_type=F32)
        m = jnp.maximum(jnp.max(s_win, axis=-1, keepdims=True), jnp.max(s_ctx, axis=-1, keepdims=True))
        p_win = jnp.exp(s_win - m)
        p_ctx = jnp.exp(s_ctx - m)
        den = jnp.sum(p_win, axis=-1, keepdims=True) + jnp.sum(p_ctx, axis=-1, keepdims=True)
        o = (jnp.dot(p_win.astype(BF16), vw, preferred_element_type=F32)
             + jnp.dot(p_ctx.astype(BF16), vc, preferred_element_type=F32))
        o_ref[0, 0, j * GRID_W:(j + 1) * GRID_W, :] = o / den


def _na_bias_table(rpb):
    cc = np.arange(GRID_W)
    col_start = np.clip(cc - NA_COLS // 2, 0, GRID_W - NA_COLS)
    kc = np.arange(GRID_W)
    inside = (kc[None, :] >= col_start[:, None]) & (kc[None, :] < col_start[:, None] + NA_COLS)
    col_off = np.clip(kc[None, :] - cc[:, None] + NA_COLS - 1, 0, 2 * NA_COLS - 2)
    tab = rpb[:, :, col_off]
    tab = jnp.where(jnp.asarray(inside)[None, None], tab, -jnp.inf)
    d0 = np.arange(NA_ROWS)[:, None] + np.arange(NA_ROWS)[None, :]
    band = tab[:, d0]
    return band.transpose(0, 1, 3, 2, 4).reshape(rpb.shape[0], NA_ROWS, GRID_W, NA_ROWS * GRID_W)


def _neighbourhood_attention(q, k, v, kc, vc, rpb):
    bsz, n_heads, s, dh = q.shape
    n_rows = s // GRID_W
    rows_per_step = 8
    lc = kc.shape[2]
    bias = _na_bias_table(rpb)
    full = pl.BlockSpec((1, 1, s, dh), lambda b, h, g: (b, h, 0, 0))
    ctx = pl.BlockSpec((1, 1, lc, dh), lambda b, h, g: (b, h, 0, 0))
    tile = pl.BlockSpec((1, 1, rows_per_step * GRID_W, dh), lambda b, h, g: (b, h, g, 0))
    return pl.pallas_call(
        functools.partial(_na_kernel, rows_per_step=rows_per_step, n_rows=n_rows),
        grid=(bsz, n_heads, n_rows // rows_per_step),
        in_specs=[tile, full, full, ctx, ctx,
                  pl.BlockSpec((1, NA_ROWS, GRID_W, NA_ROWS * GRID_W), lambda b, h, g: (h, 0, 0, 0))],
        out_specs=tile,
        out_shape=jax.ShapeDtypeStruct((bsz, n_heads, s, dh), F32),
        compiler_params=pltpu.CompilerParams(
            dimension_semantics=("arbitrary", "arbitrary", "arbitrary")),
        name="neighbourhood_attention",
    )(q, k.astype(BF16), v.astype(BF16), kc.astype(BF16), vc.astype(BF16), bias)


def _moe_kernel(be_ref, nb_ref, x_ref, w1_ref, b1_ref, w2_ref, b2_ref, o_ref):
    i = pl.program_id(0)

    @pl.when(i < nb_ref[0])
    def _():
        y = jnp.dot(x_ref[...], w1_ref[0], preferred_element_type=F32) + b1_ref[0]
        half = y.shape[1] // 2
        glu = jnp.minimum(y[:, :half], SWIGLU_LIMIT)
        lin = jnp.clip(y[:, half:], -SWIGLU_LIMIT, SWIGLU_LIMIT)
        act = glu * jax.nn.sigmoid(SWIGLU_ALPHA * glu) * (lin + 1.0)
        o_ref[...] = jnp.dot(act.astype(BF16), w2_ref[0], preferred_element_type=F32) + b2_ref[0]

    @pl.when(i >= nb_ref[0])
    def _():
        o_ref[...] = jnp.zeros_like(o_ref)


def _moe_experts(x_slots, blk_e, n_used, w1, b1, w2, b2):
    n_slots, d = x_slots.shape
    n_blocks = n_slots // MOE_BLOCK
    de2 = w1.shape[2]
    grid_spec = pltpu.PrefetchScalarGridSpec(
        num_scalar_prefetch=2,
        grid=(n_blocks,),
        in_specs=[pl.BlockSpec((MOE_BLOCK, d), lambda i, be, nb: (i, 0)),
                  pl.BlockSpec((1, d, de2), lambda i, be, nb: (be[i], 0, 0)),
                  pl.BlockSpec((1, 1, de2), lambda i, be, nb: (be[i], 0, 0)),
                  pl.BlockSpec((1, de2 // 2, d), lambda i, be, nb: (be[i], 0, 0)),
                  pl.BlockSpec((1, 1, d), lambda i, be, nb: (be[i], 0, 0))],
        out_specs=pl.BlockSpec((MOE_BLOCK, d), lambda i, be, nb: (i, 0)),
    )
    return pl.pallas_call(
        _moe_kernel,
        grid_spec=grid_spec,
        out_shape=jax.ShapeDtypeStruct((n_slots, d), F32),
        compiler_params=pltpu.CompilerParams(dimension_semantics=("arbitrary",),
                                             vmem_limit_bytes=VMEM_LIMIT),
        name="moe_experts",
    )(blk_e, n_used, x_slots, w1, b1[:, None, :], w2, b2[:, None, :])


DEINT_GROUP = 256


def _deinterleave_kernel(w_ref, p_ref, o_ref):
    half = o_ref.shape[1] // 2
    g_out = DEINT_GROUP // 2
    for g in range(w_ref.shape[1] // DEINT_GROUP):
        t = jnp.dot(w_ref[:, g * DEINT_GROUP:(g + 1) * DEINT_GROUP].astype(BF16), p_ref[...],
                    preferred_element_type=F32).astype(BF16)
        o_ref[:, g * g_out:(g + 1) * g_out] = t[:, :g_out]
        o_ref[:, half + g * g_out:half + (g + 1) * g_out] = t[:, g_out:]


def _deinterleave_columns(w):
    m, n = w.shape
    tm = 512
    perm = np.zeros((DEINT_GROUP, DEINT_GROUP), np.float32)
    idx = np.arange(DEINT_GROUP // 2)
    perm[2 * idx, idx] = 1.0
    perm[2 * idx + 1, DEINT_GROUP // 2 + idx] = 1.0
    return pl.pallas_call(
        _deinterleave_kernel,
        grid=(m // tm,),
        in_specs=[pl.BlockSpec((tm, n), lambda i: (i, 0)),
                  pl.BlockSpec((DEINT_GROUP, DEINT_GROUP), lambda i: (0, 0))],
        out_specs=pl.BlockSpec((tm, n), lambda i: (i, 0)),
        out_shape=jax.ShapeDtypeStruct((m, n), BF16),
        name="deinterleave_columns",
    )(w, jnp.asarray(perm, BF16))


ROUTE_TILE = 1024


def _route_kernel(x_ref, sh_ref, sc_ref, wt_ref, b_ref, before_ref, h_ref, e_ref, g_ref, r_ref, cnt_ref,
                  carry_ref):
    i = pl.program_id(0)

    @pl.when(i == 0)
    def _():
        carry_ref[...] = jnp.zeros_like(carry_ref)

    x = x_ref[...]
    ms = jnp.mean(x * x, axis=-1, keepdims=True)
    h = x * lax.rsqrt(ms + RMS_EPS) * (1.0 + sc_ref[0]) + sh_ref[0]
    h_ref[...] = h.astype(BF16)
    logits = _mm(wt_ref[...], h, "nt", 6) + b_ref[...]
    n_e, tm = logits.shape
    eidx = lax.broadcasted_iota(jnp.int32, (n_e, tm), 0)
    work = logits
    top_v, top_e = [], []
    for _ in range(TOP_K):
        m = jnp.max(work, axis=0, keepdims=True)
        sel = jnp.min(jnp.where(work == m, eidx, n_e), axis=0, keepdims=True)
        top_v.append(m)
        top_e.append(sel)
        work = jnp.where(eidx == sel, -jnp.inf, work)
    ex = [jnp.exp(v - top_v[0]) for v in top_v]
    den = ex[0] + ex[1] + ex[2] + ex[3]
    g_ref[...] = jnp.concatenate([e_ / den for e_ in ex], axis=0)
    e_ref[...] = jnp.concatenate(top_e, axis=0)
    chosen = [eidx == sel for sel in top_e]
    ind = sum(c.astype(F32) for c in chosen)
    carry = carry_ref[...]
    cnt = (jnp.dot(ind.astype(BF16), before_ref[...], preferred_element_type=F32)
           + jnp.concatenate([carry] * (tm // carry.shape[1]), axis=1))
    r_ref[...] = jnp.concatenate(
        [jnp.sum(jnp.where(c, cnt, 0.0), axis=0, keepdims=True) for c in chosen], axis=0).astype(jnp.int32)
    carry = carry + jnp.sum(ind, axis=1, keepdims=True)
    carry_ref[...] = carry
    cnt_ref[...] = carry


def _route(tok, shift, scale, router_w, router_b, rows_per_mod):
    t, d = tok.shape
    n_e = router_w.shape[1]
    tm = ROUTE_TILE
    tiles_per_mod = rows_per_mod // tm
    mod_spec = pl.BlockSpec((1, 1, d), lambda i: (i // tiles_per_mod, 0, 0))
    before = jnp.asarray(np.triu(np.ones((tm, tm), np.float32), 1), BF16)
    kt_spec = pl.BlockSpec((TOP_K, tm), lambda i: (0, i))
    h, top_e, gate, rank, cnt = pl.pallas_call(
        _route_kernel,
        grid=(t // tm,),
        in_specs=[pl.BlockSpec((tm, d), lambda i: (i, 0)), mod_spec, mod_spec,
                  pl.BlockSpec((n_e, d), lambda i: (0, 0)), pl.BlockSpec((n_e, tm), lambda i: (0, 0)),
                  pl.BlockSpec((tm, tm), lambda i: (0, 0))],
        out_specs=[pl.BlockSpec((tm, d), lambda i: (i, 0)), kt_spec, kt_spec, kt_spec,
                   pl.BlockSpec((n_e, 128), lambda i: (0, 0))],
        out_shape=[jax.ShapeDtypeStruct((t, d), BF16), jax.ShapeDtypeStruct((TOP_K, t), jnp.int32),
                   jax.ShapeDtypeStruct((TOP_K, t), F32), jax.ShapeDtypeStruct((TOP_K, t), jnp.int32),
                   jax.ShapeDtypeStruct((n_e, 128), F32)],
        scratch_shapes=[pltpu.VMEM((n_e, 128), F32)],
        compiler_params=pltpu.CompilerParams(dimension_semantics=("arbitrary",),
                                             vmem_limit_bytes=VMEM_LIMIT),
        name="moe_route",
    )(tok, shift[:, None, :], scale[:, None, :], router_w.T, jnp.broadcast_to(router_b[:, None], (n_e, tm)),
      before)
    return h, top_e, gate, rank, cnt[:, 0].astype(jnp.int32)


def _moe(tok, shift, scale, rows_per_mod, router_w, router_b, w1, b1, w2, b2):
    t, d = tok.shape
    h, top_e, gate, rank, counts = _route(tok, shift, scale, router_w, router_b, rows_per_mod)
    n_assign = t * TOP_K
    n_blocks = -(-n_assign // MOE_BLOCK) + N_EXPERTS
    n_slots = n_blocks * MOE_BLOCK
    padded = (counts + MOE_BLOCK - 1) // MOE_BLOCK * MOE_BLOCK
    start = jnp.cumsum(counts) - counts
    p_end = jnp.cumsum(padded)
    p_start = p_end - padded
    dest = p_start[top_e] + rank
    blk_first = jnp.arange(n_blocks, dtype=jnp.int32) * MOE_BLOCK
    blk_e = jnp.minimum(jnp.sum(p_end[None, :] <= blk_first[:, None], axis=1), N_EXPERTS - 1).astype(jnp.int32)
    n_used = (p_end[-1] // MOE_BLOCK).astype(jnp.int32).reshape(1)
    order = jnp.argsort(top_e.T.reshape(-1))
    slot_e = jnp.repeat(blk_e, MOE_BLOCK)
    j = jnp.arange(n_slots, dtype=jnp.int32) - p_start[slot_e]
    src = jnp.clip(start[slot_e] + j, 0, n_assign - 1)
    slot_tok = jnp.where(j < counts[slot_e], order[src] // TOP_K, t).astype(jnp.int32)
    x_slots = jnp.concatenate([h, jnp.zeros((1, d), BF16)], axis=0)[slot_tok]
    w1p = _deinterleave_columns(w1.reshape(N_EXPERTS * d, -1)).reshape(w1.shape)
    b1p = jnp.concatenate([b1[..., ::2], b1[..., 1::2]], axis=-1)
    y = _moe_experts(x_slots, blk_e, n_used, w1p, b1p, w2.astype(BF16), b2)
    return jnp.sum(y[dest] * gate[:, :, None], axis=0)


def _rms(x):
    return x * lax.rsqrt(jnp.mean(x * x, axis=-1, keepdims=True) + RMS_EPS)


def _conv3(u, w):
    up = jnp.pad(u, ((0, 0), (1, 1), (0, 0)))
    return up[:, :-2] * w[0] + up[:, 1:-1] * w[1] + up[:, 2:] * w[2]


def _rwkv7_streams(u, shift_w, w0, w2, a0, a2, g2, k_k, k_a):
    bsz, length, _ = u.shape
    d_a = w0.shape[-1]
    n_heads = d_a // HEAD_DIM
    u = _conv3(u, shift_w)
    splits = [d_a, 2 * d_a, 3 * d_a, 3 * d_a + 2 * DECAY_LORA, 3 * d_a + 2 * DECAY_LORA + 2 * AAA_LORA]
    r, k, v, wd, ad, gd = jnp.split(u, splits, axis=-1)
    lora_w = jnp.tanh(wd.reshape(bsz, length, 2, DECAY_LORA))
    w_log = -jax.nn.softplus(-(w0 + jnp.einsum('bldr,drc->bldc', lora_w, w2, precision=_HI))) - 0.5
    log_decay = -jnp.exp(w_log)
    a = jax.nn.sigmoid(a0 + jnp.einsum('bldr,drc->bldc', ad.reshape(bsz, length, 2, AAA_LORA), a2,
                                       precision=_HI))
    kk = (k * k_k).reshape(bsz, length, n_heads, HEAD_DIM)
    kk = kk / jnp.maximum(jnp.linalg.norm(kk, axis=-1, keepdims=True), 1e-12)
    kk = kk.reshape(bsz, length, d_a)
    k_dir = k[:, :, None, :] * (1.0 + (a - 1.0) * k_a)
    al = kk[:, :, None, :] * a
    g = jnp.dot(jax.nn.sigmoid(gd), g2, precision=_HI)
    to_dir = lambda t: jnp.moveaxis(t, 2, 0)
    return r, v, kk, to_dir(log_decay), to_dir(k_dir), to_dir(al), g


def _rwkv7_readout(y, r, v, k_dir, g, r_k, ln_w, ln_b):
    bsz, length, d_a = y.shape
    n_heads = d_a // HEAD_DIM
    heads = lambda t: t.reshape(t.shape[:-1] + (n_heads, HEAD_DIM))
    yh = heads(y)
    mu = jnp.mean(yh, axis=-1, keepdims=True)
    var = jnp.mean(jnp.square(yh - mu), axis=-1, keepdims=True)
    yn = ((yh - mu) * lax.rsqrt(var + LN_X_EPS)).reshape(bsz, length, d_a) * ln_w + ln_b
    rk = jnp.sum(heads(r)[None] * heads(k_dir) * r_k, axis=-1, keepdims=True)
    bonus = jnp.sum(rk * heads(v)[None], axis=0).reshape(bsz, length, d_a)
    return (yn + bonus) * g


def _hgrn2_streams(u, lb):
    q, f_fwd, f_bwd, i, og = jnp.split(u, 5, axis=-1)
    ks, lfs = [], []
    for f in (f_fwd, f_bwd):
        fg = lb + (1.0 - lb) * jax.nn.sigmoid(f)
        ks.append(1.0 - fg)
        lfs.append(jnp.log(fg))
    return jax.nn.silu(q), i, og, jnp.stack(lfs), jnp.stack(ks)


def _hgrn2_readout(o, og, norm_w):
    bsz, length, d_b = o.shape
    oh = o.reshape(bsz, length, d_b // HEAD_DIM, HEAD_DIM)
    return (_rms(oh) * norm_w).reshape(bsz, length, d_b) * jax.nn.silu(og)


def _even_mixer(u_lat, u_ctx, shift_w, w0, w2, a0, a2, g2, k_k, k_a, r_k, ln_w, ln_b, lb, norm_w):
    n_ctx = u_ctx.shape[1]
    a_cols = 3 * w0.shape[-1] + 2 * DECAY_LORA + 2 * AAA_LORA + GATE_LORA
    cat = lambda tc, tl: jnp.concatenate([tc, tl], axis=-2)
    sa_l = _rwkv7_streams(u_lat[..., :a_cols], shift_w, w0, w2, a0, a2, g2, k_k, k_a)
    sa_c = _rwkv7_streams(u_ctx[..., :a_cols], shift_w, w0, w2, a0, a2, g2, k_k, k_a)
    r, v, kap, lw, kd, al, g = (cat(c_, l_) for c_, l_ in zip(sa_c, sa_l))
    ya = _chunk_recurrence([r, v, kap], [lw, kd, al], n_ctx, delta=True)
    ya = _rwkv7_readout(ya[0] + ya[1], r, v, kd, g, r_k, ln_w, ln_b)
    sb_l = _hgrn2_streams(u_lat[..., a_cols:], lb)
    sb_c = _hgrn2_streams(u_ctx[..., a_cols:], lb)
    q, i, og, lf, kb = (cat(c_, l_) for c_, l_ in zip(sb_c, sb_l))
    yb = _chunk_recurrence([q, i], [lf, kb], n_ctx, delta=False)
    yb = _hgrn2_readout(yb[0] + yb[1], og, norm_w)
    y = jnp.concatenate([ya, yb], axis=-1)
    return y[:, n_ctx:], y[:, :n_ctx]


def _axial_rope(x):
    bsz, length, n_heads, dh = x.shape
    t = jnp.arange(length)
    pos = jnp.stack([t // GRID_W, t % GRID_W], axis=-1).astype(F32)
    nf = dh // 4
    inv = ROPE_THETA ** (-jnp.arange(nf, dtype=F32) / nf)
    ang = pos[:, None, :, None] * inv
    cos, sin = jnp.cos(ang), jnp.sin(ang)
    xr = x.reshape(bsz, length, n_heads, 2, 2, nf)
    x1, x2 = xr[..., 0, :], xr[..., 1, :]
    out = jnp.stack([x1 * cos - x2 * sin, x1 * sin + x2 * cos], axis=-2)
    return out.reshape(bsz, length, n_heads, dh)


def _hyena_filters(length, d_d, w1, b1, fr1, w2, b2, fr2, w3):
    t = jnp.linspace(0.0, 1.0, length, dtype=F32)[:, None]
    bands = (HYENA_EMB - 1) // 2
    f = jnp.linspace(1e-4, bands - 1, bands, dtype=F32)
    ang = (2.0 * math.pi / length) * jnp.arange(length, dtype=F32)[:, None] * f
    z = jnp.concatenate([t, jnp.cos(ang), -jnp.sin(ang)], axis=-1)
    hid = jnp.sin(fr1 * (jnp.dot(z, w1, precision=_HI) + b1))
    hid = jnp.sin(fr2 * (jnp.dot(hid, w2, precision=_HI) + b2))
    h = jnp.dot(hid, w3, precision=_HI).reshape(length, HYENA_ORDER, 2, d_d)
    deltas = jnp.abs(jnp.linspace(math.log(HYENA_TARGET) / HYENA_SLOW_DECAY,
                                  math.log(HYENA_TARGET) / HYENA_FAST_DECAY, d_d, dtype=F32))
    return h * jnp.exp(-t * deltas)[:, None, None, :]


DFT_R = 128
DFT_COL_TILE = 4096
DFT_K1_TILE = 4


def _hi_lo(a):
    hi = a.astype(BF16)
    return hi, (a - hi.astype(F32)).astype(BF16)


def _mm3(f_hi, f_lo, x_hi, x_lo):
    d = lambda p, q: jnp.dot(p, q, preferred_element_type=F32)
    return d(f_hi, x_hi) + (d(f_hi, x_lo) + d(f_lo, x_hi))


def _dft_matrices(n1):
    k1 = np.arange(n1, dtype=np.float64)
    k2 = np.arange(DFT_R, dtype=np.float64)
    as_f32 = lambda a: jnp.asarray(a, F32)
    cs = lambda ang: (as_f32(np.cos(ang)), as_f32(-np.sin(ang)))
    f1 = cs(2.0 * np.pi * np.outer(k1, k1) / n1)
    f2 = cs(2.0 * np.pi * np.outer(k2, k2) / DFT_R)
    tw = cs(2.0 * np.pi * np.outer(k1, k2) / (n1 * DFT_R))
    tw = tuple(jnp.broadcast_to(t[:, :, None], (n1, DFT_R, 128)) for t in tw)
    return f1, f2, tw


def _dft_rows_kernel(x_ref, frh_ref, frl_ref, fih_ref, fil_ref, or_ref, oi_ref):
    x_hi, x_lo = _hi_lo(x_ref[0])
    or_ref[0] = _mm3(frh_ref[...], frl_ref[...], x_hi, x_lo)
    oi_ref[0] = _mm3(fih_ref[...], fil_ref[...], x_hi, x_lo)


def _dft_rows(x2d, f1):
    bsz, rows, cols = x2d.shape
    n1 = f1[0].shape[0]
    mats = [m for f in f1 for m in _hi_lo(f[:, :rows])]
    mat_spec = pl.BlockSpec((n1, rows), lambda b, j: (0, 0))
    out_spec = pl.BlockSpec((1, n1, DFT_COL_TILE), lambda b, j: (b, 0, j))
    out = jax.ShapeDtypeStruct((bsz, n1, cols), F32)
    return pl.pallas_call(
        _dft_rows_kernel,
        grid=(bsz, cols // DFT_COL_TILE),
        in_specs=[pl.BlockSpec((1, rows, DFT_COL_TILE), lambda b, j: (b, 0, j))] + [mat_spec] * 4,
        out_specs=[out_spec, out_spec],
        out_shape=[out, out],
        name="dft_rows",
    )(x2d, *mats)


def _dft_mid_kernel(*refs, conv):
    if conv:
        (ar_ref, ai_ref, tr_ref, ti_ref, kr_ref, ki_ref, frh_ref, frl_ref, fih_ref, fil_ref,
         or_ref, oi_ref) = refs
    else:
        ar_ref, ai_ref, tr_ref, ti_ref, frh_ref, frl_ref, fih_ref, fil_ref, or_ref, oi_ref = refs
    frh, frl, fih, fil = frh_ref[...], frl_ref[...], fih_ref[...], fil_ref[...]
    reps = ar_ref.shape[3] // tr_ref.shape[2]
    rows = range(ar_ref.shape[1])
    wide = lambda t: jnp.concatenate([t] * reps, axis=-1)
    tr = [wide(tr_ref[j]) for j in rows]
    ti = [wide(ti_ref[j]) for j in rows]
    ar = [ar_ref[0, j] for j in rows]
    ai = [ai_ref[0, j] for j in rows]
    pr = [_hi_lo(ar[j] * tr[j] - ai[j] * ti[j]) for j in rows]
    pi = [_hi_lo(ar[j] * ti[j] + ai[j] * tr[j]) for j in rows]
    xr = [_mm3(frh, frl, *pr[j]) - _mm3(fih, fil, *pi[j]) for j in rows]
    xi = [_mm3(frh, frl, *pi[j]) + _mm3(fih, fil, *pr[j]) for j in rows]
    if not conv:
        for j in rows:
            or_ref[0, j] = xr[j]
            oi_ref[0, j] = xi[j]
        return
    yr = [_hi_lo(xr[j] * kr_ref[0, j] - xi[j] * ki_ref[0, j]) for j in rows]
    yi = [_hi_lo(xr[j] * ki_ref[0, j] + xi[j] * kr_ref[0, j]) for j in rows]
    br = [_mm3(frh, frl, *yr[j]) + _mm3(fih, fil, *yi[j]) for j in rows]
    bi = [_mm3(frh, frl, *yi[j]) - _mm3(fih, fil, *yr[j]) for j in rows]
    for j in rows:
        or_ref[0, j] = br[j] * tr[j] + bi[j] * ti[j]
        oi_ref[0, j] = bi[j] * tr[j] - br[j] * ti[j]


def _dft_mid(a_r, a_i, f2, tw, kf=None):
    (f_r, f_i), (t_r, t_i) = f2, tw
    bsz, n1, _, ch = a_r.shape
    slab = pl.BlockSpec((1, DFT_K1_TILE, DFT_R, ch), lambda b, g: (b, g, 0, 0))
    tw = pl.BlockSpec((DFT_K1_TILE, DFT_R, t_r.shape[2]), lambda b, g: (g, 0, 0))
    mat = pl.BlockSpec((DFT_R, DFT_R), lambda b, g: (0, 0))
    args = [a_r, a_i, t_r, t_i]
    specs = [slab, slab, tw, tw]
    if kf is not None:
        kf_r, kf_i, o = kf
        kspec = pl.BlockSpec((1, DFT_K1_TILE, DFT_R, ch), lambda b, g: (o, g, 0, 0))
        args += [kf_r, kf_i]
        specs += [kspec, kspec]
    args += [m for f in (f_r, f_i) for m in _hi_lo(f)]
    specs += [mat] * 4
    out = jax.ShapeDtypeStruct(a_r.shape, F32)
    return pl.pallas_call(
        functools.partial(_dft_mid_kernel, conv=kf is not None),
        grid=(bsz, n1 // DFT_K1_TILE),
        in_specs=specs,
        out_specs=[slab, slab],
        out_shape=[out, out],
        compiler_params=pltpu.CompilerParams(dimension_semantics=("arbitrary", "arbitrary"),
                                             vmem_limit_bytes=VMEM_LIMIT),
        name="dft_mid_conv" if kf is not None else "dft_mid",
    )(*args)


def _idft_rows_kernel(br_ref, bi_ref, z_ref, g_ref, bias_ref, grh_ref, grl_ref, gih_ref, gil_ref, o_ref):
    y = (_mm3(grh_ref[...], grl_ref[...], *_hi_lo(br_ref[0]))
         + _mm3(gih_ref[...], gil_ref[...], *_hi_lo(bi_ref[0])))
    o_ref[0] = g_ref[0] * (y + z_ref[0] * bias_ref[...])


def _idft_rows(b_r, b_i, z2d, gate2d, bias_cols, f1):
    bsz, rows, cols = z2d.shape
    n1 = f1[0].shape[0]
    scale = 1.0 / (n1 * DFT_R)
    mats = [m for f in f1 for m in _hi_lo(f[:rows] * scale)]
    mat_spec = pl.BlockSpec((rows, n1), lambda b, j: (0, 0))
    in_spec = pl.BlockSpec((1, n1, DFT_COL_TILE), lambda b, j: (b, 0, j))
    io_spec = pl.BlockSpec((1, rows, DFT_COL_TILE), lambda b, j: (b, 0, j))
    return pl.pallas_call(
        _idft_rows_kernel,
        grid=(bsz, cols // DFT_COL_TILE),
        in_specs=[in_spec, in_spec, io_spec, io_spec, pl.BlockSpec((1, DFT_COL_TILE), lambda b, j: (0, j))]
        + [mat_spec] * 4,
        out_specs=io_spec,
        out_shape=jax.ShapeDtypeStruct((bsz, rows, cols), F32),
        name="idft_rows",
    )(b_r, b_i, z2d, gate2d, bias_cols, *mats)


def _hyena(u, short_w, w1, b1, fr1, w2, b2, fr2, w3, bias):
    bsz, length, _ = u.shape
    u = _conv3(u, short_w)
    v, x1, x2 = jnp.split(u, 3, axis=-1)
    ch = v.shape[-1]
    rows = length // DFT_R
    h = _hyena_filters(length, ch, w1, b1, fr1, w2, b2, fr2, w3)
    h_fwd, h_bwd = h[:, :, 0], h[:, :, 1]
    k2 = jnp.concatenate([h_fwd.at[0].add(h_bwd[0]), jnp.zeros_like(h_fwd[:1]), h_bwd[:0:-1]], axis=0)
    k2 = k2.transpose(1, 0, 2).reshape(HYENA_ORDER, 2 * rows, DFT_R * ch)
    n1 = 2 * rows
    f1, f2, tw = _dft_matrices(n1)
    four_d = lambda t: t.reshape(t.shape[0], n1, DFT_R, ch)
    kf_r, kf_i = _dft_mid(*(four_d(t) for t in _dft_rows(k2, f1)), f2, tw)
    z = v
    for o, gate in enumerate((x1, x2)):
        z2d = z.reshape(bsz, rows, DFT_R * ch)
        a_r, a_i = _dft_rows(z2d, f1)
        b_r, b_i = _dft_mid(four_d(a_r), four_d(a_i), f2, tw, kf=(kf_r, kf_i, o))
        flat = lambda t: t.reshape(bsz, n1, DFT_R * ch)
        y = _idft_rows(flat(b_r), flat(b_i), z2d, gate.reshape(z2d.shape), jnp.tile(bias[o], DFT_R)[None, :], f1)
        z = y.reshape(bsz, length, ch)
    return z


def _odd_mixer_latent(u_lat, u_ctx, q_norm, k_norm, rpb, short_w, w1, b1, fr1, w2, b2, fr2, w3, bias):
    n_heads = rpb.shape[0]
    d_c = n_heads * HEAD_DIM
    bsz, s, _ = u_lat.shape

    def qkv(t):
        length = t.shape[1]
        q, k, v = jnp.split(t[..., :3 * d_c], 3, axis=-1)
        heads = lambda a_: a_.reshape(bsz, length, n_heads, HEAD_DIM)
        return _rms(heads(q)) * q_norm, _rms(heads(k)) * k_norm, heads(v)

    q, k, v = qkv(u_lat)
    _, kc, vc = qkv(u_ctx)
    hm = lambda t: t.transpose(0, 2, 1, 3)
    o = _neighbourhood_attention(hm(_axial_rope(q) * HEAD_DIM ** -0.5), hm(_axial_rope(k)), hm(v),
                                 hm(kc), hm(vc), rpb)
    y_na = o.transpose(0, 2, 1, 3).reshape(bsz, s, d_c)
    y_hy = _hyena(u_lat[..., 3 * d_c:], short_w, w1, b1, fr1, w2, b2, fr2, w3, bias)
    return jnp.concatenate([y_na, y_hy], axis=-1)


def kernel(x, c, ctx, c_ctx, mod_w, mod_b, router_w, router_b, moe_w1, moe_b1, moe_w2, moe_b2, ab_w_in, ab_w_out, rwkv_shift, rwkv_w0, rwkv_w2, rwkv_a0, rwkv_a2, rwkv_g2, rwkv_k_k, rwkv_k_a, rwkv_r_k, rwkv_ln_w, rwkv_ln_b, hgrn_lb_logits, hgrn_norm_w, cd_w_in, cd_w_out, na_q_norm, na_k_norm, na_rpb, hy_short, hy_w1, hy_b1, hy_freq1, hy_w2, hy_b2, hy_freq2, hy_w3, hy_bias):
    bsz, s, d = x.shape
    n_ctx = ctx.shape[1]
    depth = mod_w.shape[0]
    n_lat = bsz * s
    lb_all = jnp.cumsum(jax.nn.softmax(hgrn_lb_logits, axis=0), axis=0)
    tok = jnp.concatenate([x.reshape(n_lat, d), ctx.reshape(bsz * n_ctx, d)], axis=0)
    cond = jnp.concatenate([jax.nn.silu(c), jax.nn.silu(c_ctx)[None, :],
                            jnp.zeros((8 - bsz - 1, d), F32)], axis=0)
    for l in range(depth):
        last = l == depth - 1
        j = l // 2
        mod = _small_linear(cond, mod_w[l], mod_b[l])[:bsz + 1]
        sh1, sc1, g1, sh2, sc2, g2 = jnp.split(mod, 6, axis=-1)
        w_in = ab_w_in[j] if l % 2 == 0 else cd_w_in[j]
        u = _mod_linear(tok, sh1, sc1, w_in, s)
        u_lat = u[:n_lat].reshape(bsz, s, -1)
        u_ctx = u[n_lat:].reshape(bsz, n_ctx, -1)
        if l % 2 == 0:
            y_lat, y_ctx = _even_mixer(u_lat, u_ctx, rwkv_shift[j], rwkv_w0[j], rwkv_w2[j], rwkv_a0[j],
                                       rwkv_a2[j], rwkv_g2[j], rwkv_k_k[j], rwkv_k_a[j], rwkv_r_k[j],
                                       rwkv_ln_w[j], rwkv_ln_b[j], lb_all[j], hgrn_norm_w[j])
            w_out = ab_w_out[j]
        else:
            y_lat = _odd_mixer_latent(u_lat, u_ctx, na_q_norm[j], na_k_norm[j], na_rpb[j], hy_short[j],
                                      hy_w1[j], hy_b1[j], hy_freq1[j], hy_w2[j], hy_b2[j], hy_freq2[j],
                                      hy_w3[j], hy_bias[j])
            y_ctx = jnp.zeros((bsz, n_ctx, d), F32)
            w_out = cd_w_out[j]
        if last:
            tok = tok[:n_lat]
            y = y_lat.reshape(n_lat, d)
        else:
            y = jnp.concatenate([y_lat.reshape(n_lat, d), y_ctx.reshape(bsz * n_ctx, d)], axis=0)
        tok = _res_linear(y, w_out, tok, g1, s)
        out = _moe(tok, sh2, sc2, s, router_w[l], router_b[l], moe_w1[l], moe_b1[l], moe_w2[l], moe_b2[l])
        tok = _gated_add(tok, out, g2, s)
    return tok[:n_lat].reshape(bsz, s, d)


def _gated_add_kernel(x_ref, y_ref, g_ref, o_ref):
    o_ref[...] = x_ref[...] + g_ref[0] * y_ref[...]


def _gated_add(x, y, gate, rows_per_mod):
    m, d = x.shape
    tiles_per_mod = rows_per_mod // ROW_TILE
    tile = pl.BlockSpec((ROW_TILE, d), lambda i: (i, 0))
    return pl.pallas_call(
        _gated_add_kernel,
        grid=(m // ROW_TILE,),
        in_specs=[tile, tile, pl.BlockSpec((1, 1, d), lambda i: (i // tiles_per_mod, 0, 0))],
        out_specs=tile,
        out_shape=jax.ShapeDtypeStruct((m, d), F32),
        name="gated_add",
    )(x, y, gate[:, None, :])
```

```python
import functools
import math

import numpy as np
import jax
import jax.numpy as jnp
from jax import lax
from jax.experimental import pallas as pl
from jax.experimental.pallas import tpu as pltpu

F32 = jnp.float32
BF16 = jnp.bfloat16

HEAD_DIM = 64
GRID_W = 64
DECAY_LORA = 64
AAA_LORA = 64
GATE_LORA = 128
LN_X_EPS = 1e-5 * HEAD_DIM
NA_ROWS = 8
NA_COLS = 16
ROPE_THETA = 10000.0
HYENA_ORDER = 2
HYENA_EMB = 33
HYENA_FAST_DECAY = 0.3
HYENA_SLOW_DECAY = 1.5
HYENA_TARGET = 1e-2
N_EXPERTS = 32
TOP_K = 4
SWIGLU_ALPHA = 1.702
SWIGLU_LIMIT = 7.0
MOE_BLOCK = 256
RMS_EPS = 1e-6

ROW_TILE = 256
REC_CHUNK = 64
REC_PASSES = 1
REC_CHAIN_PASSES = 3
VMEM_LIMIT = 56 * 1024 * 1024

_HI = lax.Precision.HIGHEST


_DIMS = {"nn": (((1,), (0,)), ((), ())), "nt": (((1,), (1,)), ((), ())), "tn": (((0,), (0,)), ((), ()))}


def _mm(a, b, form, passes):
    dims = _DIMS[form]
    if passes == 6:
        return lax.dot_general(a, b, dims, precision=_HI, preferred_element_type=F32)
    a_hi = a.astype(BF16)
    b_hi = b.astype(BF16)
    out = lax.dot_general(a_hi, b_hi, dims, preferred_element_type=F32)
    if passes == 3:
        a_lo = (a - a_hi.astype(F32)).astype(BF16)
        b_lo = (b - b_hi.astype(F32)).astype(BF16)
        out = out + (lax.dot_general(a_hi, b_lo, dims, preferred_element_type=F32)
                     + lax.dot_general(a_lo, b_hi, dims, preferred_element_type=F32))
    return out


def _dot(a, b):
    return _mm(a, b, "nn", 6)


def _mod_linear_kernel(x_ref, sh_ref, sc_ref, w_ref, o_ref):
    x = x_ref[...]
    ms = jnp.mean(x * x, axis=-1, keepdims=True)
    h = x * lax.rsqrt(ms + RMS_EPS) * (1.0 + sc_ref[0]) + sh_ref[0]
    o_ref[...] = jnp.dot(h.astype(BF16), w_ref[...], preferred_element_type=F32)


def _mod_linear(x, shift, scale, w, rows_per_mod):
    m, d = x.shape
    n = w.shape[1]
    tiles_per_mod = rows_per_mod // ROW_TILE
    mod_spec = pl.BlockSpec((1, 1, d), lambda i: (i // tiles_per_mod, 0, 0))
    return pl.pallas_call(
        _mod_linear_kernel,
        grid=(m // ROW_TILE,),
        in_specs=[pl.BlockSpec((ROW_TILE, d), lambda i: (i, 0)), mod_spec, mod_spec,
                  pl.BlockSpec((d, n), lambda i: (0, 0))],
        out_specs=pl.BlockSpec((ROW_TILE, n), lambda i: (i, 0)),
        out_shape=jax.ShapeDtypeStruct((m, n), F32),
        compiler_params=pltpu.CompilerParams(dimension_semantics=("arbitrary",),
                                             vmem_limit_bytes=VMEM_LIMIT),
        name="mod_linear",
    )(x, shift[:, None, :], scale[:, None, :], w.astype(BF16))


def _res_linear_kernel(y_ref, w_ref, res_ref, g_ref, o_ref):
    o_ref[...] = res_ref[...] + g_ref[0] * jnp.dot(y_ref[...].astype(BF16), w_ref[...],
                                                   preferred_element_type=F32)


def _res_linear(y, w, res, gate, rows_per_mod):
    m, k = y.shape
    n = w.shape[1]
    tiles_per_mod = rows_per_mod // ROW_TILE
    return pl.pallas_call(
        _res_linear_kernel,
        grid=(m // ROW_TILE,),
        in_specs=[pl.BlockSpec((ROW_TILE, k), lambda i: (i, 0)),
                  pl.BlockSpec((k, n), lambda i: (0, 0)),
                  pl.BlockSpec((ROW_TILE, n), lambda i: (i, 0)),
                  pl.BlockSpec((1, 1, n), lambda i: (i // tiles_per_mod, 0, 0))],
        out_specs=pl.BlockSpec((ROW_TILE, n), lambda i: (i, 0)),
        out_shape=jax.ShapeDtypeStruct((m, n), F32),
        compiler_params=pltpu.CompilerParams(dimension_semantics=("arbitrary",),
                                             vmem_limit_bytes=VMEM_LIMIT),
        name="res_linear",
    )(y, w.astype(BF16), res, gate[:, None, :])


def _small_linear_kernel(x_ref, w_ref, b_ref, o_ref):
    o_ref[...] = _dot(x_ref[...], w_ref[...]) + b_ref[...]


def _small_linear(x, w, b):
    m, k = x.shape
    n = w.shape[1]
    tn = 1024 if n % 1024 == 0 else n
    return pl.pallas_call(
        _small_linear_kernel,
        grid=(n // tn,),
        in_specs=[pl.BlockSpec((m, k), lambda j: (0, 0)), pl.BlockSpec((k, tn), lambda j: (0, j)),
                  pl.BlockSpec((1, tn), lambda j: (0, j))],
        out_specs=pl.BlockSpec((m, tn), lambda j: (0, j)),
        out_shape=jax.ShapeDtypeStruct((m, n), F32),
        name="small_linear",
    )(x, w, b[None, :])


def _rec_kernel(*refs, delta, n_heads):
    if delta:
        r_ref, v_ref, kap_ref, lw_ref, k_ref, al_ref, y_ref, s_ref = refs
        r_all, v_all, lw, k = r_ref[...], v_ref[...], lw_ref[0], k_ref[0]
    else:
        q_ref, i_ref, f_ref, lb_ref, y_ref, s_ref = refs
        fg = lb_ref[...] + (1.0 - lb_ref[...]) * jax.nn.sigmoid(f_ref[...])
        r_all, v_all, lw, k = jax.nn.silu(q_ref[...]), i_ref[...], jnp.log(fg), 1.0 - fg
    c = r_all.shape[0]
    d = pl.program_id(0)
    i = pl.program_id(2)

    @pl.when(i == 0)
    def _():
        s_ref[...] = jnp.zeros_like(s_ref)

    fwd = d == 0
    row = lax.broadcasted_iota(jnp.int32, (c, c), 0)
    col = lax.broadcasted_iota(jnp.int32, (c, c), 1)
    ahead = jnp.where(fwd, row - col, col - row)
    incl = ahead >= 0
    strict = ahead > 0
    mid = c // 2
    heads = range(n_heads)
    hs = lambda t: [t[:, h * HEAD_DIM:(h + 1) * HEAD_DIM] for h in heads]
    mm = functools.partial(_mm, passes=REC_PASSES)

    b = _mm(incl.astype(F32), lw, "nn", 6)
    bm = b[mid:mid + 1, :]
    tot = jnp.where(fwd, b[c - 1:c, :], b[0:1, :])
    e_neg = jnp.exp(bm - b)
    e_end = jnp.exp(tot - b)
    v = hs(v_all)
    rq = hs(r_all * jnp.exp(b - bm))
    kd = hs(k * e_neg)
    k_end = hs(k * e_end)
    g_mid = hs(jnp.exp(bm))
    g_tot = hs(jnp.exp(tot))
    s0 = [s_ref[h] for h in heads]
    s0m = [s0[h] * g_mid[h] for h in heads]
    if delta:
        al = al_ref[0]
        kq = hs(kap_ref[...] * jnp.exp(b - lw - bm))
        ad = hs(al * e_neg)
        al_end = hs(al * e_end)
        q2 = [jnp.concatenate([kq[h], rq[h]], axis=0) for h in heads]
        k2 = [jnp.concatenate([kd[h], ad[h]], axis=0) for h in heads]
        a = [mm(q2[h], k2[h], "nt") for h in heads]
        p = [mm(q2[h], s0m[h], "nt") for h in heads]
        z = [-(p[h][:c] + mm(jnp.where(strict, a[h][:c, :c], 0.0), v[h], "nn")) for h in heads]
        mc = functools.partial(_mm, passes=REC_CHAIN_PASSES)
        m = [jnp.where(strict, a[h][:c, c:], 0.0) for h in heads]
        pair = (row >> 1) == (col >> 1)
        eye = (row == col).astype(F32)
        t = [eye - jnp.where(pair, m[h], 0.0) for h in heads]
        for lvl in range(1, int(math.log2(c))):
            off = ((row >> (lvl + 1)) == (col >> (lvl + 1))) & ((row >> lvl) != (col >> lvl))
            tc = [mc(t[h], jnp.where(off, m[h], 0.0), "nn") for h in heads]
            t = [t[h] - mc(tc[h], t[h], "nn") for h in heads]
        z = [mc(t[h], z[h], "nn") for h in heads]
        y = [p[h][c:] + mm(jnp.where(incl, a[h][c:, :c], 0.0), v[h], "nn")
             + mm(jnp.where(incl, a[h][c:, c:], 0.0), z[h], "nn") for h in heads]
        s_new = [s0[h] * g_tot[h] + mm(v[h], k_end[h], "tn") + mm(z[h], al_end[h], "tn") for h in heads]
    else:
        a = [mm(rq[h], kd[h], "nt") for h in heads]
        y = [mm(rq[h], s0m[h], "nt") + mm(jnp.where(incl, a[h], 0.0), v[h], "nn") for h in heads]
        s_new = [s0[h] * g_tot[h] + mm(v[h], k_end[h], "tn") for h in heads]
    for h in heads:
        s_ref[h] = s_new[h]
    y_ref[0] = jnp.concatenate(y, axis=-1)


def _seq_block_index(bsz, s, n_ctx, rows):
    n_c, n_l = n_ctx // rows, s // rows

    def index(d, b, i):
        back = jnp.where(i < n_c, n_c - 1 - i, n_l + 2 * n_c - 1 - i)
        pos = jnp.where(d == 0, i, back)
        return jnp.where(pos < n_c, bsz * n_l + b * n_c + pos, b * n_l + pos - n_c)

    return index


def _rec_call(args, specs, bsz, s, n_ctx, width, delta):
    c = REC_CHUNK
    n_heads = width // HEAD_DIM
    blk = _seq_block_index(bsz, s, n_ctx, c)
    n_rows = bsz * (s + n_ctx)
    return pl.pallas_call(
        functools.partial(_rec_kernel, delta=delta, n_heads=n_heads),
        grid=(2, bsz, (s + n_ctx) // c),
        in_specs=specs,
        out_specs=pl.BlockSpec((1, c, width), lambda d, b, i: (d, blk(d, b, i), 0)),
        out_shape=jax.ShapeDtypeStruct((2, n_rows, width), F32),
        scratch_shapes=[pltpu.VMEM((n_heads, HEAD_DIM, HEAD_DIM), F32)],
        compiler_params=pltpu.CompilerParams(
            dimension_semantics=("arbitrary", "arbitrary", "arbitrary")),
        name="rwkv7_rec" if delta else "hgrn2_rec",
    )(*args)


def _rwkv7_recurrence(r, v, kap, lw, k, al, bsz, s, n_ctx):
    width = r.shape[1]
    blk = _seq_block_index(bsz, s, n_ctx, REC_CHUNK)
    shared = pl.BlockSpec((REC_CHUNK, width), lambda d, b, i: (blk(d, b, i), 0))
    per_dir = pl.BlockSpec((1, REC_CHUNK, width), lambda d, b, i: (d, blk(d, b, i), 0))
    return _rec_call([r, v, kap, lw, k, al], [shared] * 3 + [per_dir] * 3, bsz, s, n_ctx, width, True)


def _hgrn2_recurrence(u, lb, col_q, col_f, col_i, bsz, s, n_ctx):
    width = lb.shape[0]
    blk = _seq_block_index(bsz, s, n_ctx, REC_CHUNK)
    col = lambda j: pl.BlockSpec((REC_CHUNK, width), lambda d, b, i: (blk(d, b, i), j))
    f_spec = pl.BlockSpec((REC_CHUNK, width), lambda d, b, i: (blk(d, b, i), col_f + d))
    lb_spec = pl.BlockSpec((1, width), lambda d, b, i: (0, 0))
    return _rec_call([u, u, u, lb[None, :]], [col(col_q), col(col_i), f_spec, lb_spec], bsz, s, n_ctx, width,
                     False)


def _na_kernel(q_ref, k_ref, v_ref, kc_ref, vc_ref, bias_ref, o_ref, *, rows_per_step, n_rows):
    g = pl.program_id(2)
    kc = kc_ref[0, 0]
    vc = vc_ref[0, 0]
    band = NA_ROWS * GRID_W
    for j in range(rows_per_step):
        qr = g * rows_per_step + j
        start = jnp.clip(qr - NA_ROWS // 2, 0, n_rows - NA_ROWS)
        off = pl.multiple_of(start * GRID_W, GRID_W)
        q = q_ref[0, 0, j * GRID_W:(j + 1) * GRID_W, :].astype(BF16)
        kw = k_ref[0, 0, pl.ds(off, band), :]
        vw = v_ref[0, 0, pl.ds(off, band), :]
        s_win = lax.dot_general(q, kw, (((1,), (1,)), ((), ())), preferred_element_type=F32)
        s_win = s_win + bias_ref[0, start - qr + NA_ROWS - 1]
        s_ctx = lax.dot_general(q, kc, (((1,), (1,)), ((), ())), preferred_element_type=F32)
        m = jnp.maximum(jnp.max(s_win, axis=-1, keepdims=True), jnp.max(s_ctx, axis=-1, keepdims=True))
        p_win = jnp.exp(s_win - m)
        p_ctx = jnp.exp(s_ctx - m)
        den = jnp.sum(p_win, axis=-1, keepdims=True) + jnp.sum(p_ctx, axis=-1, keepdims=True)
        o = (jnp.dot(p_win.astype(BF16), vw, preferred_element_type=F32)
             + jnp.dot(p_ctx.astype(BF16), vc, preferred_element_type=F32))
        o_ref[0, 0, j * GRID_W:(j + 1) * GRID_W, :] = o / den


def _na_bias_table(rpb):
    cc = np.arange(GRID_W)
    col_start = np.clip(cc - NA_COLS // 2, 0, GRID_W - NA_COLS)
    kc = np.arange(GRID_W)
    inside = (kc[None, :] >= col_start[:, None]) & (kc[None, :] < col_start[:, None] + NA_COLS)
    col_off = np.clip(kc[None, :] - cc[:, None] + NA_COLS - 1, 0, 2 * NA_COLS - 2)
    tab = rpb[:, :, col_off]
    tab = jnp.where(jnp.asarray(inside)[None, None], tab, -jnp.inf)
    d0 = np.arange(NA_ROWS)[:, None] + np.arange(NA_ROWS)[None, :]
    band = tab[:, d0]
    return band.transpose(0, 1, 3, 2, 4).reshape(rpb.shape[0], NA_ROWS, GRID_W, NA_ROWS * GRID_W)


def _neighbourhood_attention(q, k, v, kc, vc, rpb):
    bsz, n_heads, s, dh = q.shape
    n_rows = s // GRID_W
    rows_per_step = 8
    lc = kc.shape[2]
    bias = _na_bias_table(rpb)
    full = pl.BlockSpec((1, 1, s, dh), lambda b, h, g: (b, h, 0, 0))
    ctx = pl.BlockSpec((1, 1, lc, dh), lambda b, h, g: (b, h, 0, 0))
    tile = pl.BlockSpec((1, 1, rows_per_step * GRID_W, dh), lambda b, h, g: (b, h, g, 0))
    return pl.pallas_call(
        functools.partial(_na_kernel, rows_per_step=rows_per_step, n_rows=n_rows),
        grid=(bsz, n_heads, n_rows // rows_per_step),
        in_specs=[tile, full, full, ctx, ctx,
                  pl.BlockSpec((1, NA_ROWS, GRID_W, NA_ROWS * GRID_W), lambda b, h, g: (h, 0, 0, 0))],
        out_specs=tile,
        out_shape=jax.ShapeDtypeStruct((bsz, n_heads, s, dh), F32),
        compiler_params=pltpu.CompilerParams(
            dimension_semantics=("arbitrary", "arbitrary", "arbitrary")),
        name="neighbourhood_attention",
    )(q, k.astype(BF16), v.astype(BF16), kc.astype(BF16), vc.astype(BF16), bias)


def _moe_kernel(be_ref, nb_ref, x_ref, w1_ref, b1_ref, w2_ref, b2_ref, o_ref):
    i = pl.program_id(0)

    @pl.when(i < nb_ref[0])
    def _():
        y = jnp.dot(x_ref[...], w1_ref[0], preferred_element_type=F32) + b1_ref[0]
        half = y.shape[1] // 2
        glu = jnp.minimum(y[:, :half], SWIGLU_LIMIT)
        lin = jnp.clip(y[:, half:], -SWIGLU_LIMIT, SWIGLU_LIMIT)
        act = glu * jax.nn.sigmoid(SWIGLU_ALPHA * glu) * (lin + 1.0)
        o_ref[...] = jnp.dot(act.astype(BF16), w2_ref[0], preferred_element_type=F32) + b2_ref[0]

    @pl.when(i >= nb_ref[0])
    def _():
        o_ref[...] = jnp.zeros_like(o_ref)


def _moe_experts(x_slots, blk_e, n_used, w1, b1, w2, b2):
    n_slots, d = x_slots.shape
    n_blocks = n_slots // MOE_BLOCK
    de2 = w1.shape[2]
    grid_spec = pltpu.PrefetchScalarGridSpec(
        num_scalar_prefetch=2,
        grid=(n_blocks,),
        in_specs=[pl.BlockSpec((MOE_BLOCK, d), lambda i, be, nb: (i, 0)),
                  pl.BlockSpec((1, d, de2), lambda i, be, nb: (be[i], 0, 0)),
                  pl.BlockSpec((1, 1, de2), lambda i, be, nb: (be[i], 0, 0)),
                  pl.BlockSpec((1, de2 // 2, d), lambda i, be, nb: (be[i], 0, 0)),
                  pl.BlockSpec((1, 1, d), lambda i, be, nb: (be[i], 0, 0))],
        out_specs=pl.BlockSpec((MOE_BLOCK, d), lambda i, be, nb: (i, 0)),
    )
    return pl.pallas_call(
        _moe_kernel,
        grid_spec=grid_spec,
        out_shape=jax.ShapeDtypeStruct((n_slots, d), F32),
        compiler_params=pltpu.CompilerParams(dimension_semantics=("arbitrary",),
                                             vmem_limit_bytes=VMEM_LIMIT),
        name="moe_experts",
    )(blk_e, n_used, x_slots, w1, b1[:, None, :], w2, b2[:, None, :])


DEINT_GROUP = 256


def _deinterleave_kernel(w_ref, p_ref, o_ref):
    half = o_ref.shape[1] // 2
    g_out = DEINT_GROUP // 2
    for g in range(w_ref.shape[1] // DEINT_GROUP):
        t = jnp.dot(w_ref[:, g * DEINT_GROUP:(g + 1) * DEINT_GROUP].astype(BF16), p_ref[...],
                    preferred_element_type=F32).astype(BF16)
        o_ref[:, g * g_out:(g + 1) * g_out] = t[:, :g_out]
        o_ref[:, half + g * g_out:half + (g + 1) * g_out] = t[:, g_out:]


def _deinterleave_columns(w):
    m, n = w.shape
    tm = 512
    perm = np.zeros((DEINT_GROUP, DEINT_GROUP), np.float32)
    idx = np.arange(DEINT_GROUP // 2)
    perm[2 * idx, idx] = 1.0
    perm[2 * idx + 1, DEINT_GROUP // 2 + idx] = 1.0
    return pl.pallas_call(
        _deinterleave_kernel,
        grid=(m // tm,),
        in_specs=[pl.BlockSpec((tm, n), lambda i: (i, 0)),
                  pl.BlockSpec((DEINT_GROUP, DEINT_GROUP), lambda i: (0, 0))],
        out_specs=pl.BlockSpec((tm, n), lambda i: (i, 0)),
        out_shape=jax.ShapeDtypeStruct((m, n), BF16),
        name="deinterleave_columns",
    )(w, jnp.asarray(perm, BF16))


ROUTE_TILE = 1024


def _route_kernel(x_ref, sh_ref, sc_ref, wt_ref, b_ref, before_ref, h_ref, e_ref, g_ref, r_ref, cnt_ref,
                  carry_ref):
    i = pl.program_id(0)

    @pl.when(i == 0)
    def _():
        carry_ref[...] = jnp.zeros_like(carry_ref)

    x = x_ref[...]
    ms = jnp.mean(x * x, axis=-1, keepdims=True)
    h = x * lax.rsqrt(ms + RMS_EPS) * (1.0 + sc_ref[0]) + sh_ref[0]
    h_ref[...] = h.astype(BF16)
    logits = _mm(wt_ref[...], h, "nt", 6) + b_ref[...]
    n_e, tm = logits.shape
    eidx = lax.broadcasted_iota(jnp.int32, (n_e, tm), 0)
    work = logits
    top_v, top_e = [], []
    for _ in range(TOP_K):
        m = jnp.max(work, axis=0, keepdims=True)
        sel = jnp.min(jnp.where(work == m, eidx, n_e), axis=0, keepdims=True)
        top_v.append(m)
        top_e.append(sel)
        work = jnp.where(eidx == sel, -jnp.inf, work)
    ex = [jnp.exp(v - top_v[0]) for v in top_v]
    den = ex[0] + ex[1] + ex[2] + ex[3]
    g_ref[...] = jnp.concatenate([e_ / den for e_ in ex], axis=0)
    e_ref[...] = jnp.concatenate(top_e, axis=0)
    chosen = [eidx == sel for sel in top_e]
    ind = sum(c.astype(F32) for c in chosen)
    carry = carry_ref[...]
    cnt = (jnp.dot(ind.astype(BF16), before_ref[...], preferred_element_type=F32)
           + jnp.concatenate([carry] * (tm // carry.shape[1]), axis=1))
    r_ref[...] = jnp.concatenate(
        [jnp.sum(jnp.where(c, cnt, 0.0), axis=0, keepdims=True) for c in chosen], axis=0).astype(jnp.int32)
    carry = carry + jnp.sum(ind, axis=1, keepdims=True)
    carry_ref[...] = carry
    cnt_ref[...] = carry


def _route(tok, shift, scale, router_w, router_b, rows_per_mod):
    t, d = tok.shape
    n_e = router_w.shape[1]
    tm = ROUTE_TILE
    tiles_per_mod = rows_per_mod // tm
    mod_spec = pl.BlockSpec((1, 1, d), lambda i: (i // tiles_per_mod, 0, 0))
    before = jnp.asarray(np.triu(np.ones((tm, tm), np.float32), 1), BF16)
    kt_spec = pl.BlockSpec((TOP_K, tm), lambda i: (0, i))
    h, top_e, gate, rank, cnt = pl.pallas_call(
        _route_kernel,
        grid=(t // tm,),
        in_specs=[pl.BlockSpec((tm, d), lambda i: (i, 0)), mod_spec, mod_spec,
                  pl.BlockSpec((n_e, d), lambda i: (0, 0)), pl.BlockSpec((n_e, tm), lambda i: (0, 0)),
                  pl.BlockSpec((tm, tm), lambda i: (0, 0))],
        out_specs=[pl.BlockSpec((tm, d), lambda i: (i, 0)), kt_spec, kt_spec, kt_spec,
                   pl.BlockSpec((n_e, 128), lambda i: (0, 0))],
        out_shape=[jax.ShapeDtypeStruct((t, d), BF16), jax.ShapeDtypeStruct((TOP_K, t), jnp.int32),
                   jax.ShapeDtypeStruct((TOP_K, t), F32), jax.ShapeDtypeStruct((TOP_K, t), jnp.int32),
                   jax.ShapeDtypeStruct((n_e, 128), F32)],
        scratch_shapes=[pltpu.VMEM((n_e, 128), F32)],
        compiler_params=pltpu.CompilerParams(dimension_semantics=("arbitrary",),
                                             vmem_limit_bytes=VMEM_LIMIT),
        name="moe_route",
    )(tok, shift[:, None, :], scale[:, None, :], router_w.T, jnp.broadcast_to(router_b[:, None], (n_e, tm)),
      before)
    return h, top_e, gate, rank, cnt[:, 0].astype(jnp.int32)


def _moe(tok, shift, scale, rows_per_mod, router_w, router_b, w1, b1, w2, b2):
    t, d = tok.shape
    h, top_e, gate, rank, counts = _route(tok, shift, scale, router_w, router_b, rows_per_mod)
    n_assign = t * TOP_K
    n_blocks = -(-n_assign // MOE_BLOCK) + N_EXPERTS
    n_slots = n_blocks * MOE_BLOCK
    padded = (counts + MOE_BLOCK - 1) // MOE_BLOCK * MOE_BLOCK
    start = jnp.cumsum(counts) - counts
    p_end = jnp.cumsum(padded)
    p_start = p_end - padded
    experts = jnp.arange(N_EXPERTS, dtype=jnp.int32)
    dest = jnp.sum(jnp.where(top_e[:, :, None] == experts, p_start, 0), axis=-1) + rank
    blk_first = jnp.arange(n_blocks, dtype=jnp.int32) * MOE_BLOCK
    blk_e = jnp.minimum(jnp.sum(p_end[None, :] <= blk_first[:, None], axis=1), N_EXPERTS - 1).astype(jnp.int32)
    n_used = (p_end[-1] // MOE_BLOCK).astype(jnp.int32).reshape(1)
    order = jnp.argsort(top_e.T.reshape(-1))
    per_slot = lambda per_expert: jnp.repeat(per_expert[blk_e], MOE_BLOCK)
    j = jnp.arange(n_slots, dtype=jnp.int32) - per_slot(p_start)
    src = jnp.clip(per_slot(start) + j, 0, n_assign - 1)
    slot_tok = jnp.where(j < per_slot(counts), order[src] // TOP_K, t).astype(jnp.int32)
    x_slots = jnp.concatenate([h, jnp.zeros((1, d), BF16)], axis=0)[slot_tok]
    w1p = _deinterleave_columns(w1.reshape(N_EXPERTS * d, -1)).reshape(w1.shape)
    b1p = jnp.concatenate([b1[..., ::2], b1[..., 1::2]], axis=-1)
    y = _moe_experts(x_slots, blk_e, n_used, w1p, b1p, w2.astype(BF16), b2)
    return jnp.sum(y[dest] * gate[:, :, None], axis=0)


def _rms(x):
    return x * lax.rsqrt(jnp.mean(x * x, axis=-1, keepdims=True) + RMS_EPS)


def _conv3(u, w):
    up = jnp.pad(u, ((0, 0), (1, 1), (0, 0)))
    return up[:, :-2] * w[0] + up[:, 1:-1] * w[1] + up[:, 2:] * w[2]


def _seg_sum(x, ones_bd):
    x_hi, x_lo = _hi_lo(x)
    return (jnp.dot(x_hi, ones_bd, preferred_element_type=F32)
            + jnp.dot(x_lo, ones_bd, preferred_element_type=F32))


def _head_ones(width):
    h = np.arange(width) // HEAD_DIM
    return jnp.asarray(h[:, None] == h[None, :], BF16)


def _rwkv7_prep_kernel(first_ref, last_ref, u_ref, up_ref, un_ref, sw_ref, w0_ref, w2_ref, a0_ref, a2_ref,
                       g2_ref, kk_ref, ka_ref, bd_ref, r_ref, v_ref, kap_ref, g_ref, lw_ref, k_ref, al_ref,
                       *, d_a, lora_off):
    i = pl.program_id(0)
    keep_prev = (first_ref[i] == 0).astype(F32)
    keep_next = (last_ref[i] == 0).astype(F32)
    rows = u_ref.shape[0]

    def conv(lo, width, tap_lo):
        x = u_ref[:, lo:lo + width]
        ridx = lax.broadcasted_iota(jnp.int32, x.shape, 0)
        x_prev = jnp.where(ridx == 0, up_ref[7:8, lo:lo + width] * keep_prev, pltpu.roll(x, 1, axis=0))
        x_next = jnp.where(ridx == rows - 1, un_ref[0:1, lo:lo + width] * keep_next,
                           pltpu.roll(x, rows - 1, axis=0))
        w = sw_ref[:, tap_lo:tap_lo + width]
        return x_prev * w[0:1] + x * w[1:2] + x_next * w[2:3]

    r = conv(0, d_a, 0)
    k = conv(d_a, d_a, d_a)
    v = conv(2 * d_a, d_a, 2 * d_a)
    wd = conv(lora_off, 2 * DECAY_LORA, 3 * d_a)
    ad = conv(lora_off + 2 * DECAY_LORA, 2 * AAA_LORA, 3 * d_a + 2 * DECAY_LORA)
    gd = conv(lora_off + 2 * DECAY_LORA + 2 * AAA_LORA, GATE_LORA, 3 * d_a + 2 * DECAY_LORA + 2 * AAA_LORA)
    mm3 = lambda p, q: _mm(p, q, "nn", 3)
    kk = k * kk_ref[...]
    kap = kk / jnp.maximum(jnp.sqrt(_seg_sum(kk * kk, bd_ref[...])), 1e-12)
    lora_w = jnp.tanh(wd)
    for d in range(2):
        z = w0_ref[d:d + 1] + mm3(lora_w[:, d * DECAY_LORA:(d + 1) * DECAY_LORA], w2_ref[d])
        lw_ref[d] = -math.exp(-0.5) * jax.nn.sigmoid(z)
        a = jax.nn.sigmoid(a0_ref[d:d + 1] + mm3(ad[:, d * AAA_LORA:(d + 1) * AAA_LORA], a2_ref[d]))
        k_ref[d] = k * (1.0 + (a - 1.0) * ka_ref[...])
        al_ref[d] = kap * a
    g_ref[...] = mm3(jax.nn.sigmoid(gd), g2_ref[...])
    r_ref[...] = r
    v_ref[...] = v
    kap_ref[...] = kap


def _segment_flags(bsz, s, n_ctx, tile):
    lat, ctx = np.arange(bsz * s // tile), np.arange(bsz * n_ctx // tile)
    first = np.concatenate([lat % (s // tile) == 0, ctx % (n_ctx // tile) == 0])
    last = np.concatenate([lat % (s // tile) == s // tile - 1, ctx % (n_ctx // tile) == n_ctx // tile - 1])
    return jnp.asarray(first, jnp.int32), jnp.asarray(last, jnp.int32)


def _rwkv7_prep(u, lora_off, bsz, s, n_ctx, shift_w, w0, w2, a0, a2, g2, k_k, k_a):
    t, cols = u.shape
    d_a = w0.shape[-1]
    tile = ROW_TILE
    halo = 8
    first, last = _segment_flags(bsz, s, n_ctx, tile)
    full = lambda arr: pl.BlockSpec(arr.shape, lambda i, f, l: (0,) * arr.ndim)
    row = lambda arr: arr[None, :]
    params = [shift_w, w0, w2, a0, a2, g2, row(k_k), row(k_a), _head_ones(d_a)]
    one = pl.BlockSpec((tile, d_a), lambda i, f, l: (i, 0))
    two = pl.BlockSpec((2, tile, d_a), lambda i, f, l: (0, i, 0))
    flat = jax.ShapeDtypeStruct((t, d_a), F32)
    both = jax.ShapeDtypeStruct((2, t, d_a), F32)
    grid_spec = pltpu.PrefetchScalarGridSpec(
        num_scalar_prefetch=2,
        grid=(t // tile,),
        in_specs=[pl.BlockSpec((tile, cols), lambda i, f, l: (i, 0)),
                  pl.BlockSpec((halo, cols), lambda i, f, l: (jnp.maximum(i * (tile // halo) - 1, 0), 0)),
                  pl.BlockSpec((halo, cols),
                               lambda i, f, l: (jnp.minimum((i + 1) * (tile // halo), t // halo - 1), 0))]
        + [full(p) for p in params],
        out_specs=[one, one, one, one, two, two, two],
    )
    return pl.pallas_call(
        functools.partial(_rwkv7_prep_kernel, d_a=d_a, lora_off=lora_off),
        grid_spec=grid_spec,
        out_shape=[flat, flat, flat, flat, both, both, both],
        compiler_params=pltpu.CompilerParams(dimension_semantics=("arbitrary",),
                                             vmem_limit_bytes=VMEM_LIMIT),
        name="rwkv7_prep",
    )(first, last, u, u, u, *params)


def _even_out_kernel(ya_ref, r_ref, v_ref, k_ref, g_ref, yb_ref, og_ref, rk_ref, lnw_ref, lnb_ref, nw_ref,
                     bd_ref, w_ref, res_ref, gate_ref, o_ref):
    bd = bd_ref[...]
    inv = 1.0 / HEAD_DIM
    y = ya_ref[0] + ya_ref[1]
    yc = y - _seg_sum(y, bd) * inv
    var = _seg_sum(yc * yc, bd) * inv
    yn = yc * lax.rsqrt(var + LN_X_EPS) * lnw_ref[...] + lnb_ref[...]
    bonus = _seg_sum(r_ref[...] * (k_ref[0] + k_ref[1]) * rk_ref[...], bd) * v_ref[...]
    out_a = (yn + bonus) * g_ref[...]
    o = yb_ref[0] + yb_ref[1]
    out_b = o * lax.rsqrt(_seg_sum(o * o, bd) * inv + RMS_EPS) * nw_ref[...] * jax.nn.silu(og_ref[...])
    ycat = jnp.concatenate([out_a, out_b], axis=-1).astype(BF16)
    o_ref[...] = res_ref[...] + gate_ref[0] * jnp.dot(ycat, w_ref[...], preferred_element_type=F32)


def _even_out(ya, r, v, k, g, yb, u, col_og, r_k, ln_w, ln_b, norm_w, w_out, res, gate, rows_per_mod):
    t, width = r.shape
    d = res.shape[1]
    tile = ROW_TILE
    tiles_per_mod = rows_per_mod // tile
    one = pl.BlockSpec((tile, width), lambda i: (i, 0))
    two = pl.BlockSpec((2, tile, width), lambda i: (0, i, 0))
    vec = pl.BlockSpec((1, width), lambda i: (0, 0))
    n_heads = width // HEAD_DIM
    return pl.pallas_call(
        _even_out_kernel,
        grid=(t // tile,),
        in_specs=[two, one, one, two, one, two, pl.BlockSpec((tile, width), lambda i: (i, col_og)),
                  vec, vec, vec, vec, pl.BlockSpec((width, width), lambda i: (0, 0)),
                  pl.BlockSpec(w_out.shape, lambda i: (0, 0)), pl.BlockSpec((tile, d), lambda i: (i, 0)),
                  pl.BlockSpec((1, 1, d), lambda i: (i // tiles_per_mod, 0, 0))],
        out_specs=pl.BlockSpec((tile, d), lambda i: (i, 0)),
        out_shape=jax.ShapeDtypeStruct((t, d), F32),
        compiler_params=pltpu.CompilerParams(dimension_semantics=("arbitrary",),
                                             vmem_limit_bytes=VMEM_LIMIT),
        name="even_readout_out_proj",
    )(ya, r, v, k, g, yb, u, r_k.reshape(1, width), ln_w[None, :], ln_b[None, :],
      jnp.tile(norm_w, n_heads)[None, :], _head_ones(width), w_out.astype(BF16), res, gate[:, None, :])


def _even_layer(tok, sh1, sc1, g1, bsz, s, n_ctx, w_in, w_out, shift_w, w0, w2, a0, a2, g2, k_k, k_a, r_k, ln_w,
                ln_b, lb, norm_w):
    d_a = w0.shape[-1]
    d_b = lb.shape[0]
    rkv = 3 * d_a
    a_cols = rkv + 2 * DECAY_LORA + 2 * AAA_LORA + GATE_LORA
    w_perm = jnp.concatenate([w_in[:, :rkv], w_in[:, a_cols:], w_in[:, rkv:a_cols]], axis=1)
    u = _mod_linear(tok, sh1, sc1, w_perm, s)
    r, v, kap, g, lw, k, al = _rwkv7_prep(u, rkv + 5 * d_b, bsz, s, n_ctx, shift_w, w0, w2, a0, a2, g2, k_k, k_a)
    ya = _rwkv7_recurrence(r, v, kap, lw, k, al, bsz, s, n_ctx)
    col0 = rkv // d_b
    yb = _hgrn2_recurrence(u, lb, col0, col0 + 1, col0 + 3, bsz, s, n_ctx)
    return _even_out(ya, r, v, k, g, yb, u, col0 + 4, r_k, ln_w, ln_b, norm_w, w_out, tok, g1, s)


def _axial_rope(x):
    bsz, length, n_heads, dh = x.shape
    t = jnp.arange(length)
    pos = jnp.stack([t // GRID_W, t % GRID_W], axis=-1).astype(F32)
    nf = dh // 4
    inv = ROPE_THETA ** (-jnp.arange(nf, dtype=F32) / nf)
    ang = pos[:, None, :, None] * inv
    cos, sin = jnp.cos(ang), jnp.sin(ang)
    xr = x.reshape(bsz, length, n_heads, 2, 2, nf)
    x1, x2 = xr[..., 0, :], xr[..., 1, :]
    out = jnp.stack([x1 * cos - x2 * sin, x1 * sin + x2 * cos], axis=-2)
    return out.reshape(bsz, length, n_heads, dh)


def _hyena_filters(length, d_d, w1, b1, fr1, w2, b2, fr2, w3):
    t = jnp.linspace(0.0, 1.0, length, dtype=F32)[:, None]
    bands = (HYENA_EMB - 1) // 2
    f = jnp.linspace(1e-4, bands - 1, bands, dtype=F32)
    ang = (2.0 * math.pi / length) * jnp.arange(length, dtype=F32)[:, None] * f
    z = jnp.concatenate([t, jnp.cos(ang), -jnp.sin(ang)], axis=-1)
    hid = jnp.sin(fr1 * (jnp.dot(z, w1, precision=_HI) + b1))
    hid = jnp.sin(fr2 * (jnp.dot(hid, w2, precision=_HI) + b2))
    h = jnp.dot(hid, w3, precision=_HI).reshape(length, HYENA_ORDER, 2, d_d)
    deltas = jnp.abs(jnp.linspace(math.log(HYENA_TARGET) / HYENA_SLOW_DECAY,
                                  math.log(HYENA_TARGET) / HYENA_FAST_DECAY, d_d, dtype=F32))
    return h * jnp.exp(-t * deltas)[:, None, None, :]


DFT_R = 128
DFT_COL_TILE = 4096
DFT_K1_TILE = 4


def _hi_lo(a):
    hi = a.astype(BF16)
    return hi, (a - hi.astype(F32)).astype(BF16)


def _mm3(f_hi, f_lo, x_hi, x_lo):
    d = lambda p, q: jnp.dot(p, q, preferred_element_type=F32)
    return d(f_hi, x_hi) + (d(f_hi, x_lo) + d(f_lo, x_hi))


def _dft_matrices(n1):
    k1 = np.arange(n1, dtype=np.float64)
    k2 = np.arange(DFT_R, dtype=np.float64)
    as_f32 = lambda a: jnp.asarray(a, F32)
    cs = lambda ang: (as_f32(np.cos(ang)), as_f32(-np.sin(ang)))
    f1 = cs(2.0 * np.pi * np.outer(k1, k1) / n1)
    f2 = cs(2.0 * np.pi * np.outer(k2, k2) / DFT_R)
    tw = cs(2.0 * np.pi * np.outer(k1, k2) / (n1 * DFT_R))
    tw = tuple(jnp.broadcast_to(t[:, :, None], (n1, DFT_R, 128)) for t in tw)
    return f1, f2, tw


def _dft_rows_kernel(x_ref, frh_ref, frl_ref, fih_ref, fil_ref, or_ref, oi_ref):
    x_hi, x_lo = _hi_lo(x_ref[0])
    or_ref[0] = _mm3(frh_ref[...], frl_ref[...], x_hi, x_lo)
    oi_ref[0] = _mm3(fih_ref[...], fil_ref[...], x_hi, x_lo)


def _dft_rows(x2d, f1):
    bsz, rows, cols = x2d.shape
    n1 = f1[0].shape[0]
    mats = [m for f in f1 for m in _hi_lo(f[:, :rows])]
    mat_spec = pl.BlockSpec((n1, rows), lambda b, j: (0, 0))
    out_spec = pl.BlockSpec((1, n1, DFT_COL_TILE), lambda b, j: (b, 0, j))
    out = jax.ShapeDtypeStruct((bsz, n1, cols), F32)
    return pl.pallas_call(
        _dft_rows_kernel,
        grid=(bsz, cols // DFT_COL_TILE),
        in_specs=[pl.BlockSpec((1, rows, DFT_COL_TILE), lambda b, j: (b, 0, j))] + [mat_spec] * 4,
        out_specs=[out_spec, out_spec],
        out_shape=[out, out],
        name="dft_rows",
    )(x2d, *mats)


def _dft_mid_kernel(*refs, conv):
    if conv:
        (ar_ref, ai_ref, tr_ref, ti_ref, kr_ref, ki_ref, frh_ref, frl_ref, fih_ref, fil_ref,
         or_ref, oi_ref) = refs
    else:
        ar_ref, ai_ref, tr_ref, ti_ref, frh_ref, frl_ref, fih_ref, fil_ref, or_ref, oi_ref = refs
    frh, frl, fih, fil = frh_ref[...], frl_ref[...], fih_ref[...], fil_ref[...]
    reps = ar_ref.shape[3] // tr_ref.shape[2]
    rows = range(ar_ref.shape[1])
    wide = lambda t: jnp.concatenate([t] * reps, axis=-1)
    tr = [wide(tr_ref[j]) for j in rows]
    ti = [wide(ti_ref[j]) for j in rows]
    ar = [ar_ref[0, j] for j in rows]
    ai = [ai_ref[0, j] for j in rows]
    pr = [_hi_lo(ar[j] * tr[j] - ai[j] * ti[j]) for j in rows]
    pi = [_hi_lo(ar[j] * ti[j] + ai[j] * tr[j]) for j in rows]
    xr = [_mm3(frh, frl, *pr[j]) - _mm3(fih, fil, *pi[j]) for j in rows]
    xi = [_mm3(frh, frl, *pi[j]) + _mm3(fih, fil, *pr[j]) for j in rows]
    if not conv:
        for j in rows:
            or_ref[0, j] = xr[j]
            oi_ref[0, j] = xi[j]
        return
    yr = [_hi_lo(xr[j] * kr_ref[0, j] - xi[j] * ki_ref[0, j]) for j in rows]
    yi = [_hi_lo(xr[j] * ki_ref[0, j] + xi[j] * kr_ref[0, j]) for j in rows]
    br = [_mm3(frh, frl, *yr[j]) + _mm3(fih, fil, *yi[j]) for j in rows]
    bi = [_mm3(frh, frl, *yi[j]) - _mm3(fih, fil, *yr[j]) for j in rows]
    for j in rows:
        or_ref[0, j] = br[j] * tr[j] + bi[j] * ti[j]
        oi_ref[0, j] = bi[j] * tr[j] - br[j] * ti[j]


def _dft_mid(a_r, a_i, f2, tw, kf=None):
    (f_r, f_i), (t_r, t_i) = f2, tw
    bsz, n1, _, ch = a_r.shape
    slab = pl.BlockSpec((1, DFT_K1_TILE, DFT_R, ch), lambda b, g: (b, g, 0, 0))
    tw = pl.BlockSpec((DFT_K1_TILE, DFT_R, t_r.shape[2]), lambda b, g: (g, 0, 0))
    mat = pl.BlockSpec((DFT_R, DFT_R), lambda b, g: (0, 0))
    args = [a_r, a_i, t_r, t_i]
    specs = [slab, slab, tw, tw]
    if kf is not None:
        kf_r, kf_i, o = kf
        kspec = pl.BlockSpec((1, DFT_K1_TILE, DFT_R, ch), lambda b, g: (o, g, 0, 0))
        args += [kf_r, kf_i]
        specs += [kspec, kspec]
    args += [m for f in (f_r, f_i) for m in _hi_lo(f)]
    specs += [mat] * 4
    out = jax.ShapeDtypeStruct(a_r.shape, F32)
    return pl.pallas_call(
        functools.partial(_dft_mid_kernel, conv=kf is not None),
        grid=(bsz, n1 // DFT_K1_TILE),
        in_specs=specs,
        out_specs=[slab, slab],
        out_shape=[out, out],
        compiler_params=pltpu.CompilerParams(dimension_semantics=("arbitrary", "arbitrary"),
                                             vmem_limit_bytes=VMEM_LIMIT),
        name="dft_mid_conv" if kf is not None else "dft_mid",
    )(*args)


def _idft_rows_kernel(br_ref, bi_ref, z_ref, g_ref, bias_ref, grh_ref, grl_ref, gih_ref, gil_ref, o_ref):
    y = (_mm3(grh_ref[...], grl_ref[...], *_hi_lo(br_ref[0]))
         + _mm3(gih_ref[...], gil_ref[...], *_hi_lo(bi_ref[0])))
    o_ref[0] = g_ref[0] * (y + z_ref[0] * bias_ref[...])


def _idft_rows(b_r, b_i, z2d, gate2d, bias_cols, f1):
    bsz, rows, cols = z2d.shape
    n1 = f1[0].shape[0]
    scale = 1.0 / (n1 * DFT_R)
    mats = [m for f in f1 for m in _hi_lo(f[:rows] * scale)]
    mat_spec = pl.BlockSpec((rows, n1), lambda b, j: (0, 0))
    in_spec = pl.BlockSpec((1, n1, DFT_COL_TILE), lambda b, j: (b, 0, j))
    io_spec = pl.BlockSpec((1, rows, DFT_COL_TILE), lambda b, j: (b, 0, j))
    return pl.pallas_call(
        _idft_rows_kernel,
        grid=(bsz, cols // DFT_COL_TILE),
        in_specs=[in_spec, in_spec, io_spec, io_spec, pl.BlockSpec((1, DFT_COL_TILE), lambda b, j: (0, j))]
        + [mat_spec] * 4,
        out_specs=io_spec,
        out_shape=jax.ShapeDtypeStruct((bsz, rows, cols), F32),
        name="idft_rows",
    )(b_r, b_i, z2d, gate2d, bias_cols, *mats)


def _hyena(u, short_w, w1, b1, fr1, w2, b2, fr2, w3, bias):
    bsz, length, _ = u.shape
    u = _conv3(u, short_w)
    v, x1, x2 = jnp.split(u, 3, axis=-1)
    ch = v.shape[-1]
    rows = length // DFT_R
    h = _hyena_filters(length, ch, w1, b1, fr1, w2, b2, fr2, w3)
    h_fwd, h_bwd = h[:, :, 0], h[:, :, 1]
    k2 = jnp.concatenate([h_fwd.at[0].add(h_bwd[0]), jnp.zeros_like(h_fwd[:1]), h_bwd[:0:-1]], axis=0)
    k2 = k2.transpose(1, 0, 2).reshape(HYENA_ORDER, 2 * rows, DFT_R * ch)
    n1 = 2 * rows
    f1, f2, tw = _dft_matrices(n1)
    four_d = lambda t: t.reshape(t.shape[0], n1, DFT_R, ch)
    kf_r, kf_i = _dft_mid(*(four_d(t) for t in _dft_rows(k2, f1)), f2, tw)
    z = v
    for o, gate in enumerate((x1, x2)):
        z2d = z.reshape(bsz, rows, DFT_R * ch)
        a_r, a_i = _dft_rows(z2d, f1)
        b_r, b_i = _dft_mid(four_d(a_r), four_d(a_i), f2, tw, kf=(kf_r, kf_i, o))
        flat = lambda t: t.reshape(bsz, n1, DFT_R * ch)
        y = _idft_rows(flat(b_r), flat(b_i), z2d, gate.reshape(z2d.shape), jnp.tile(bias[o], DFT_R)[None, :], f1)
        z = y.reshape(bsz, length, ch)
    return z


def _odd_mixer_latent(u_lat, u_ctx, q_norm, k_norm, rpb, short_w, w1, b1, fr1, w2, b2, fr2, w3, bias):
    n_heads = rpb.shape[0]
    d_c = n_heads * HEAD_DIM
    bsz, s, _ = u_lat.shape

    def qkv(t):
        length = t.shape[1]
        q, k, v = jnp.split(t[..., :3 * d_c], 3, axis=-1)
        heads = lambda a_: a_.reshape(bsz, length, n_heads, HEAD_DIM)
        return _rms(heads(q)) * q_norm, _rms(heads(k)) * k_norm, heads(v)

    q, k, v = qkv(u_lat)
    _, kc, vc = qkv(u_ctx)
    hm = lambda t: t.transpose(0, 2, 1, 3)
    o = _neighbourhood_attention(hm(_axial_rope(q) * HEAD_DIM ** -0.5), hm(_axial_rope(k)), hm(v),
                                 hm(kc), hm(vc), rpb)
    y_na = o.transpose(0, 2, 1, 3).reshape(bsz, s, d_c)
    y_hy = _hyena(u_lat[..., 3 * d_c:], short_w, w1, b1, fr1, w2, b2, fr2, w3, bias)
    return jnp.concatenate([y_na, y_hy], axis=-1)


def kernel(x, c, ctx, c_ctx, mod_w, mod_b, router_w, router_b, moe_w1, moe_b1, moe_w2, moe_b2, ab_w_in, ab_w_out, rwkv_shift, rwkv_w0, rwkv_w2, rwkv_a0, rwkv_a2, rwkv_g2, rwkv_k_k, rwkv_k_a, rwkv_r_k, rwkv_ln_w, rwkv_ln_b, hgrn_lb_logits, hgrn_norm_w, cd_w_in, cd_w_out, na_q_norm, na_k_norm, na_rpb, hy_short, hy_w1, hy_b1, hy_freq1, hy_w2, hy_b2, hy_freq2, hy_w3, hy_bias):
    bsz, s, d = x.shape
    n_ctx = ctx.shape[1]
    depth = mod_w.shape[0]
    n_lat = bsz * s
    lb_all = jnp.cumsum(jax.nn.softmax(hgrn_lb_logits, axis=0), axis=0)
    tok = jnp.concatenate([x.reshape(n_lat, d), ctx.reshape(bsz * n_ctx, d)], axis=0)
    cond = jnp.concatenate([jax.nn.silu(c), jax.nn.silu(c_ctx)[None, :],
                            jnp.zeros((8 - bsz - 1, d), F32)], axis=0)
    assert depth == 2, "layer 0 = RWKV-7 || HGRN2 with context outputs, layer 1 = attention || Hyena, latent only"
    for l in range(depth):
        j = l // 2
        mod = _small_linear(cond, mod_w[l], mod_b[l])[:bsz + 1]
        sh1, sc1, g1, sh2, sc2, g2 = jnp.split(mod, 6, axis=-1)
        if l % 2 == 0:
            tok = _even_layer(tok, sh1, sc1, g1, bsz, s, n_ctx, ab_w_in[j], ab_w_out[j], rwkv_shift[j],
                              rwkv_w0[j], rwkv_w2[j], rwkv_a0[j], rwkv_a2[j], rwkv_g2[j], rwkv_k_k[j],
                              rwkv_k_a[j], rwkv_r_k[j], rwkv_ln_w[j], rwkv_ln_b[j], lb_all[j], hgrn_norm_w[j])
        else:
            u = _mod_linear(tok, sh1, sc1, cd_w_in[j], s)
            u_lat = u[:n_lat].reshape(bsz, s, -1)
            u_ctx = u[n_lat:].reshape(bsz, n_ctx, -1)
            y_lat = _odd_mixer_latent(u_lat, u_ctx, na_q_norm[j], na_k_norm[j], na_rpb[j], hy_short[j],
                                      hy_w1[j], hy_b1[j], hy_freq1[j], hy_w2[j], hy_b2[j], hy_freq2[j],
                                      hy_w3[j], hy_bias[j])
            tok = _res_linear(y_lat.reshape(n_lat, d), cd_w_out[j], tok, g1, s)
        out = _moe(tok, sh2, sc2, s, router_w[l], router_b[l], moe_w1[l], moe_b1[l], moe_w2[l], moe_b2[l])
        tok = _gated_add(tok, out, g2, s)
    return tok[:n_lat].reshape(bsz, s, d)


def _gated_add_kernel(x_ref, y_ref, g_ref, o_ref):
    o_ref[...] = x_ref[...] + g_ref[0] * y_ref[...]


def _gated_add(x, y, gate, rows_per_mod):
    m, d = x.shape
    tiles_per_mod = rows_per_mod // ROW_TILE
    tile = pl.BlockSpec((ROW_TILE, d), lambda i: (i, 0))
    return pl.pallas_call(
        _gated_add_kernel,
        grid=(m // ROW_TILE,),
        in_specs=[tile, tile, pl.BlockSpec((1, 1, d), lambda i: (i // tiles_per_mod, 0, 0))],
        out_specs=tile,
        out_shape=jax.ShapeDtypeStruct((m, d), F32),
        name="gated_add",
    )(x, y, gate[:, None, :])
```

```python
import functools
import math

import numpy as np
import jax
import jax.numpy as jnp
from jax import lax
from jax.experimental import pallas as pl
from jax.experimental.pallas import tpu as pltpu

F32 = jnp.float32
BF16 = jnp.bfloat16

HEAD_DIM = 64
GRID_W = 64
DECAY_LORA = 64
AAA_LORA = 64
GATE_LORA = 128
LN_X_EPS = 1e-5 * HEAD_DIM
NA_ROWS = 8
NA_COLS = 16
ROPE_THETA = 10000.0
HYENA_ORDER = 2
HYENA_EMB = 33
HYENA_FAST_DECAY = 0.3
HYENA_SLOW_DECAY = 1.5
HYENA_TARGET = 1e-2
N_EXPERTS = 32
TOP_K = 4
SWIGLU_ALPHA = 1.702
SWIGLU_LIMIT = 7.0
MOE_BLOCK = 256
RMS_EPS = 1e-6

ROW_TILE = 256
REC_CHUNK = 64
REC_PASSES = 1
REC_CHAIN_PASSES = 1
VMEM_LIMIT = 56 * 1024 * 1024

_HI = lax.Precision.HIGHEST


_DIMS = {"nn": (((1,), (0,)), ((), ())), "nt": (((1,), (1,)), ((), ())), "tn": (((0,), (0,)), ((), ()))}


def _mm(a, b, form, passes):
    dims = _DIMS[form]
    if passes == 6:
        return lax.dot_general(a, b, dims, precision=_HI, preferred_element_type=F32)
    a_hi = a.astype(BF16)
    b_hi = b.astype(BF16)
    out = lax.dot_general(a_hi, b_hi, dims, preferred_element_type=F32)
    if passes == 3:
        a_lo = (a - a_hi.astype(F32)).astype(BF16)
        b_lo = (b - b_hi.astype(F32)).astype(BF16)
        out = out + (lax.dot_general(a_hi, b_lo, dims, preferred_element_type=F32)
                     + lax.dot_general(a_lo, b_hi, dims, preferred_element_type=F32))
    return out


def _dot(a, b):
    return _mm(a, b, "nn", 6)


def _mod_linear_kernel(x_ref, sh_ref, sc_ref, w_ref, o_ref):
    x = x_ref[...]
    ms = jnp.mean(x * x, axis=-1, keepdims=True)
    h = x * lax.rsqrt(ms + RMS_EPS) * (1.0 + sc_ref[0]) + sh_ref[0]
    o_ref[...] = jnp.dot(h.astype(BF16), w_ref[...], preferred_element_type=F32)


def _mod_linear(x, shift, scale, w, rows_per_mod):
    m, d = x.shape
    n = w.shape[1]
    tiles_per_mod = rows_per_mod // ROW_TILE
    mod_spec = pl.BlockSpec((1, 1, d), lambda i: (i // tiles_per_mod, 0, 0))
    return pl.pallas_call(
        _mod_linear_kernel,
        grid=(m // ROW_TILE,),
        in_specs=[pl.BlockSpec((ROW_TILE, d), lambda i: (i, 0)), mod_spec, mod_spec,
                  pl.BlockSpec((d, n), lambda i: (0, 0))],
        out_specs=pl.BlockSpec((ROW_TILE, n), lambda i: (i, 0)),
        out_shape=jax.ShapeDtypeStruct((m, n), F32),
        compiler_params=pltpu.CompilerParams(dimension_semantics=("arbitrary",),
                                             vmem_limit_bytes=VMEM_LIMIT),
        name="mod_linear",
    )(x, shift[:, None, :], scale[:, None, :], w.astype(BF16))


def _res_linear_kernel(y_ref, w_ref, res_ref, g_ref, o_ref):
    o_ref[...] = res_ref[...] + g_ref[0] * jnp.dot(y_ref[...].astype(BF16), w_ref[...],
                                                   preferred_element_type=F32)


def _res_linear(y, w, res, gate, rows_per_mod):
    m, k = y.shape
    n = w.shape[1]
    tiles_per_mod = rows_per_mod // ROW_TILE
    return pl.pallas_call(
        _res_linear_kernel,
        grid=(m // ROW_TILE,),
        in_specs=[pl.BlockSpec((ROW_TILE, k), lambda i: (i, 0)),
                  pl.BlockSpec((k, n), lambda i: (0, 0)),
                  pl.BlockSpec((ROW_TILE, n), lambda i: (i, 0)),
                  pl.BlockSpec((1, 1, n), lambda i: (i // tiles_per_mod, 0, 0))],
        out_specs=pl.BlockSpec((ROW_TILE, n), lambda i: (i, 0)),
        out_shape=jax.ShapeDtypeStruct((m, n), F32),
        compiler_params=pltpu.CompilerParams(dimension_semantics=("arbitrary",),
                                             vmem_limit_bytes=VMEM_LIMIT),
        name="res_linear",
    )(y, w.astype(BF16), res, gate[:, None, :])


def _small_linear_kernel(x_ref, w_ref, b_ref, o_ref):
    o_ref[...] = _dot(x_ref[...], w_ref[...]) + b_ref[...]


def _small_linear(x, w, b):
    m, k = x.shape
    n = w.shape[1]
    tn = 1024 if n % 1024 == 0 else n
    return pl.pallas_call(
        _small_linear_kernel,
        grid=(n // tn,),
        in_specs=[pl.BlockSpec((m, k), lambda j: (0, 0)), pl.BlockSpec((k, tn), lambda j: (0, j)),
                  pl.BlockSpec((1, tn), lambda j: (0, j))],
        out_specs=pl.BlockSpec((m, tn), lambda j: (0, j)),
        out_shape=jax.ShapeDtypeStruct((m, n), F32),
        name="small_linear",
    )(x, w, b[None, :])


def _rec_kernel(*refs, delta, n_heads):
    n_in = 6 if delta else 3
    ins = (refs[:n_in], refs[n_in:2 * n_in])
    rest = refs[2 * n_in:]
    if delta:
        y_refs, s_ref = rest[:2], rest[2]
    else:
        lb_ref, y_refs, s_ref = rest[0], rest[1:3], rest[3]

    @pl.when(pl.program_id(1) == 0)
    def _():
        s_ref[...] = jnp.zeros_like(s_ref)

    c = ins[0][0].shape[0]
    row = lax.broadcasted_iota(jnp.int32, (c, c), 0)
    col = lax.broadcasted_iota(jnp.int32, (c, c), 1)
    mid = c // 2
    heads = range(n_heads)
    hs = lambda t: [t[:, h * HEAD_DIM:(h + 1) * HEAD_DIM] for h in heads]
    mm = functools.partial(_mm, passes=REC_PASSES)

    streams = []
    for d in range(2):
        ahead = row - col if d == 0 else col - row
        incl, strict = ahead >= 0, ahead > 0
        if delta:
            r_ref, v_ref, kap_ref, lw_ref, k_ref, al_ref = ins[d]
            r_all, v_all, lw, k = r_ref[...], v_ref[...], lw_ref[0], k_ref[0]
        else:
            q_ref, i_ref, f_ref = ins[d]
            fg = lb_ref[...] + (1.0 - lb_ref[...]) * jax.nn.sigmoid(f_ref[...])
            r_all, v_all, lw, k = jax.nn.silu(q_ref[...]), i_ref[...], jnp.log(fg), 1.0 - fg
        b = _mm(incl.astype(F32), lw, "nn", 6)
        bm = b[mid:mid + 1, :]
        tot = b[c - 1:c, :] if d == 0 else b[0:1, :]
        e_neg = jnp.exp(bm - b)
        e_end = jnp.exp(tot - b)
        st = dict(incl=incl, strict=strict, v=hs(v_all), rq=hs(r_all * jnp.exp(b - bm)), kd=hs(k * e_neg),
                  k_end=hs(k * e_end), g_mid=hs(jnp.exp(bm)), g_tot=hs(jnp.exp(tot)))
        if delta:
            al = al_ref[0]
            st.update(kq=hs(kap_ref[...] * jnp.exp(b - lw - bm)), ad=hs(al * e_neg), al_end=hs(al * e_end))
        streams.append(st)

    units = [(d, h) for d in range(2) for h in heads]
    n_u = range(len(units))
    per_head = lambda name: [streams[d][name][h] for d, h in units]
    per_dir = lambda name: [streams[d][name] for d, _ in units]
    incl, strict = per_dir("incl"), per_dir("strict")
    v, rq, kd, k_end, g_mid, g_tot = (per_head(n) for n in ("v", "rq", "kd", "k_end", "g_mid", "g_tot"))
    s0 = [s_ref[d, h] for d, h in units]
    s0m = [s0[n] * g_mid[n] for n in n_u]
    if delta:
        kq, ad, al_end = per_head("kq"), per_head("ad"), per_head("al_end")
        q2 = [jnp.concatenate([kq[n], rq[n]], axis=0) for n in n_u]
        k2 = [jnp.concatenate([kd[n], ad[n]], axis=0) for n in n_u]
        a = [mm(q2[n], k2[n], "nt") for n in n_u]
        p = [mm(q2[n], s0m[n], "nt") for n in n_u]
        z = [-(p[n][:c] + mm(jnp.where(strict[n], a[n][:c, :c], 0.0), v[n], "nn")) for n in n_u]
        mc = functools.partial(_mm, passes=REC_CHAIN_PASSES)
        m = [jnp.where(strict[n], a[n][:c, c:], 0.0) for n in n_u]
        pair = (row >> 1) == (col >> 1)
        eye = (row == col).astype(F32)
        t = [eye - jnp.where(pair, m[n], 0.0) for n in n_u]
        for lvl in range(1, int(math.log2(c))):
            off = ((row >> (lvl + 1)) == (col >> (lvl + 1))) & ((row >> lvl) != (col >> lvl))
            tc = [mc(t[n], jnp.where(off, m[n], 0.0), "nn") for n in n_u]
            t = [t[n] - mc(tc[n], t[n], "nn") for n in n_u]
        z = [mc(t[n], z[n], "nn") for n in n_u]
        y = [p[n][c:] + mm(jnp.where(incl[n], a[n][c:, :c], 0.0), v[n], "nn")
             + mm(jnp.where(incl[n], a[n][c:, c:], 0.0), z[n], "nn") for n in n_u]
        s_new = [s0[n] * g_tot[n] + mm(v[n], k_end[n], "tn") + mm(z[n], al_end[n], "tn") for n in n_u]
    else:
        a = [mm(rq[n], kd[n], "nt") for n in n_u]
        y = [mm(rq[n], s0m[n], "nt") + mm(jnp.where(incl[n], a[n], 0.0), v[n], "nn") for n in n_u]
        s_new = [s0[n] * g_tot[n] + mm(v[n], k_end[n], "tn") for n in n_u]
    for n, (d, h) in enumerate(units):
        s_ref[d, h] = s_new[n]
    for d in range(2):
        y_refs[d][...] = jnp.concatenate(y[d * n_heads:(d + 1) * n_heads], axis=-1)


def _seq_block_index(bsz, s, n_ctx, rows):
    n_c, n_l = n_ctx // rows, s // rows

    def index(d, b, i):
        back = jnp.where(i < n_c, n_c - 1 - i, n_l + 2 * n_c - 1 - i)
        pos = jnp.where(d == 0, i, back)
        return jnp.where(pos < n_c, bsz * n_l + b * n_c + pos, b * n_l + pos - n_c)

    return index


def _rec_call(args, specs, bsz, s, n_ctx, width, delta):
    c = REC_CHUNK
    n_heads = width // HEAD_DIM
    blk = _seq_block_index(bsz, s, n_ctx, c)
    out = jax.ShapeDtypeStruct((bsz * (s + n_ctx), width), F32)
    return pl.pallas_call(
        functools.partial(_rec_kernel, delta=delta, n_heads=n_heads),
        grid=(bsz, (s + n_ctx) // c),
        in_specs=specs,
        out_specs=[pl.BlockSpec((c, width), lambda b, i, d=d: (blk(d, b, i), 0)) for d in range(2)],
        out_shape=[out, out],
        scratch_shapes=[pltpu.VMEM((2, n_heads, HEAD_DIM, HEAD_DIM), F32)],
        compiler_params=pltpu.CompilerParams(dimension_semantics=("arbitrary", "arbitrary")),
        name="rwkv7_rec" if delta else "hgrn2_rec",
    )(*args)


def _rwkv7_recurrence(r, v, kap, lw, k, al, bsz, s, n_ctx):
    width = r.shape[1]
    blk = _seq_block_index(bsz, s, n_ctx, REC_CHUNK)
    specs = []
    for d in range(2):
        shared = pl.BlockSpec((REC_CHUNK, width), lambda b, i, d=d: (blk(d, b, i), 0))
        per_dir = pl.BlockSpec((1, REC_CHUNK, width), lambda b, i, d=d: (d, blk(d, b, i), 0))
        specs += [shared] * 3 + [per_dir] * 3
    return _rec_call([r, v, kap, lw, k, al] * 2, specs, bsz, s, n_ctx, width, True)


def _hgrn2_recurrence(u, lb, col_q, col_f, col_i, bsz, s, n_ctx):
    width = lb.shape[0]
    blk = _seq_block_index(bsz, s, n_ctx, REC_CHUNK)
    specs = []
    for d in range(2):
        specs += [pl.BlockSpec((REC_CHUNK, width), lambda b, i, d=d, j=j: (blk(d, b, i), j))
                  for j in (col_q, col_i, col_f + d)]
    specs.append(pl.BlockSpec((1, width), lambda b, i: (0, 0)))
    return _rec_call([u] * 6 + [lb[None, :]], specs, bsz, s, n_ctx, width, False)


def _na_kernel(q_ref, k_ref, v_ref, kc_ref, vc_ref, bias_ref, o_ref, *, rows_per_step, n_rows):
    g = pl.program_id(2)
    kc = kc_ref[0, 0]
    vc = vc_ref[0, 0]
    band = NA_ROWS * GRID_W
    js = range(rows_per_step)
    nt = lambda p, q_: lax.dot_general(p, q_, (((1,), (1,)), ((), ())), preferred_element_type=F32)
    qr = [g * rows_per_step + j for j in js]
    start = [jnp.clip(qr[j] - NA_ROWS // 2, 0, n_rows - NA_ROWS) for j in js]
    off = [pl.multiple_of(start[j] * GRID_W, GRID_W) for j in js]
    q = [q_ref[0, 0, j * GRID_W:(j + 1) * GRID_W, :].astype(BF16) for j in js]
    s_win = [nt(q[j], k_ref[0, 0, pl.ds(off[j], band), :]) + bias_ref[0, start[j] - qr[j] + NA_ROWS - 1]
             for j in js]
    s_ctx = [nt(q[j], kc) for j in js]
    m = [jnp.maximum(jnp.max(s_win[j], axis=-1, keepdims=True), jnp.max(s_ctx[j], axis=-1, keepdims=True))
         for j in js]
    p_win = [jnp.exp(s_win[j] - m[j]) for j in js]
    p_ctx = [jnp.exp(s_ctx[j] - m[j]) for j in js]
    den = [jnp.sum(p_win[j], axis=-1, keepdims=True) + jnp.sum(p_ctx[j], axis=-1, keepdims=True) for j in js]
    o = [jnp.dot(p_win[j].astype(BF16), v_ref[0, 0, pl.ds(off[j], band), :], preferred_element_type=F32)
         + jnp.dot(p_ctx[j].astype(BF16), vc, preferred_element_type=F32) for j in js]
    for j in js:
        o_ref[0, 0, j * GRID_W:(j + 1) * GRID_W, :] = o[j] / den[j]


def _na_bias_table(rpb):
    cc = np.arange(GRID_W)
    col_start = np.clip(cc - NA_COLS // 2, 0, GRID_W - NA_COLS)
    kc = np.arange(GRID_W)
    inside = (kc[None, :] >= col_start[:, None]) & (kc[None, :] < col_start[:, None] + NA_COLS)
    col_off = np.clip(kc[None, :] - cc[:, None] + NA_COLS - 1, 0, 2 * NA_COLS - 2)
    tab = rpb[:, :, col_off]
    tab = jnp.where(jnp.asarray(inside)[None, None], tab, -jnp.inf)
    d0 = np.arange(NA_ROWS)[:, None] + np.arange(NA_ROWS)[None, :]
    band = tab[:, d0]
    return band.transpose(0, 1, 3, 2, 4).reshape(rpb.shape[0], NA_ROWS, GRID_W, NA_ROWS * GRID_W)


def _neighbourhood_attention(q, k, v, kc, vc, rpb):
    bsz, n_heads, s, dh = q.shape
    n_rows = s // GRID_W
    rows_per_step = 8
    lc = kc.shape[2]
    bias = _na_bias_table(rpb)
    full = pl.BlockSpec((1, 1, s, dh), lambda b, h, g: (b, h, 0, 0))
    ctx = pl.BlockSpec((1, 1, lc, dh), lambda b, h, g: (b, h, 0, 0))
    tile = pl.BlockSpec((1, 1, rows_per_step * GRID_W, dh), lambda b, h, g: (b, h, g, 0))
    return pl.pallas_call(
        functools.partial(_na_kernel, rows_per_step=rows_per_step, n_rows=n_rows),
        grid=(bsz, n_heads, n_rows // rows_per_step),
        in_specs=[tile, full, full, ctx, ctx,
                  pl.BlockSpec((1, NA_ROWS, GRID_W, NA_ROWS * GRID_W), lambda b, h, g: (h, 0, 0, 0))],
        out_specs=tile,
        out_shape=jax.ShapeDtypeStruct((bsz, n_heads, s, dh), F32),
        compiler_params=pltpu.CompilerParams(
            dimension_semantics=("arbitrary", "arbitrary", "arbitrary")),
        name="neighbourhood_attention",
    )(q, k.astype(BF16), v.astype(BF16), kc.astype(BF16), vc.astype(BF16), bias)


def _moe_kernel(be_ref, nb_ref, x_ref, w1_ref, b1_ref, w2_ref, b2_ref, o_ref):
    i = pl.program_id(0)

    @pl.when(i < nb_ref[0])
    def _():
        y = jnp.dot(x_ref[...], w1_ref[0], preferred_element_type=F32) + b1_ref[0]
        half = y.shape[1] // 2
        glu = jnp.minimum(y[:, :half], SWIGLU_LIMIT)
        lin = jnp.clip(y[:, half:], -SWIGLU_LIMIT, SWIGLU_LIMIT)
        act = glu * jax.nn.sigmoid(SWIGLU_ALPHA * glu) * (lin + 1.0)
        o_ref[...] = jnp.dot(act.astype(BF16), w2_ref[0], preferred_element_type=F32) + b2_ref[0]

    @pl.when(i >= nb_ref[0])
    def _():
        o_ref[...] = jnp.zeros_like(o_ref)


def _moe_experts(x_slots, blk_e, n_used, w1, b1, w2, b2):
    n_slots, d = x_slots.shape
    n_blocks = n_slots // MOE_BLOCK
    de2 = w1.shape[2]
    grid_spec = pltpu.PrefetchScalarGridSpec(
        num_scalar_prefetch=2,
        grid=(n_blocks,),
        in_specs=[pl.BlockSpec((MOE_BLOCK, d), lambda i, be, nb: (i, 0)),
                  pl.BlockSpec((1, d, de2), lambda i, be, nb: (be[i], 0, 0)),
                  pl.BlockSpec((1, 1, de2), lambda i, be, nb: (be[i], 0, 0)),
                  pl.BlockSpec((1, de2 // 2, d), lambda i, be, nb: (be[i], 0, 0)),
                  pl.BlockSpec((1, 1, d), lambda i, be, nb: (be[i], 0, 0))],
        out_specs=pl.BlockSpec((MOE_BLOCK, d), lambda i, be, nb: (i, 0)),
    )
    return pl.pallas_call(
        _moe_kernel,
        grid_spec=grid_spec,
        out_shape=jax.ShapeDtypeStruct((n_slots, d), F32),
        compiler_params=pltpu.CompilerParams(dimension_semantics=("arbitrary",),
                                             vmem_limit_bytes=VMEM_LIMIT),
        name="moe_experts",
    )(blk_e, n_used, x_slots, w1, b1[:, None, :], w2, b2[:, None, :])


DEINT_GROUP = 256


def _deinterleave_kernel(w_ref, p_ref, o_ref):
    half = o_ref.shape[1] // 2
    g_out = DEINT_GROUP // 2
    for g in range(w_ref.shape[1] // DEINT_GROUP):
        t = jnp.dot(w_ref[:, g * DEINT_GROUP:(g + 1) * DEINT_GROUP].astype(BF16), p_ref[...],
                    preferred_element_type=F32).astype(BF16)
        o_ref[:, g * g_out:(g + 1) * g_out] = t[:, :g_out]
        o_ref[:, half + g * g_out:half + (g + 1) * g_out] = t[:, g_out:]


def _deinterleave_columns(w):
    m, n = w.shape
    tm = 512
    perm = np.zeros((DEINT_GROUP, DEINT_GROUP), np.float32)
    idx = np.arange(DEINT_GROUP // 2)
    perm[2 * idx, idx] = 1.0
    perm[2 * idx + 1, DEINT_GROUP // 2 + idx] = 1.0
    return pl.pallas_call(
        _deinterleave_kernel,
        grid=(m // tm,),
        in_specs=[pl.BlockSpec((tm, n), lambda i: (i, 0)),
                  pl.BlockSpec((DEINT_GROUP, DEINT_GROUP), lambda i: (0, 0))],
        out_specs=pl.BlockSpec((tm, n), lambda i: (i, 0)),
        out_shape=jax.ShapeDtypeStruct((m, n), BF16),
        name="deinterleave_columns",
    )(w, jnp.asarray(perm, BF16))


ROUTE_TILE = 1024


def _route_kernel(x_ref, sh_ref, sc_ref, wt_ref, b_ref, before_ref, h_ref, e_ref, g_ref, r_ref, cnt_ref,
                  carry_ref):
    i = pl.program_id(0)

    @pl.when(i == 0)
    def _():
        carry_ref[...] = jnp.zeros_like(carry_ref)

    x = x_ref[...]
    ms = jnp.mean(x * x, axis=-1, keepdims=True)
    h = x * lax.rsqrt(ms + RMS_EPS) * (1.0 + sc_ref[0]) + sh_ref[0]
    h_ref[...] = h.astype(BF16)
    logits = _mm(wt_ref[...], h, "nt", 6) + b_ref[...]
    n_e, tm = logits.shape
    eidx = lax.broadcasted_iota(jnp.int32, (n_e, tm), 0)
    work = logits
    top_v, top_e = [], []
    for _ in range(TOP_K):
        m = jnp.max(work, axis=0, keepdims=True)
        sel = jnp.min(jnp.where(work == m, eidx, n_e), axis=0, keepdims=True)
        top_v.append(m)
        top_e.append(sel)
        work = jnp.where(eidx == sel, -jnp.inf, work)
    ex = [jnp.exp(v - top_v[0]) for v in top_v]
    den = ex[0] + ex[1] + ex[2] + ex[3]
    g_ref[...] = jnp.concatenate([e_ / den for e_ in ex], axis=0)
    e_ref[...] = jnp.concatenate(top_e, axis=0)
    chosen = [eidx == sel for sel in top_e]
    ind = sum(c.astype(F32) for c in chosen)
    carry = carry_ref[...]
    cnt = (jnp.dot(ind.astype(BF16), before_ref[...], preferred_element_type=F32)
           + jnp.concatenate([carry] * (tm // carry.shape[1]), axis=1))
    r_ref[...] = jnp.concatenate(
        [jnp.sum(jnp.where(c, cnt, 0.0), axis=0, keepdims=True) for c in chosen], axis=0).astype(jnp.int32)
    carry = carry + jnp.sum(ind, axis=1, keepdims=True)
    carry_ref[...] = carry
    cnt_ref[...] = carry


def _route(tok, shift, scale, router_w, router_b, rows_per_mod):
    t, d = tok.shape
    n_e = router_w.shape[1]
    tm = ROUTE_TILE
    tiles_per_mod = rows_per_mod // tm
    mod_spec = pl.BlockSpec((1, 1, d), lambda i: (i // tiles_per_mod, 0, 0))
    before = jnp.asarray(np.triu(np.ones((tm, tm), np.float32), 1), BF16)
    kt_spec = pl.BlockSpec((TOP_K, tm), lambda i: (0, i))
    h, top_e, gate, rank, cnt = pl.pallas_call(
        _route_kernel,
        grid=(t // tm,),
        in_specs=[pl.BlockSpec((tm, d), lambda i: (i, 0)), mod_spec, mod_spec,
                  pl.BlockSpec((n_e, d), lambda i: (0, 0)), pl.BlockSpec((n_e, tm), lambda i: (0, 0)),
                  pl.BlockSpec((tm, tm), lambda i: (0, 0))],
        out_specs=[pl.BlockSpec((tm, d), lambda i: (i, 0)), kt_spec, kt_spec, kt_spec,
                   pl.BlockSpec((n_e, 128), lambda i: (0, 0))],
        out_shape=[jax.ShapeDtypeStruct((t, d), BF16), jax.ShapeDtypeStruct((TOP_K, t), jnp.int32),
                   jax.ShapeDtypeStruct((TOP_K, t), F32), jax.ShapeDtypeStruct((TOP_K, t), jnp.int32),
                   jax.ShapeDtypeStruct((n_e, 128), F32)],
        scratch_shapes=[pltpu.VMEM((n_e, 128), F32)],
        compiler_params=pltpu.CompilerParams(dimension_semantics=("arbitrary",),
                                             vmem_limit_bytes=VMEM_LIMIT),
        name="moe_route",
    )(tok, shift[:, None, :], scale[:, None, :], router_w.T, jnp.broadcast_to(router_b[:, None], (n_e, tm)),
      before)
    return h, top_e, gate, rank, cnt[:, 0].astype(jnp.int32)


def _moe(tok, shift, scale, rows_per_mod, router_w, router_b, w1, b1, w2, b2):
    t, d = tok.shape
    h, top_e, gate, rank, counts = _route(tok, shift, scale, router_w, router_b, rows_per_mod)
    n_assign = t * TOP_K
    n_blocks = -(-n_assign // MOE_BLOCK) + N_EXPERTS
    n_slots = n_blocks * MOE_BLOCK
    padded = (counts + MOE_BLOCK - 1) // MOE_BLOCK * MOE_BLOCK
    start = jnp.cumsum(counts) - counts
    p_end = jnp.cumsum(padded)
    p_start = p_end - padded
    experts = jnp.arange(N_EXPERTS, dtype=jnp.int32)
    dest = jnp.sum(jnp.where(top_e[:, :, None] == experts, p_start, 0), axis=-1) + rank
    blk_first = jnp.arange(n_blocks, dtype=jnp.int32) * MOE_BLOCK
    blk_e = jnp.minimum(jnp.sum(p_end[None, :] <= blk_first[:, None], axis=1), N_EXPERTS - 1).astype(jnp.int32)
    n_used = (p_end[-1] // MOE_BLOCK).astype(jnp.int32).reshape(1)
    order = jnp.argsort(top_e.T.reshape(-1))
    per_slot = lambda per_expert: jnp.repeat(per_expert[blk_e], MOE_BLOCK)
    j = jnp.arange(n_slots, dtype=jnp.int32) - per_slot(p_start)
    src = jnp.clip(per_slot(start) + j, 0, n_assign - 1)
    slot_tok = jnp.where(j < per_slot(counts), order[src] // TOP_K, t).astype(jnp.int32)
    x_slots = jnp.concatenate([h, jnp.zeros((1, d), BF16)], axis=0)[slot_tok]
    w1p = _deinterleave_columns(w1.reshape(N_EXPERTS * d, -1)).reshape(w1.shape)
    b1p = jnp.concatenate([b1[..., ::2], b1[..., 1::2]], axis=-1)
    y = _moe_experts(x_slots, blk_e, n_used, w1p, b1p, w2.astype(BF16), b2)
    return jnp.sum(y[dest] * gate[:, :, None], axis=0)


def _rms(x):
    return x * lax.rsqrt(jnp.mean(x * x, axis=-1, keepdims=True) + RMS_EPS)


def _conv3(u, w):
    up = jnp.pad(u, ((0, 0), (1, 1), (0, 0)))
    return up[:, :-2] * w[0] + up[:, 1:-1] * w[1] + up[:, 2:] * w[2]


def _seg_sum(x, ones_bd):
    x_hi, x_lo = _hi_lo(x)
    return (jnp.dot(x_hi, ones_bd, preferred_element_type=F32)
            + jnp.dot(x_lo, ones_bd, preferred_element_type=F32))


def _head_ones(width):
    h = np.arange(width) // HEAD_DIM
    return jnp.asarray(h[:, None] == h[None, :], BF16)


def _rwkv7_prep_kernel(first_ref, last_ref, u_ref, up_ref, un_ref, sw_ref, w0_ref, w2_ref, a0_ref, a2_ref,
                       g2_ref, kk_ref, ka_ref, bd_ref, r_ref, v_ref, kap_ref, g_ref, lw_ref, k_ref, al_ref,
                       *, d_a, lora_off):
    i = pl.program_id(0)
    keep_prev = (first_ref[i] == 0).astype(F32)
    keep_next = (last_ref[i] == 0).astype(F32)
    rows = u_ref.shape[0]

    def conv(lo, width, tap_lo):
        x = u_ref[:, lo:lo + width]
        ridx = lax.broadcasted_iota(jnp.int32, x.shape, 0)
        x_prev = jnp.where(ridx == 0, up_ref[7:8, lo:lo + width] * keep_prev, pltpu.roll(x, 1, axis=0))
        x_next = jnp.where(ridx == rows - 1, un_ref[0:1, lo:lo + width] * keep_next,
                           pltpu.roll(x, rows - 1, axis=0))
        w = sw_ref[:, tap_lo:tap_lo + width]
        return x_prev * w[0:1] + x * w[1:2] + x_next * w[2:3]

    r = conv(0, d_a, 0)
    k = conv(d_a, d_a, d_a)
    v = conv(2 * d_a, d_a, 2 * d_a)
    wd = conv(lora_off, 2 * DECAY_LORA, 3 * d_a)
    ad = conv(lora_off + 2 * DECAY_LORA, 2 * AAA_LORA, 3 * d_a + 2 * DECAY_LORA)
    gd = conv(lora_off + 2 * DECAY_LORA + 2 * AAA_LORA, GATE_LORA, 3 * d_a + 2 * DECAY_LORA + 2 * AAA_LORA)
    mm3 = lambda p, q: _mm(p, q, "nn", 3)
    kk = k * kk_ref[...]
    kap = kk / jnp.maximum(jnp.sqrt(_seg_sum(kk * kk, bd_ref[...])), 1e-12)
    lora_w = jnp.tanh(wd)
    for d in range(2):
        z = w0_ref[d:d + 1] + mm3(lora_w[:, d * DECAY_LORA:(d + 1) * DECAY_LORA], w2_ref[d])
        lw_ref[d] = -math.exp(-0.5) * jax.nn.sigmoid(z)
        a = jax.nn.sigmoid(a0_ref[d:d + 1] + mm3(ad[:, d * AAA_LORA:(d + 1) * AAA_LORA], a2_ref[d]))
        k_ref[d] = k * (1.0 + (a - 1.0) * ka_ref[...])
        al_ref[d] = kap * a
    g_ref[...] = mm3(jax.nn.sigmoid(gd), g2_ref[...])
    r_ref[...] = r
    v_ref[...] = v
    kap_ref[...] = kap


def _segment_flags(bsz, s, n_ctx, tile):
    lat, ctx = np.arange(bsz * s // tile), np.arange(bsz * n_ctx // tile)
    first = np.concatenate([lat % (s // tile) == 0, ctx % (n_ctx // tile) == 0])
    last = np.concatenate([lat % (s // tile) == s // tile - 1, ctx % (n_ctx // tile) == n_ctx // tile - 1])
    return jnp.asarray(first, jnp.int32), jnp.asarray(last, jnp.int32)


def _rwkv7_prep(u, lora_off, bsz, s, n_ctx, shift_w, w0, w2, a0, a2, g2, k_k, k_a):
    t, cols = u.shape
    d_a = w0.shape[-1]
    tile = ROW_TILE
    halo = 8
    first, last = _segment_flags(bsz, s, n_ctx, tile)
    full = lambda arr: pl.BlockSpec(arr.shape, lambda i, f, l: (0,) * arr.ndim)
    row = lambda arr: arr[None, :]
    params = [shift_w, w0, w2, a0, a2, g2, row(k_k), row(k_a), _head_ones(d_a)]
    one = pl.BlockSpec((tile, d_a), lambda i, f, l: (i, 0))
    two = pl.BlockSpec((2, tile, d_a), lambda i, f, l: (0, i, 0))
    flat = jax.ShapeDtypeStruct((t, d_a), F32)
    both = jax.ShapeDtypeStruct((2, t, d_a), F32)
    grid_spec = pltpu.PrefetchScalarGridSpec(
        num_scalar_prefetch=2,
        grid=(t // tile,),
        in_specs=[pl.BlockSpec((tile, cols), lambda i, f, l: (i, 0)),
                  pl.BlockSpec((halo, cols), lambda i, f, l: (jnp.maximum(i * (tile // halo) - 1, 0), 0)),
                  pl.BlockSpec((halo, cols),
                               lambda i, f, l: (jnp.minimum((i + 1) * (tile // halo), t // halo - 1), 0))]
        + [full(p) for p in params],
        out_specs=[one, one, one, one, two, two, two],
    )
    return pl.pallas_call(
        functools.partial(_rwkv7_prep_kernel, d_a=d_a, lora_off=lora_off),
        grid_spec=grid_spec,
        out_shape=[flat, flat, flat, flat, both, both, both],
        compiler_params=pltpu.CompilerParams(dimension_semantics=("arbitrary",),
                                             vmem_limit_bytes=VMEM_LIMIT),
        name="rwkv7_prep",
    )(first, last, u, u, u, *params)


def _even_out_kernel(yaf_ref, yab_ref, r_ref, v_ref, k_ref, g_ref, ybf_ref, ybb_ref, og_ref, rk_ref, lnw_ref,
                     lnb_ref, nw_ref, bd_ref, w_ref, res_ref, gate_ref, o_ref):
    bd = bd_ref[...]
    inv = 1.0 / HEAD_DIM
    y = yaf_ref[...] + yab_ref[...]
    yc = y - _seg_sum(y, bd) * inv
    var = _seg_sum(yc * yc, bd) * inv
    yn = yc * lax.rsqrt(var + LN_X_EPS) * lnw_ref[...] + lnb_ref[...]
    bonus = _seg_sum(r_ref[...] * (k_ref[0] + k_ref[1]) * rk_ref[...], bd) * v_ref[...]
    out_a = (yn + bonus) * g_ref[...]
    o = ybf_ref[...] + ybb_ref[...]
    out_b = o * lax.rsqrt(_seg_sum(o * o, bd) * inv + RMS_EPS) * nw_ref[...] * jax.nn.silu(og_ref[...])
    ycat = jnp.concatenate([out_a, out_b], axis=-1).astype(BF16)
    o_ref[...] = res_ref[...] + gate_ref[0] * jnp.dot(ycat, w_ref[...], preferred_element_type=F32)


def _even_out(ya, r, v, k, g, yb, u, col_og, r_k, ln_w, ln_b, norm_w, w_out, res, gate, rows_per_mod):
    t, width = r.shape
    d = res.shape[1]
    tile = ROW_TILE
    tiles_per_mod = rows_per_mod // tile
    one = pl.BlockSpec((tile, width), lambda i: (i, 0))
    two = pl.BlockSpec((2, tile, width), lambda i: (0, i, 0))
    vec = pl.BlockSpec((1, width), lambda i: (0, 0))
    n_heads = width // HEAD_DIM
    return pl.pallas_call(
        _even_out_kernel,
        grid=(t // tile,),
        in_specs=[one, one, one, one, two, one, one, one, pl.BlockSpec((tile, width), lambda i: (i, col_og)),
                  vec, vec, vec, vec, pl.BlockSpec((width, width), lambda i: (0, 0)),
                  pl.BlockSpec(w_out.shape, lambda i: (0, 0)), pl.BlockSpec((tile, d), lambda i: (i, 0)),
                  pl.BlockSpec((1, 1, d), lambda i: (i // tiles_per_mod, 0, 0))],
        out_specs=pl.BlockSpec((tile, d), lambda i: (i, 0)),
        out_shape=jax.ShapeDtypeStruct((t, d), F32),
        compiler_params=pltpu.CompilerParams(dimension_semantics=("arbitrary",),
                                             vmem_limit_bytes=VMEM_LIMIT),
        name="even_readout_out_proj",
    )(*ya, r, v, k, g, *yb, u, r_k.reshape(1, width), ln_w[None, :], ln_b[None, :],
      jnp.tile(norm_w, n_heads)[None, :], _head_ones(width), w_out.astype(BF16), res, gate[:, None, :])


def _even_layer(tok, sh1, sc1, g1, bsz, s, n_ctx, w_in, w_out, shift_w, w0, w2, a0, a2, g2, k_k, k_a, r_k, ln_w,
                ln_b, lb, norm_w):
    d_a = w0.shape[-1]
    d_b = lb.shape[0]
    rkv = 3 * d_a
    a_cols = rkv + 2 * DECAY_LORA + 2 * AAA_LORA + GATE_LORA
    w_perm = jnp.concatenate([w_in[:, :rkv], w_in[:, a_cols:], w_in[:, rkv:a_cols]], axis=1)
    u = _mod_linear(tok, sh1, sc1, w_perm, s)
    r, v, kap, g, lw, k, al = _rwkv7_prep(u, rkv + 5 * d_b, bsz, s, n_ctx, shift_w, w0, w2, a0, a2, g2, k_k, k_a)
    ya = _rwkv7_recurrence(r, v, kap, lw, k, al, bsz, s, n_ctx)
    col0 = rkv // d_b
    yb = _hgrn2_recurrence(u, lb, col0, col0 + 1, col0 + 3, bsz, s, n_ctx)
    return _even_out(ya, r, v, k, g, yb, u, col0 + 4, r_k, ln_w, ln_b, norm_w, w_out, tok, g1, s)


def _axial_rope(x):
    bsz, length, n_heads, dh = x.shape
    t = jnp.arange(length)
    pos = jnp.stack([t // GRID_W, t % GRID_W], axis=-1).astype(F32)
    nf = dh // 4
    inv = ROPE_THETA ** (-jnp.arange(nf, dtype=F32) / nf)
    ang = pos[:, None, :, None] * inv
    cos, sin = jnp.cos(ang), jnp.sin(ang)
    xr = x.reshape(bsz, length, n_heads, 2, 2, nf)
    x1, x2 = xr[..., 0, :], xr[..., 1, :]
    out = jnp.stack([x1 * cos - x2 * sin, x1 * sin + x2 * cos], axis=-2)
    return out.reshape(bsz, length, n_heads, dh)


def _hyena_filters(length, d_d, w1, b1, fr1, w2, b2, fr2, w3):
    t = jnp.linspace(0.0, 1.0, length, dtype=F32)[:, None]
    bands = (HYENA_EMB - 1) // 2
    f = jnp.linspace(1e-4, bands - 1, bands, dtype=F32)
    ang = (2.0 * math.pi / length) * jnp.arange(length, dtype=F32)[:, None] * f
    z = jnp.concatenate([t, jnp.cos(ang), -jnp.sin(ang)], axis=-1)
    hid = jnp.sin(fr1 * (jnp.dot(z, w1, precision=_HI) + b1))
    hid = jnp.sin(fr2 * (jnp.dot(hid, w2, precision=_HI) + b2))
    h = jnp.dot(hid, w3, precision=_HI).reshape(length, HYENA_ORDER, 2, d_d)
    deltas = jnp.abs(jnp.linspace(math.log(HYENA_TARGET) / HYENA_SLOW_DECAY,
                                  math.log(HYENA_TARGET) / HYENA_FAST_DECAY, d_d, dtype=F32))
    return h * jnp.exp(-t * deltas)[:, None, None, :]


DFT_R = 128
DFT_COL_TILE = 4096
DFT_K1_TILE = 4


def _hi_lo(a):
    hi = a.astype(BF16)
    return hi, (a - hi.astype(F32)).astype(BF16)


def _mm3(f_hi, f_lo, x_hi, x_lo):
    d = lambda p, q: jnp.dot(p, q, preferred_element_type=F32)
    return d(f_hi, x_hi) + (d(f_hi, x_lo) + d(f_lo, x_hi))


def _dft_matrices(n1):
    k1 = np.arange(n1, dtype=np.float64)
    k2 = np.arange(DFT_R, dtype=np.float64)
    as_f32 = lambda a: jnp.asarray(a, F32)
    cs = lambda ang: (as_f32(np.cos(ang)), as_f32(-np.sin(ang)))
    f1 = cs(2.0 * np.pi * np.outer(k1, k1) / n1)
    f2 = cs(2.0 * np.pi * np.outer(k2, k2) / DFT_R)
    tw = cs(2.0 * np.pi * np.outer(k1, k2) / (n1 * DFT_R))
    tw = tuple(jnp.broadcast_to(t[:, :, None], (n1, DFT_R, 128)) for t in tw)
    return f1, f2, tw


def _dft_rows_kernel(x_ref, frh_ref, frl_ref, fih_ref, fil_ref, or_ref, oi_ref):
    x_hi, x_lo = _hi_lo(x_ref[0])
    or_ref[0] = _mm3(frh_ref[...], frl_ref[...], x_hi, x_lo)
    oi_ref[0] = _mm3(fih_ref[...], fil_ref[...], x_hi, x_lo)


def _dft_rows(x2d, f1):
    bsz, rows, cols = x2d.shape
    n1 = f1[0].shape[0]
    mats = [m for f in f1 for m in _hi_lo(f[:, :rows])]
    mat_spec = pl.BlockSpec((n1, rows), lambda b, j: (0, 0))
    out_spec = pl.BlockSpec((1, n1, DFT_COL_TILE), lambda b, j: (b, 0, j))
    out = jax.ShapeDtypeStruct((bsz, n1, cols), F32)
    return pl.pallas_call(
        _dft_rows_kernel,
        grid=(bsz, cols // DFT_COL_TILE),
        in_specs=[pl.BlockSpec((1, rows, DFT_COL_TILE), lambda b, j: (b, 0, j))] + [mat_spec] * 4,
        out_specs=[out_spec, out_spec],
        out_shape=[out, out],
        name="dft_rows",
    )(x2d, *mats)


def _dft_mid_kernel(*refs, conv):
    if conv:
        (ar_ref, ai_ref, tr_ref, ti_ref, kr_ref, ki_ref, frh_ref, frl_ref, fih_ref, fil_ref,
         or_ref, oi_ref) = refs
    else:
        ar_ref, ai_ref, tr_ref, ti_ref, frh_ref, frl_ref, fih_ref, fil_ref, or_ref, oi_ref = refs
    frh, frl, fih, fil = frh_ref[...], frl_ref[...], fih_ref[...], fil_ref[...]
    reps = ar_ref.shape[3] // tr_ref.shape[2]
    rows = range(ar_ref.shape[1])
    wide = lambda t: jnp.concatenate([t] * reps, axis=-1)
    tr = [wide(tr_ref[j]) for j in rows]
    ti = [wide(ti_ref[j]) for j in rows]
    ar = [ar_ref[0, j] for j in rows]
    ai = [ai_ref[0, j] for j in rows]
    pr = [_hi_lo(ar[j] * tr[j] - ai[j] * ti[j]) for j in rows]
    pi = [_hi_lo(ar[j] * ti[j] + ai[j] * tr[j]) for j in rows]
    xr = [_mm3(frh, frl, *pr[j]) - _mm3(fih, fil, *pi[j]) for j in rows]
    xi = [_mm3(frh, frl, *pi[j]) + _mm3(fih, fil, *pr[j]) for j in rows]
    if not conv:
        for j in rows:
            or_ref[0, j] = xr[j]
            oi_ref[0, j] = xi[j]
        return
    yr = [_hi_lo(xr[j] * kr_ref[0, j] - xi[j] * ki_ref[0, j]) for j in rows]
    yi = [_hi_lo(xr[j] * ki_ref[0, j] + xi[j] * kr_ref[0, j]) for j in rows]
    br = [_mm3(frh, frl, *yr[j]) + _mm3(fih, fil, *yi[j]) for j in rows]
    bi = [_mm3(frh, frl, *yi[j]) - _mm3(fih, fil, *yr[j]) for j in rows]
    for j in rows:
        or_ref[0, j] = br[j] * tr[j] + bi[j] * ti[j]
        oi_ref[0, j] = bi[j] * tr[j] - br[j] * ti[j]


def _dft_mid(a_r, a_i, f2, tw, kf=None):
    (f_r, f_i), (t_r, t_i) = f2, tw
    bsz, n1, _, ch = a_r.shape
    slab = pl.BlockSpec((1, DFT_K1_TILE, DFT_R, ch), lambda b, g: (b, g, 0, 0))
    tw = pl.BlockSpec((DFT_K1_TILE, DFT_R, t_r.shape[2]), lambda b, g: (g, 0, 0))
    mat = pl.BlockSpec((DFT_R, DFT_R), lambda b, g: (0, 0))
    args = [a_r, a_i, t_r, t_i]
    specs = [slab, slab, tw, tw]
    if kf is not None:
        kf_r, kf_i, o = kf
        kspec = pl.BlockSpec((1, DFT_K1_TILE, DFT_R, ch), lambda b, g: (o, g, 0, 0))
        args += [kf_r, kf_i]
        specs += [kspec, kspec]
    args += [m for f in (f_r, f_i) for m in _hi_lo(f)]
    specs += [mat] * 4
    out = jax.ShapeDtypeStruct(a_r.shape, F32)
    return pl.pallas_call(
        functools.partial(_dft_mid_kernel, conv=kf is not None),
        grid=(bsz, n1 // DFT_K1_TILE),
        in_specs=specs,
        out_specs=[slab, slab],
        out_shape=[out, out],
        compiler_params=pltpu.CompilerParams(dimension_semantics=("arbitrary", "arbitrary"),
                                             vmem_limit_bytes=VMEM_LIMIT),
        name="dft_mid_conv" if kf is not None else "dft_mid",
    )(*args)


def _idft_rows_kernel(br_ref, bi_ref, z_ref, g_ref, bias_ref, grh_ref, grl_ref, gih_ref, gil_ref, o_ref):
    y = (_mm3(grh_ref[...], grl_ref[...], *_hi_lo(br_ref[0]))
         + _mm3(gih_ref[...], gil_ref[...], *_hi_lo(bi_ref[0])))
    o_ref[0] = g_ref[0] * (y + z_ref[0] * bias_ref[...])


def _idft_rows(b_r, b_i, z2d, gate2d, bias_cols, f1):
    bsz, rows, cols = z2d.shape
    n1 = f1[0].shape[0]
    scale = 1.0 / (n1 * DFT_R)
    mats = [m for f in f1 for m in _hi_lo(f[:rows] * scale)]
    mat_spec = pl.BlockSpec((rows, n1), lambda b, j: (0, 0))
    in_spec = pl.BlockSpec((1, n1, DFT_COL_TILE), lambda b, j: (b, 0, j))
    io_spec = pl.BlockSpec((1, rows, DFT_COL_TILE), lambda b, j: (b, 0, j))
    return pl.pallas_call(
        _idft_rows_kernel,
        grid=(bsz, cols // DFT_COL_TILE),
        in_specs=[in_spec, in_spec, io_spec, io_spec, pl.BlockSpec((1, DFT_COL_TILE), lambda b, j: (0, j))]
        + [mat_spec] * 4,
        out_specs=io_spec,
        out_shape=jax.ShapeDtypeStruct((bsz, rows, cols), F32),
        name="idft_rows",
    )(b_r, b_i, z2d, gate2d, bias_cols, *mats)


def _hyena(u, short_w, w1, b1, fr1, w2, b2, fr2, w3, bias):
    bsz, length, _ = u.shape
    u = _conv3(u, short_w)
    v, x1, x2 = jnp.split(u, 3, axis=-1)
    ch = v.shape[-1]
    rows = length // DFT_R
    h = _hyena_filters(length, ch, w1, b1, fr1, w2, b2, fr2, w3)
    h_fwd, h_bwd = h[:, :, 0], h[:, :, 1]
    k2 = jnp.concatenate([h_fwd.at[0].add(h_bwd[0]), jnp.zeros_like(h_fwd[:1]), h_bwd[:0:-1]], axis=0)
    k2 = k2.transpose(1, 0, 2).reshape(HYENA_ORDER, 2 * rows, DFT_R * ch)
    n1 = 2 * rows
    f1, f2, tw = _dft_matrices(n1)
    four_d = lambda t: t.reshape(t.shape[0], n1, DFT_R, ch)
    kf_r, kf_i = _dft_mid(*(four_d(t) for t in _dft_rows(k2, f1)), f2, tw)
    z = v
    for o, gate in enumerate((x1, x2)):
        z2d = z.reshape(bsz, rows, DFT_R * ch)
        a_r, a_i = _dft_rows(z2d, f1)
        b_r, b_i = _dft_mid(four_d(a_r), four_d(a_i), f2, tw, kf=(kf_r, kf_i, o))
        flat = lambda t: t.reshape(bsz, n1, DFT_R * ch)
        y = _idft_rows(flat(b_r), flat(b_i), z2d, gate.reshape(z2d.shape), jnp.tile(bias[o], DFT_R)[None, :], f1)
        z = y.reshape(bsz, length, ch)
    return z


def _odd_mixer_latent(u_lat, u_ctx, q_norm, k_norm, rpb, short_w, w1, b1, fr1, w2, b2, fr2, w3, bias):
    n_heads = rpb.shape[0]
    d_c = n_heads * HEAD_DIM
    bsz, s, _ = u_lat.shape

    def qkv(t):
        length = t.shape[1]
        q, k, v = jnp.split(t[..., :3 * d_c], 3, axis=-1)
        heads = lambda a_: a_.reshape(bsz, length, n_heads, HEAD_DIM)
        return _rms(heads(q)) * q_norm, _rms(heads(k)) * k_norm, heads(v)

    q, k, v = qkv(u_lat)
    _, kc, vc = qkv(u_ctx)
    hm = lambda t: t.transpose(0, 2, 1, 3)
    o = _neighbourhood_attention(hm(_axial_rope(q) * HEAD_DIM ** -0.5), hm(_axial_rope(k)), hm(v),
                                 hm(kc), hm(vc), rpb)
    y_na = o.transpose(0, 2, 1, 3).reshape(bsz, s, d_c)
    y_hy = _hyena(u_lat[..., 3 * d_c:], short_w, w1, b1, fr1, w2, b2, fr2, w3, bias)
    return jnp.concatenate([y_na, y_hy], axis=-1)


def kernel(x, c, ctx, c_ctx, mod_w, mod_b, router_w, router_b, moe_w1, moe_b1, moe_w2, moe_b2, ab_w_in, ab_w_out, rwkv_shift, rwkv_w0, rwkv_w2, rwkv_a0, rwkv_a2, rwkv_g2, rwkv_k_k, rwkv_k_a, rwkv_r_k, rwkv_ln_w, rwkv_ln_b, hgrn_lb_logits, hgrn_norm_w, cd_w_in, cd_w_out, na_q_norm, na_k_norm, na_rpb, hy_short, hy_w1, hy_b1, hy_freq1, hy_w2, hy_b2, hy_freq2, hy_w3, hy_bias):
    bsz, s, d = x.shape
    n_ctx = ctx.shape[1]
    depth = mod_w.shape[0]
    n_lat = bsz * s
    lb_all = jnp.cumsum(jax.nn.softmax(hgrn_lb_logits, axis=0), axis=0)
    tok = jnp.concatenate([x.reshape(n_lat, d), ctx.reshape(bsz * n_ctx, d)], axis=0)
    cond = jnp.concatenate([jax.nn.silu(c), jax.nn.silu(c_ctx)[None, :],
                            jnp.zeros((8 - bsz - 1, d), F32)], axis=0)
    assert depth == 2, "layer 0 = RWKV-7 || HGRN2 with context outputs, layer 1 = attention || Hyena, latent only"
    for l in range(depth):
        j = l // 2
        mod = _small_linear(cond, mod_w[l], mod_b[l])[:bsz + 1]
        sh1, sc1, g1, sh2, sc2, g2 = jnp.split(mod, 6, axis=-1)
        if l % 2 == 0:
            tok = _even_layer(tok, sh1, sc1, g1, bsz, s, n_ctx, ab_w_in[j], ab_w_out[j], rwkv_shift[j],
                              rwkv_w0[j], rwkv_w2[j], rwkv_a0[j], rwkv_a2[j], rwkv_g2[j], rwkv_k_k[j],
                              rwkv_k_a[j], rwkv_r_k[j], rwkv_ln_w[j], rwkv_ln_b[j], lb_all[j], hgrn_norm_w[j])
        else:
            u = _mod_linear(tok, sh1, sc1, cd_w_in[j], s)
            u_lat = u[:n_lat].reshape(bsz, s, -1)
            u_ctx = u[n_lat:].reshape(bsz, n_ctx, -1)
            y_lat = _odd_mixer_latent(u_lat, u_ctx, na_q_norm[j], na_k_norm[j], na_rpb[j], hy_short[j],
                                      hy_w1[j], hy_b1[j], hy_freq1[j], hy_w2[j], hy_b2[j], hy_freq2[j],
                                      hy_w3[j], hy_bias[j])
            tok = _res_linear(y_lat.reshape(n_lat, d), cd_w_out[j], tok, g1, s)
        out = _moe(tok, sh2, sc2, s, router_w[l], router_b[l], moe_w1[l], moe_b1[l], moe_w2[l], moe_b2[l])
        tok = _gated_add(tok, out, g2, s)
    return tok[:n_lat].reshape(bsz, s, d)


def _gated_add_kernel(x_ref, y_ref, g_ref, o_ref):
    o_ref[...] = x_ref[...] + g_ref[0] * y_ref[...]


def _gated_add(x, y, gate, rows_per_mod):
    m, d = x.shape
    tiles_per_mod = rows_per_mod // ROW_TILE
    tile = pl.BlockSpec((ROW_TILE, d), lambda i: (i, 0))
    return pl.pallas_call(
        _gated_add_kernel,
        grid=(m // ROW_TILE,),
        in_specs=[tile, tile, pl.BlockSpec((1, 1, d), lambda i: (i // tiles_per_mod, 0, 0))],
        out_specs=tile,
        out_shape=jax.ShapeDtypeStruct((m, d), F32),
        name="gated_add",
    )(x, y, gate[:, None, :])
```

```python
import functools
import math

import numpy as np
import jax
import jax.numpy as jnp
from jax import lax
from jax.experimental import pallas as pl
from jax.experimental.pallas import tpu as pltpu

F32 = jnp.float32
BF16 = jnp.bfloat16

HEAD_DIM = 64
GRID_W = 64
DECAY_LORA = 64
AAA_LORA = 64
GATE_LORA = 128
LN_X_EPS = 1e-5 * HEAD_DIM
NA_ROWS = 8
NA_COLS = 16
ROPE_THETA = 10000.0
HYENA_ORDER = 2
HYENA_EMB = 33
HYENA_FAST_DECAY = 0.3
HYENA_SLOW_DECAY = 1.5
HYENA_TARGET = 1e-2
N_EXPERTS = 32
TOP_K = 4
SWIGLU_ALPHA = 1.702
SWIGLU_LIMIT = 7.0
MOE_BLOCK = 256
RMS_EPS = 1e-6

ROW_TILE = 256
REC_CHUNK = 64
REC_PASSES = 1
REC_CHAIN_PASSES = 1
VMEM_LIMIT = 56 * 1024 * 1024

_HI = lax.Precision.HIGHEST


_DIMS = {"nn": (((1,), (0,)), ((), ())), "nt": (((1,), (1,)), ((), ())), "tn": (((0,), (0,)), ((), ()))}


def _mm(a, b, form, passes):
    dims = _DIMS[form]
    if passes == 6:
        return lax.dot_general(a, b, dims, precision=_HI, preferred_element_type=F32)
    a_hi = a.astype(BF16)
    b_hi = b.astype(BF16)
    out = lax.dot_general(a_hi, b_hi, dims, preferred_element_type=F32)
    if passes == 3:
        a_lo = (a - a_hi.astype(F32)).astype(BF16)
        b_lo = (b - b_hi.astype(F32)).astype(BF16)
        out = out + (lax.dot_general(a_hi, b_lo, dims, preferred_element_type=F32)
                     + lax.dot_general(a_lo, b_hi, dims, preferred_element_type=F32))
    return out


def _dot(a, b):
    return _mm(a, b, "nn", 6)


def _mod_linear_kernel(x_ref, sh_ref, sc_ref, w_ref, o_ref):
    x = x_ref[...]
    ms = jnp.mean(x * x, axis=-1, keepdims=True)
    h = x * lax.rsqrt(ms + RMS_EPS) * (1.0 + sc_ref[0]) + sh_ref[0]
    o_ref[...] = jnp.dot(h.astype(BF16), w_ref[...], preferred_element_type=F32)


def _mod_linear(x, shift, scale, w, rows_per_mod):
    m, d = x.shape
    n = w.shape[1]
    tiles_per_mod = rows_per_mod // ROW_TILE
    mod_spec = pl.BlockSpec((1, 1, d), lambda i: (i // tiles_per_mod, 0, 0))
    return pl.pallas_call(
        _mod_linear_kernel,
        grid=(m // ROW_TILE,),
        in_specs=[pl.BlockSpec((ROW_TILE, d), lambda i: (i, 0)), mod_spec, mod_spec,
                  pl.BlockSpec((d, n), lambda i: (0, 0))],
        out_specs=pl.BlockSpec((ROW_TILE, n), lambda i: (i, 0)),
        out_shape=jax.ShapeDtypeStruct((m, n), F32),
        compiler_params=pltpu.CompilerParams(dimension_semantics=("arbitrary",),
                                             vmem_limit_bytes=VMEM_LIMIT),
        name="mod_linear",
    )(x, shift[:, None, :], scale[:, None, :], w.astype(BF16))


def _res_linear2_kernel(ya_ref, yb_ref, w_ref, res_ref, g_ref, o_ref):
    y = jnp.concatenate([ya_ref[...], yb_ref[...]], axis=-1).astype(BF16)
    o_ref[...] = res_ref[...] + g_ref[0] * jnp.dot(y, w_ref[...], preferred_element_type=F32)


def _res_linear2(ya, yb, w, res, gate, rows_per_mod):
    m, ka = ya.shape
    kb = yb.shape[1]
    n = w.shape[1]
    tiles_per_mod = rows_per_mod // ROW_TILE
    return pl.pallas_call(
        _res_linear2_kernel,
        grid=(m // ROW_TILE,),
        in_specs=[pl.BlockSpec((ROW_TILE, ka), lambda i: (i, 0)), pl.BlockSpec((ROW_TILE, kb), lambda i: (i, 0)),
                  pl.BlockSpec((ka + kb, n), lambda i: (0, 0)),
                  pl.BlockSpec((ROW_TILE, n), lambda i: (i, 0)),
                  pl.BlockSpec((1, 1, n), lambda i: (i // tiles_per_mod, 0, 0))],
        out_specs=pl.BlockSpec((ROW_TILE, n), lambda i: (i, 0)),
        out_shape=jax.ShapeDtypeStruct((m, n), F32),
        compiler_params=pltpu.CompilerParams(dimension_semantics=("arbitrary",),
                                             vmem_limit_bytes=VMEM_LIMIT),
        name="res_linear",
    )(ya, yb, w.astype(BF16), res, gate[:, None, :])


def _small_linear_kernel(x_ref, w_ref, b_ref, o_ref):
    o_ref[...] = _dot(x_ref[...], w_ref[...]) + b_ref[...]


def _small_linear(x, w, b):
    m, k = x.shape
    n = w.shape[1]
    tn = 1024 if n % 1024 == 0 else n
    return pl.pallas_call(
        _small_linear_kernel,
        grid=(n // tn,),
        in_specs=[pl.BlockSpec((m, k), lambda j: (0, 0)), pl.BlockSpec((k, tn), lambda j: (0, j)),
                  pl.BlockSpec((1, tn), lambda j: (0, j))],
        out_specs=pl.BlockSpec((m, tn), lambda j: (0, j)),
        out_shape=jax.ShapeDtypeStruct((m, n), F32),
        name="small_linear",
    )(x, w, b[None, :])


def _rec_kernel(*refs, delta, n_heads):
    n_in = 6 if delta else 3
    ins = (refs[:n_in], refs[n_in:2 * n_in])
    rest = refs[2 * n_in:]
    if delta:
        y_refs, s_ref = rest[:2], rest[2]
    else:
        lb_ref, y_refs, s_ref = rest[0], rest[1:3], rest[3]

    @pl.when(pl.program_id(1) == 0)
    def _():
        s_ref[...] = jnp.zeros_like(s_ref)

    c = ins[0][0].shape[0]
    row = lax.broadcasted_iota(jnp.int32, (c, c), 0)
    col = lax.broadcasted_iota(jnp.int32, (c, c), 1)
    mid = c // 2
    heads = range(n_heads)
    hs = lambda t: [t[:, h * HEAD_DIM:(h + 1) * HEAD_DIM] for h in heads]
    mm = functools.partial(_mm, passes=REC_PASSES)

    streams = []
    for d in range(2):
        ahead = row - col if d == 0 else col - row
        incl, strict = ahead >= 0, ahead > 0
        if delta:
            r_ref, v_ref, kap_ref, lw_ref, k_ref, al_ref = ins[d]
            r_all, v_all, lw, k = r_ref[...], v_ref[...], lw_ref[0], k_ref[0]
        else:
            q_ref, i_ref, f_ref = ins[d]
            fg = lb_ref[...] + (1.0 - lb_ref[...]) * jax.nn.sigmoid(f_ref[...])
            r_all, v_all, lw, k = jax.nn.silu(q_ref[...]), i_ref[...], jnp.log(fg), 1.0 - fg
        b = _mm(incl.astype(F32), lw, "nn", 6)
        bm = b[mid:mid + 1, :]
        tot = b[c - 1:c, :] if d == 0 else b[0:1, :]
        e_neg = jnp.exp(bm - b)
        e_end = jnp.exp(tot - b)
        st = dict(incl=incl, strict=strict, v=hs(v_all), rq=hs(r_all * jnp.exp(b - bm)), kd=hs(k * e_neg),
                  k_end=hs(k * e_end), g_mid=hs(jnp.exp(bm)), g_tot=hs(jnp.exp(tot)))
        if delta:
            al = al_ref[0]
            st.update(kq=hs(kap_ref[...] * jnp.exp(b - lw - bm)), ad=hs(al * e_neg), al_end=hs(al * e_end))
        streams.append(st)

    units = [(d, h) for d in range(2) for h in heads]
    n_u = range(len(units))
    per_head = lambda name: [streams[d][name][h] for d, h in units]
    per_dir = lambda name: [streams[d][name] for d, _ in units]
    incl, strict = per_dir("incl"), per_dir("strict")
    v, rq, kd, k_end, g_mid, g_tot = (per_head(n) for n in ("v", "rq", "kd", "k_end", "g_mid", "g_tot"))
    s0 = [s_ref[d, h] for d, h in units]
    s0m = [s0[n] * g_mid[n] for n in n_u]
    if delta:
        kq, ad, al_end = per_head("kq"), per_head("ad"), per_head("al_end")
        q2 = [jnp.concatenate([kq[n], rq[n]], axis=0) for n in n_u]
        k2 = [jnp.concatenate([kd[n], ad[n]], axis=0) for n in n_u]
        a = [mm(q2[n], k2[n], "nt") for n in n_u]
        p = [mm(q2[n], s0m[n], "nt") for n in n_u]
        z = [-(p[n][:c] + mm(jnp.where(strict[n], a[n][:c, :c], 0.0), v[n], "nn")) for n in n_u]
        mc = functools.partial(_mm, passes=REC_CHAIN_PASSES)
        m = [jnp.where(strict[n], a[n][:c, c:], 0.0) for n in n_u]
        pair = (row >> 1) == (col >> 1)
        eye = (row == col).astype(F32)
        t = [eye - jnp.where(pair, m[n], 0.0) for n in n_u]
        for lvl in range(1, int(math.log2(c))):
            off = ((row >> (lvl + 1)) == (col >> (lvl + 1))) & ((row >> lvl) != (col >> lvl))
            tc = [mc(t[n], jnp.where(off, m[n], 0.0), "nn") for n in n_u]
            t = [t[n] - mc(tc[n], t[n], "nn") for n in n_u]
        z = [mc(t[n], z[n], "nn") for n in n_u]
        y = [p[n][c:] + mm(jnp.where(incl[n], a[n][c:, :c], 0.0), v[n], "nn")
             + mm(jnp.where(incl[n], a[n][c:, c:], 0.0), z[n], "nn") for n in n_u]
        s_new = [s0[n] * g_tot[n] + mm(v[n], k_end[n], "tn") + mm(z[n], al_end[n], "tn") for n in n_u]
    else:
        a = [mm(rq[n], kd[n], "nt") for n in n_u]
        y = [mm(rq[n], s0m[n], "nt") + mm(jnp.where(incl[n], a[n], 0.0), v[n], "nn") for n in n_u]
        s_new = [s0[n] * g_tot[n] + mm(v[n], k_end[n], "tn") for n in n_u]
    for n, (d, h) in enumerate(units):
        s_ref[d, h] = s_new[n]
    for d in range(2):
        y_refs[d][...] = jnp.concatenate(y[d * n_heads:(d + 1) * n_heads], axis=-1)


def _seq_block_index(bsz, s, n_ctx, rows):
    n_c, n_l = n_ctx // rows, s // rows

    def index(d, b, i):
        back = jnp.where(i < n_c, n_c - 1 - i, n_l + 2 * n_c - 1 - i)
        pos = jnp.where(d == 0, i, back)
        return jnp.where(pos < n_c, bsz * n_l + b * n_c + pos, b * n_l + pos - n_c)

    return index


def _rec_call(args, specs, bsz, s, n_ctx, width, delta):
    c = REC_CHUNK
    n_heads = width // HEAD_DIM
    blk = _seq_block_index(bsz, s, n_ctx, c)
    out = jax.ShapeDtypeStruct((bsz * (s + n_ctx), width), F32)
    return pl.pallas_call(
        functools.partial(_rec_kernel, delta=delta, n_heads=n_heads),
        grid=(bsz, (s + n_ctx) // c),
        in_specs=specs,
        out_specs=[pl.BlockSpec((c, width), lambda b, i, d=d: (blk(d, b, i), 0)) for d in range(2)],
        out_shape=[out, out],
        scratch_shapes=[pltpu.VMEM((2, n_heads, HEAD_DIM, HEAD_DIM), F32)],
        compiler_params=pltpu.CompilerParams(dimension_semantics=("arbitrary", "arbitrary")),
        name="rwkv7_rec" if delta else "hgrn2_rec",
    )(*args)


def _rwkv7_recurrence(r, v, kap, lw, k, al, bsz, s, n_ctx):
    width = r.shape[1]
    blk = _seq_block_index(bsz, s, n_ctx, REC_CHUNK)
    specs = []
    for d in range(2):
        shared = pl.BlockSpec((REC_CHUNK, width), lambda b, i, d=d: (blk(d, b, i), 0))
        per_dir = pl.BlockSpec((1, REC_CHUNK, width), lambda b, i, d=d: (d, blk(d, b, i), 0))
        specs += [shared] * 3 + [per_dir] * 3
    return _rec_call([r, v, kap, lw, k, al] * 2, specs, bsz, s, n_ctx, width, True)


def _hgrn2_recurrence(u, lb, col_q, col_f, col_i, bsz, s, n_ctx):
    width = lb.shape[0]
    blk = _seq_block_index(bsz, s, n_ctx, REC_CHUNK)
    specs = []
    for d in range(2):
        specs += [pl.BlockSpec((REC_CHUNK, width), lambda b, i, d=d, j=j: (blk(d, b, i), j))
                  for j in (col_q, col_i, col_f + d)]
    specs.append(pl.BlockSpec((1, width), lambda b, i: (0, 0)))
    return _rec_call([u] * 6 + [lb[None, :]], specs, bsz, s, n_ctx, width, False)


def _na_kernel(q_ref, k_ref, v_ref, kc_ref, vc_ref, bias_ref, o_ref, *, rows_per_step, n_rows):
    g = pl.program_id(2)
    band = NA_ROWS * GRID_W
    n_h = q_ref.shape[1]
    units = [(h, j) for j in range(rows_per_step) for h in range(n_h)]
    nt = lambda p, q_: lax.dot_general(p, q_, (((1,), (1,)), ((), ())), preferred_element_type=F32)
    qr = [g * rows_per_step + j for j in range(rows_per_step)]
    start = [jnp.clip(r - NA_ROWS // 2, 0, n_rows - NA_ROWS) for r in qr]
    off = [pl.multiple_of(st * GRID_W, GRID_W) for st in start]
    q = [q_ref[0, h, j * GRID_W:(j + 1) * GRID_W, :] for h, j in units]
    s_win = [nt(q[n], k_ref[0, h, pl.ds(off[j], band), :]) + bias_ref[h, start[j] - qr[j] + NA_ROWS - 1]
             for n, (h, j) in enumerate(units)]
    s_ctx = [nt(q[n], kc_ref[0, h]) for n, (h, j) in enumerate(units)]
    ns = range(len(units))
    m = [jnp.maximum(jnp.max(s_win[n], axis=-1, keepdims=True), jnp.max(s_ctx[n], axis=-1, keepdims=True))
         for n in ns]
    p_win = [jnp.exp(s_win[n] - m[n]) for n in ns]
    p_ctx = [jnp.exp(s_ctx[n] - m[n]) for n in ns]
    den = [jnp.sum(p_win[n], axis=-1, keepdims=True) + jnp.sum(p_ctx[n], axis=-1, keepdims=True) for n in ns]
    o = [(jnp.dot(p_win[n].astype(BF16), v_ref[0, h, pl.ds(off[j], band), :], preferred_element_type=F32)
          + jnp.dot(p_ctx[n].astype(BF16), vc_ref[0, h], preferred_element_type=F32)) / den[n]
         for n, (h, j) in enumerate(units)]
    for j in range(rows_per_step):
        o_ref[j * GRID_W:(j + 1) * GRID_W, :] = jnp.concatenate(o[j * n_h:(j + 1) * n_h], axis=-1)


def _na_bias_table(rpb):
    cc = np.arange(GRID_W)
    col_start = np.clip(cc - NA_COLS // 2, 0, GRID_W - NA_COLS)
    kc = np.arange(GRID_W)
    inside = (kc[None, :] >= col_start[:, None]) & (kc[None, :] < col_start[:, None] + NA_COLS)
    col_off = np.clip(kc[None, :] - cc[:, None] + NA_COLS - 1, 0, 2 * NA_COLS - 2)
    tab = rpb[:, :, col_off]
    tab = jnp.where(jnp.asarray(inside)[None, None], tab, -jnp.inf)
    d0 = np.arange(NA_ROWS)[:, None] + np.arange(NA_ROWS)[None, :]
    band = tab[:, d0]
    return band.transpose(0, 1, 3, 2, 4).reshape(rpb.shape[0], NA_ROWS, GRID_W, NA_ROWS * GRID_W)


NA_HEADS_PER_STEP = 128 // HEAD_DIM


def _neighbourhood_attention(q, k, v, kc, vc, rpb):
    bsz, n_heads, s, dh = q.shape
    n_rows = s // GRID_W
    rows_per_step = 8
    hp = NA_HEADS_PER_STEP
    lc = kc.shape[2]
    bias = _na_bias_table(rpb)
    steps = n_rows // rows_per_step
    full = pl.BlockSpec((1, hp, s, dh), lambda b, h, g: (b, h, 0, 0))
    ctx = pl.BlockSpec((1, hp, lc, dh), lambda b, h, g: (b, h, 0, 0))
    return pl.pallas_call(
        functools.partial(_na_kernel, rows_per_step=rows_per_step, n_rows=n_rows),
        grid=(bsz, n_heads // hp, steps),
        in_specs=[pl.BlockSpec((1, hp, rows_per_step * GRID_W, dh), lambda b, h, g: (b, h, g, 0)),
                  full, full, ctx, ctx,
                  pl.BlockSpec((hp, NA_ROWS, GRID_W, NA_ROWS * GRID_W), lambda b, h, g: (h, 0, 0, 0))],
        out_specs=pl.BlockSpec((rows_per_step * GRID_W, hp * dh), lambda b, h, g: (b * steps + g, h)),
        out_shape=jax.ShapeDtypeStruct((bsz * s, n_heads * dh), F32),
        compiler_params=pltpu.CompilerParams(
            dimension_semantics=("arbitrary", "arbitrary", "arbitrary"), vmem_limit_bytes=VMEM_LIMIT),
        name="neighbourhood_attention",
    )(q, k, v, kc, vc, bias)


def _na_prep_kernel(u_ref, cos_ref, sin_ref, qn_ref, kn_ref, bd_ref, q_ref, k_ref, v_ref, *, n_heads):
    width = n_heads * HEAD_DIM
    bd = bd_ref[...]
    lane = lax.broadcasted_iota(jnp.int32, (u_ref.shape[0], 128), 1)
    first = (lane & (HEAD_DIM // 4)) == 0

    def rope(x):
        parts = []
        for j in range(width // 128):
            cols = slice(128 * j, 128 * (j + 1))
            xj = x[:, cols]
            partner = jnp.where(first, pltpu.roll(xj, 128 - HEAD_DIM // 4, axis=1),
                                pltpu.roll(xj, HEAD_DIM // 4, axis=1))
            parts.append(xj * cos_ref[:, cols] + partner * sin_ref[:, cols])
        return jnp.concatenate(parts, axis=-1)

    def normed(x, w_ref):
        return x * lax.rsqrt(_seg_sum(x * x, bd) * (1.0 / HEAD_DIM) + RMS_EPS) * w_ref[...]

    q = rope(normed(u_ref[:, :width], qn_ref)) * HEAD_DIM ** -0.5
    k = rope(normed(u_ref[:, width:2 * width], kn_ref))
    v = u_ref[:, 2 * width:3 * width]
    for h in range(n_heads):
        cols = slice(h * HEAD_DIM, (h + 1) * HEAD_DIM)
        q_ref[0, h] = q[:, cols].astype(BF16)
        k_ref[0, h] = k[:, cols].astype(BF16)
        v_ref[0, h] = v[:, cols].astype(BF16)


def _rope_tables(s, n_heads):
    t = jnp.arange(s)
    nf = HEAD_DIM // 4
    inv = ROPE_THETA ** (-jnp.arange(nf, dtype=F32) / nf)
    ang_r = (t // GRID_W).astype(F32)[:, None] * inv
    ang_c = (t % GRID_W).astype(F32)[:, None] * inv
    cos = jnp.concatenate([jnp.cos(ang_r)] * 2 + [jnp.cos(ang_c)] * 2, axis=-1)
    sin = jnp.concatenate([-jnp.sin(ang_r), jnp.sin(ang_r), -jnp.sin(ang_c), jnp.sin(ang_c)], axis=-1)
    return jnp.tile(cos, (1, n_heads)), jnp.tile(sin, (1, n_heads))


def _na_prep(u, bsz, s, n_heads, q_norm, k_norm):
    width = n_heads * HEAD_DIM
    tile = ROW_TILE
    per_batch = s // tile
    cos, sin = _rope_tables(s, n_heads)
    tab = pl.BlockSpec((tile, width), lambda i: (i % per_batch, 0))
    vec = pl.BlockSpec((1, width), lambda i: (0, 0))
    out_spec = pl.BlockSpec((1, n_heads, tile, HEAD_DIM), lambda i: (i // per_batch, 0, i % per_batch, 0))
    out = jax.ShapeDtypeStruct((bsz, n_heads, s, HEAD_DIM), BF16)
    return pl.pallas_call(
        functools.partial(_na_prep_kernel, n_heads=n_heads),
        grid=(bsz * per_batch,),
        in_specs=[pl.BlockSpec((tile, 3 * width), lambda i: (i, 0)), tab, tab, vec, vec,
                  pl.BlockSpec((width, width), lambda i: (0, 0))],
        out_specs=[out_spec, out_spec, out_spec],
        out_shape=[out, out, out],
        name="na_prep",
    )(u, cos, sin, jnp.tile(q_norm, n_heads)[None, :], jnp.tile(k_norm, n_heads)[None, :], _head_ones(width))


def _moe_kernel(be_ref, nb_ref, x_ref, w1_ref, b1_ref, w2_ref, b2_ref, o_ref):
    i = pl.program_id(0)

    @pl.when(i < nb_ref[0])
    def _():
        y = jnp.dot(x_ref[...], w1_ref[0], preferred_element_type=F32) + b1_ref[0]
        half = y.shape[1] // 2
        glu = jnp.minimum(y[:, :half], SWIGLU_LIMIT)
        lin = jnp.clip(y[:, half:], -SWIGLU_LIMIT, SWIGLU_LIMIT)
        act = glu * jax.nn.sigmoid(SWIGLU_ALPHA * glu) * (lin + 1.0)
        o_ref[...] = jnp.dot(act.astype(BF16), w2_ref[0], preferred_element_type=F32) + b2_ref[0]

    @pl.when(i >= nb_ref[0])
    def _():
        o_ref[...] = jnp.zeros_like(o_ref)


def _moe_experts(x_slots, blk_e, n_used, w1, b1, w2, b2):
    n_slots, d = x_slots.shape
    n_blocks = n_slots // MOE_BLOCK
    de2 = w1.shape[2]
    grid_spec = pltpu.PrefetchScalarGridSpec(
        num_scalar_prefetch=2,
        grid=(n_blocks,),
        in_specs=[pl.BlockSpec((MOE_BLOCK, d), lambda i, be, nb: (i, 0)),
                  pl.BlockSpec((1, d, de2), lambda i, be, nb: (be[i], 0, 0)),
                  pl.BlockSpec((1, 1, de2), lambda i, be, nb: (be[i], 0, 0)),
                  pl.BlockSpec((1, de2 // 2, d), lambda i, be, nb: (be[i], 0, 0)),
                  pl.BlockSpec((1, 1, d), lambda i, be, nb: (be[i], 0, 0))],
        out_specs=pl.BlockSpec((MOE_BLOCK, d), lambda i, be, nb: (i, 0)),
    )
    return pl.pallas_call(
        _moe_kernel,
        grid_spec=grid_spec,
        out_shape=jax.ShapeDtypeStruct((n_slots, d), F32),
        compiler_params=pltpu.CompilerParams(dimension_semantics=("arbitrary",),
                                             vmem_limit_bytes=VMEM_LIMIT),
        name="moe_experts",
    )(blk_e, n_used, x_slots, w1, b1[:, None, :], w2, b2[:, None, :])


DEINT_GROUP = 256


def _deinterleave_kernel(w_ref, p_ref, o_ref):
    half = o_ref.shape[1] // 2
    g_out = DEINT_GROUP // 2
    for g in range(w_ref.shape[1] // DEINT_GROUP):
        t = jnp.dot(w_ref[:, g * DEINT_GROUP:(g + 1) * DEINT_GROUP].astype(BF16), p_ref[...],
                    preferred_element_type=F32).astype(BF16)
        o_ref[:, g * g_out:(g + 1) * g_out] = t[:, :g_out]
        o_ref[:, half + g * g_out:half + (g + 1) * g_out] = t[:, g_out:]


def _deinterleave_columns(w):
    m, n = w.shape
    tm = 1024
    perm = np.zeros((DEINT_GROUP, DEINT_GROUP), np.float32)
    idx = np.arange(DEINT_GROUP // 2)
    perm[2 * idx, idx] = 1.0
    perm[2 * idx + 1, DEINT_GROUP // 2 + idx] = 1.0
    return pl.pallas_call(
        _deinterleave_kernel,
        grid=(m // tm,),
        in_specs=[pl.BlockSpec((tm, n), lambda i: (i, 0)),
                  pl.BlockSpec((DEINT_GROUP, DEINT_GROUP), lambda i: (0, 0))],
        out_specs=pl.BlockSpec((tm, n), lambda i: (i, 0)),
        out_shape=jax.ShapeDtypeStruct((m, n), BF16),
        compiler_params=pltpu.CompilerParams(dimension_semantics=("arbitrary",),
                                             vmem_limit_bytes=VMEM_LIMIT),
        name="deinterleave_columns",
    )(w, jnp.asarray(perm, BF16))


ROUTE_TILE = 1024


def _route_kernel(x_ref, sh_ref, sc_ref, wt_ref, b_ref, before_ref, h_ref, e_ref, g_ref, r_ref, cnt_ref,
                  carry_ref):
    i = pl.program_id(0)

    @pl.when(i == 0)
    def _():
        carry_ref[...] = jnp.zeros_like(carry_ref)

    x = x_ref[...]
    ms = jnp.mean(x * x, axis=-1, keepdims=True)
    h = x * lax.rsqrt(ms + RMS_EPS) * (1.0 + sc_ref[0]) + sh_ref[0]
    h_ref[...] = h.astype(BF16)
    logits = _mm(wt_ref[...], h, "nt", 6) + b_ref[...]
    n_e, tm = logits.shape
    eidx = lax.broadcasted_iota(jnp.int32, (n_e, tm), 0)
    work = logits
    top_v, top_e = [], []
    for _ in range(TOP_K):
        m = jnp.max(work, axis=0, keepdims=True)
        sel = jnp.min(jnp.where(work == m, eidx, n_e), axis=0, keepdims=True)
        top_v.append(m)
        top_e.append(sel)
        work = jnp.where(eidx == sel, -jnp.inf, work)
    ex = [jnp.exp(v - top_v[0]) for v in top_v]
    den = ex[0] + ex[1] + ex[2] + ex[3]
    g_ref[...] = jnp.concatenate([e_ / den for e_ in ex], axis=0)
    e_ref[...] = jnp.concatenate(top_e, axis=0)
    chosen = [eidx == sel for sel in top_e]
    ind = sum(c.astype(F32) for c in chosen)
    carry = carry_ref[...]
    cnt = (jnp.dot(ind.astype(BF16), before_ref[...], preferred_element_type=F32)
           + jnp.concatenate([carry] * (tm // carry.shape[1]), axis=1))
    r_ref[...] = jnp.concatenate(
        [jnp.sum(jnp.where(c, cnt, 0.0), axis=0, keepdims=True) for c in chosen], axis=0).astype(jnp.int32)
    carry = carry + jnp.sum(ind, axis=1, keepdims=True)
    carry_ref[...] = carry
    cnt_ref[...] = carry


def _route(tok, shift, scale, router_w, router_b, rows_per_mod):
    t, d = tok.shape
    n_e = router_w.shape[1]
    tm = ROUTE_TILE
    tiles_per_mod = rows_per_mod // tm
    mod_spec = pl.BlockSpec((1, 1, d), lambda i: (i // tiles_per_mod, 0, 0))
    before = jnp.asarray(np.triu(np.ones((tm, tm), np.float32), 1), BF16)
    kt_spec = pl.BlockSpec((TOP_K, tm), lambda i: (0, i))
    h, top_e, gate, rank, cnt = pl.pallas_call(
        _route_kernel,
        grid=(t // tm,),
        in_specs=[pl.BlockSpec((tm, d), lambda i: (i, 0)), mod_spec, mod_spec,
                  pl.BlockSpec((n_e, d), lambda i: (0, 0)), pl.BlockSpec((n_e, tm), lambda i: (0, 0)),
                  pl.BlockSpec((tm, tm), lambda i: (0, 0))],
        out_specs=[pl.BlockSpec((tm, d), lambda i: (i, 0)), kt_spec, kt_spec, kt_spec,
                   pl.BlockSpec((n_e, 128), lambda i: (0, 0))],
        out_shape=[jax.ShapeDtypeStruct((t, d), BF16), jax.ShapeDtypeStruct((TOP_K, t), jnp.int32),
                   jax.ShapeDtypeStruct((TOP_K, t), F32), jax.ShapeDtypeStruct((TOP_K, t), jnp.int32),
                   jax.ShapeDtypeStruct((n_e, 128), F32)],
        scratch_shapes=[pltpu.VMEM((n_e, 128), F32)],
        compiler_params=pltpu.CompilerParams(dimension_semantics=("arbitrary",),
                                             vmem_limit_bytes=VMEM_LIMIT),
        name="moe_route",
    )(tok, shift[:, None, :], scale[:, None, :], router_w.T, jnp.broadcast_to(router_b[:, None], (n_e, tm)),
      before)
    return h, top_e, gate, rank, cnt[:, 0].astype(jnp.int32)


def _moe(tok, shift, scale, rows_per_mod, router_w, router_b, w1, b1, w2, b2):
    t, d = tok.shape
    h, top_e, gate, rank, counts = _route(tok, shift, scale, router_w, router_b, rows_per_mod)
    n_assign = t * TOP_K
    n_blocks = -(-n_assign // MOE_BLOCK) + N_EXPERTS
    n_slots = n_blocks * MOE_BLOCK
    padded = (counts + MOE_BLOCK - 1) // MOE_BLOCK * MOE_BLOCK
    start = jnp.cumsum(counts) - counts
    p_end = jnp.cumsum(padded)
    p_start = p_end - padded
    experts = jnp.arange(N_EXPERTS, dtype=jnp.int32)
    dest = jnp.sum(jnp.where(top_e[:, :, None] == experts, p_start, 0), axis=-1) + rank
    blk_first = jnp.arange(n_blocks, dtype=jnp.int32) * MOE_BLOCK
    blk_e = jnp.minimum(jnp.sum(p_end[None, :] <= blk_first[:, None], axis=1), N_EXPERTS - 1).astype(jnp.int32)
    n_used = (p_end[-1] // MOE_BLOCK).astype(jnp.int32).reshape(1)
    order = jnp.argsort(top_e.T.reshape(-1))
    per_slot = lambda per_expert: jnp.repeat(per_expert[blk_e], MOE_BLOCK)
    j = jnp.arange(n_slots, dtype=jnp.int32) - per_slot(p_start)
    src = jnp.clip(per_slot(start) + j, 0, n_assign - 1)
    slot_tok = jnp.where(j < per_slot(counts), order[src] // TOP_K, t).astype(jnp.int32)
    x_slots = jnp.concatenate([h, jnp.zeros((1, d), BF16)], axis=0)[slot_tok]
    w1p = _deinterleave_columns(w1.reshape(N_EXPERTS * d, -1)).reshape(w1.shape)
    b1p = jnp.concatenate([b1[..., ::2], b1[..., 1::2]], axis=-1)
    y = _moe_experts(x_slots, blk_e, n_used, w1p, b1p, w2.astype(BF16), b2)
    return y[dest], gate


def _rms(x):
    return x * lax.rsqrt(jnp.mean(x * x, axis=-1, keepdims=True) + RMS_EPS)


def _conv3(u, w):
    up = jnp.pad(u, ((0, 0), (1, 1), (0, 0)))
    return up[:, :-2] * w[0] + up[:, 1:-1] * w[1] + up[:, 2:] * w[2]


def _seg_sum(x, ones_bd):
    x_hi, x_lo = _hi_lo(x)
    return (jnp.dot(x_hi, ones_bd, preferred_element_type=F32)
            + jnp.dot(x_lo, ones_bd, preferred_element_type=F32))


def _head_ones(width):
    h = np.arange(width) // HEAD_DIM
    return jnp.asarray(h[:, None] == h[None, :], BF16)


def _rwkv7_prep_kernel(first_ref, last_ref, u_ref, up_ref, un_ref, sw_ref, w0_ref, w2_ref, a0_ref, a2_ref,
                       g2_ref, kk_ref, ka_ref, bd_ref, r_ref, v_ref, kap_ref, g_ref, lw_ref, k_ref, al_ref,
                       *, d_a, lora_off):
    i = pl.program_id(0)
    keep_prev = (first_ref[i] == 0).astype(F32)
    keep_next = (last_ref[i] == 0).astype(F32)
    rows = u_ref.shape[0]

    def conv(lo, width, tap_lo):
        x = u_ref[:, lo:lo + width]
        ridx = lax.broadcasted_iota(jnp.int32, x.shape, 0)
        x_prev = jnp.where(ridx == 0, up_ref[7:8, lo:lo + width] * keep_prev, pltpu.roll(x, 1, axis=0))
        x_next = jnp.where(ridx == rows - 1, un_ref[0:1, lo:lo + width] * keep_next,
                           pltpu.roll(x, rows - 1, axis=0))
        w = sw_ref[:, tap_lo:tap_lo + width]
        return x_prev * w[0:1] + x * w[1:2] + x_next * w[2:3]

    r = conv(0, d_a, 0)
    k = conv(d_a, d_a, d_a)
    v = conv(2 * d_a, d_a, 2 * d_a)
    wd = conv(lora_off, 2 * DECAY_LORA, 3 * d_a)
    ad = conv(lora_off + 2 * DECAY_LORA, 2 * AAA_LORA, 3 * d_a + 2 * DECAY_LORA)
    gd = conv(lora_off + 2 * DECAY_LORA + 2 * AAA_LORA, GATE_LORA, 3 * d_a + 2 * DECAY_LORA + 2 * AAA_LORA)
    mm3 = lambda p, q: _mm(p, q, "nn", 3)
    kk = k * kk_ref[...]
    kap = kk / jnp.maximum(jnp.sqrt(_seg_sum(kk * kk, bd_ref[...])), 1e-12)
    lora_w = jnp.tanh(wd)
    for d in range(2):
        z = w0_ref[d:d + 1] + mm3(lora_w[:, d * DECAY_LORA:(d + 1) * DECAY_LORA], w2_ref[d])
        lw_ref[d] = -math.exp(-0.5) * jax.nn.sigmoid(z)
        a = jax.nn.sigmoid(a0_ref[d:d + 1] + mm3(ad[:, d * AAA_LORA:(d + 1) * AAA_LORA], a2_ref[d]))
        k_ref[d] = k * (1.0 + (a - 1.0) * ka_ref[...])
        al_ref[d] = kap * a
    g_ref[...] = mm3(jax.nn.sigmoid(gd), g2_ref[...])
    r_ref[...] = r
    v_ref[...] = v
    kap_ref[...] = kap


def _segment_flags(bsz, s, n_ctx, tile):
    lat, ctx = np.arange(bsz * s // tile), np.arange(bsz * n_ctx // tile)
    first = np.concatenate([lat % (s // tile) == 0, ctx % (n_ctx // tile) == 0])
    last = np.concatenate([lat % (s // tile) == s // tile - 1, ctx % (n_ctx // tile) == n_ctx // tile - 1])
    return jnp.asarray(first, jnp.int32), jnp.asarray(last, jnp.int32)


def _rwkv7_prep(u, lora_off, bsz, s, n_ctx, shift_w, w0, w2, a0, a2, g2, k_k, k_a):
    t, cols = u.shape
    d_a = w0.shape[-1]
    tile = ROW_TILE
    halo = 8
    first, last = _segment_flags(bsz, s, n_ctx, tile)
    full = lambda arr: pl.BlockSpec(arr.shape, lambda i, f, l: (0,) * arr.ndim)
    row = lambda arr: arr[None, :]
    params = [shift_w, w0, w2, a0, a2, g2, row(k_k), row(k_a), _head_ones(d_a)]
    one = pl.BlockSpec((tile, d_a), lambda i, f, l: (i, 0))
    two = pl.BlockSpec((2, tile, d_a), lambda i, f, l: (0, i, 0))
    flat = jax.ShapeDtypeStruct((t, d_a), F32)
    both = jax.ShapeDtypeStruct((2, t, d_a), F32)
    grid_spec = pltpu.PrefetchScalarGridSpec(
        num_scalar_prefetch=2,
        grid=(t // tile,),
        in_specs=[pl.BlockSpec((tile, cols), lambda i, f, l: (i, 0)),
                  pl.BlockSpec((halo, cols), lambda i, f, l: (jnp.maximum(i * (tile // halo) - 1, 0), 0)),
                  pl.BlockSpec((halo, cols),
                               lambda i, f, l: (jnp.minimum((i + 1) * (tile // halo), t // halo - 1), 0))]
        + [full(p) for p in params],
        out_specs=[one, one, one, one, two, two, two],
    )
    return pl.pallas_call(
        functools.partial(_rwkv7_prep_kernel, d_a=d_a, lora_off=lora_off),
        grid_spec=grid_spec,
        out_shape=[flat, flat, flat, flat, both, both, both],
        compiler_params=pltpu.CompilerParams(dimension_semantics=("arbitrary",),
                                             vmem_limit_bytes=VMEM_LIMIT),
        name="rwkv7_prep",
    )(first, last, u, u, u, *params)


def _even_out_kernel(yaf_ref, yab_ref, r_ref, v_ref, k_ref, g_ref, ybf_ref, ybb_ref, og_ref, rk_ref, lnw_ref,
                     lnb_ref, nw_ref, bd_ref, w_ref, res_ref, gate_ref, o_ref):
    bd = bd_ref[...]
    inv = 1.0 / HEAD_DIM
    y = yaf_ref[...] + yab_ref[...]
    yc = y - _seg_sum(y, bd) * inv
    var = _seg_sum(yc * yc, bd) * inv
    yn = yc * lax.rsqrt(var + LN_X_EPS) * lnw_ref[...] + lnb_ref[...]
    bonus = _seg_sum(r_ref[...] * (k_ref[0] + k_ref[1]) * rk_ref[...], bd) * v_ref[...]
    out_a = (yn + bonus) * g_ref[...]
    o = ybf_ref[...] + ybb_ref[...]
    out_b = o * lax.rsqrt(_seg_sum(o * o, bd) * inv + RMS_EPS) * nw_ref[...] * jax.nn.silu(og_ref[...])
    ycat = jnp.concatenate([out_a, out_b], axis=-1).astype(BF16)
    o_ref[...] = res_ref[...] + gate_ref[0] * jnp.dot(ycat, w_ref[...], preferred_element_type=F32)


def _even_out(ya, r, v, k, g, yb, u, col_og, r_k, ln_w, ln_b, norm_w, w_out, res, gate, rows_per_mod):
    t, width = r.shape
    d = res.shape[1]
    tile = ROW_TILE
    tiles_per_mod = rows_per_mod // tile
    one = pl.BlockSpec((tile, width), lambda i: (i, 0))
    two = pl.BlockSpec((2, tile, width), lambda i: (0, i, 0))
    vec = pl.BlockSpec((1, width), lambda i: (0, 0))
    n_heads = width // HEAD_DIM
    return pl.pallas_call(
        _even_out_kernel,
        grid=(t // tile,),
        in_specs=[one, one, one, one, two, one, one, one, pl.BlockSpec((tile, width), lambda i: (i, col_og)),
                  vec, vec, vec, vec, pl.BlockSpec((width, width), lambda i: (0, 0)),
                  pl.BlockSpec(w_out.shape, lambda i: (0, 0)), pl.BlockSpec((tile, d), lambda i: (i, 0)),
                  pl.BlockSpec((1, 1, d), lambda i: (i // tiles_per_mod, 0, 0))],
        out_specs=pl.BlockSpec((tile, d), lambda i: (i, 0)),
        out_shape=jax.ShapeDtypeStruct((t, d), F32),
        compiler_params=pltpu.CompilerParams(dimension_semantics=("arbitrary",),
                                             vmem_limit_bytes=VMEM_LIMIT),
        name="even_readout_out_proj",
    )(*ya, r, v, k, g, *yb, u, r_k.reshape(1, width), ln_w[None, :], ln_b[None, :],
      jnp.tile(norm_w, n_heads)[None, :], _head_ones(width), w_out.astype(BF16), res, gate[:, None, :])


def _even_layer(tok, sh1, sc1, g1, bsz, s, n_ctx, w_in, w_out, shift_w, w0, w2, a0, a2, g2, k_k, k_a, r_k, ln_w,
                ln_b, lb, norm_w):
    d_a = w0.shape[-1]
    d_b = lb.shape[0]
    rkv = 3 * d_a
    a_cols = rkv + 2 * DECAY_LORA + 2 * AAA_LORA + GATE_LORA
    w_perm = jnp.concatenate([w_in[:, :rkv], w_in[:, a_cols:], w_in[:, rkv:a_cols]], axis=1)
    u = _mod_linear(tok, sh1, sc1, w_perm, s)
    r, v, kap, g, lw, k, al = _rwkv7_prep(u, rkv + 5 * d_b, bsz, s, n_ctx, shift_w, w0, w2, a0, a2, g2, k_k, k_a)
    ya = _rwkv7_recurrence(r, v, kap, lw, k, al, bsz, s, n_ctx)
    col0 = rkv // d_b
    yb = _hgrn2_recurrence(u, lb, col0, col0 + 1, col0 + 3, bsz, s, n_ctx)
    return _even_out(ya, r, v, k, g, yb, u, col0 + 4, r_k, ln_w, ln_b, norm_w, w_out, tok, g1, s)


def _hyena_filters(length, d_d, w1, b1, fr1, w2, b2, fr2, w3):
    t = jnp.linspace(0.0, 1.0, length, dtype=F32)[:, None]
    bands = (HYENA_EMB - 1) // 2
    f = jnp.linspace(1e-4, bands - 1, bands, dtype=F32)
    ang = (2.0 * math.pi / length) * jnp.arange(length, dtype=F32)[:, None] * f
    z = jnp.concatenate([t, jnp.cos(ang), -jnp.sin(ang)], axis=-1)
    hid = jnp.sin(fr1 * (jnp.dot(z, w1, precision=_HI) + b1))
    hid = jnp.sin(fr2 * (jnp.dot(hid, w2, precision=_HI) + b2))
    h = jnp.dot(hid, w3, precision=_HI).reshape(length, HYENA_ORDER, 2, d_d)
    deltas = jnp.abs(jnp.linspace(math.log(HYENA_TARGET) / HYENA_SLOW_DECAY,
                                  math.log(HYENA_TARGET) / HYENA_FAST_DECAY, d_d, dtype=F32))
    return h * jnp.exp(-t * deltas)[:, None, None, :]


DFT_R = 128
DFT_COL_TILE = 4096
DFT_K1_TILE = 4


def _hi_lo(a):
    hi = a.astype(BF16)
    return hi, (a - hi.astype(F32)).astype(BF16)


def _mm3(f_hi, f_lo, x_hi, x_lo):
    d = lambda p, q: jnp.dot(p, q, preferred_element_type=F32)
    return d(f_hi, x_hi) + (d(f_hi, x_lo) + d(f_lo, x_hi))


def _dft_matrices(n1):
    k1 = np.arange(n1, dtype=np.float64)
    k2 = np.arange(DFT_R, dtype=np.float64)
    as_f32 = lambda a: jnp.asarray(a, F32)
    cs = lambda ang: (as_f32(np.cos(ang)), as_f32(-np.sin(ang)))
    f1 = cs(2.0 * np.pi * np.outer(k1, k1) / n1)
    f2 = cs(2.0 * np.pi * np.outer(k2, k2) / DFT_R)
    tw = cs(2.0 * np.pi * np.outer(k1, k2) / (n1 * DFT_R))
    tw = tuple(jnp.broadcast_to(t[:, :, None], (n1, DFT_R, 128)) for t in tw)
    return f1, f2, tw


def _dft_rows_kernel(x_ref, frh_ref, frl_ref, fih_ref, fil_ref, or_ref, oi_ref):
    x_hi, x_lo = _hi_lo(x_ref[0])
    or_ref[0] = _mm3(frh_ref[...], frl_ref[...], x_hi, x_lo)
    oi_ref[0] = _mm3(fih_ref[...], fil_ref[...], x_hi, x_lo)


def _dft_rows(x2d, f1):
    bsz, rows, cols = x2d.shape
    n1 = f1[0].shape[0]
    mats = [m for f in f1 for m in _hi_lo(f[:, :rows])]
    mat_spec = pl.BlockSpec((n1, rows), lambda b, j: (0, 0))
    out_spec = pl.BlockSpec((1, n1, DFT_COL_TILE), lambda b, j: (b, 0, j))
    out = jax.ShapeDtypeStruct((bsz, n1, cols), F32)
    return pl.pallas_call(
        _dft_rows_kernel,
        grid=(bsz, cols // DFT_COL_TILE),
        in_specs=[pl.BlockSpec((1, rows, DFT_COL_TILE), lambda b, j: (b, 0, j))] + [mat_spec] * 4,
        out_specs=[out_spec, out_spec],
        out_shape=[out, out],
        name="dft_rows",
    )(x2d, *mats)


def _dft_mid_kernel(*refs, conv):
    if conv:
        (ar_ref, ai_ref, tr_ref, ti_ref, kr_ref, ki_ref, frh_ref, frl_ref, fih_ref, fil_ref,
         or_ref, oi_ref) = refs
    else:
        ar_ref, ai_ref, tr_ref, ti_ref, frh_ref, frl_ref, fih_ref, fil_ref, or_ref, oi_ref = refs
    frh, frl, fih, fil = frh_ref[...], frl_ref[...], fih_ref[...], fil_ref[...]
    reps = ar_ref.shape[3] // tr_ref.shape[2]
    rows = range(ar_ref.shape[1])
    wide = lambda t: jnp.concatenate([t] * reps, axis=-1)
    tr = [wide(tr_ref[j]) for j in rows]
    ti = [wide(ti_ref[j]) for j in rows]
    ar = [ar_ref[0, j] for j in rows]
    ai = [ai_ref[0, j] for j in rows]
    pr = [_hi_lo(ar[j] * tr[j] - ai[j] * ti[j]) for j in rows]
    pi = [_hi_lo(ar[j] * ti[j] + ai[j] * tr[j]) for j in rows]
    xr = [_mm3(frh, frl, *pr[j]) - _mm3(fih, fil, *pi[j]) for j in rows]
    xi = [_mm3(frh, frl, *pi[j]) + _mm3(fih, fil, *pr[j]) for j in rows]
    if not conv:
        for j in rows:
            or_ref[0, j] = xr[j]
            oi_ref[0, j] = xi[j]
        return
    yr = [_hi_lo(xr[j] * kr_ref[0, j] - xi[j] * ki_ref[0, j]) for j in rows]
    yi = [_hi_lo(xr[j] * ki_ref[0, j] + xi[j] * kr_ref[0, j]) for j in rows]
    br = [_mm3(frh, frl, *yr[j]) + _mm3(fih, fil, *yi[j]) for j in rows]
    bi = [_mm3(frh, frl, *yi[j]) - _mm3(fih, fil, *yr[j]) for j in rows]
    for j in rows:
        or_ref[0, j] = br[j] * tr[j] + bi[j] * ti[j]
        oi_ref[0, j] = bi[j] * tr[j] - br[j] * ti[j]


def _dft_mid(a_r, a_i, f2, tw, kf=None):
    (f_r, f_i), (t_r, t_i) = f2, tw
    bsz, n1, _, ch = a_r.shape
    slab = pl.BlockSpec((1, DFT_K1_TILE, DFT_R, ch), lambda b, g: (b, g, 0, 0))
    tw = pl.BlockSpec((DFT_K1_TILE, DFT_R, t_r.shape[2]), lambda b, g: (g, 0, 0))
    mat = pl.BlockSpec((DFT_R, DFT_R), lambda b, g: (0, 0))
    args = [a_r, a_i, t_r, t_i]
    specs = [slab, slab, tw, tw]
    if kf is not None:
        kf_r, kf_i, o = kf
        kspec = pl.BlockSpec((1, DFT_K1_TILE, DFT_R, ch), lambda b, g: (o, g, 0, 0))
        args += [kf_r, kf_i]
        specs += [kspec, kspec]
    args += [m for f in (f_r, f_i) for m in _hi_lo(f)]
    specs += [mat] * 4
    out = jax.ShapeDtypeStruct(a_r.shape, F32)
    return pl.pallas_call(
        functools.partial(_dft_mid_kernel, conv=kf is not None),
        grid=(bsz, n1 // DFT_K1_TILE),
        in_specs=specs,
        out_specs=[slab, slab],
        out_shape=[out, out],
        compiler_params=pltpu.CompilerParams(dimension_semantics=("arbitrary", "arbitrary"),
                                             vmem_limit_bytes=VMEM_LIMIT),
        name="dft_mid_conv" if kf is not None else "dft_mid",
    )(*args)


def _idft_rows_kernel(br_ref, bi_ref, z_ref, g_ref, bias_ref, grh_ref, grl_ref, gih_ref, gil_ref, o_ref):
    y = (_mm3(grh_ref[...], grl_ref[...], *_hi_lo(br_ref[0]))
         + _mm3(gih_ref[...], gil_ref[...], *_hi_lo(bi_ref[0])))
    o_ref[0] = g_ref[0] * (y + z_ref[0] * bias_ref[...])


def _idft_rows(b_r, b_i, z2d, gate2d, bias_cols, f1):
    bsz, rows, cols = z2d.shape
    n1 = f1[0].shape[0]
    scale = 1.0 / (n1 * DFT_R)
    mats = [m for f in f1 for m in _hi_lo(f[:rows] * scale)]
    mat_spec = pl.BlockSpec((rows, n1), lambda b, j: (0, 0))
    in_spec = pl.BlockSpec((1, n1, DFT_COL_TILE), lambda b, j: (b, 0, j))
    io_spec = pl.BlockSpec((1, rows, DFT_COL_TILE), lambda b, j: (b, 0, j))
    return pl.pallas_call(
        _idft_rows_kernel,
        grid=(bsz, cols // DFT_COL_TILE),
        in_specs=[in_spec, in_spec, io_spec, io_spec, pl.BlockSpec((1, DFT_COL_TILE), lambda b, j: (0, j))]
        + [mat_spec] * 4,
        out_specs=io_spec,
        out_shape=jax.ShapeDtypeStruct((bsz, rows, cols), F32),
        name="idft_rows",
    )(b_r, b_i, z2d, gate2d, bias_cols, *mats)


def _hyena(u, short_w, w1, b1, fr1, w2, b2, fr2, w3, bias):
    bsz, length, _ = u.shape
    u = _conv3(u, short_w)
    v, x1, x2 = jnp.split(u, 3, axis=-1)
    ch = v.shape[-1]
    rows = length // DFT_R
    h = _hyena_filters(length, ch, w1, b1, fr1, w2, b2, fr2, w3)
    h_fwd, h_bwd = h[:, :, 0], h[:, :, 1]
    k2 = jnp.concatenate([h_fwd.at[0].add(h_bwd[0]), jnp.zeros_like(h_fwd[:1]), h_bwd[:0:-1]], axis=0)
    k2 = k2.transpose(1, 0, 2).reshape(HYENA_ORDER, 2 * rows, DFT_R * ch)
    n1 = 2 * rows
    f1, f2, tw = _dft_matrices(n1)
    four_d = lambda t: t.reshape(t.shape[0], n1, DFT_R, ch)
    kf_r, kf_i = _dft_mid(*(four_d(t) for t in _dft_rows(k2, f1)), f2, tw)
    z = v
    for o, gate in enumerate((x1, x2)):
        z2d = z.reshape(bsz, rows, DFT_R * ch)
        a_r, a_i = _dft_rows(z2d, f1)
        b_r, b_i = _dft_mid(four_d(a_r), four_d(a_i), f2, tw, kf=(kf_r, kf_i, o))
        flat = lambda t: t.reshape(bsz, n1, DFT_R * ch)
        y = _idft_rows(flat(b_r), flat(b_i), z2d, gate.reshape(z2d.shape), jnp.tile(bias[o], DFT_R)[None, :], f1)
        z = y.reshape(bsz, length, ch)
    return z


def _odd_layer(tok, sh1, sc1, g1, bsz, s, n_ctx, w_in, w_out, q_norm, k_norm, rpb, short_w, w1, b1, fr1, w2,
               b2, fr2, w3, bias):
    n_heads = rpb.shape[0]
    d_c = n_heads * HEAD_DIM
    n_lat = bsz * s
    u = _mod_linear(tok, sh1, sc1, w_in, s)
    q, k, v = _na_prep(u, bsz, s, n_heads, q_norm, k_norm)
    u_ctx = u[n_lat:, d_c:3 * d_c].reshape(bsz, n_ctx, 2, n_heads, HEAD_DIM)
    head_major = lambda t: t.transpose(0, 2, 1, 3).astype(BF16)
    kc = head_major(_rms(u_ctx[:, :, 0]) * k_norm)
    vc = head_major(u_ctx[:, :, 1])
    y_na = _neighbourhood_attention(q, k, v, kc, vc, rpb)
    y_hy = _hyena(u[:n_lat, 3 * d_c:].reshape(bsz, s, -1), short_w, w1, b1, fr1, w2, b2, fr2, w3, bias)
    return _res_linear2(y_na, y_hy.reshape(n_lat, -1), w_out, tok, g1, s)


def kernel(x, c, ctx, c_ctx, mod_w, mod_b, router_w, router_b, moe_w1, moe_b1, moe_w2, moe_b2, ab_w_in, ab_w_out, rwkv_shift, rwkv_w0, rwkv_w2, rwkv_a0, rwkv_a2, rwkv_g2, rwkv_k_k, rwkv_k_a, rwkv_r_k, rwkv_ln_w, rwkv_ln_b, hgrn_lb_logits, hgrn_norm_w, cd_w_in, cd_w_out, na_q_norm, na_k_norm, na_rpb, hy_short, hy_w1, hy_b1, hy_freq1, hy_w2, hy_b2, hy_freq2, hy_w3, hy_bias):
    bsz, s, d = x.shape
    n_ctx = ctx.shape[1]
    depth = mod_w.shape[0]
    n_lat = bsz * s
    lb_all = jnp.cumsum(jax.nn.softmax(hgrn_lb_logits, axis=0), axis=0)
    tok = jnp.concatenate([x.reshape(n_lat, d), ctx.reshape(bsz * n_ctx, d)], axis=0)
    cond = jnp.concatenate([jax.nn.silu(c), jax.nn.silu(c_ctx)[None, :],
                            jnp.zeros((8 - bsz - 1, d), F32)], axis=0)
    assert depth == 2, "layer 0 = RWKV-7 || HGRN2 with context outputs, layer 1 = attention || Hyena, latent only"
    for l in range(depth):
        j = l // 2
        mod = _small_linear(cond, mod_w[l], mod_b[l])[:bsz + 1]
        sh1, sc1, g1, sh2, sc2, g2 = jnp.split(mod, 6, axis=-1)
        if l % 2 == 0:
            tok = _even_layer(tok, sh1, sc1, g1, bsz, s, n_ctx, ab_w_in[j], ab_w_out[j], rwkv_shift[j],
                              rwkv_w0[j], rwkv_w2[j], rwkv_a0[j], rwkv_a2[j], rwkv_g2[j], rwkv_k_k[j],
                              rwkv_k_a[j], rwkv_r_k[j], rwkv_ln_w[j], rwkv_ln_b[j], lb_all[j], hgrn_norm_w[j])
        else:
            tok = _odd_layer(tok, sh1, sc1, g1, bsz, s, n_ctx, cd_w_in[j], cd_w_out[j], na_q_norm[j], na_k_norm[j],
                             na_rpb[j], hy_short[j], hy_w1[j], hy_b1[j], hy_freq1[j], hy_w2[j], hy_b2[j],
                             hy_freq2[j], hy_w3[j], hy_bias[j])
        yk, gate = _moe(tok, sh2, sc2, s, router_w[l], router_b[l], moe_w1[l], moe_b1[l], moe_w2[l], moe_b2[l])
        tok = _combine(tok, yk, gate, g2, s)
    return tok[:n_lat].reshape(bsz, s, d)


def _combine_kernel(x_ref, y_ref, w_ref, g_ref, o_ref):
    reps = x_ref.shape[1] // w_ref.shape[2]
    acc = y_ref[0] * jnp.concatenate([w_ref[0]] * reps, axis=-1)
    for k in range(1, y_ref.shape[0]):
        acc = acc + y_ref[k] * jnp.concatenate([w_ref[k]] * reps, axis=-1)
    o_ref[...] = x_ref[...] + g_ref[0] * acc


def _combine(x, yk, gate, res_gate, rows_per_mod):
    m, d = x.shape
    n_k = yk.shape[0]
    tiles_per_mod = rows_per_mod // ROW_TILE
    tile = pl.BlockSpec((ROW_TILE, d), lambda i: (i, 0))
    lanes = 128
    return pl.pallas_call(
        _combine_kernel,
        grid=(m // ROW_TILE,),
        in_specs=[tile, pl.BlockSpec((n_k, ROW_TILE, d), lambda i: (0, i, 0)),
                  pl.BlockSpec((n_k, ROW_TILE, lanes), lambda i: (0, i, 0)),
                  pl.BlockSpec((1, 1, d), lambda i: (i // tiles_per_mod, 0, 0))],
        out_specs=tile,
        out_shape=jax.ShapeDtypeStruct((m, d), F32),
        name="moe_combine",
    )(x, yk, jnp.broadcast_to(gate[:, :, None], gate.shape + (lanes,)), res_gate[:, None, :])
```

```python
import functools
import math

import numpy as np
import jax
import jax.numpy as jnp
from jax import lax
from jax.experimental import pallas as pl
from jax.experimental.pallas import tpu as pltpu

F32 = jnp.float32
BF16 = jnp.bfloat16

HEAD_DIM = 64
GRID_W = 64
DECAY_LORA = 64
AAA_LORA = 64
GATE_LORA = 128
LN_X_EPS = 1e-5 * HEAD_DIM
NA_ROWS = 8
NA_COLS = 16
ROPE_THETA = 10000.0
HYENA_ORDER = 2
HYENA_EMB = 33
HYENA_FAST_DECAY = 0.3
HYENA_SLOW_DECAY = 1.5
HYENA_TARGET = 1e-2
N_EXPERTS = 32
TOP_K = 4
SWIGLU_ALPHA = 1.702
SWIGLU_LIMIT = 7.0
MOE_BLOCK = 256
RMS_EPS = 1e-6

ROW_TILE = 256
REC_CHUNK = 64
REC_PASSES = 1
REC_CHAIN_PASSES = 1
VMEM_LIMIT = 56 * 1024 * 1024

_HI = lax.Precision.HIGHEST


_DIMS = {"nn": (((1,), (0,)), ((), ())), "nt": (((1,), (1,)), ((), ())), "tn": (((0,), (0,)), ((), ()))}


def _mm(a, b, form, passes):
    dims = _DIMS[form]
    if passes == 6:
        return lax.dot_general(a, b, dims, precision=_HI, preferred_element_type=F32)
    a_hi = a.astype(BF16)
    b_hi = b.astype(BF16)
    out = lax.dot_general(a_hi, b_hi, dims, preferred_element_type=F32)
    if passes == 3:
        a_lo = (a - a_hi.astype(F32)).astype(BF16)
        b_lo = (b - b_hi.astype(F32)).astype(BF16)
        out = out + (lax.dot_general(a_hi, b_lo, dims, preferred_element_type=F32)
                     + lax.dot_general(a_lo, b_hi, dims, preferred_element_type=F32))
    return out


def _dot(a, b):
    return _mm(a, b, "nn", 6)


def _mod_linear_kernel(x_ref, sh_ref, sc_ref, w_ref, o_ref):
    x = x_ref[...]
    ms = jnp.mean(x * x, axis=-1, keepdims=True)
    h = x * lax.rsqrt(ms + RMS_EPS) * (1.0 + sc_ref[0]) + sh_ref[0]
    o_ref[...] = jnp.dot(h.astype(BF16), w_ref[...], preferred_element_type=F32)


def _mod_linear(x, shift, scale, w, rows_per_mod):
    m, d = x.shape
    n = w.shape[1]
    tiles_per_mod = rows_per_mod // ROW_TILE
    mod_spec = pl.BlockSpec((1, 1, d), lambda i: (i // tiles_per_mod, 0, 0))
    return pl.pallas_call(
        _mod_linear_kernel,
        grid=(m // ROW_TILE,),
        in_specs=[pl.BlockSpec((ROW_TILE, d), lambda i: (i, 0)), mod_spec, mod_spec,
                  pl.BlockSpec((d, n), lambda i: (0, 0))],
        out_specs=pl.BlockSpec((ROW_TILE, n), lambda i: (i, 0)),
        out_shape=jax.ShapeDtypeStruct((m, n), F32),
        compiler_params=pltpu.CompilerParams(dimension_semantics=("arbitrary",),
                                             vmem_limit_bytes=VMEM_LIMIT),
        name="mod_linear",
    )(x, shift[:, None, :], scale[:, None, :], w.astype(BF16))


def _res_linear2_kernel(ya_ref, yb_ref, w_ref, res_ref, g_ref, o_ref):
    y = jnp.concatenate([ya_ref[...], yb_ref[...]], axis=-1).astype(BF16)
    o_ref[...] = res_ref[...] + g_ref[0] * jnp.dot(y, w_ref[...], preferred_element_type=F32)


def _res_linear2(ya, yb, w, res, gate, rows_per_mod):
    m, ka = ya.shape
    kb = yb.shape[1]
    n = w.shape[1]
    tiles_per_mod = rows_per_mod // ROW_TILE
    return pl.pallas_call(
        _res_linear2_kernel,
        grid=(m // ROW_TILE,),
        in_specs=[pl.BlockSpec((ROW_TILE, ka), lambda i: (i, 0)), pl.BlockSpec((ROW_TILE, kb), lambda i: (i, 0)),
                  pl.BlockSpec((ka + kb, n), lambda i: (0, 0)),
                  pl.BlockSpec((ROW_TILE, n), lambda i: (i, 0)),
                  pl.BlockSpec((1, 1, n), lambda i: (i // tiles_per_mod, 0, 0))],
        out_specs=pl.BlockSpec((ROW_TILE, n), lambda i: (i, 0)),
        out_shape=jax.ShapeDtypeStruct((m, n), F32),
        compiler_params=pltpu.CompilerParams(dimension_semantics=("arbitrary",),
                                             vmem_limit_bytes=VMEM_LIMIT),
        name="res_linear",
    )(ya, yb, w.astype(BF16), res, gate[:, None, :])


def _small_linear_kernel(x_ref, w_ref, b_ref, o_ref):
    o_ref[...] = _dot(x_ref[...], w_ref[...]) + b_ref[...]


def _small_linear(x, w, b):
    m, k = x.shape
    n = w.shape[1]
    tn = 1024 if n % 1024 == 0 else n
    return pl.pallas_call(
        _small_linear_kernel,
        grid=(n // tn,),
        in_specs=[pl.BlockSpec((m, k), lambda j: (0, 0)), pl.BlockSpec((k, tn), lambda j: (0, j)),
                  pl.BlockSpec((1, tn), lambda j: (0, j))],
        out_specs=pl.BlockSpec((m, tn), lambda j: (0, j)),
        out_shape=jax.ShapeDtypeStruct((m, n), F32),
        name="small_linear",
    )(x, w, b[None, :])


def _rec_kernel(*refs, delta, n_heads):
    n_in = 6 if delta else 3
    ins = (refs[:n_in], refs[n_in:2 * n_in])
    rest = refs[2 * n_in:]
    if delta:
        y_refs, s_ref = rest[:2], rest[2]
    else:
        lb_ref, y_refs, s_ref = rest[0], rest[1:3], rest[3]

    @pl.when(pl.program_id(1) == 0)
    def _():
        s_ref[...] = jnp.zeros_like(s_ref)

    c = ins[0][0].shape[0]
    row = lax.broadcasted_iota(jnp.int32, (c, c), 0)
    col = lax.broadcasted_iota(jnp.int32, (c, c), 1)
    mid = c // 2
    heads = range(n_heads)
    hs = lambda t: [t[:, h * HEAD_DIM:(h + 1) * HEAD_DIM] for h in heads]
    mm = functools.partial(_mm, passes=REC_PASSES)

    streams = []
    for d in range(2):
        ahead = row - col if d == 0 else col - row
        incl, strict = ahead >= 0, ahead > 0
        if delta:
            r_ref, v_ref, kap_ref, lw_ref, k_ref, al_ref = ins[d]
            r_all, v_all, lw, k = r_ref[...], v_ref[...], lw_ref[0], k_ref[0]
        else:
            q_ref, i_ref, f_ref = ins[d]
            fg = lb_ref[...] + (1.0 - lb_ref[...]) * jax.nn.sigmoid(f_ref[...])
            r_all, v_all, lw, k = jax.nn.silu(q_ref[...]), i_ref[...], jnp.log(fg), 1.0 - fg
        b = _mm(incl.astype(F32), lw, "nn", 6)
        bm = b[mid:mid + 1, :]
        tot = b[c - 1:c, :] if d == 0 else b[0:1, :]
        e_neg = jnp.exp(bm - b)
        e_end = jnp.exp(tot - b)
        st = dict(incl=incl, strict=strict, v=hs(v_all), rq=hs(r_all * jnp.exp(b - bm)), kd=hs(k * e_neg),
                  k_end=hs(k * e_end), g_mid=hs(jnp.exp(bm)), g_tot=hs(jnp.exp(tot)))
        if delta:
            al = al_ref[0]
            st.update(kq=hs(kap_ref[...] * jnp.exp(b - lw - bm)), ad=hs(al * e_neg), al_end=hs(al * e_end))
        streams.append(st)

    units = [(d, h) for d in range(2) for h in heads]
    n_u = range(len(units))
    per_head = lambda name: [streams[d][name][h] for d, h in units]
    per_dir = lambda name: [streams[d][name] for d, _ in units]
    incl, strict = per_dir("incl"), per_dir("strict")
    v, rq, kd, k_end, g_mid, g_tot = (per_head(n) for n in ("v", "rq", "kd", "k_end", "g_mid", "g_tot"))
    s0 = [s_ref[d, h] for d, h in units]
    s0m = [s0[n] * g_mid[n] for n in n_u]
    if delta:
        kq, ad, al_end = per_head("kq"), per_head("ad"), per_head("al_end")
        q2 = [jnp.concatenate([kq[n], rq[n]], axis=0) for n in n_u]
        k2 = [jnp.concatenate([kd[n], ad[n]], axis=0) for n in n_u]
        a = [mm(q2[n], k2[n], "nt") for n in n_u]
        p = [mm(q2[n], s0m[n], "nt") for n in n_u]
        z = [-(p[n][:c] + mm(jnp.where(strict[n], a[n][:c, :c], 0.0), v[n], "nn")) for n in n_u]
        mc = functools.partial(_mm, passes=REC_CHAIN_PASSES)
        m = [jnp.where(strict[n], a[n][:c, c:], 0.0) for n in n_u]
        pair = (row >> 1) == (col >> 1)
        eye = (row == col).astype(F32)
        t = [eye - jnp.where(pair, m[n], 0.0) for n in n_u]
        for lvl in range(1, int(math.log2(c))):
            off = ((row >> (lvl + 1)) == (col >> (lvl + 1))) & ((row >> lvl) != (col >> lvl))
            tc = [mc(t[n], jnp.where(off, m[n], 0.0), "nn") for n in n_u]
            t = [t[n] - mc(tc[n], t[n], "nn") for n in n_u]
        z = [mc(t[n], z[n], "nn") for n in n_u]
        y = [p[n][c:] + mm(jnp.where(incl[n], a[n][c:, :c], 0.0), v[n], "nn")
             + mm(jnp.where(incl[n], a[n][c:, c:], 0.0), z[n], "nn") for n in n_u]
        s_new = [s0[n] * g_tot[n] + mm(v[n], k_end[n], "tn") + mm(z[n], al_end[n], "tn") for n in n_u]
    else:
        a = [mm(rq[n], kd[n], "nt") for n in n_u]
        y = [mm(rq[n], s0m[n], "nt") + mm(jnp.where(incl[n], a[n], 0.0), v[n], "nn") for n in n_u]
        s_new = [s0[n] * g_tot[n] + mm(v[n], k_end[n], "tn") for n in n_u]
    for n, (d, h) in enumerate(units):
        s_ref[d, h] = s_new[n]
    for d in range(2):
        y_refs[d][...] = jnp.concatenate(y[d * n_heads:(d + 1) * n_heads], axis=-1)


def _seq_block_index(bsz, s, n_ctx, rows):
    n_c, n_l = n_ctx // rows, s // rows

    def index(d, b, i):
        back = jnp.where(i < n_c, n_c - 1 - i, n_l + 2 * n_c - 1 - i)
        pos = jnp.where(d == 0, i, back)
        return jnp.where(pos < n_c, bsz * n_l + b * n_c + pos, b * n_l + pos - n_c)

    return index


def _rec_call(args, specs, bsz, s, n_ctx, width, delta):
    c = REC_CHUNK
    n_heads = width // HEAD_DIM
    blk = _seq_block_index(bsz, s, n_ctx, c)
    out = jax.ShapeDtypeStruct((bsz * (s + n_ctx), width), F32)
    return pl.pallas_call(
        functools.partial(_rec_kernel, delta=delta, n_heads=n_heads),
        grid=(bsz, (s + n_ctx) // c),
        in_specs=specs,
        out_specs=[pl.BlockSpec((c, width), lambda b, i, d=d: (blk(d, b, i), 0)) for d in range(2)],
        out_shape=[out, out],
        scratch_shapes=[pltpu.VMEM((2, n_heads, HEAD_DIM, HEAD_DIM), F32)],
        compiler_params=pltpu.CompilerParams(dimension_semantics=("arbitrary", "arbitrary")),
        name="rwkv7_rec" if delta else "hgrn2_rec",
    )(*args)


def _rwkv7_recurrence(r, v, kap, lw, k, al, bsz, s, n_ctx):
    width = r.shape[1]
    blk = _seq_block_index(bsz, s, n_ctx, REC_CHUNK)
    specs = []
    for d in range(2):
        shared = pl.BlockSpec((REC_CHUNK, width), lambda b, i, d=d: (blk(d, b, i), 0))
        per_dir = pl.BlockSpec((1, REC_CHUNK, width), lambda b, i, d=d: (d, blk(d, b, i), 0))
        specs += [shared] * 3 + [per_dir] * 3
    return _rec_call([r, v, kap, lw, k, al] * 2, specs, bsz, s, n_ctx, width, True)


def _hgrn2_recurrence(u, lb, col_q, col_f, col_i, bsz, s, n_ctx):
    width = lb.shape[0]
    blk = _seq_block_index(bsz, s, n_ctx, REC_CHUNK)
    specs = []
    for d in range(2):
        specs += [pl.BlockSpec((REC_CHUNK, width), lambda b, i, d=d, j=j: (blk(d, b, i), j))
                  for j in (col_q, col_i, col_f + d)]
    specs.append(pl.BlockSpec((1, width), lambda b, i: (0, 0)))
    return _rec_call([u] * 6 + [lb[None, :]], specs, bsz, s, n_ctx, width, False)


def _na_kernel(q_ref, k_ref, v_ref, kc_ref, vc_ref, bias_ref, o_ref, *, rows_per_step, n_rows):
    g = pl.program_id(2)
    band = NA_ROWS * GRID_W
    n_h = q_ref.shape[1]
    units = [(h, j) for j in range(rows_per_step) for h in range(n_h)]
    nt = lambda p, q_: lax.dot_general(p, q_, (((1,), (1,)), ((), ())), preferred_element_type=F32)
    qr = [g * rows_per_step + j for j in range(rows_per_step)]
    start = [jnp.clip(r - NA_ROWS // 2, 0, n_rows - NA_ROWS) for r in qr]
    off = [pl.multiple_of(st * GRID_W, GRID_W) for st in start]
    q = [q_ref[0, h, j * GRID_W:(j + 1) * GRID_W, :] for h, j in units]
    s_win = [nt(q[n], k_ref[0, h, pl.ds(off[j], band), :]) + bias_ref[h, start[j] - qr[j] + NA_ROWS - 1]
             for n, (h, j) in enumerate(units)]
    s_ctx = [nt(q[n], kc_ref[0, h]) for n, (h, j) in enumerate(units)]
    ns = range(len(units))
    m = [jnp.maximum(jnp.max(s_win[n], axis=-1, keepdims=True), jnp.max(s_ctx[n], axis=-1, keepdims=True))
         for n in ns]
    p_win = [jnp.exp(s_win[n] - m[n]) for n in ns]
    p_ctx = [jnp.exp(s_ctx[n] - m[n]) for n in ns]
    den = [jnp.sum(p_win[n], axis=-1, keepdims=True) + jnp.sum(p_ctx[n], axis=-1, keepdims=True) for n in ns]
    o = [(jnp.dot(p_win[n].astype(BF16), v_ref[0, h, pl.ds(off[j], band), :], preferred_element_type=F32)
          + jnp.dot(p_ctx[n].astype(BF16), vc_ref[0, h], preferred_element_type=F32)) / den[n]
         for n, (h, j) in enumerate(units)]
    for j in range(rows_per_step):
        o_ref[j * GRID_W:(j + 1) * GRID_W, :] = jnp.concatenate(o[j * n_h:(j + 1) * n_h], axis=-1)


def _na_bias_table(rpb):
    cc = np.arange(GRID_W)
    col_start = np.clip(cc - NA_COLS // 2, 0, GRID_W - NA_COLS)
    kc = np.arange(GRID_W)
    inside = (kc[None, :] >= col_start[:, None]) & (kc[None, :] < col_start[:, None] + NA_COLS)
    col_off = np.clip(kc[None, :] - cc[:, None] + NA_COLS - 1, 0, 2 * NA_COLS - 2)
    tab = rpb[:, :, col_off]
    tab = jnp.where(jnp.asarray(inside)[None, None], tab, -jnp.inf)
    d0 = np.arange(NA_ROWS)[:, None] + np.arange(NA_ROWS)[None, :]
    band = tab[:, d0]
    return band.transpose(0, 1, 3, 2, 4).reshape(rpb.shape[0], NA_ROWS, GRID_W, NA_ROWS * GRID_W)


NA_HEADS_PER_STEP = 128 // HEAD_DIM


def _neighbourhood_attention(q, k, v, kc, vc, rpb):
    bsz, n_heads, s, dh = q.shape
    n_rows = s // GRID_W
    rows_per_step = 8
    hp = NA_HEADS_PER_STEP
    lc = kc.shape[2]
    bias = _na_bias_table(rpb)
    steps = n_rows // rows_per_step
    full = pl.BlockSpec((1, hp, s, dh), lambda b, h, g: (b, h, 0, 0))
    ctx = pl.BlockSpec((1, hp, lc, dh), lambda b, h, g: (b, h, 0, 0))
    return pl.pallas_call(
        functools.partial(_na_kernel, rows_per_step=rows_per_step, n_rows=n_rows),
        grid=(bsz, n_heads // hp, steps),
        in_specs=[pl.BlockSpec((1, hp, rows_per_step * GRID_W, dh), lambda b, h, g: (b, h, g, 0)),
                  full, full, ctx, ctx,
                  pl.BlockSpec((hp, NA_ROWS, GRID_W, NA_ROWS * GRID_W), lambda b, h, g: (h, 0, 0, 0))],
        out_specs=pl.BlockSpec((rows_per_step * GRID_W, hp * dh), lambda b, h, g: (b * steps + g, h)),
        out_shape=jax.ShapeDtypeStruct((bsz * s, n_heads * dh), F32),
        compiler_params=pltpu.CompilerParams(
            dimension_semantics=("arbitrary", "arbitrary", "arbitrary"), vmem_limit_bytes=VMEM_LIMIT),
        name="neighbourhood_attention",
    )(q, k, v, kc, vc, bias)


def _na_prep_kernel(u_ref, cos_ref, sin_ref, qn_ref, kn_ref, bd_ref, q_ref, k_ref, v_ref, *, n_heads):
    width = n_heads * HEAD_DIM
    bd = bd_ref[...]
    lane = lax.broadcasted_iota(jnp.int32, (u_ref.shape[0], 128), 1)
    first = (lane & (HEAD_DIM // 4)) == 0

    def rope(x):
        parts = []
        for j in range(width // 128):
            cols = slice(128 * j, 128 * (j + 1))
            xj = x[:, cols]
            partner = jnp.where(first, pltpu.roll(xj, 128 - HEAD_DIM // 4, axis=1),
                                pltpu.roll(xj, HEAD_DIM // 4, axis=1))
            parts.append(xj * cos_ref[:, cols] + partner * sin_ref[:, cols])
        return jnp.concatenate(parts, axis=-1)

    def normed(x, w_ref):
        return x * lax.rsqrt(_seg_sum(x * x, bd) * (1.0 / HEAD_DIM) + RMS_EPS) * w_ref[...]

    q = rope(normed(u_ref[:, :width], qn_ref)) * HEAD_DIM ** -0.5
    k = rope(normed(u_ref[:, width:2 * width], kn_ref))
    v = u_ref[:, 2 * width:3 * width]
    for h in range(n_heads):
        cols = slice(h * HEAD_DIM, (h + 1) * HEAD_DIM)
        q_ref[0, h] = q[:, cols].astype(BF16)
        k_ref[0, h] = k[:, cols].astype(BF16)
        v_ref[0, h] = v[:, cols].astype(BF16)


def _rope_tables(s, n_heads):
    t = jnp.arange(s)
    nf = HEAD_DIM // 4
    inv = ROPE_THETA ** (-jnp.arange(nf, dtype=F32) / nf)
    ang_r = (t // GRID_W).astype(F32)[:, None] * inv
    ang_c = (t % GRID_W).astype(F32)[:, None] * inv
    cos = jnp.concatenate([jnp.cos(ang_r)] * 2 + [jnp.cos(ang_c)] * 2, axis=-1)
    sin = jnp.concatenate([-jnp.sin(ang_r), jnp.sin(ang_r), -jnp.sin(ang_c), jnp.sin(ang_c)], axis=-1)
    return jnp.tile(cos, (1, n_heads)), jnp.tile(sin, (1, n_heads))


def _na_prep(u, bsz, s, n_heads, q_norm, k_norm):
    width = n_heads * HEAD_DIM
    tile = ROW_TILE
    per_batch = s // tile
    cos, sin = _rope_tables(s, n_heads)
    tab = pl.BlockSpec((tile, width), lambda i: (i % per_batch, 0))
    vec = pl.BlockSpec((1, width), lambda i: (0, 0))
    out_spec = pl.BlockSpec((1, n_heads, tile, HEAD_DIM), lambda i: (i // per_batch, 0, i % per_batch, 0))
    out = jax.ShapeDtypeStruct((bsz, n_heads, s, HEAD_DIM), BF16)
    return pl.pallas_call(
        functools.partial(_na_prep_kernel, n_heads=n_heads),
        grid=(bsz * per_batch,),
        in_specs=[pl.BlockSpec((tile, 3 * width), lambda i: (i, 0)), tab, tab, vec, vec,
                  pl.BlockSpec((width, width), lambda i: (0, 0))],
        out_specs=[out_spec, out_spec, out_spec],
        out_shape=[out, out, out],
        name="na_prep",
    )(u, cos, sin, jnp.tile(q_norm, n_heads)[None, :], jnp.tile(k_norm, n_heads)[None, :], _head_ones(width))


def _moe_kernel(be_ref, nb_ref, x_ref, w1_ref, b1_ref, w2_ref, b2_ref, o_ref):
    i = pl.program_id(0)

    @pl.when(i < nb_ref[0])
    def _():
        y = jnp.dot(x_ref[...], w1_ref[0], preferred_element_type=F32) + b1_ref[0]
        half = y.shape[1] // 2
        glu = jnp.minimum(y[:, :half], SWIGLU_LIMIT)
        lin = jnp.clip(y[:, half:], -SWIGLU_LIMIT, SWIGLU_LIMIT)
        act = glu * jax.nn.sigmoid(SWIGLU_ALPHA * glu) * (lin + 1.0)
        o_ref[...] = jnp.dot(act.astype(BF16), w2_ref[0], preferred_element_type=F32) + b2_ref[0]

    @pl.when(i >= nb_ref[0])
    def _():
        o_ref[...] = jnp.zeros_like(o_ref)


def _moe_experts(x_slots, blk_e, n_used, w1, b1, w2, b2):
    n_slots, d = x_slots.shape
    n_blocks = n_slots // MOE_BLOCK
    de2 = w1.shape[2]
    grid_spec = pltpu.PrefetchScalarGridSpec(
        num_scalar_prefetch=2,
        grid=(n_blocks,),
        in_specs=[pl.BlockSpec((MOE_BLOCK, d), lambda i, be, nb: (i, 0)),
                  pl.BlockSpec((1, d, de2), lambda i, be, nb: (be[i], 0, 0)),
                  pl.BlockSpec((1, 1, de2), lambda i, be, nb: (be[i], 0, 0)),
                  pl.BlockSpec((1, de2 // 2, d), lambda i, be, nb: (be[i], 0, 0)),
                  pl.BlockSpec((1, 1, d), lambda i, be, nb: (be[i], 0, 0))],
        out_specs=pl.BlockSpec((MOE_BLOCK, d), lambda i, be, nb: (i, 0)),
    )
    return pl.pallas_call(
        _moe_kernel,
        grid_spec=grid_spec,
        out_shape=jax.ShapeDtypeStruct((n_slots, d), F32),
        compiler_params=pltpu.CompilerParams(dimension_semantics=("arbitrary",),
                                             vmem_limit_bytes=VMEM_LIMIT),
        name="moe_experts",
    )(blk_e, n_used, x_slots, w1, b1[:, None, :], w2, b2[:, None, :])


DEINT_GROUP = 256


def _deinterleave_kernel(w_ref, p_ref, o_ref):
    half = o_ref.shape[1] // 2
    g_out = DEINT_GROUP // 2
    for g in range(w_ref.shape[1] // DEINT_GROUP):
        t = jnp.dot(w_ref[:, g * DEINT_GROUP:(g + 1) * DEINT_GROUP].astype(BF16), p_ref[...],
                    preferred_element_type=F32).astype(BF16)
        o_ref[:, g * g_out:(g + 1) * g_out] = t[:, :g_out]
        o_ref[:, half + g * g_out:half + (g + 1) * g_out] = t[:, g_out:]


def _deinterleave_columns(w, part, n_parts):
    m, n = w.shape[0] // n_parts, w.shape[1]
    tm = 1024
    perm = np.zeros((DEINT_GROUP, DEINT_GROUP), np.float32)
    idx = np.arange(DEINT_GROUP // 2)
    perm[2 * idx, idx] = 1.0
    perm[2 * idx + 1, DEINT_GROUP // 2 + idx] = 1.0
    return pl.pallas_call(
        _deinterleave_kernel,
        grid=(m // tm,),
        in_specs=[pl.BlockSpec((tm, n), lambda i: (part * (m // tm) + i, 0)),
                  pl.BlockSpec((DEINT_GROUP, DEINT_GROUP), lambda i: (0, 0))],
        out_specs=pl.BlockSpec((tm, n), lambda i: (i, 0)),
        out_shape=jax.ShapeDtypeStruct((m, n), BF16),
        compiler_params=pltpu.CompilerParams(dimension_semantics=("arbitrary",),
                                             vmem_limit_bytes=VMEM_LIMIT),
        name="deinterleave_columns",
    )(w, jnp.asarray(perm, BF16))


ROUTE_TILE = 1024


def _route_kernel(x_ref, sh_ref, sc_ref, wt_ref, b_ref, before_ref, h_ref, e_ref, g_ref, r_ref, cnt_ref,
                  carry_ref):
    i = pl.program_id(0)

    @pl.when(i == 0)
    def _():
        carry_ref[...] = jnp.zeros_like(carry_ref)

    x = x_ref[...]
    ms = jnp.mean(x * x, axis=-1, keepdims=True)
    h = x * lax.rsqrt(ms + RMS_EPS) * (1.0 + sc_ref[0]) + sh_ref[0]
    h_ref[...] = h.astype(BF16)
    logits = _mm(wt_ref[...], h, "nt", 6) + b_ref[...]
    n_e, tm = logits.shape
    eidx = lax.broadcasted_iota(jnp.int32, (n_e, tm), 0)
    work = logits
    top_v, top_e = [], []
    for _ in range(TOP_K):
        m = jnp.max(work, axis=0, keepdims=True)
        sel = jnp.min(jnp.where(work == m, eidx, n_e), axis=0, keepdims=True)
        top_v.append(m)
        top_e.append(sel)
        work = jnp.where(eidx == sel, -jnp.inf, work)
    ex = [jnp.exp(v - top_v[0]) for v in top_v]
    den = ex[0] + ex[1] + ex[2] + ex[3]
    g_ref[...] = jnp.concatenate([e_ / den for e_ in ex], axis=0)
    e_ref[...] = jnp.concatenate(top_e, axis=0)
    chosen = [eidx == sel for sel in top_e]
    ind = sum(c.astype(F32) for c in chosen)
    carry = carry_ref[...]
    cnt = (jnp.dot(ind.astype(BF16), before_ref[...], preferred_element_type=F32)
           + jnp.concatenate([carry] * (tm // carry.shape[1]), axis=1))
    r_ref[...] = jnp.concatenate(
        [jnp.sum(jnp.where(c, cnt, 0.0), axis=0, keepdims=True) for c in chosen], axis=0).astype(jnp.int32)
    carry = carry + jnp.sum(ind, axis=1, keepdims=True)
    carry_ref[...] = carry
    cnt_ref[...] = carry


def _route(tok, shift, scale, router_w, router_b, rows_per_mod):
    t, d = tok.shape
    n_e = router_w.shape[1]
    tm = ROUTE_TILE
    tiles_per_mod = rows_per_mod // tm
    mod_spec = pl.BlockSpec((1, 1, d), lambda i: (i // tiles_per_mod, 0, 0))
    before = jnp.asarray(np.triu(np.ones((tm, tm), np.float32), 1), BF16)
    kt_spec = pl.BlockSpec((TOP_K, tm), lambda i: (0, i))
    h, top_e, gate, rank, cnt = pl.pallas_call(
        _route_kernel,
        grid=(t // tm,),
        in_specs=[pl.BlockSpec((tm, d), lambda i: (i, 0)), mod_spec, mod_spec,
                  pl.BlockSpec((n_e, d), lambda i: (0, 0)), pl.BlockSpec((n_e, tm), lambda i: (0, 0)),
                  pl.BlockSpec((tm, tm), lambda i: (0, 0))],
        out_specs=[pl.BlockSpec((tm, d), lambda i: (i, 0)), kt_spec, kt_spec, kt_spec,
                   pl.BlockSpec((n_e, 128), lambda i: (0, 0))],
        out_shape=[jax.ShapeDtypeStruct((t, d), BF16), jax.ShapeDtypeStruct((TOP_K, t), jnp.int32),
                   jax.ShapeDtypeStruct((TOP_K, t), F32), jax.ShapeDtypeStruct((TOP_K, t), jnp.int32),
                   jax.ShapeDtypeStruct((n_e, 128), F32)],
        scratch_shapes=[pltpu.VMEM((n_e, 128), F32)],
        compiler_params=pltpu.CompilerParams(dimension_semantics=("arbitrary",),
                                             vmem_limit_bytes=VMEM_LIMIT),
        name="moe_route",
    )(tok, shift[:, None, :], scale[:, None, :], router_w.T, jnp.broadcast_to(router_b[:, None], (n_e, tm)),
      before)
    return h, top_e, gate, rank, cnt[:, 0].astype(jnp.int32)


def _moe(tok, shift, scale, rows_per_mod, router_w, router_b, w1_layers, layer, b1, w2, b2):
    t, d = tok.shape
    h, top_e, gate, rank, counts = _route(tok, shift, scale, router_w, router_b, rows_per_mod)
    n_assign = t * TOP_K
    n_blocks = -(-n_assign // MOE_BLOCK) + N_EXPERTS
    n_slots = n_blocks * MOE_BLOCK
    padded = (counts + MOE_BLOCK - 1) // MOE_BLOCK * MOE_BLOCK
    start = jnp.cumsum(counts) - counts
    p_end = jnp.cumsum(padded)
    p_start = p_end - padded
    experts = jnp.arange(N_EXPERTS, dtype=jnp.int32)
    dest = jnp.sum(jnp.where(top_e[:, :, None] == experts, p_start, 0), axis=-1) + rank
    blk_first = jnp.arange(n_blocks, dtype=jnp.int32) * MOE_BLOCK
    blk_e = jnp.minimum(jnp.sum(p_end[None, :] <= blk_first[:, None], axis=1), N_EXPERTS - 1).astype(jnp.int32)
    n_used = (p_end[-1] // MOE_BLOCK).astype(jnp.int32).reshape(1)
    order = jnp.argsort(top_e.T.reshape(-1))
    per_slot = lambda per_expert: jnp.repeat(per_expert[blk_e], MOE_BLOCK)
    j = jnp.arange(n_slots, dtype=jnp.int32) - per_slot(p_start)
    src = jnp.clip(per_slot(start) + j, 0, n_assign - 1)
    slot_tok = jnp.where(j < per_slot(counts), order[src] // TOP_K, t).astype(jnp.int32)
    x_slots = jnp.concatenate([h, jnp.zeros((1, d), BF16)], axis=0)[slot_tok]
    n_layers, _, _, de2 = w1_layers.shape
    w1p = _deinterleave_columns(w1_layers.reshape(-1, de2), layer, n_layers).reshape(N_EXPERTS, d, de2)
    b1p = jnp.concatenate([b1[..., ::2], b1[..., 1::2]], axis=-1)
    y = _moe_experts(x_slots, blk_e, n_used, w1p, b1p, w2.astype(BF16), b2)
    return y[dest], gate


def _rms(x):
    return x * lax.rsqrt(jnp.mean(x * x, axis=-1, keepdims=True) + RMS_EPS)


def _conv3(u, w):
    up = jnp.pad(u, ((0, 0), (1, 1), (0, 0)))
    return up[:, :-2] * w[0] + up[:, 1:-1] * w[1] + up[:, 2:] * w[2]


def _seg_sum(x, ones_bd):
    x_hi, x_lo = _hi_lo(x)
    return (jnp.dot(x_hi, ones_bd, preferred_element_type=F32)
            + jnp.dot(x_lo, ones_bd, preferred_element_type=F32))


def _head_ones(width):
    h = np.arange(width) // HEAD_DIM
    return jnp.asarray(h[:, None] == h[None, :], BF16)


def _rwkv7_prep_kernel(first_ref, last_ref, u_ref, up_ref, un_ref, sw_ref, w0_ref, w2_ref, a0_ref, a2_ref,
                       g2_ref, kk_ref, ka_ref, bd_ref, r_ref, v_ref, kap_ref, g_ref, lw_ref, k_ref, al_ref,
                       *, d_a, lora_off):
    i = pl.program_id(0)
    keep_prev = (first_ref[i] == 0).astype(F32)
    keep_next = (last_ref[i] == 0).astype(F32)
    rows = u_ref.shape[0]

    def conv(lo, width, tap_lo):
        x = u_ref[:, lo:lo + width]
        ridx = lax.broadcasted_iota(jnp.int32, x.shape, 0)
        x_prev = jnp.where(ridx == 0, up_ref[7:8, lo:lo + width] * keep_prev, pltpu.roll(x, 1, axis=0))
        x_next = jnp.where(ridx == rows - 1, un_ref[0:1, lo:lo + width] * keep_next,
                           pltpu.roll(x, rows - 1, axis=0))
        w = sw_ref[:, tap_lo:tap_lo + width]
        return x_prev * w[0:1] + x * w[1:2] + x_next * w[2:3]

    r = conv(0, d_a, 0)
    k = conv(d_a, d_a, d_a)
    v = conv(2 * d_a, d_a, 2 * d_a)
    wd = conv(lora_off, 2 * DECAY_LORA, 3 * d_a)
    ad = conv(lora_off + 2 * DECAY_LORA, 2 * AAA_LORA, 3 * d_a + 2 * DECAY_LORA)
    gd = conv(lora_off + 2 * DECAY_LORA + 2 * AAA_LORA, GATE_LORA, 3 * d_a + 2 * DECAY_LORA + 2 * AAA_LORA)
    mm3 = lambda p, q: _mm(p, q, "nn", 3)
    kk = k * kk_ref[...]
    kap = kk / jnp.maximum(jnp.sqrt(_seg_sum(kk * kk, bd_ref[...])), 1e-12)
    lora_w = jnp.tanh(wd)
    for d in range(2):
        z = w0_ref[d:d + 1] + mm3(lora_w[:, d * DECAY_LORA:(d + 1) * DECAY_LORA], w2_ref[d])
        lw_ref[d] = -math.exp(-0.5) * jax.nn.sigmoid(z)
        a = jax.nn.sigmoid(a0_ref[d:d + 1] + mm3(ad[:, d * AAA_LORA:(d + 1) * AAA_LORA], a2_ref[d]))
        k_ref[d] = k * (1.0 + (a - 1.0) * ka_ref[...])
        al_ref[d] = kap * a
    g_ref[...] = mm3(jax.nn.sigmoid(gd), g2_ref[...])
    r_ref[...] = r
    v_ref[...] = v
    kap_ref[...] = kap


def _segment_flags(bsz, s, n_ctx, tile):
    lat, ctx = np.arange(bsz * s // tile), np.arange(bsz * n_ctx // tile)
    first = np.concatenate([lat % (s // tile) == 0, ctx % (n_ctx // tile) == 0])
    last = np.concatenate([lat % (s // tile) == s // tile - 1, ctx % (n_ctx // tile) == n_ctx // tile - 1])
    return jnp.asarray(first, jnp.int32), jnp.asarray(last, jnp.int32)


def _rwkv7_prep(u, lora_off, bsz, s, n_ctx, shift_w, w0, w2, a0, a2, g2, k_k, k_a):
    t, cols = u.shape
    d_a = w0.shape[-1]
    tile = ROW_TILE
    halo = 8
    first, last = _segment_flags(bsz, s, n_ctx, tile)
    full = lambda arr: pl.BlockSpec(arr.shape, lambda i, f, l: (0,) * arr.ndim)
    row = lambda arr: arr[None, :]
    params = [shift_w, w0, w2, a0, a2, g2, row(k_k), row(k_a), _head_ones(d_a)]
    one = pl.BlockSpec((tile, d_a), lambda i, f, l: (i, 0))
    two = pl.BlockSpec((2, tile, d_a), lambda i, f, l: (0, i, 0))
    flat = jax.ShapeDtypeStruct((t, d_a), F32)
    both = jax.ShapeDtypeStruct((2, t, d_a), F32)
    grid_spec = pltpu.PrefetchScalarGridSpec(
        num_scalar_prefetch=2,
        grid=(t // tile,),
        in_specs=[pl.BlockSpec((tile, cols), lambda i, f, l: (i, 0)),
                  pl.BlockSpec((halo, cols), lambda i, f, l: (jnp.maximum(i * (tile // halo) - 1, 0), 0)),
                  pl.BlockSpec((halo, cols),
                               lambda i, f, l: (jnp.minimum((i + 1) * (tile // halo), t // halo - 1), 0))]
        + [full(p) for p in params],
        out_specs=[one, one, one, one, two, two, two],
    )
    return pl.pallas_call(
        functools.partial(_rwkv7_prep_kernel, d_a=d_a, lora_off=lora_off),
        grid_spec=grid_spec,
        out_shape=[flat, flat, flat, flat, both, both, both],
        compiler_params=pltpu.CompilerParams(dimension_semantics=("arbitrary",),
                                             vmem_limit_bytes=VMEM_LIMIT),
        name="rwkv7_prep",
    )(first, last, u, u, u, *params)


def _even_out_kernel(yaf_ref, yab_ref, r_ref, v_ref, k_ref, g_ref, ybf_ref, ybb_ref, og_ref, rk_ref, lnw_ref,
                     lnb_ref, nw_ref, bd_ref, w_ref, res_ref, gate_ref, o_ref):
    bd = bd_ref[...]
    inv = 1.0 / HEAD_DIM
    y = yaf_ref[...] + yab_ref[...]
    yc = y - _seg_sum(y, bd) * inv
    var = _seg_sum(yc * yc, bd) * inv
    yn = yc * lax.rsqrt(var + LN_X_EPS) * lnw_ref[...] + lnb_ref[...]
    bonus = _seg_sum(r_ref[...] * (k_ref[0] + k_ref[1]) * rk_ref[...], bd) * v_ref[...]
    out_a = (yn + bonus) * g_ref[...]
    o = ybf_ref[...] + ybb_ref[...]
    out_b = o * lax.rsqrt(_seg_sum(o * o, bd) * inv + RMS_EPS) * nw_ref[...] * jax.nn.silu(og_ref[...])
    ycat = jnp.concatenate([out_a, out_b], axis=-1).astype(BF16)
    o_ref[...] = res_ref[...] + gate_ref[0] * jnp.dot(ycat, w_ref[...], preferred_element_type=F32)


def _even_out(ya, r, v, k, g, yb, u, col_og, r_k, ln_w, ln_b, norm_w, w_out, res, gate, rows_per_mod):
    t, width = r.shape
    d = res.shape[1]
    tile = ROW_TILE
    tiles_per_mod = rows_per_mod // tile
    one = pl.BlockSpec((tile, width), lambda i: (i, 0))
    two = pl.BlockSpec((2, tile, width), lambda i: (0, i, 0))
    vec = pl.BlockSpec((1, width), lambda i: (0, 0))
    n_heads = width // HEAD_DIM
    return pl.pallas_call(
        _even_out_kernel,
        grid=(t // tile,),
        in_specs=[one, one, one, one, two, one, one, one, pl.BlockSpec((tile, width), lambda i: (i, col_og)),
                  vec, vec, vec, vec, pl.BlockSpec((width, width), lambda i: (0, 0)),
                  pl.BlockSpec(w_out.shape, lambda i: (0, 0)), pl.BlockSpec((tile, d), lambda i: (i, 0)),
                  pl.BlockSpec((1, 1, d), lambda i: (i // tiles_per_mod, 0, 0))],
        out_specs=pl.BlockSpec((tile, d), lambda i: (i, 0)),
        out_shape=jax.ShapeDtypeStruct((t, d), F32),
        compiler_params=pltpu.CompilerParams(dimension_semantics=("arbitrary",),
                                             vmem_limit_bytes=VMEM_LIMIT),
        name="even_readout_out_proj",
    )(*ya, r, v, k, g, *yb, u, r_k.reshape(1, width), ln_w[None, :], ln_b[None, :],
      jnp.tile(norm_w, n_heads)[None, :], _head_ones(width), w_out.astype(BF16), res, gate[:, None, :])


def _even_layer(tok, sh1, sc1, g1, bsz, s, n_ctx, w_in, w_out, shift_w, w0, w2, a0, a2, g2, k_k, k_a, r_k, ln_w,
                ln_b, lb, norm_w):
    d_a = w0.shape[-1]
    d_b = lb.shape[0]
    rkv = 3 * d_a
    a_cols = rkv + 2 * DECAY_LORA + 2 * AAA_LORA + GATE_LORA
    w_perm = jnp.concatenate([w_in[:, :rkv], w_in[:, a_cols:], w_in[:, rkv:a_cols]], axis=1)
    u = _mod_linear(tok, sh1, sc1, w_perm, s)
    r, v, kap, g, lw, k, al = _rwkv7_prep(u, rkv + 5 * d_b, bsz, s, n_ctx, shift_w, w0, w2, a0, a2, g2, k_k, k_a)
    ya = _rwkv7_recurrence(r, v, kap, lw, k, al, bsz, s, n_ctx)
    col0 = rkv // d_b
    yb = _hgrn2_recurrence(u, lb, col0, col0 + 1, col0 + 3, bsz, s, n_ctx)
    return _even_out(ya, r, v, k, g, yb, u, col0 + 4, r_k, ln_w, ln_b, norm_w, w_out, tok, g1, s)


def _hyena_filters(length, d_d, w1, b1, fr1, w2, b2, fr2, w3):
    t = jnp.linspace(0.0, 1.0, length, dtype=F32)[:, None]
    bands = (HYENA_EMB - 1) // 2
    f = jnp.linspace(1e-4, bands - 1, bands, dtype=F32)
    ang = (2.0 * math.pi / length) * jnp.arange(length, dtype=F32)[:, None] * f
    z = jnp.concatenate([t, jnp.cos(ang), -jnp.sin(ang)], axis=-1)
    hid = jnp.sin(fr1 * (jnp.dot(z, w1, precision=_HI) + b1))
    hid = jnp.sin(fr2 * (jnp.dot(hid, w2, precision=_HI) + b2))
    h = jnp.dot(hid, w3, precision=_HI).reshape(length, HYENA_ORDER, 2, d_d)
    deltas = jnp.abs(jnp.linspace(math.log(HYENA_TARGET) / HYENA_SLOW_DECAY,
                                  math.log(HYENA_TARGET) / HYENA_FAST_DECAY, d_d, dtype=F32))
    return h * jnp.exp(-t * deltas)[:, None, None, :]


DFT_R = 128
DFT_COL_TILE = 4096
DFT_K1_TILE = 4


def _hi_lo(a):
    hi = a.astype(BF16)
    return hi, (a - hi.astype(F32)).astype(BF16)


def _mm3(f_hi, f_lo, x_hi, x_lo):
    d = lambda p, q: jnp.dot(p, q, preferred_element_type=F32)
    return d(f_hi, x_hi) + (d(f_hi, x_lo) + d(f_lo, x_hi))


def _dft_matrices(n1):
    k1 = np.arange(n1, dtype=np.float64)
    k2 = np.arange(DFT_R, dtype=np.float64)
    as_f32 = lambda a: jnp.asarray(a, F32)
    cs = lambda ang: (as_f32(np.cos(ang)), as_f32(-np.sin(ang)))
    f1 = cs(2.0 * np.pi * np.outer(k1, k1) / n1)
    f2 = cs(2.0 * np.pi * np.outer(k2, k2) / DFT_R)
    tw = cs(2.0 * np.pi * np.outer(k1, k2) / (n1 * DFT_R))
    tw = tuple(jnp.broadcast_to(t[:, :, None], (n1, DFT_R, 128)) for t in tw)
    return f1, f2, tw


def _dft_rows_kernel(x_ref, frh_ref, frl_ref, fih_ref, fil_ref, or_ref, oi_ref):
    frh, frl, fih, fil = frh_ref[...], frl_ref[...], fih_ref[...], fil_ref[...]
    x0 = _hi_lo(x_ref[0])
    if x_ref.shape[0] == 1:
        or_ref[0] = _mm3(frh, frl, *x0)
        oi_ref[0] = _mm3(fih, fil, *x0)
    else:
        x1 = _hi_lo(x_ref[1])
        or_ref[0] = _mm3(frh, frl, *x0) - _mm3(fih, fil, *x1)
        oi_ref[0] = _mm3(frh, frl, *x1) + _mm3(fih, fil, *x0)


def _dft_rows(x2d, f1, pack):
    bsz, rows, cols = x2d.shape
    n1 = f1[0].shape[0]
    mats = [m for f in f1 for m in _hi_lo(f[:, :rows])]
    mat_spec = pl.BlockSpec((n1, rows), lambda b, j: (0, 0))
    out_spec = pl.BlockSpec((1, n1, DFT_COL_TILE), lambda b, j: (b, 0, j))
    out = jax.ShapeDtypeStruct((bsz // pack, n1, cols), F32)
    return pl.pallas_call(
        _dft_rows_kernel,
        grid=(bsz // pack, cols // DFT_COL_TILE),
        in_specs=[pl.BlockSpec((pack, rows, DFT_COL_TILE), lambda b, j: (b, 0, j))] + [mat_spec] * 4,
        out_specs=[out_spec, out_spec],
        out_shape=[out, out],
        name="dft_rows",
    )(x2d, *mats)


def _dft_mid_kernel(*refs, conv):
    if conv:
        (ar_ref, ai_ref, tr_ref, ti_ref, kr_ref, ki_ref, frh_ref, frl_ref, fih_ref, fil_ref,
         or_ref, oi_ref) = refs
    else:
        ar_ref, ai_ref, tr_ref, ti_ref, frh_ref, frl_ref, fih_ref, fil_ref, or_ref, oi_ref = refs
    frh, frl, fih, fil = frh_ref[...], frl_ref[...], fih_ref[...], fil_ref[...]
    reps = ar_ref.shape[3] // tr_ref.shape[2]
    rows = range(ar_ref.shape[1])
    wide = lambda t: jnp.concatenate([t] * reps, axis=-1)
    tr = [wide(tr_ref[j]) for j in rows]
    ti = [wide(ti_ref[j]) for j in rows]
    ar = [ar_ref[0, j] for j in rows]
    ai = [ai_ref[0, j] for j in rows]
    pr = [_hi_lo(ar[j] * tr[j] - ai[j] * ti[j]) for j in rows]
    pi = [_hi_lo(ar[j] * ti[j] + ai[j] * tr[j]) for j in rows]
    xr = [_mm3(frh, frl, *pr[j]) - _mm3(fih, fil, *pi[j]) for j in rows]
    xi = [_mm3(frh, frl, *pi[j]) + _mm3(fih, fil, *pr[j]) for j in rows]
    if not conv:
        for j in rows:
            or_ref[0, j] = xr[j]
            oi_ref[0, j] = xi[j]
        return
    yr = [_hi_lo(xr[j] * kr_ref[0, j] - xi[j] * ki_ref[0, j]) for j in rows]
    yi = [_hi_lo(xr[j] * ki_ref[0, j] + xi[j] * kr_ref[0, j]) for j in rows]
    br = [_mm3(frh, frl, *yr[j]) + _mm3(fih, fil, *yi[j]) for j in rows]
    bi = [_mm3(frh, frl, *yi[j]) - _mm3(fih, fil, *yr[j]) for j in rows]
    for j in rows:
        or_ref[0, j] = br[j] * tr[j] + bi[j] * ti[j]
        oi_ref[0, j] = bi[j] * tr[j] - br[j] * ti[j]


def _dft_mid(a_r, a_i, f2, tw, kf=None):
    (f_r, f_i), (t_r, t_i) = f2, tw
    bsz, n1, _, ch = a_r.shape
    slab = pl.BlockSpec((1, DFT_K1_TILE, DFT_R, ch), lambda b, g: (b, g, 0, 0))
    tw = pl.BlockSpec((DFT_K1_TILE, DFT_R, t_r.shape[2]), lambda b, g: (g, 0, 0))
    mat = pl.BlockSpec((DFT_R, DFT_R), lambda b, g: (0, 0))
    args = [a_r, a_i, t_r, t_i]
    specs = [slab, slab, tw, tw]
    if kf is not None:
        kf_r, kf_i, o = kf
        kspec = pl.BlockSpec((1, DFT_K1_TILE, DFT_R, ch), lambda b, g: (o, g, 0, 0))
        args += [kf_r, kf_i]
        specs += [kspec, kspec]
    args += [m for f in (f_r, f_i) for m in _hi_lo(f)]
    specs += [mat] * 4
    out = jax.ShapeDtypeStruct(a_r.shape, F32)
    return pl.pallas_call(
        functools.partial(_dft_mid_kernel, conv=kf is not None),
        grid=(bsz, n1 // DFT_K1_TILE),
        in_specs=specs,
        out_specs=[slab, slab],
        out_shape=[out, out],
        compiler_params=pltpu.CompilerParams(dimension_semantics=("arbitrary", "arbitrary"),
                                             vmem_limit_bytes=VMEM_LIMIT),
        name="dft_mid_conv" if kf is not None else "dft_mid",
    )(*args)


def _idft_rows_kernel(br_ref, bi_ref, z_ref, g_ref, bias_ref, grh_ref, grl_ref, gih_ref, gil_ref, o_ref):
    grh, grl, gih, gil = grh_ref[...], grl_ref[...], gih_ref[...], gil_ref[...]
    b_r, b_i = _hi_lo(br_ref[0]), _hi_lo(bi_ref[0])
    y = (_mm3(grh, grl, *b_r) + _mm3(gih, gil, *b_i), _mm3(grh, grl, *b_i) - _mm3(gih, gil, *b_r))
    for e in range(2):
        o_ref[e] = g_ref[e] * (y[e] + z_ref[e] * bias_ref[...])


def _idft_rows(b_r, b_i, z2d, gate2d, bias_cols, f1):
    bsz, rows, cols = z2d.shape
    n1 = f1[0].shape[0]
    scale = 1.0 / (n1 * DFT_R)
    mats = [m for f in f1 for m in _hi_lo(f[:rows] * scale)]
    mat_spec = pl.BlockSpec((rows, n1), lambda b, j: (0, 0))
    in_spec = pl.BlockSpec((1, n1, DFT_COL_TILE), lambda b, j: (b, 0, j))
    io_spec = pl.BlockSpec((2, rows, DFT_COL_TILE), lambda b, j: (b, 0, j))
    return pl.pallas_call(
        _idft_rows_kernel,
        grid=(bsz // 2, cols // DFT_COL_TILE),
        in_specs=[in_spec, in_spec, io_spec, io_spec, pl.BlockSpec((1, DFT_COL_TILE), lambda b, j: (0, j))]
        + [mat_spec] * 4,
        out_specs=io_spec,
        out_shape=jax.ShapeDtypeStruct((bsz, rows, cols), F32),
        name="idft_rows",
    )(b_r, b_i, z2d, gate2d, bias_cols, *mats)


def _hyena(u, short_w, w1, b1, fr1, w2, b2, fr2, w3, bias):
    bsz, length, _ = u.shape
    u = _conv3(u, short_w)
    v, x1, x2 = jnp.split(u, 3, axis=-1)
    ch = v.shape[-1]
    rows = length // DFT_R
    h = _hyena_filters(length, ch, w1, b1, fr1, w2, b2, fr2, w3)
    n1 = 2 * rows
    f1, f2, tw = _dft_matrices(n1)
    four_d = lambda t: t.reshape(t.shape[0], n1, DFT_R, ch)
    hh = h.transpose(1, 2, 0, 3).reshape(HYENA_ORDER * 2, rows, DFT_R * ch)
    hf_r, hf_i = (t.reshape(HYENA_ORDER, 2, n1, DFT_R, ch)
                  for t in _dft_mid(*(four_d(t) for t in _dft_rows(hh, f1, 1)), f2, tw))
    kf_r = hf_r[:, 0] + hf_r[:, 1]
    kf_i = hf_i[:, 0] - hf_i[:, 1]
    assert bsz % 2 == 0, "batch entries are convolved in pairs (one complex sequence per pair)"
    z = v
    for o, gate in enumerate((x1, x2)):
        z2d = z.reshape(bsz, rows, DFT_R * ch)
        a_r, a_i = _dft_rows(z2d, f1, 2)
        b_r, b_i = _dft_mid(four_d(a_r), four_d(a_i), f2, tw, kf=(kf_r, kf_i, o))
        flat = lambda t: t.reshape(bsz // 2, n1, DFT_R * ch)
        y = _idft_rows(flat(b_r), flat(b_i), z2d, gate.reshape(z2d.shape), jnp.tile(bias[o], DFT_R)[None, :], f1)
        z = y.reshape(bsz, length, ch)
    return z


def _odd_layer(tok, sh1, sc1, g1, bsz, s, n_ctx, w_in, w_out, q_norm, k_norm, rpb, short_w, w1, b1, fr1, w2,
               b2, fr2, w3, bias):
    n_heads = rpb.shape[0]
    d_c = n_heads * HEAD_DIM
    n_lat = bsz * s
    u = _mod_linear(tok, sh1, sc1, w_in, s)
    q, k, v = _na_prep(u, bsz, s, n_heads, q_norm, k_norm)
    u_ctx = u[n_lat:, d_c:3 * d_c].reshape(bsz, n_ctx, 2, n_heads, HEAD_DIM)
    head_major = lambda t: t.transpose(0, 2, 1, 3).astype(BF16)
    kc = head_major(_rms(u_ctx[:, :, 0]) * k_norm)
    vc = head_major(u_ctx[:, :, 1])
    y_na = _neighbourhood_attention(q, k, v, kc, vc, rpb)
    y_hy = _hyena(u[:n_lat, 3 * d_c:].reshape(bsz, s, -1), short_w, w1, b1, fr1, w2, b2, fr2, w3, bias)
    return _res_linear2(y_na, y_hy.reshape(n_lat, -1), w_out, tok, g1, s)


def kernel(x, c, ctx, c_ctx, mod_w, mod_b, router_w, router_b, moe_w1, moe_b1, moe_w2, moe_b2, ab_w_in, ab_w_out, rwkv_shift, rwkv_w0, rwkv_w2, rwkv_a0, rwkv_a2, rwkv_g2, rwkv_k_k, rwkv_k_a, rwkv_r_k, rwkv_ln_w, rwkv_ln_b, hgrn_lb_logits, hgrn_norm_w, cd_w_in, cd_w_out, na_q_norm, na_k_norm, na_rpb, hy_short, hy_w1, hy_b1, hy_freq1, hy_w2, hy_b2, hy_freq2, hy_w3, hy_bias):
    bsz, s, d = x.shape
    n_ctx = ctx.shape[1]
    depth = mod_w.shape[0]
    n_lat = bsz * s
    lb_all = jnp.cumsum(jax.nn.softmax(hgrn_lb_logits, axis=0), axis=0)
    tok = jnp.concatenate([x.reshape(n_lat, d), ctx.reshape(bsz * n_ctx, d)], axis=0)
    cond = jnp.concatenate([jax.nn.silu(c), jax.nn.silu(c_ctx)[None, :],
                            jnp.zeros((8 - bsz - 1, d), F32)], axis=0)
    assert depth == 2, "layer 0 = RWKV-7 || HGRN2 with context outputs, layer 1 = attention || Hyena, latent only"
    for l in range(depth):
        j = l // 2
        mod = _small_linear(cond, mod_w[l], mod_b[l])[:bsz + 1]
        sh1, sc1, g1, sh2, sc2, g2 = jnp.split(mod, 6, axis=-1)
        if l % 2 == 0:
            tok = _even_layer(tok, sh1, sc1, g1, bsz, s, n_ctx, ab_w_in[j], ab_w_out[j], rwkv_shift[j],
                              rwkv_w0[j], rwkv_w2[j], rwkv_a0[j], rwkv_a2[j], rwkv_g2[j], rwkv_k_k[j],
                              rwkv_k_a[j], rwkv_r_k[j], rwkv_ln_w[j], rwkv_ln_b[j], lb_all[j], hgrn_norm_w[j])
        else:
            tok = _odd_layer(tok, sh1, sc1, g1, bsz, s, n_ctx, cd_w_in[j], cd_w_out[j], na_q_norm[j], na_k_norm[j],
                             na_rpb[j], hy_short[j], hy_w1[j], hy_b1[j], hy_freq1[j], hy_w2[j], hy_b2[j],
                             hy_freq2[j], hy_w3[j], hy_bias[j])
        yk, gate = _moe(tok, sh2, sc2, s, router_w[l], router_b[l], moe_w1, l, moe_b1[l], moe_w2[l], moe_b2[l])
        tok = _combine(tok, yk, gate, g2, s)
    return tok[:n_lat].reshape(bsz, s, d)


def _combine_kernel(x_ref, y_ref, w_ref, g_ref, o_ref):
    reps = x_ref.shape[1] // w_ref.shape[2]
    acc = y_ref[0] * jnp.concatenate([w_ref[0]] * reps, axis=-1)
    for k in range(1, y_ref.shape[0]):
        acc = acc + y_ref[k] * jnp.concatenate([w_ref[k]] * reps, axis=-1)
    o_ref[...] = x_ref[...] + g_ref[0] * acc


def _combine(x, yk, gate, res_gate, rows_per_mod):
    m, d = x.shape
    n_k = yk.shape[0]
    tiles_per_mod = rows_per_mod // ROW_TILE
    tile = pl.BlockSpec((ROW_TILE, d), lambda i: (i, 0))
    lanes = 128
    return pl.pallas_call(
        _combine_kernel,
        grid=(m // ROW_TILE,),
        in_specs=[tile, pl.BlockSpec((n_k, ROW_TILE, d), lambda i: (0, i, 0)),
                  pl.BlockSpec((n_k, ROW_TILE, lanes), lambda i: (0, i, 0)),
                  pl.BlockSpec((1, 1, d), lambda i: (i // tiles_per_mod, 0, 0))],
        out_specs=tile,
        out_shape=jax.ShapeDtypeStruct((m, d), F32),
        name="moe_combine",
    )(x, yk, jnp.broadcast_to(gate[:, :, None], gate.shape + (lanes,)), res_gate[:, None, :])
```

```python
import functools
import math

import numpy as np
import jax
import jax.numpy as jnp
from jax import lax
from jax.experimental import pallas as pl
from jax.experimental.pallas import tpu as pltpu

F32 = jnp.float32
BF16 = jnp.bfloat16

HEAD_DIM = 64
GRID_W = 64
DECAY_LORA = 64
AAA_LORA = 64
GATE_LORA = 128
LN_X_EPS = 1e-5 * HEAD_DIM
NA_ROWS = 8
NA_COLS = 16
ROPE_THETA = 10000.0
HYENA_ORDER = 2
HYENA_EMB = 33
HYENA_FAST_DECAY = 0.3
HYENA_SLOW_DECAY = 1.5
HYENA_TARGET = 1e-2
N_EXPERTS = 32
TOP_K = 4
SWIGLU_ALPHA = 1.702
SWIGLU_LIMIT = 7.0
MOE_BLOCK = 256
RMS_EPS = 1e-6

ROW_TILE = 256
COMBINE_TILE = 512
REC_CHUNK = 64
REC_PASSES = 1
REC_CHAIN_PASSES = 1
VMEM_LIMIT = 56 * 1024 * 1024

_HI = lax.Precision.HIGHEST


_DIMS = {"nn": (((1,), (0,)), ((), ())), "nt": (((1,), (1,)), ((), ())), "tn": (((0,), (0,)), ((), ()))}


def _mm(a, b, form, passes):
    dims = _DIMS[form]
    if passes == 6:
        return lax.dot_general(a, b, dims, precision=_HI, preferred_element_type=F32)
    a_hi = a.astype(BF16)
    b_hi = b.astype(BF16)
    out = lax.dot_general(a_hi, b_hi, dims, preferred_element_type=F32)
    if passes == 3:
        a_lo = (a - a_hi.astype(F32)).astype(BF16)
        b_lo = (b - b_hi.astype(F32)).astype(BF16)
        out = out + (lax.dot_general(a_hi, b_lo, dims, preferred_element_type=F32)
                     + lax.dot_general(a_lo, b_hi, dims, preferred_element_type=F32))
    return out


def _dot(a, b):
    return _mm(a, b, "nn", 6)


def _mod_linear_kernel(x_ref, sh_ref, sc_ref, w_ref, o_ref):
    x = x_ref[...]
    ms = jnp.mean(x * x, axis=-1, keepdims=True)
    h = x * lax.rsqrt(ms + RMS_EPS) * (1.0 + sc_ref[0]) + sh_ref[0]
    o_ref[...] = jnp.dot(h.astype(BF16), w_ref[...], preferred_element_type=F32)


def _mod_linear(x, shift, scale, w, rows_per_mod):
    m, d = x.shape
    n = w.shape[1]
    tiles_per_mod = rows_per_mod // ROW_TILE
    mod_spec = pl.BlockSpec((1, 1, d), lambda i: (i // tiles_per_mod, 0, 0))
    return pl.pallas_call(
        _mod_linear_kernel,
        grid=(m // ROW_TILE,),
        in_specs=[pl.BlockSpec((ROW_TILE, d), lambda i: (i, 0)), mod_spec, mod_spec,
                  pl.BlockSpec((d, n), lambda i: (0, 0))],
        out_specs=pl.BlockSpec((ROW_TILE, n), lambda i: (i, 0)),
        out_shape=jax.ShapeDtypeStruct((m, n), F32),
        compiler_params=pltpu.CompilerParams(dimension_semantics=("arbitrary",),
                                             vmem_limit_bytes=VMEM_LIMIT),
        name="mod_linear",
    )(x, shift[:, None, :], scale[:, None, :], w.astype(BF16))


def _res_linear2_kernel(ya_ref, yb_ref, w_ref, res_ref, g_ref, o_ref):
    y = jnp.concatenate([ya_ref[...], yb_ref[...]], axis=-1).astype(BF16)
    o_ref[...] = res_ref[...] + g_ref[0] * jnp.dot(y, w_ref[...], preferred_element_type=F32)


def _res_linear2(ya, yb, w, res, gate, rows_per_mod):
    m, ka = ya.shape
    kb = yb.shape[1]
    n = w.shape[1]
    tiles_per_mod = rows_per_mod // ROW_TILE
    return pl.pallas_call(
        _res_linear2_kernel,
        grid=(m // ROW_TILE,),
        in_specs=[pl.BlockSpec((ROW_TILE, ka), lambda i: (i, 0)), pl.BlockSpec((ROW_TILE, kb), lambda i: (i, 0)),
                  pl.BlockSpec((ka + kb, n), lambda i: (0, 0)),
                  pl.BlockSpec((ROW_TILE, n), lambda i: (i, 0)),
                  pl.BlockSpec((1, 1, n), lambda i: (i // tiles_per_mod, 0, 0))],
        out_specs=pl.BlockSpec((ROW_TILE, n), lambda i: (i, 0)),
        out_shape=jax.ShapeDtypeStruct((m, n), F32),
        compiler_params=pltpu.CompilerParams(dimension_semantics=("arbitrary",),
                                             vmem_limit_bytes=VMEM_LIMIT),
        name="res_linear",
    )(ya, yb, w.astype(BF16), res, gate[:, None, :])


def _small_linear_kernel(x_ref, w_ref, b_ref, o_ref):
    o_ref[...] = _dot(x_ref[...], w_ref[...]) + b_ref[...]


def _small_linear(x, w, b):
    m, k = x.shape
    n = w.shape[1]
    tn = 1024 if n % 1024 == 0 else n
    return pl.pallas_call(
        _small_linear_kernel,
        grid=(n // tn,),
        in_specs=[pl.BlockSpec((m, k), lambda j: (0, 0)), pl.BlockSpec((k, tn), lambda j: (0, j)),
                  pl.BlockSpec((1, tn), lambda j: (0, j))],
        out_specs=pl.BlockSpec((m, tn), lambda j: (0, j)),
        out_shape=jax.ShapeDtypeStruct((m, n), F32),
        name="small_linear",
    )(x, w, b[None, :])


def _rec_kernel(*refs, delta, n_heads):
    n_in = 6 if delta else 3
    ins = (refs[:n_in], refs[n_in:2 * n_in])
    rest = refs[2 * n_in:]
    if delta:
        y_refs, s_ref = rest[:2], rest[2]
    else:
        lb_ref, y_refs, s_ref = rest[0], rest[1:3], rest[3]

    @pl.when(pl.program_id(1) == 0)
    def _():
        s_ref[...] = jnp.zeros_like(s_ref)

    c = ins[0][0].shape[0]
    row = lax.broadcasted_iota(jnp.int32, (c, c), 0)
    col = lax.broadcasted_iota(jnp.int32, (c, c), 1)
    mid = c // 2
    heads = range(n_heads)
    hs = lambda t: [t[:, h * HEAD_DIM:(h + 1) * HEAD_DIM] for h in heads]
    mm = functools.partial(_mm, passes=REC_PASSES)

    streams = []
    for d in range(2):
        ahead = row - col if d == 0 else col - row
        incl, strict = ahead >= 0, ahead > 0
        if delta:
            r_ref, v_ref, kap_ref, lw_ref, k_ref, al_ref = ins[d]
            r_all, v_all, lw, k = r_ref[...], v_ref[...], lw_ref[0], k_ref[0]
        else:
            q_ref, i_ref, f_ref = ins[d]
            fg = lb_ref[...] + (1.0 - lb_ref[...]) * jax.nn.sigmoid(f_ref[...])
            r_all, v_all, lw, k = jax.nn.silu(q_ref[...]), i_ref[...], jnp.log(fg), 1.0 - fg
        b = _mm(incl.astype(F32), lw, "nn", 6)
        bm = b[mid:mid + 1, :]
        tot = b[c - 1:c, :] if d == 0 else b[0:1, :]
        e_neg = jnp.exp(bm - b)
        e_end = jnp.exp(tot - b)
        st = dict(incl=incl, strict=strict, v=hs(v_all), rq=hs(r_all * jnp.exp(b - bm)), kd=hs(k * e_neg),
                  k_end=hs(k * e_end), g_mid=hs(jnp.exp(bm)), g_tot=hs(jnp.exp(tot)))
        if delta:
            al = al_ref[0]
            st.update(kq=hs(kap_ref[...] * jnp.exp(b - lw - bm)), ad=hs(al * e_neg), al_end=hs(al * e_end))
        streams.append(st)

    units = [(d, h) for d in range(2) for h in heads]
    n_u = range(len(units))
    per_head = lambda name: [streams[d][name][h] for d, h in units]
    per_dir = lambda name: [streams[d][name] for d, _ in units]
    incl, strict = per_dir("incl"), per_dir("strict")
    v, rq, kd, k_end, g_mid, g_tot = (per_head(n) for n in ("v", "rq", "kd", "k_end", "g_mid", "g_tot"))
    s0 = [s_ref[d, h] for d, h in units]
    s0m = [s0[n] * g_mid[n] for n in n_u]
    if delta:
        kq, ad, al_end = per_head("kq"), per_head("ad"), per_head("al_end")
        q2 = [jnp.concatenate([kq[n], rq[n]], axis=0) for n in n_u]
        k2 = [jnp.concatenate([kd[n], ad[n]], axis=0) for n in n_u]
        a = [mm(q2[n], k2[n], "nt") for n in n_u]
        p = [mm(q2[n], s0m[n], "nt") for n in n_u]
        z = [-(p[n][:c] + mm(jnp.where(strict[n], a[n][:c, :c], 0.0), v[n], "nn")) for n in n_u]
        mc = functools.partial(_mm, passes=REC_CHAIN_PASSES)
        m = [jnp.where(strict[n], a[n][:c, c:], 0.0) for n in n_u]
        pair = (row >> 1) == (col >> 1)
        eye = (row == col).astype(F32)
        t = [eye - jnp.where(pair, m[n], 0.0) for n in n_u]
        for lvl in range(1, int(math.log2(c))):
            off = ((row >> (lvl + 1)) == (col >> (lvl + 1))) & ((row >> lvl) != (col >> lvl))
            tc = [mc(t[n], jnp.where(off, m[n], 0.0), "nn") for n in n_u]
            t = [t[n] - mc(tc[n], t[n], "nn") for n in n_u]
        z = [mc(t[n], z[n], "nn") for n in n_u]
        y = [p[n][c:] + mm(jnp.where(incl[n], a[n][c:, :c], 0.0), v[n], "nn")
             + mm(jnp.where(incl[n], a[n][c:, c:], 0.0), z[n], "nn") for n in n_u]
        s_new = [s0[n] * g_tot[n] + mm(v[n], k_end[n], "tn") + mm(z[n], al_end[n], "tn") for n in n_u]
    else:
        a = [mm(rq[n], kd[n], "nt") for n in n_u]
        y = [mm(rq[n], s0m[n], "nt") + mm(jnp.where(incl[n], a[n], 0.0), v[n], "nn") for n in n_u]
        s_new = [s0[n] * g_tot[n] + mm(v[n], k_end[n], "tn") for n in n_u]
    for n, (d, h) in enumerate(units):
        s_ref[d, h] = s_new[n]
    for d in range(2):
        y_refs[d][...] = jnp.concatenate(y[d * n_heads:(d + 1) * n_heads], axis=-1)


def _seq_block_index(bsz, s, n_ctx, rows):
    n_c, n_l = n_ctx // rows, s // rows

    def index(d, b, i):
        back = jnp.where(i < n_c, n_c - 1 - i, n_l + 2 * n_c - 1 - i)
        pos = jnp.where(d == 0, i, back)
        return jnp.where(pos < n_c, bsz * n_l + b * n_c + pos, b * n_l + pos - n_c)

    return index


def _rec_call(args, specs, bsz, s, n_ctx, width, delta):
    c = REC_CHUNK
    n_heads = width // HEAD_DIM
    blk = _seq_block_index(bsz, s, n_ctx, c)
    out = jax.ShapeDtypeStruct((bsz * (s + n_ctx), width), F32)
    return pl.pallas_call(
        functools.partial(_rec_kernel, delta=delta, n_heads=n_heads),
        grid=(bsz, (s + n_ctx) // c),
        in_specs=specs,
        out_specs=[pl.BlockSpec((c, width), lambda b, i, d=d: (blk(d, b, i), 0)) for d in range(2)],
        out_shape=[out, out],
        scratch_shapes=[pltpu.VMEM((2, n_heads, HEAD_DIM, HEAD_DIM), F32)],
        compiler_params=pltpu.CompilerParams(dimension_semantics=("arbitrary", "arbitrary")),
        name="rwkv7_rec" if delta else "hgrn2_rec",
    )(*args)


def _rwkv7_recurrence(r, v, kap, lw, k, al, bsz, s, n_ctx):
    width = r.shape[1]
    blk = _seq_block_index(bsz, s, n_ctx, REC_CHUNK)
    specs = []
    for d in range(2):
        shared = pl.BlockSpec((REC_CHUNK, width), lambda b, i, d=d: (blk(d, b, i), 0))
        per_dir = pl.BlockSpec((1, REC_CHUNK, width), lambda b, i, d=d: (d, blk(d, b, i), 0))
        specs += [shared] * 3 + [per_dir] * 3
    return _rec_call([r, v, kap, lw, k, al] * 2, specs, bsz, s, n_ctx, width, True)


def _hgrn2_recurrence(u, lb, col_q, col_f, col_i, bsz, s, n_ctx):
    width = lb.shape[0]
    blk = _seq_block_index(bsz, s, n_ctx, REC_CHUNK)
    specs = []
    for d in range(2):
        specs += [pl.BlockSpec((REC_CHUNK, width), lambda b, i, d=d, j=j: (blk(d, b, i), j))
                  for j in (col_q, col_i, col_f + d)]
    specs.append(pl.BlockSpec((1, width), lambda b, i: (0, 0)))
    return _rec_call([u] * 6 + [lb[None, :]], specs, bsz, s, n_ctx, width, False)


def _na_kernel(q_ref, k_ref, v_ref, kc_ref, vc_ref, bias_ref, o_ref, *, rows_per_step, n_rows):
    g = pl.program_id(2)
    band = NA_ROWS * GRID_W
    n_h = q_ref.shape[1]
    units = [(h, j) for j in range(rows_per_step) for h in range(n_h)]
    nt = lambda p, q_: lax.dot_general(p, q_, (((1,), (1,)), ((), ())), preferred_element_type=F32)
    qr = [g * rows_per_step + j for j in range(rows_per_step)]
    start = [jnp.clip(r - NA_ROWS // 2, 0, n_rows - NA_ROWS) for r in qr]
    off = [pl.multiple_of(st * GRID_W, GRID_W) for st in start]
    q = [q_ref[0, h, j * GRID_W:(j + 1) * GRID_W, :] for h, j in units]
    s_win = [nt(q[n], k_ref[0, h, pl.ds(off[j], band), :]) + bias_ref[h, start[j] - qr[j] + NA_ROWS - 1]
             for n, (h, j) in enumerate(units)]
    s_ctx = [nt(q[n], kc_ref[0, h]) for n, (h, j) in enumerate(units)]
    ns = range(len(units))
    m = [jnp.maximum(jnp.max(s_win[n], axis=-1, keepdims=True), jnp.max(s_ctx[n], axis=-1, keepdims=True))
         for n in ns]
    p_win = [jnp.exp(s_win[n] - m[n]) for n in ns]
    p_ctx = [jnp.exp(s_ctx[n] - m[n]) for n in ns]
    den = [jnp.sum(p_win[n], axis=-1, keepdims=True) + jnp.sum(p_ctx[n], axis=-1, keepdims=True) for n in ns]
    o = [(jnp.dot(p_win[n].astype(BF16), v_ref[0, h, pl.ds(off[j], band), :], preferred_element_type=F32)
          + jnp.dot(p_ctx[n].astype(BF16), vc_ref[0, h], preferred_element_type=F32)) / den[n]
         for n, (h, j) in enumerate(units)]
    for j in range(rows_per_step):
        o_ref[j * GRID_W:(j + 1) * GRID_W, :] = jnp.concatenate(o[j * n_h:(j + 1) * n_h], axis=-1)


def _na_bias_table(rpb):
    cc = np.arange(GRID_W)
    col_start = np.clip(cc - NA_COLS // 2, 0, GRID_W - NA_COLS)
    kc = np.arange(GRID_W)
    inside = (kc[None, :] >= col_start[:, None]) & (kc[None, :] < col_start[:, None] + NA_COLS)
    col_off = np.clip(kc[None, :] - cc[:, None] + NA_COLS - 1, 0, 2 * NA_COLS - 2)
    tab = rpb[:, :, col_off]
    tab = jnp.where(jnp.asarray(inside)[None, None], tab, -jnp.inf)
    d0 = np.arange(NA_ROWS)[:, None] + np.arange(NA_ROWS)[None, :]
    band = tab[:, d0]
    return band.transpose(0, 1, 3, 2, 4).reshape(rpb.shape[0], NA_ROWS, GRID_W, NA_ROWS * GRID_W)


NA_HEADS_PER_STEP = 128 // HEAD_DIM


def _neighbourhood_attention(q, k, v, kc, vc, rpb):
    bsz, n_heads, s, dh = q.shape
    n_rows = s // GRID_W
    rows_per_step = 8
    hp = NA_HEADS_PER_STEP
    lc = kc.shape[2]
    bias = _na_bias_table(rpb)
    steps = n_rows // rows_per_step
    full = pl.BlockSpec((1, hp, s, dh), lambda b, h, g: (b, h, 0, 0))
    ctx = pl.BlockSpec((1, hp, lc, dh), lambda b, h, g: (b, h, 0, 0))
    return pl.pallas_call(
        functools.partial(_na_kernel, rows_per_step=rows_per_step, n_rows=n_rows),
        grid=(bsz, n_heads // hp, steps),
        in_specs=[pl.BlockSpec((1, hp, rows_per_step * GRID_W, dh), lambda b, h, g: (b, h, g, 0)),
                  full, full, ctx, ctx,
                  pl.BlockSpec((hp, NA_ROWS, GRID_W, NA_ROWS * GRID_W), lambda b, h, g: (h, 0, 0, 0))],
        out_specs=pl.BlockSpec((rows_per_step * GRID_W, hp * dh), lambda b, h, g: (b * steps + g, h)),
        out_shape=jax.ShapeDtypeStruct((bsz * s, n_heads * dh), F32),
        compiler_params=pltpu.CompilerParams(
            dimension_semantics=("arbitrary", "arbitrary", "arbitrary"), vmem_limit_bytes=VMEM_LIMIT),
        name="neighbourhood_attention",
    )(q, k, v, kc, vc, bias)


def _na_prep_kernel(u_ref, cos_ref, sin_ref, qn_ref, kn_ref, bd_ref, q_ref, k_ref, v_ref, *, n_heads):
    width = n_heads * HEAD_DIM
    bd = bd_ref[...]
    lane = lax.broadcasted_iota(jnp.int32, (u_ref.shape[0], 128), 1)
    first = (lane & (HEAD_DIM // 4)) == 0

    def rope(x):
        parts = []
        for j in range(width // 128):
            cols = slice(128 * j, 128 * (j + 1))
            xj = x[:, cols]
            partner = jnp.where(first, pltpu.roll(xj, 128 - HEAD_DIM // 4, axis=1),
                                pltpu.roll(xj, HEAD_DIM // 4, axis=1))
            parts.append(xj * cos_ref[:, cols] + partner * sin_ref[:, cols])
        return jnp.concatenate(parts, axis=-1)

    def normed(x, w_ref):
        return x * lax.rsqrt(_seg_sum(x * x, bd) * (1.0 / HEAD_DIM) + RMS_EPS) * w_ref[...]

    q = rope(normed(u_ref[:, :width], qn_ref)) * HEAD_DIM ** -0.5
    k = rope(normed(u_ref[:, width:2 * width], kn_ref))
    v = u_ref[:, 2 * width:3 * width]
    for h in range(n_heads):
        cols = slice(h * HEAD_DIM, (h + 1) * HEAD_DIM)
        q_ref[0, h] = q[:, cols].astype(BF16)
        k_ref[0, h] = k[:, cols].astype(BF16)
        v_ref[0, h] = v[:, cols].astype(BF16)


def _rope_tables(s, n_heads):
    t = jnp.arange(s)
    nf = HEAD_DIM // 4
    inv = ROPE_THETA ** (-jnp.arange(nf, dtype=F32) / nf)
    ang_r = (t // GRID_W).astype(F32)[:, None] * inv
    ang_c = (t % GRID_W).astype(F32)[:, None] * inv
    cos = jnp.concatenate([jnp.cos(ang_r)] * 2 + [jnp.cos(ang_c)] * 2, axis=-1)
    sin = jnp.concatenate([-jnp.sin(ang_r), jnp.sin(ang_r), -jnp.sin(ang_c), jnp.sin(ang_c)], axis=-1)
    return jnp.tile(cos, (1, n_heads)), jnp.tile(sin, (1, n_heads))


def _na_prep(u, bsz, s, n_heads, q_norm, k_norm):
    width = n_heads * HEAD_DIM
    tile = ROW_TILE
    per_batch = s // tile
    cos, sin = _rope_tables(s, n_heads)
    tab = pl.BlockSpec((tile, width), lambda i: (i % per_batch, 0))
    vec = pl.BlockSpec((1, width), lambda i: (0, 0))
    out_spec = pl.BlockSpec((1, n_heads, tile, HEAD_DIM), lambda i: (i // per_batch, 0, i % per_batch, 0))
    out = jax.ShapeDtypeStruct((bsz, n_heads, s, HEAD_DIM), BF16)
    return pl.pallas_call(
        functools.partial(_na_prep_kernel, n_heads=n_heads),
        grid=(bsz * per_batch,),
        in_specs=[pl.BlockSpec((tile, 3 * width), lambda i: (i, 0)), tab, tab, vec, vec,
                  pl.BlockSpec((width, width), lambda i: (0, 0))],
        out_specs=[out_spec, out_spec, out_spec],
        out_shape=[out, out, out],
        name="na_prep",
    )(u, cos, sin, jnp.tile(q_norm, n_heads)[None, :], jnp.tile(k_norm, n_heads)[None, :], _head_ones(width))


DEINT_GROUP = 256


def _moe_kernel(be_ref, nb_ref, x_ref, w1_ref, b1_ref, w2_ref, b2_ref, perm_ref, o_ref, w1_s, w2_s):
    i = pl.program_id(0)
    active = i < nb_ref[0]
    fresh = jnp.logical_or(i == 0, be_ref[i] != be_ref[jnp.maximum(i - 1, 0)])

    @pl.when(jnp.logical_and(active, fresh))
    def _():
        half = w1_s.shape[1] // 2
        g_out = DEINT_GROUP // 2
        for g in range(w1_s.shape[1] // DEINT_GROUP):
            t = jnp.dot(w1_ref[0, 0, :, g * DEINT_GROUP:(g + 1) * DEINT_GROUP].astype(BF16), perm_ref[...],
                        preferred_element_type=F32).astype(BF16)
            w1_s[:, g * g_out:(g + 1) * g_out] = t[:, :g_out]
            w1_s[:, half + g * g_out:half + (g + 1) * g_out] = t[:, g_out:]
        w2_s[...] = w2_ref[0, 0].astype(BF16)

    @pl.when(active)
    def _():
        y = jnp.dot(x_ref[...], w1_s[...], preferred_element_type=F32) + b1_ref[0]
        half = y.shape[1] // 2
        glu = jnp.minimum(y[:, :half], SWIGLU_LIMIT)
        lin = jnp.clip(y[:, half:], -SWIGLU_LIMIT, SWIGLU_LIMIT)
        act = glu * jax.nn.sigmoid(SWIGLU_ALPHA * glu) * (lin + 1.0)
        o_ref[...] = jnp.dot(act.astype(BF16), w2_s[...], preferred_element_type=F32) + b2_ref[0]

    @pl.when(jnp.logical_not(active))
    def _():
        o_ref[...] = jnp.zeros_like(o_ref)


def _moe_experts(x_slots, blk_e, n_used, w1_layers, w2_layers, layer, b1, b2):
    n_slots, d = x_slots.shape
    n_blocks = n_slots // MOE_BLOCK
    de2 = w1_layers.shape[3]
    perm = np.zeros((DEINT_GROUP, DEINT_GROUP), np.float32)
    idx = np.arange(DEINT_GROUP // 2)
    perm[2 * idx, idx] = 1.0
    perm[2 * idx + 1, DEINT_GROUP // 2 + idx] = 1.0
    grid_spec = pltpu.PrefetchScalarGridSpec(
        num_scalar_prefetch=2,
        grid=(n_blocks,),
        in_specs=[pl.BlockSpec((MOE_BLOCK, d), lambda i, be, nb: (i, 0)),
                  pl.BlockSpec((1, 1, d, de2), lambda i, be, nb: (layer, be[i], 0, 0)),
                  pl.BlockSpec((1, 1, de2), lambda i, be, nb: (be[i], 0, 0)),
                  pl.BlockSpec((1, 1, de2 // 2, d), lambda i, be, nb: (layer, be[i], 0, 0)),
                  pl.BlockSpec((1, 1, d), lambda i, be, nb: (be[i], 0, 0)),
                  pl.BlockSpec((DEINT_GROUP, DEINT_GROUP), lambda i, be, nb: (0, 0))],
        out_specs=pl.BlockSpec((MOE_BLOCK, d), lambda i, be, nb: (i, 0)),
        scratch_shapes=[pltpu.VMEM((d, de2), BF16), pltpu.VMEM((de2 // 2, d), BF16)],
    )
    return pl.pallas_call(
        _moe_kernel,
        grid_spec=grid_spec,
        out_shape=jax.ShapeDtypeStruct((n_slots, d), F32),
        compiler_params=pltpu.CompilerParams(dimension_semantics=("arbitrary",),
                                             vmem_limit_bytes=VMEM_LIMIT),
        name="moe_experts",
    )(blk_e, n_used, x_slots, w1_layers, b1[:, None, :], w2_layers, b2[:, None, :], jnp.asarray(perm, BF16))


ROUTE_TILE = 1024


def _route_kernel(x_ref, sh_ref, sc_ref, wt_ref, b_ref, before_ref, h_ref, e_ref, g_ref, r_ref, cnt_ref,
                  carry_ref):
    i = pl.program_id(0)

    @pl.when(i == 0)
    def _():
        carry_ref[...] = jnp.zeros_like(carry_ref)

    x = x_ref[...]
    ms = jnp.mean(x * x, axis=-1, keepdims=True)
    h = x * lax.rsqrt(ms + RMS_EPS) * (1.0 + sc_ref[0]) + sh_ref[0]
    h_ref[...] = h.astype(BF16)
    logits = _mm(wt_ref[...], h, "nt", 6) + b_ref[...]
    n_e, tm = logits.shape
    eidx = lax.broadcasted_iota(jnp.int32, (n_e, tm), 0)
    work = logits
    top_v, top_e = [], []
    for _ in range(TOP_K):
        m = jnp.max(work, axis=0, keepdims=True)
        sel = jnp.min(jnp.where(work == m, eidx, n_e), axis=0, keepdims=True)
        top_v.append(m)
        top_e.append(sel)
        work = jnp.where(eidx == sel, -jnp.inf, work)
    ex = [jnp.exp(v - top_v[0]) for v in top_v]
    den = ex[0] + ex[1] + ex[2] + ex[3]
    g_ref[...] = jnp.concatenate([e_ / den for e_ in ex], axis=0)
    e_ref[...] = jnp.concatenate(top_e, axis=0)
    chosen = [eidx == sel for sel in top_e]
    ind = sum(c.astype(F32) for c in chosen)
    carry = carry_ref[...]
    cnt = (jnp.dot(ind.astype(BF16), before_ref[...], preferred_element_type=F32)
           + jnp.concatenate([carry] * (tm // carry.shape[1]), axis=1))
    r_ref[...] = jnp.concatenate(
        [jnp.sum(jnp.where(c, cnt, 0.0), axis=0, keepdims=True) for c in chosen], axis=0).astype(jnp.int32)
    carry = carry + jnp.sum(ind, axis=1, keepdims=True)
    carry_ref[...] = carry
    cnt_ref[...] = carry


def _route(tok, shift, scale, router_w, router_b, rows_per_mod):
    t, d = tok.shape
    n_e = router_w.shape[1]
    tm = ROUTE_TILE
    tiles_per_mod = rows_per_mod // tm
    mod_spec = pl.BlockSpec((1, 1, d), lambda i: (i // tiles_per_mod, 0, 0))
    before = jnp.asarray(np.triu(np.ones((tm, tm), np.float32), 1), BF16)
    kt_spec = pl.BlockSpec((TOP_K, tm), lambda i: (0, i))
    h, top_e, gate, rank, cnt = pl.pallas_call(
        _route_kernel,
        grid=(t // tm,),
        in_specs=[pl.BlockSpec((tm, d), lambda i: (i, 0)), mod_spec, mod_spec,
                  pl.BlockSpec((n_e, d), lambda i: (0, 0)), pl.BlockSpec((n_e, tm), lambda i: (0, 0)),
                  pl.BlockSpec((tm, tm), lambda i: (0, 0))],
        out_specs=[pl.BlockSpec((tm, d), lambda i: (i, 0)), kt_spec, kt_spec, kt_spec,
                   pl.BlockSpec((n_e, 128), lambda i: (0, 0))],
        out_shape=[jax.ShapeDtypeStruct((t, d), BF16), jax.ShapeDtypeStruct((TOP_K, t), jnp.int32),
                   jax.ShapeDtypeStruct((TOP_K, t), F32), jax.ShapeDtypeStruct((TOP_K, t), jnp.int32),
                   jax.ShapeDtypeStruct((n_e, 128), F32)],
        scratch_shapes=[pltpu.VMEM((n_e, 128), F32)],
        compiler_params=pltpu.CompilerParams(dimension_semantics=("arbitrary",),
                                             vmem_limit_bytes=VMEM_LIMIT),
        name="moe_route",
    )(tok, shift[:, None, :], scale[:, None, :], router_w.T, jnp.broadcast_to(router_b[:, None], (n_e, tm)),
      before)
    return h, top_e, gate, rank, cnt[:, 0].astype(jnp.int32)


def _moe(tok, shift, scale, rows_per_mod, router_w, router_b, w1_layers, w2_layers, layer, b1, b2):
    t, d = tok.shape
    h, top_e, gate, rank, counts = _route(tok, shift, scale, router_w, router_b, rows_per_mod)
    n_assign = t * TOP_K
    n_blocks = -(-n_assign // MOE_BLOCK) + N_EXPERTS
    n_slots = n_blocks * MOE_BLOCK
    padded = (counts + MOE_BLOCK - 1) // MOE_BLOCK * MOE_BLOCK
    start = jnp.cumsum(counts) - counts
    p_end = jnp.cumsum(padded)
    p_start = p_end - padded
    experts = jnp.arange(N_EXPERTS, dtype=jnp.int32)
    dest = jnp.sum(jnp.where(top_e[:, :, None] == experts, p_start, 0), axis=-1) + rank
    blk_first = jnp.arange(n_blocks, dtype=jnp.int32) * MOE_BLOCK
    blk_e = jnp.minimum(jnp.sum(p_end[None, :] <= blk_first[:, None], axis=1), N_EXPERTS - 1).astype(jnp.int32)
    n_used = (p_end[-1] // MOE_BLOCK).astype(jnp.int32).reshape(1)
    order = jnp.argsort(top_e.T.reshape(-1))
    per_slot = lambda per_expert: jnp.repeat(per_expert[blk_e], MOE_BLOCK)
    j = jnp.arange(n_slots, dtype=jnp.int32) - per_slot(p_start)
    src = jnp.clip(per_slot(start) + j, 0, n_assign - 1)
    slot_tok = jnp.where(j < per_slot(counts), order[src] // TOP_K, t).astype(jnp.int32)
    x_slots = jnp.concatenate([h, jnp.zeros((1, d), BF16)], axis=0)[slot_tok]
    b1p = jnp.concatenate([b1[..., ::2], b1[..., 1::2]], axis=-1)
    y = _moe_experts(x_slots, blk_e, n_used, w1_layers, w2_layers, layer, b1p, b2)
    return y[dest], gate


def _rms(x):
    return x * lax.rsqrt(jnp.mean(x * x, axis=-1, keepdims=True) + RMS_EPS)


def _conv3(u, w):
    up = jnp.pad(u, ((0, 0), (1, 1), (0, 0)))
    return up[:, :-2] * w[0] + up[:, 1:-1] * w[1] + up[:, 2:] * w[2]


def _seg_sum(x, ones_bd):
    x_hi, x_lo = _hi_lo(x)
    return (jnp.dot(x_hi, ones_bd, preferred_element_type=F32)
            + jnp.dot(x_lo, ones_bd, preferred_element_type=F32))


def _head_ones(width):
    h = np.arange(width) // HEAD_DIM
    return jnp.asarray(h[:, None] == h[None, :], BF16)


def _rwkv7_prep_kernel(first_ref, last_ref, u_ref, up_ref, un_ref, sw_ref, w0_ref, w2_ref, a0_ref, a2_ref,
                       g2_ref, kk_ref, ka_ref, bd_ref, r_ref, v_ref, kap_ref, g_ref, lw_ref, k_ref, al_ref,
                       *, d_a, lora_off):
    i = pl.program_id(0)
    keep_prev = (first_ref[i] == 0).astype(F32)
    keep_next = (last_ref[i] == 0).astype(F32)
    rows = u_ref.shape[0]

    def conv(lo, width, tap_lo):
        x = u_ref[:, lo:lo + width]
        ridx = lax.broadcasted_iota(jnp.int32, x.shape, 0)
        x_prev = jnp.where(ridx == 0, up_ref[7:8, lo:lo + width] * keep_prev, pltpu.roll(x, 1, axis=0))
        x_next = jnp.where(ridx == rows - 1, un_ref[0:1, lo:lo + width] * keep_next,
                           pltpu.roll(x, rows - 1, axis=0))
        w = sw_ref[:, tap_lo:tap_lo + width]
        return x_prev * w[0:1] + x * w[1:2] + x_next * w[2:3]

    r = conv(0, d_a, 0)
    k = conv(d_a, d_a, d_a)
    v = conv(2 * d_a, d_a, 2 * d_a)
    wd = conv(lora_off, 2 * DECAY_LORA, 3 * d_a)
    ad = conv(lora_off + 2 * DECAY_LORA, 2 * AAA_LORA, 3 * d_a + 2 * DECAY_LORA)
    gd = conv(lora_off + 2 * DECAY_LORA + 2 * AAA_LORA, GATE_LORA, 3 * d_a + 2 * DECAY_LORA + 2 * AAA_LORA)
    mm3 = lambda p, q: _mm(p, q, "nn", 3)
    kk = k * kk_ref[...]
    kap = kk / jnp.maximum(jnp.sqrt(_seg_sum(kk * kk, bd_ref[...])), 1e-12)
    lora_w = jnp.tanh(wd)
    for d in range(2):
        z = w0_ref[d:d + 1] + mm3(lora_w[:, d * DECAY_LORA:(d + 1) * DECAY_LORA], w2_ref[d])
        lw_ref[d] = -math.exp(-0.5) * jax.nn.sigmoid(z)
        a = jax.nn.sigmoid(a0_ref[d:d + 1] + mm3(ad[:, d * AAA_LORA:(d + 1) * AAA_LORA], a2_ref[d]))
        k_ref[d] = k * (1.0 + (a - 1.0) * ka_ref[...])
        al_ref[d] = kap * a
    g_ref[...] = mm3(jax.nn.sigmoid(gd), g2_ref[...])
    r_ref[...] = r
    v_ref[...] = v
    kap_ref[...] = kap


def _segment_flags(bsz, s, n_ctx, tile):
    lat, ctx = np.arange(bsz * s // tile), np.arange(bsz * n_ctx // tile)
    first = np.concatenate([lat % (s // tile) == 0, ctx % (n_ctx // tile) == 0])
    last = np.concatenate([lat % (s // tile) == s // tile - 1, ctx % (n_ctx // tile) == n_ctx // tile - 1])
    return jnp.asarray(first, jnp.int32), jnp.asarray(last, jnp.int32)


def _rwkv7_prep(u, lora_off, bsz, s, n_ctx, shift_w, w0, w2, a0, a2, g2, k_k, k_a):
    t, cols = u.shape
    d_a = w0.shape[-1]
    tile = ROW_TILE
    halo = 8
    first, last = _segment_flags(bsz, s, n_ctx, tile)
    full = lambda arr: pl.BlockSpec(arr.shape, lambda i, f, l: (0,) * arr.ndim)
    row = lambda arr: arr[None, :]
    params = [shift_w, w0, w2, a0, a2, g2, row(k_k), row(k_a), _head_ones(d_a)]
    one = pl.BlockSpec((tile, d_a), lambda i, f, l: (i, 0))
    two = pl.BlockSpec((2, tile, d_a), lambda i, f, l: (0, i, 0))
    flat = jax.ShapeDtypeStruct((t, d_a), F32)
    both = jax.ShapeDtypeStruct((2, t, d_a), F32)
    grid_spec = pltpu.PrefetchScalarGridSpec(
        num_scalar_prefetch=2,
        grid=(t // tile,),
        in_specs=[pl.BlockSpec((tile, cols), lambda i, f, l: (i, 0)),
                  pl.BlockSpec((halo, cols), lambda i, f, l: (jnp.maximum(i * (tile // halo) - 1, 0), 0)),
                  pl.BlockSpec((halo, cols),
                               lambda i, f, l: (jnp.minimum((i + 1) * (tile // halo), t // halo - 1), 0))]
        + [full(p) for p in params],
        out_specs=[one, one, one, one, two, two, two],
    )
    return pl.pallas_call(
        functools.partial(_rwkv7_prep_kernel, d_a=d_a, lora_off=lora_off),
        grid_spec=grid_spec,
        out_shape=[flat, flat, flat, flat, both, both, both],
        compiler_params=pltpu.CompilerParams(dimension_semantics=("arbitrary",),
                                             vmem_limit_bytes=VMEM_LIMIT),
        name="rwkv7_prep",
    )(first, last, u, u, u, *params)


def _even_out_kernel(yaf_ref, yab_ref, r_ref, v_ref, k_ref, g_ref, ybf_ref, ybb_ref, og_ref, rk_ref, lnw_ref,
                     lnb_ref, nw_ref, bd_ref, w_ref, res_ref, gate_ref, o_ref):
    bd = bd_ref[...]
    inv = 1.0 / HEAD_DIM
    y = yaf_ref[...] + yab_ref[...]
    yc = y - _seg_sum(y, bd) * inv
    var = _seg_sum(yc * yc, bd) * inv
    yn = yc * lax.rsqrt(var + LN_X_EPS) * lnw_ref[...] + lnb_ref[...]
    bonus = _seg_sum(r_ref[...] * (k_ref[0] + k_ref[1]) * rk_ref[...], bd) * v_ref[...]
    out_a = (yn + bonus) * g_ref[...]
    o = ybf_ref[...] + ybb_ref[...]
    out_b = o * lax.rsqrt(_seg_sum(o * o, bd) * inv + RMS_EPS) * nw_ref[...] * jax.nn.silu(og_ref[...])
    ycat = jnp.concatenate([out_a, out_b], axis=-1).astype(BF16)
    o_ref[...] = res_ref[...] + gate_ref[0] * jnp.dot(ycat, w_ref[...], preferred_element_type=F32)


def _even_out(ya, r, v, k, g, yb, u, col_og, r_k, ln_w, ln_b, norm_w, w_out, res, gate, rows_per_mod):
    t, width = r.shape
    d = res.shape[1]
    tile = ROW_TILE
    tiles_per_mod = rows_per_mod // tile
    one = pl.BlockSpec((tile, width), lambda i: (i, 0))
    two = pl.BlockSpec((2, tile, width), lambda i: (0, i, 0))
    vec = pl.BlockSpec((1, width), lambda i: (0, 0))
    n_heads = width // HEAD_DIM
    return pl.pallas_call(
        _even_out_kernel,
        grid=(t // tile,),
        in_specs=[one, one, one, one, two, one, one, one, pl.BlockSpec((tile, width), lambda i: (i, col_og)),
                  vec, vec, vec, vec, pl.BlockSpec((width, width), lambda i: (0, 0)),
                  pl.BlockSpec(w_out.shape, lambda i: (0, 0)), pl.BlockSpec((tile, d), lambda i: (i, 0)),
                  pl.BlockSpec((1, 1, d), lambda i: (i // tiles_per_mod, 0, 0))],
        out_specs=pl.BlockSpec((tile, d), lambda i: (i, 0)),
        out_shape=jax.ShapeDtypeStruct((t, d), F32),
        compiler_params=pltpu.CompilerParams(dimension_semantics=("arbitrary",),
                                             vmem_limit_bytes=VMEM_LIMIT),
        name="even_readout_out_proj",
    )(*ya, r, v, k, g, *yb, u, r_k.reshape(1, width), ln_w[None, :], ln_b[None, :],
      jnp.tile(norm_w, n_heads)[None, :], _head_ones(width), w_out.astype(BF16), res, gate[:, None, :])


def _even_layer(tok, sh1, sc1, g1, bsz, s, n_ctx, w_in, w_out, shift_w, w0, w2, a0, a2, g2, k_k, k_a, r_k, ln_w,
                ln_b, lb, norm_w):
    d_a = w0.shape[-1]
    d_b = lb.shape[0]
    rkv = 3 * d_a
    a_cols = rkv + 2 * DECAY_LORA + 2 * AAA_LORA + GATE_LORA
    w_perm = jnp.concatenate([w_in[:, :rkv], w_in[:, a_cols:], w_in[:, rkv:a_cols]], axis=1)
    u = _mod_linear(tok, sh1, sc1, w_perm, s)
    r, v, kap, g, lw, k, al = _rwkv7_prep(u, rkv + 5 * d_b, bsz, s, n_ctx, shift_w, w0, w2, a0, a2, g2, k_k, k_a)
    ya = _rwkv7_recurrence(r, v, kap, lw, k, al, bsz, s, n_ctx)
    col0 = rkv // d_b
    yb = _hgrn2_recurrence(u, lb, col0, col0 + 1, col0 + 3, bsz, s, n_ctx)
    return _even_out(ya, r, v, k, g, yb, u, col0 + 4, r_k, ln_w, ln_b, norm_w, w_out, tok, g1, s)


def _hyena_filters(length, d_d, w1, b1, fr1, w2, b2, fr2, w3):
    t = jnp.linspace(0.0, 1.0, length, dtype=F32)[:, None]
    bands = (HYENA_EMB - 1) // 2
    f = jnp.linspace(1e-4, bands - 1, bands, dtype=F32)
    ang = (2.0 * math.pi / length) * jnp.arange(length, dtype=F32)[:, None] * f
    z = jnp.concatenate([t, jnp.cos(ang), -jnp.sin(ang)], axis=-1)
    hid = jnp.sin(fr1 * (jnp.dot(z, w1, precision=_HI) + b1))
    hid = jnp.sin(fr2 * (jnp.dot(hid, w2, precision=_HI) + b2))
    h = jnp.dot(hid, w3, precision=_HI).reshape(length, HYENA_ORDER, 2, d_d)
    deltas = jnp.abs(jnp.linspace(math.log(HYENA_TARGET) / HYENA_SLOW_DECAY,
                                  math.log(HYENA_TARGET) / HYENA_FAST_DECAY, d_d, dtype=F32))
    return h * jnp.exp(-t * deltas)[:, None, None, :]


DFT_R = 128
DFT_COL_TILE = 4096
DFT_K1_TILE = 4


def _hi_lo(a):
    hi = a.astype(BF16)
    return hi, (a - hi.astype(F32)).astype(BF16)


def _mm3(f_hi, f_lo, x_hi, x_lo):
    d = lambda p, q: jnp.dot(p, q, preferred_element_type=F32)
    return d(f_hi, x_hi) + (d(f_hi, x_lo) + d(f_lo, x_hi))


def _dft_matrices(n1):
    k1 = np.arange(n1, dtype=np.float64)
    k2 = np.arange(DFT_R, dtype=np.float64)
    as_f32 = lambda a: jnp.asarray(a, F32)
    cs = lambda ang: (as_f32(np.cos(ang)), as_f32(-np.sin(ang)))
    f1 = cs(2.0 * np.pi * np.outer(k1, k1) / n1)
    f2 = cs(2.0 * np.pi * np.outer(k2, k2) / DFT_R)
    tw = cs(2.0 * np.pi * np.outer(k1, k2) / (n1 * DFT_R))
    tw = tuple(jnp.broadcast_to(t[:, :, None], (n1, DFT_R, 128)) for t in tw)
    return f1, f2, tw


def _dft_rows_kernel(x_ref, frh_ref, frl_ref, fih_ref, fil_ref, or_ref, oi_ref):
    frh, frl, fih, fil = frh_ref[...], frl_ref[...], fih_ref[...], fil_ref[...]
    x0 = _hi_lo(x_ref[0])
    if x_ref.shape[0] == 1:
        or_ref[0] = _mm3(frh, frl, *x0)
        oi_ref[0] = _mm3(fih, fil, *x0)
    else:
        x1 = _hi_lo(x_ref[1])
        or_ref[0] = _mm3(frh, frl, *x0) - _mm3(fih, fil, *x1)
        oi_ref[0] = _mm3(frh, frl, *x1) + _mm3(fih, fil, *x0)


def _dft_rows(x2d, f1, pack):
    bsz, rows, cols = x2d.shape
    n1 = f1[0].shape[0]
    mats = [m for f in f1 for m in _hi_lo(f[:, :rows])]
    mat_spec = pl.BlockSpec((n1, rows), lambda b, j: (0, 0))
    out_spec = pl.BlockSpec((1, n1, DFT_COL_TILE), lambda b, j: (b, 0, j))
    out = jax.ShapeDtypeStruct((bsz // pack, n1, cols), F32)
    return pl.pallas_call(
        _dft_rows_kernel,
        grid=(bsz // pack, cols // DFT_COL_TILE),
        in_specs=[pl.BlockSpec((pack, rows, DFT_COL_TILE), lambda b, j: (b, 0, j))] + [mat_spec] * 4,
        out_specs=[out_spec, out_spec],
        out_shape=[out, out],
        name="dft_rows",
    )(x2d, *mats)


def _dft_mid_kernel(*refs, conv):
    if conv:
        (ar_ref, ai_ref, tr_ref, ti_ref, kr_ref, ki_ref, frh_ref, frl_ref, fih_ref, fil_ref,
         or_ref, oi_ref) = refs
    else:
        ar_ref, ai_ref, tr_ref, ti_ref, frh_ref, frl_ref, fih_ref, fil_ref, or_ref, oi_ref = refs
    frh, frl, fih, fil = frh_ref[...], frl_ref[...], fih_ref[...], fil_ref[...]
    reps = ar_ref.shape[3] // tr_ref.shape[2]
    rows = range(ar_ref.shape[1])
    wide = lambda t: jnp.concatenate([t] * reps, axis=-1)
    tr = [wide(tr_ref[j]) for j in rows]
    ti = [wide(ti_ref[j]) for j in rows]
    ar = [ar_ref[0, j] for j in rows]
    ai = [ai_ref[0, j] for j in rows]
    pr = [_hi_lo(ar[j] * tr[j] - ai[j] * ti[j]) for j in rows]
    pi = [_hi_lo(ar[j] * ti[j] + ai[j] * tr[j]) for j in rows]
    xr = [_mm3(frh, frl, *pr[j]) - _mm3(fih, fil, *pi[j]) for j in rows]
    xi = [_mm3(frh, frl, *pi[j]) + _mm3(fih, fil, *pr[j]) for j in rows]
    if not conv:
        for j in rows:
            or_ref[0, j] = xr[j]
            oi_ref[0, j] = xi[j]
        return
    yr = [_hi_lo(xr[j] * kr_ref[0, j] - xi[j] * ki_ref[0, j]) for j in rows]
    yi = [_hi_lo(xr[j] * ki_ref[0, j] + xi[j] * kr_ref[0, j]) for j in rows]
    br = [_mm3(frh, frl, *yr[j]) + _mm3(fih, fil, *yi[j]) for j in rows]
    bi = [_mm3(frh, frl, *yi[j]) - _mm3(fih, fil, *yr[j]) for j in rows]
    for j in rows:
        or_ref[0, j] = br[j] * tr[j] + bi[j] * ti[j]
        oi_ref[0, j] = bi[j] * tr[j] - br[j] * ti[j]


def _dft_mid(a_r, a_i, f2, tw, kf=None):
    (f_r, f_i), (t_r, t_i) = f2, tw
    bsz, n1, _, ch = a_r.shape
    slab = pl.BlockSpec((1, DFT_K1_TILE, DFT_R, ch), lambda b, g: (b, g, 0, 0))
    tw = pl.BlockSpec((DFT_K1_TILE, DFT_R, t_r.shape[2]), lambda b, g: (g, 0, 0))
    mat = pl.BlockSpec((DFT_R, DFT_R), lambda b, g: (0, 0))
    args = [a_r, a_i, t_r, t_i]
    specs = [slab, slab, tw, tw]
    if kf is not None:
        kf_r, kf_i, o = kf
        kspec = pl.BlockSpec((1, DFT_K1_TILE, DFT_R, ch), lambda b, g: (o, g, 0, 0))
        args += [kf_r, kf_i]
        specs += [kspec, kspec]
    args += [m for f in (f_r, f_i) for m in _hi_lo(f)]
    specs += [mat] * 4
    out = jax.ShapeDtypeStruct(a_r.shape, F32)
    return pl.pallas_call(
        functools.partial(_dft_mid_kernel, conv=kf is not None),
        grid=(bsz, n1 // DFT_K1_TILE),
        in_specs=specs,
        out_specs=[slab, slab],
        out_shape=[out, out],
        compiler_params=pltpu.CompilerParams(dimension_semantics=("arbitrary", "arbitrary"),
                                             vmem_limit_bytes=VMEM_LIMIT),
        name="dft_mid_conv" if kf is not None else "dft_mid",
    )(*args)


def _idft_rows_kernel(br_ref, bi_ref, z_ref, g_ref, bias_ref, grh_ref, grl_ref, gih_ref, gil_ref, o_ref):
    grh, grl, gih, gil = grh_ref[...], grl_ref[...], gih_ref[...], gil_ref[...]
    b_r, b_i = _hi_lo(br_ref[0]), _hi_lo(bi_ref[0])
    y = (_mm3(grh, grl, *b_r) + _mm3(gih, gil, *b_i), _mm3(grh, grl, *b_i) - _mm3(gih, gil, *b_r))
    for e in range(2):
        o_ref[e] = g_ref[e] * (y[e] + z_ref[e] * bias_ref[...])


def _idft_rows(b_r, b_i, z2d, gate2d, bias_cols, f1):
    bsz, rows, cols = z2d.shape
    n1 = f1[0].shape[0]
    scale = 1.0 / (n1 * DFT_R)
    mats = [m for f in f1 for m in _hi_lo(f[:rows] * scale)]
    mat_spec = pl.BlockSpec((rows, n1), lambda b, j: (0, 0))
    in_spec = pl.BlockSpec((1, n1, DFT_COL_TILE), lambda b, j: (b, 0, j))
    io_spec = pl.BlockSpec((2, rows, DFT_COL_TILE), lambda b, j: (b, 0, j))
    return pl.pallas_call(
        _idft_rows_kernel,
        grid=(bsz // 2, cols // DFT_COL_TILE),
        in_specs=[in_spec, in_spec, io_spec, io_spec, pl.BlockSpec((1, DFT_COL_TILE), lambda b, j: (0, j))]
        + [mat_spec] * 4,
        out_specs=io_spec,
        out_shape=jax.ShapeDtypeStruct((bsz, rows, cols), F32),
        name="idft_rows",
    )(b_r, b_i, z2d, gate2d, bias_cols, *mats)


def _hyena(u, short_w, w1, b1, fr1, w2, b2, fr2, w3, bias):
    bsz, length, _ = u.shape
    u = _conv3(u, short_w)
    v, x1, x2 = jnp.split(u, 3, axis=-1)
    ch = v.shape[-1]
    rows = length // DFT_R
    h = _hyena_filters(length, ch, w1, b1, fr1, w2, b2, fr2, w3)
    n1 = 2 * rows
    f1, f2, tw = _dft_matrices(n1)
    four_d = lambda t: t.reshape(t.shape[0], n1, DFT_R, ch)
    hh = h.transpose(1, 2, 0, 3).reshape(HYENA_ORDER * 2, rows, DFT_R * ch)
    hf_r, hf_i = (t.reshape(HYENA_ORDER, 2, n1, DFT_R, ch)
                  for t in _dft_mid(*(four_d(t) for t in _dft_rows(hh, f1, 1)), f2, tw))
    kf_r = hf_r[:, 0] + hf_r[:, 1]
    kf_i = hf_i[:, 0] - hf_i[:, 1]
    assert bsz % 2 == 0, "batch entries are convolved in pairs (one complex sequence per pair)"
    z = v
    for o, gate in enumerate((x1, x2)):
        z2d = z.reshape(bsz, rows, DFT_R * ch)
        a_r, a_i = _dft_rows(z2d, f1, 2)
        b_r, b_i = _dft_mid(four_d(a_r), four_d(a_i), f2, tw, kf=(kf_r, kf_i, o))
        flat = lambda t: t.reshape(bsz // 2, n1, DFT_R * ch)
        y = _idft_rows(flat(b_r), flat(b_i), z2d, gate.reshape(z2d.shape), jnp.tile(bias[o], DFT_R)[None, :], f1)
        z = y.reshape(bsz, length, ch)
    return z


def _odd_layer(tok, sh1, sc1, g1, bsz, s, n_ctx, w_in, w_out, q_norm, k_norm, rpb, short_w, w1, b1, fr1, w2,
               b2, fr2, w3, bias):
    n_heads = rpb.shape[0]
    d_c = n_heads * HEAD_DIM
    n_lat = bsz * s
    u = _mod_linear(tok, sh1, sc1, w_in, s)
    q, k, v = _na_prep(u, bsz, s, n_heads, q_norm, k_norm)
    u_ctx = u[n_lat:, d_c:3 * d_c].reshape(bsz, n_ctx, 2, n_heads, HEAD_DIM)
    head_major = lambda t: t.transpose(0, 2, 1, 3).astype(BF16)
    kc = head_major(_rms(u_ctx[:, :, 0]) * k_norm)
    vc = head_major(u_ctx[:, :, 1])
    y_na = _neighbourhood_attention(q, k, v, kc, vc, rpb)
    y_hy = _hyena(u[:n_lat, 3 * d_c:].reshape(bsz, s, -1), short_w, w1, b1, fr1, w2, b2, fr2, w3, bias)
    return _res_linear2(y_na, y_hy.reshape(n_lat, -1), w_out, tok, g1, s)


def kernel(x, c, ctx, c_ctx, mod_w, mod_b, router_w, router_b, moe_w1, moe_b1, moe_w2, moe_b2, ab_w_in, ab_w_out, rwkv_shift, rwkv_w0, rwkv_w2, rwkv_a0, rwkv_a2, rwkv_g2, rwkv_k_k, rwkv_k_a, rwkv_r_k, rwkv_ln_w, rwkv_ln_b, hgrn_lb_logits, hgrn_norm_w, cd_w_in, cd_w_out, na_q_norm, na_k_norm, na_rpb, hy_short, hy_w1, hy_b1, hy_freq1, hy_w2, hy_b2, hy_freq2, hy_w3, hy_bias):
    bsz, s, d = x.shape
    n_ctx = ctx.shape[1]
    depth = mod_w.shape[0]
    n_lat = bsz * s
    lb_all = jnp.cumsum(jax.nn.softmax(hgrn_lb_logits, axis=0), axis=0)
    tok = jnp.concatenate([x.reshape(n_lat, d), ctx.reshape(bsz * n_ctx, d)], axis=0)
    cond = jnp.concatenate([jax.nn.silu(c), jax.nn.silu(c_ctx)[None, :],
                            jnp.zeros((8 - bsz - 1, d), F32)], axis=0)
    assert depth == 2, "layer 0 = RWKV-7 || HGRN2 with context outputs, layer 1 = attention || Hyena, latent only"
    for l in range(depth):
        j = l // 2
        mod = _small_linear(cond, mod_w[l], mod_b[l])[:bsz + 1]
        sh1, sc1, g1, sh2, sc2, g2 = jnp.split(mod, 6, axis=-1)
        if l % 2 == 0:
            tok = _even_layer(tok, sh1, sc1, g1, bsz, s, n_ctx, ab_w_in[j], ab_w_out[j], rwkv_shift[j],
                              rwkv_w0[j], rwkv_w2[j], rwkv_a0[j], rwkv_a2[j], rwkv_g2[j], rwkv_k_k[j],
                              rwkv_k_a[j], rwkv_r_k[j], rwkv_ln_w[j], rwkv_ln_b[j], lb_all[j], hgrn_norm_w[j])
        else:
            tok = _odd_layer(tok, sh1, sc1, g1, bsz, s, n_ctx, cd_w_in[j], cd_w_out[j], na_q_norm[j], na_k_norm[j],
                             na_rpb[j], hy_short[j], hy_w1[j], hy_b1[j], hy_freq1[j], hy_w2[j], hy_b2[j],
                             hy_freq2[j], hy_w3[j], hy_bias[j])
        yk, gate = _moe(tok, sh2, sc2, s, router_w[l], router_b[l], moe_w1, moe_w2, l, moe_b1[l], moe_b2[l])
        tok = _combine(tok, yk, gate, g2, s)
    return tok[:n_lat].reshape(bsz, s, d)


def _combine_kernel(x_ref, y_ref, w_ref, g_ref, o_ref):
    reps = x_ref.shape[1] // w_ref.shape[2]
    acc = y_ref[0] * jnp.concatenate([w_ref[0]] * reps, axis=-1)
    for k in range(1, y_ref.shape[0]):
        acc = acc + y_ref[k] * jnp.concatenate([w_ref[k]] * reps, axis=-1)
    o_ref[...] = x_ref[...] + g_ref[0] * acc


def _combine(x, yk, gate, res_gate, rows_per_mod):
    m, d = x.shape
    n_k = yk.shape[0]
    rows = COMBINE_TILE
    tiles_per_mod = rows_per_mod // rows
    tile = pl.BlockSpec((rows, d), lambda i: (i, 0))
    lanes = 128
    return pl.pallas_call(
        _combine_kernel,
        grid=(m // rows,),
        in_specs=[tile, pl.BlockSpec((n_k, rows, d), lambda i: (0, i, 0)),
                  pl.BlockSpec((n_k, rows, lanes), lambda i: (0, i, 0)),
                  pl.BlockSpec((1, 1, d), lambda i: (i // tiles_per_mod, 0, 0))],
        out_specs=tile,
        out_shape=jax.ShapeDtypeStruct((m, d), F32),
        compiler_params=pltpu.CompilerParams(dimension_semantics=("arbitrary",),
                                             vmem_limit_bytes=VMEM_LIMIT),
        name="moe_combine",
    )(x, yk, jnp.broadcast_to(gate[:, :, None], gate.shape + (lanes,)), res_gate[:, None, :])
```

```python
import functools
import math

import numpy as np
import jax
import jax.numpy as jnp
from jax import lax
from jax.experimental import pallas as pl
from jax.experimental.pallas import tpu as pltpu

F32 = jnp.float32
BF16 = jnp.bfloat16

HEAD_DIM = 64
GRID_W = 64
DECAY_LORA = 64
AAA_LORA = 64
GATE_LORA = 128
LN_X_EPS = 1e-5 * HEAD_DIM
NA_ROWS = 8
NA_COLS = 16
ROPE_THETA = 10000.0
HYENA_ORDER = 2
HYENA_EMB = 33
HYENA_FAST_DECAY = 0.3
HYENA_SLOW_DECAY = 1.5
HYENA_TARGET = 1e-2
N_EXPERTS = 32
TOP_K = 4
SWIGLU_ALPHA = 1.702
SWIGLU_LIMIT = 7.0
MOE_BLOCK = 256
MOE_CALLS = 2
RMS_EPS = 1e-6

ROW_TILE = 256
COMBINE_TILE = 512
REC_CHUNK = 64
REC_PASSES = 1
REC_CHAIN_PASSES = 1
VMEM_LIMIT = 56 * 1024 * 1024

_HI = lax.Precision.HIGHEST


_DIMS = {"nn": (((1,), (0,)), ((), ())), "nt": (((1,), (1,)), ((), ())), "tn": (((0,), (0,)), ((), ()))}


def _mm(a, b, form, passes):
    dims = _DIMS[form]
    if passes == 6:
        return lax.dot_general(a, b, dims, precision=_HI, preferred_element_type=F32)
    a_hi = a.astype(BF16)
    b_hi = b.astype(BF16)
    out = lax.dot_general(a_hi, b_hi, dims, preferred_element_type=F32)
    if passes == 3:
        a_lo = (a - a_hi.astype(F32)).astype(BF16)
        b_lo = (b - b_hi.astype(F32)).astype(BF16)
        out = out + (lax.dot_general(a_hi, b_lo, dims, preferred_element_type=F32)
                     + lax.dot_general(a_lo, b_hi, dims, preferred_element_type=F32))
    return out


def _dot(a, b):
    return _mm(a, b, "nn", 6)


def _mod_linear_kernel(x_ref, sh_ref, sc_ref, w_ref, o_ref):
    x = x_ref[...]
    ms = jnp.mean(x * x, axis=-1, keepdims=True)
    h = x * lax.rsqrt(ms + RMS_EPS) * (1.0 + sc_ref[0]) + sh_ref[0]
    o_ref[...] = jnp.dot(h.astype(BF16), w_ref[...], preferred_element_type=F32)


def _mod_linear(x, shift, scale, w, rows_per_mod):
    m, d = x.shape
    n = w.shape[1]
    tiles_per_mod = rows_per_mod // ROW_TILE
    mod_spec = pl.BlockSpec((1, 1, d), lambda i: (i // tiles_per_mod, 0, 0))
    return pl.pallas_call(
        _mod_linear_kernel,
        grid=(m // ROW_TILE,),
        in_specs=[pl.BlockSpec((ROW_TILE, d), lambda i: (i, 0)), mod_spec, mod_spec,
                  pl.BlockSpec((d, n), lambda i: (0, 0))],
        out_specs=pl.BlockSpec((ROW_TILE, n), lambda i: (i, 0)),
        out_shape=jax.ShapeDtypeStruct((m, n), F32),
        compiler_params=pltpu.CompilerParams(dimension_semantics=("arbitrary",),
                                             vmem_limit_bytes=VMEM_LIMIT),
        name="mod_linear",
    )(x, shift[:, None, :], scale[:, None, :], w.astype(BF16))


def _res_linear2_kernel(ya_ref, yb_ref, w_ref, res_ref, g_ref, o_ref):
    y = jnp.concatenate([ya_ref[...], yb_ref[...]], axis=-1).astype(BF16)
    o_ref[...] = res_ref[...] + g_ref[0] * jnp.dot(y, w_ref[...], preferred_element_type=F32)


def _res_linear2(ya, yb, w, res, gate, rows_per_mod):
    m, ka = ya.shape
    kb = yb.shape[1]
    n = w.shape[1]
    tiles_per_mod = rows_per_mod // ROW_TILE
    return pl.pallas_call(
        _res_linear2_kernel,
        grid=(m // ROW_TILE,),
        in_specs=[pl.BlockSpec((ROW_TILE, ka), lambda i: (i, 0)), pl.BlockSpec((ROW_TILE, kb), lambda i: (i, 0)),
                  pl.BlockSpec((ka + kb, n), lambda i: (0, 0)),
                  pl.BlockSpec((ROW_TILE, n), lambda i: (i, 0)),
                  pl.BlockSpec((1, 1, n), lambda i: (i // tiles_per_mod, 0, 0))],
        out_specs=pl.BlockSpec((ROW_TILE, n), lambda i: (i, 0)),
        out_shape=jax.ShapeDtypeStruct((m, n), F32),
        compiler_params=pltpu.CompilerParams(dimension_semantics=("arbitrary",),
                                             vmem_limit_bytes=VMEM_LIMIT),
        name="res_linear",
    )(ya, yb, w.astype(BF16), res, gate[:, None, :])


def _small_linear_kernel(x_ref, w_ref, b_ref, o_ref):
    o_ref[...] = _dot(x_ref[...], w_ref[...]) + b_ref[...]


def _small_linear(x, w, b):
    m, k = x.shape
    n = w.shape[1]
    tn = 1024 if n % 1024 == 0 else n
    return pl.pallas_call(
        _small_linear_kernel,
        grid=(n // tn,),
        in_specs=[pl.BlockSpec((m, k), lambda j: (0, 0)), pl.BlockSpec((k, tn), lambda j: (0, j)),
                  pl.BlockSpec((1, tn), lambda j: (0, j))],
        out_specs=pl.BlockSpec((m, tn), lambda j: (0, j)),
        out_shape=jax.ShapeDtypeStruct((m, n), F32),
        name="small_linear",
    )(x, w, b[None, :])


def _rec_kernel(*refs, delta, n_heads):
    n_in = 6 if delta else 3
    ins = (refs[:n_in], refs[n_in:2 * n_in])
    rest = refs[2 * n_in:]
    if delta:
        y_refs, s_ref = rest[:2], rest[2]
    else:
        lb_ref, y_refs, s_ref = rest[0], rest[1:3], rest[3]

    @pl.when(pl.program_id(1) == 0)
    def _():
        s_ref[...] = jnp.zeros_like(s_ref)

    c = ins[0][0].shape[0]
    row = lax.broadcasted_iota(jnp.int32, (c, c), 0)
    col = lax.broadcasted_iota(jnp.int32, (c, c), 1)
    mid = c // 2
    heads = range(n_heads)
    hs = lambda t: [t[:, h * HEAD_DIM:(h + 1) * HEAD_DIM] for h in heads]
    mm = functools.partial(_mm, passes=REC_PASSES)

    streams = []
    for d in range(2):
        ahead = row - col if d == 0 else col - row
        incl, strict = ahead >= 0, ahead > 0
        if delta:
            r_ref, v_ref, kap_ref, lw_ref, k_ref, al_ref = ins[d]
            r_all, v_all, lw, k = r_ref[...], v_ref[...], lw_ref[0], k_ref[0]
        else:
            q_ref, i_ref, f_ref = ins[d]
            fg = lb_ref[...] + (1.0 - lb_ref[...]) * jax.nn.sigmoid(f_ref[...])
            r_all, v_all, lw, k = jax.nn.silu(q_ref[...]), i_ref[...], jnp.log(fg), 1.0 - fg
        b = _mm(incl.astype(F32), lw, "nn", 6)
        bm = b[mid:mid + 1, :]
        tot = b[c - 1:c, :] if d == 0 else b[0:1, :]
        e_neg = jnp.exp(bm - b)
        e_end = jnp.exp(tot - b)
        st = dict(incl=incl, strict=strict, v=hs(v_all), rq=hs(r_all * jnp.exp(b - bm)), kd=hs(k * e_neg),
                  k_end=hs(k * e_end), g_mid=hs(jnp.exp(bm)), g_tot=hs(jnp.exp(tot)))
        if delta:
            al = al_ref[0]
            st.update(kq=hs(kap_ref[...] * jnp.exp(b - lw - bm)), ad=hs(al * e_neg), al_end=hs(al * e_end))
        streams.append(st)

    units = [(d, h) for d in range(2) for h in heads]
    n_u = range(len(units))
    per_head = lambda name: [streams[d][name][h] for d, h in units]
    per_dir = lambda name: [streams[d][name] for d, _ in units]
    incl, strict = per_dir("incl"), per_dir("strict")
    v, rq, kd, k_end, g_mid, g_tot = (per_head(n) for n in ("v", "rq", "kd", "k_end", "g_mid", "g_tot"))
    s0 = [s_ref[d, h] for d, h in units]
    s0m = [s0[n] * g_mid[n] for n in n_u]
    if delta:
        kq, ad, al_end = per_head("kq"), per_head("ad"), per_head("al_end")
        q2 = [jnp.concatenate([kq[n], rq[n]], axis=0) for n in n_u]
        k2 = [jnp.concatenate([kd[n], ad[n]], axis=0) for n in n_u]
        a = [mm(q2[n], k2[n], "nt") for n in n_u]
        p = [mm(q2[n], s0m[n], "nt") for n in n_u]
        z = [-(p[n][:c] + mm(jnp.where(strict[n], a[n][:c, :c], 0.0), v[n], "nn")) for n in n_u]
        mc = functools.partial(_mm, passes=REC_CHAIN_PASSES)
        m = [jnp.where(strict[n], a[n][:c, c:], 0.0) for n in n_u]
        pair = (row >> 1) == (col >> 1)
        eye = (row == col).astype(F32)
        t = [eye - jnp.where(pair, m[n], 0.0) for n in n_u]
        for lvl in range(1, int(math.log2(c))):
            off = ((row >> (lvl + 1)) == (col >> (lvl + 1))) & ((row >> lvl) != (col >> lvl))
            tc = [mc(t[n], jnp.where(off, m[n], 0.0), "nn") for n in n_u]
            t = [t[n] - mc(tc[n], t[n], "nn") for n in n_u]
        z = [mc(t[n], z[n], "nn") for n in n_u]
        y = [p[n][c:] + mm(jnp.where(incl[n], a[n][c:, :c], 0.0), v[n], "nn")
             + mm(jnp.where(incl[n], a[n][c:, c:], 0.0), z[n], "nn") for n in n_u]
        s_new = [s0[n] * g_tot[n] + mm(v[n], k_end[n], "tn") + mm(z[n], al_end[n], "tn") for n in n_u]
    else:
        a = [mm(rq[n], kd[n], "nt") for n in n_u]
        y = [mm(rq[n], s0m[n], "nt") + mm(jnp.where(incl[n], a[n], 0.0), v[n], "nn") for n in n_u]
        s_new = [s0[n] * g_tot[n] + mm(v[n], k_end[n], "tn") for n in n_u]
    for n, (d, h) in enumerate(units):
        s_ref[d, h] = s_new[n]
    for d in range(2):
        y_refs[d][...] = jnp.concatenate(y[d * n_heads:(d + 1) * n_heads], axis=-1)


def _seq_block_index(bsz, s, n_ctx, rows):
    n_c, n_l = n_ctx // rows, s // rows

    def index(d, b, i):
        back = jnp.where(i < n_c, n_c - 1 - i, n_l + 2 * n_c - 1 - i)
        pos = jnp.where(d == 0, i, back)
        return jnp.where(pos < n_c, bsz * n_l + b * n_c + pos, b * n_l + pos - n_c)

    return index


def _rec_call(args, specs, bsz, s, n_ctx, width, delta):
    c = REC_CHUNK
    n_heads = width // HEAD_DIM
    blk = _seq_block_index(bsz, s, n_ctx, c)
    out = jax.ShapeDtypeStruct((bsz * (s + n_ctx), width), F32)
    return pl.pallas_call(
        functools.partial(_rec_kernel, delta=delta, n_heads=n_heads),
        grid=(bsz, (s + n_ctx) // c),
        in_specs=specs,
        out_specs=[pl.BlockSpec((c, width), lambda b, i, d=d: (blk(d, b, i), 0)) for d in range(2)],
        out_shape=[out, out],
        scratch_shapes=[pltpu.VMEM((2, n_heads, HEAD_DIM, HEAD_DIM), F32)],
        compiler_params=pltpu.CompilerParams(dimension_semantics=("arbitrary", "arbitrary")),
        name="rwkv7_rec" if delta else "hgrn2_rec",
    )(*args)


def _rwkv7_recurrence(r, v, kap, lw, k, al, bsz, s, n_ctx):
    width = r.shape[1]
    blk = _seq_block_index(bsz, s, n_ctx, REC_CHUNK)
    specs = []
    for d in range(2):
        shared = pl.BlockSpec((REC_CHUNK, width), lambda b, i, d=d: (blk(d, b, i), 0))
        per_dir = pl.BlockSpec((1, REC_CHUNK, width), lambda b, i, d=d: (d, blk(d, b, i), 0))
        specs += [shared] * 3 + [per_dir] * 3
    return _rec_call([r, v, kap, lw, k, al] * 2, specs, bsz, s, n_ctx, width, True)


def _hgrn2_recurrence(u, lb, col_q, col_f, col_i, bsz, s, n_ctx):
    width = lb.shape[0]
    blk = _seq_block_index(bsz, s, n_ctx, REC_CHUNK)
    specs = []
    for d in range(2):
        specs += [pl.BlockSpec((REC_CHUNK, width), lambda b, i, d=d, j=j: (blk(d, b, i), j))
                  for j in (col_q, col_i, col_f + d)]
    specs.append(pl.BlockSpec((1, width), lambda b, i: (0, 0)))
    return _rec_call([u] * 6 + [lb[None, :]], specs, bsz, s, n_ctx, width, False)


def _na_kernel(q_ref, k_ref, v_ref, kc_ref, vc_ref, bias_ref, o_ref, *, rows_per_step, n_rows):
    g = pl.program_id(2)
    band = NA_ROWS * GRID_W
    n_h = q_ref.shape[1]
    units = [(h, j) for j in range(rows_per_step) for h in range(n_h)]
    nt = lambda p, q_: lax.dot_general(p, q_, (((1,), (1,)), ((), ())), preferred_element_type=F32)
    qr = [g * rows_per_step + j for j in range(rows_per_step)]
    start = [jnp.clip(r - NA_ROWS // 2, 0, n_rows - NA_ROWS) for r in qr]
    off = [pl.multiple_of(st * GRID_W, GRID_W) for st in start]
    q = [q_ref[0, h, j * GRID_W:(j + 1) * GRID_W, :] for h, j in units]
    s_win = [nt(q[n], k_ref[0, h, pl.ds(off[j], band), :]) + bias_ref[h, start[j] - qr[j] + NA_ROWS - 1]
             for n, (h, j) in enumerate(units)]
    s_ctx = [nt(q[n], kc_ref[0, h]) for n, (h, j) in enumerate(units)]
    ns = range(len(units))
    m = [jnp.maximum(jnp.max(s_win[n], axis=-1, keepdims=True), jnp.max(s_ctx[n], axis=-1, keepdims=True))
         for n in ns]
    p_win = [jnp.exp(s_win[n] - m[n]) for n in ns]
    p_ctx = [jnp.exp(s_ctx[n] - m[n]) for n in ns]
    den = [jnp.sum(p_win[n], axis=-1, keepdims=True) + jnp.sum(p_ctx[n], axis=-1, keepdims=True) for n in ns]
    o = [(jnp.dot(p_win[n].astype(BF16), v_ref[0, h, pl.ds(off[j], band), :], preferred_element_type=F32)
          + jnp.dot(p_ctx[n].astype(BF16), vc_ref[0, h], preferred_element_type=F32)) / den[n]
         for n, (h, j) in enumerate(units)]
    for j in range(rows_per_step):
        o_ref[j * GRID_W:(j + 1) * GRID_W, :] = jnp.concatenate(o[j * n_h:(j + 1) * n_h], axis=-1)


def _na_bias_table(rpb):
    cc = np.arange(GRID_W)
    col_start = np.clip(cc - NA_COLS // 2, 0, GRID_W - NA_COLS)
    kc = np.arange(GRID_W)
    inside = (kc[None, :] >= col_start[:, None]) & (kc[None, :] < col_start[:, None] + NA_COLS)
    col_off = np.clip(kc[None, :] - cc[:, None] + NA_COLS - 1, 0, 2 * NA_COLS - 2)
    tab = rpb[:, :, col_off]
    tab = jnp.where(jnp.asarray(inside)[None, None], tab, -jnp.inf)
    d0 = np.arange(NA_ROWS)[:, None] + np.arange(NA_ROWS)[None, :]
    band = tab[:, d0]
    return band.transpose(0, 1, 3, 2, 4).reshape(rpb.shape[0], NA_ROWS, GRID_W, NA_ROWS * GRID_W)


NA_HEADS_PER_STEP = 128 // HEAD_DIM


def _neighbourhood_attention(q, k, v, kc, vc, rpb):
    bsz, n_heads, s, dh = q.shape
    n_rows = s // GRID_W
    rows_per_step = 8
    hp = NA_HEADS_PER_STEP
    lc = kc.shape[2]
    bias = _na_bias_table(rpb)
    steps = n_rows // rows_per_step
    full = pl.BlockSpec((1, hp, s, dh), lambda b, h, g: (b, h, 0, 0))
    ctx = pl.BlockSpec((1, hp, lc, dh), lambda b, h, g: (b, h, 0, 0))
    return pl.pallas_call(
        functools.partial(_na_kernel, rows_per_step=rows_per_step, n_rows=n_rows),
        grid=(bsz, n_heads // hp, steps),
        in_specs=[pl.BlockSpec((1, hp, rows_per_step * GRID_W, dh), lambda b, h, g: (b, h, g, 0)),
                  full, full, ctx, ctx,
                  pl.BlockSpec((hp, NA_ROWS, GRID_W, NA_ROWS * GRID_W), lambda b, h, g: (h, 0, 0, 0))],
        out_specs=pl.BlockSpec((rows_per_step * GRID_W, hp * dh), lambda b, h, g: (b * steps + g, h)),
        out_shape=jax.ShapeDtypeStruct((bsz * s, n_heads * dh), F32),
        compiler_params=pltpu.CompilerParams(
            dimension_semantics=("arbitrary", "arbitrary", "arbitrary"), vmem_limit_bytes=VMEM_LIMIT),
        name="neighbourhood_attention",
    )(q, k, v, kc, vc, bias)


def _na_prep_kernel(u_ref, cos_ref, sin_ref, qn_ref, kn_ref, bd_ref, q_ref, k_ref, v_ref, *, n_heads):
    width = n_heads * HEAD_DIM
    bd = bd_ref[...]
    lane = lax.broadcasted_iota(jnp.int32, (u_ref.shape[0], 128), 1)
    first = (lane & (HEAD_DIM // 4)) == 0

    def rope(x):
        parts = []
        for j in range(width // 128):
            cols = slice(128 * j, 128 * (j + 1))
            xj = x[:, cols]
            partner = jnp.where(first, pltpu.roll(xj, 128 - HEAD_DIM // 4, axis=1),
                                pltpu.roll(xj, HEAD_DIM // 4, axis=1))
            parts.append(xj * cos_ref[:, cols] + partner * sin_ref[:, cols])
        return jnp.concatenate(parts, axis=-1)

    def normed(x, w_ref):
        return x * lax.rsqrt(_seg_sum(x * x, bd) * (1.0 / HEAD_DIM) + RMS_EPS) * w_ref[...]

    q = rope(normed(u_ref[:, :width], qn_ref)) * HEAD_DIM ** -0.5
    k = rope(normed(u_ref[:, width:2 * width], kn_ref))
    v = u_ref[:, 2 * width:3 * width]
    for h in range(n_heads):
        cols = slice(h * HEAD_DIM, (h + 1) * HEAD_DIM)
        q_ref[0, h] = q[:, cols].astype(BF16)
        k_ref[0, h] = k[:, cols].astype(BF16)
        v_ref[0, h] = v[:, cols].astype(BF16)


def _rope_tables(s, n_heads):
    t = jnp.arange(s)
    nf = HEAD_DIM // 4
    inv = ROPE_THETA ** (-jnp.arange(nf, dtype=F32) / nf)
    ang_r = (t // GRID_W).astype(F32)[:, None] * inv
    ang_c = (t % GRID_W).astype(F32)[:, None] * inv
    cos = jnp.concatenate([jnp.cos(ang_r)] * 2 + [jnp.cos(ang_c)] * 2, axis=-1)
    sin = jnp.concatenate([-jnp.sin(ang_r), jnp.sin(ang_r), -jnp.sin(ang_c), jnp.sin(ang_c)], axis=-1)
    return jnp.tile(cos, (1, n_heads)), jnp.tile(sin, (1, n_heads))


def _na_prep(u, bsz, s, n_heads, q_norm, k_norm):
    width = n_heads * HEAD_DIM
    tile = ROW_TILE
    per_batch = s // tile
    cos, sin = _rope_tables(s, n_heads)
    tab = pl.BlockSpec((tile, width), lambda i: (i % per_batch, 0))
    vec = pl.BlockSpec((1, width), lambda i: (0, 0))
    out_spec = pl.BlockSpec((1, n_heads, tile, HEAD_DIM), lambda i: (i // per_batch, 0, i % per_batch, 0))
    out = jax.ShapeDtypeStruct((bsz, n_heads, s, HEAD_DIM), BF16)
    return pl.pallas_call(
        functools.partial(_na_prep_kernel, n_heads=n_heads),
        grid=(bsz * per_batch,),
        in_specs=[pl.BlockSpec((tile, 3 * width), lambda i: (i, 0)), tab, tab, vec, vec,
                  pl.BlockSpec((width, width), lambda i: (0, 0))],
        out_specs=[out_spec, out_spec, out_spec],
        out_shape=[out, out, out],
        name="na_prep",
    )(u, cos, sin, jnp.tile(q_norm, n_heads)[None, :], jnp.tile(k_norm, n_heads)[None, :], _head_ones(width))


DEINT_GROUP = 256


def _moe_kernel(be_ref, nb_ref, x_ref, w1_ref, b1_ref, w2_ref, b2_ref, perm_ref, *rest, first_block):
    o_ref, w1_s, w2_s = rest[-3:]
    blk = first_block + pl.program_id(0)
    active = blk < nb_ref[0]
    fresh = jnp.logical_or(pl.program_id(0) == 0, be_ref[blk] != be_ref[jnp.maximum(blk - 1, 0)])

    @pl.when(jnp.logical_and(active, fresh))
    def _():
        half = w1_s.shape[1] // 2
        g_out = DEINT_GROUP // 2
        for g in range(w1_s.shape[1] // DEINT_GROUP):
            t = jnp.dot(w1_ref[0, 0, :, g * DEINT_GROUP:(g + 1) * DEINT_GROUP].astype(BF16), perm_ref[...],
                        preferred_element_type=F32).astype(BF16)
            w1_s[:, g * g_out:(g + 1) * g_out] = t[:, :g_out]
            w1_s[:, half + g * g_out:half + (g + 1) * g_out] = t[:, g_out:]
        w2_s[...] = w2_ref[0, 0].astype(BF16)

    @pl.when(active)
    def _():
        y = jnp.dot(x_ref[...], w1_s[...], preferred_element_type=F32) + b1_ref[0]
        half = y.shape[1] // 2
        glu = jnp.minimum(y[:, :half], SWIGLU_LIMIT)
        lin = jnp.clip(y[:, half:], -SWIGLU_LIMIT, SWIGLU_LIMIT)
        act = glu * jax.nn.sigmoid(SWIGLU_ALPHA * glu) * (lin + 1.0)
        o_ref[...] = jnp.dot(act.astype(BF16), w2_s[...], preferred_element_type=F32) + b2_ref[0]

    @pl.when(jnp.logical_not(active))
    def _():
        o_ref[...] = jnp.zeros_like(o_ref)


def _moe_experts(x_part, first_block, n_blocks_total, y_prev, blk_e, n_used, w1_layers, w2_layers, layer, b1, b2):
    d = x_part.shape[1]
    n_blocks = x_part.shape[0] // MOE_BLOCK
    n_slots = n_blocks_total * MOE_BLOCK
    off = first_block
    de2 = w1_layers.shape[3]
    perm = np.zeros((DEINT_GROUP, DEINT_GROUP), np.float32)
    idx = np.arange(DEINT_GROUP // 2)
    perm[2 * idx, idx] = 1.0
    perm[2 * idx + 1, DEINT_GROUP // 2 + idx] = 1.0
    grid_spec = pltpu.PrefetchScalarGridSpec(
        num_scalar_prefetch=2,
        grid=(n_blocks,),
        in_specs=[pl.BlockSpec((MOE_BLOCK, d), lambda i, be, nb: (i, 0)),
                  pl.BlockSpec((1, 1, d, de2), lambda i, be, nb: (layer, be[off + i], 0, 0)),
                  pl.BlockSpec((1, 1, de2), lambda i, be, nb: (be[off + i], 0, 0)),
                  pl.BlockSpec((1, 1, de2 // 2, d), lambda i, be, nb: (layer, be[off + i], 0, 0)),
                  pl.BlockSpec((1, 1, d), lambda i, be, nb: (be[off + i], 0, 0)),
                  pl.BlockSpec((DEINT_GROUP, DEINT_GROUP), lambda i, be, nb: (0, 0))]
        + ([] if y_prev is None else [pl.BlockSpec(memory_space=pl.ANY)]),
        out_specs=pl.BlockSpec((MOE_BLOCK, d), lambda i, be, nb: (off + i, 0)),
        scratch_shapes=[pltpu.VMEM((d, de2), BF16), pltpu.VMEM((de2 // 2, d), BF16)],
    )
    args = [blk_e, n_used, x_part, w1_layers, b1[:, None, :], w2_layers, b2[:, None, :], jnp.asarray(perm, BF16)]
    return pl.pallas_call(
        functools.partial(_moe_kernel, first_block=first_block),
        grid_spec=grid_spec,
        out_shape=jax.ShapeDtypeStruct((n_slots, d), F32),
        input_output_aliases={} if y_prev is None else {len(args): 0},
        compiler_params=pltpu.CompilerParams(dimension_semantics=("arbitrary",),
                                             vmem_limit_bytes=VMEM_LIMIT),
        name="moe_experts",
    )(*args, *([] if y_prev is None else [y_prev]))


ROUTE_TILE = 1024


def _route_kernel(x_ref, sh_ref, sc_ref, wt_ref, b_ref, before_ref, h_ref, e_ref, g_ref, r_ref, cnt_ref,
                  carry_ref):
    i = pl.program_id(0)

    @pl.when(i == 0)
    def _():
        carry_ref[...] = jnp.zeros_like(carry_ref)

    x = x_ref[...]
    ms = jnp.mean(x * x, axis=-1, keepdims=True)
    h = x * lax.rsqrt(ms + RMS_EPS) * (1.0 + sc_ref[0]) + sh_ref[0]
    h_ref[...] = h.astype(BF16)
    logits = _mm(wt_ref[...], h, "nt", 6) + b_ref[...]
    n_e, tm = logits.shape
    eidx = lax.broadcasted_iota(jnp.int32, (n_e, tm), 0)
    work = logits
    top_v, top_e = [], []
    for _ in range(TOP_K):
        m = jnp.max(work, axis=0, keepdims=True)
        sel = jnp.min(jnp.where(work == m, eidx, n_e), axis=0, keepdims=True)
        top_v.append(m)
        top_e.append(sel)
        work = jnp.where(eidx == sel, -jnp.inf, work)
    ex = [jnp.exp(v - top_v[0]) for v in top_v]
    den = ex[0] + ex[1] + ex[2] + ex[3]
    g_ref[...] = jnp.concatenate([e_ / den for e_ in ex], axis=0)
    e_ref[...] = jnp.concatenate(top_e, axis=0)
    chosen = [eidx == sel for sel in top_e]
    ind = sum(c.astype(F32) for c in chosen)
    carry = carry_ref[...]
    cnt = (jnp.dot(ind.astype(BF16), before_ref[...], preferred_element_type=F32)
           + jnp.concatenate([carry] * (tm // carry.shape[1]), axis=1))
    r_ref[...] = jnp.concatenate(
        [jnp.sum(jnp.where(c, cnt, 0.0), axis=0, keepdims=True) for c in chosen], axis=0).astype(jnp.int32)
    carry = carry + jnp.sum(ind, axis=1, keepdims=True)
    carry_ref[...] = carry
    cnt_ref[...] = carry


def _route(tok, shift, scale, router_w, router_b, rows_per_mod):
    t, d = tok.shape
    n_e = router_w.shape[1]
    tm = ROUTE_TILE
    tiles_per_mod = rows_per_mod // tm
    mod_spec = pl.BlockSpec((1, 1, d), lambda i: (i // tiles_per_mod, 0, 0))
    before = jnp.asarray(np.triu(np.ones((tm, tm), np.float32), 1), BF16)
    kt_spec = pl.BlockSpec((TOP_K, tm), lambda i: (0, i))
    h, top_e, gate, rank, cnt = pl.pallas_call(
        _route_kernel,
        grid=(t // tm,),
        in_specs=[pl.BlockSpec((tm, d), lambda i: (i, 0)), mod_spec, mod_spec,
                  pl.BlockSpec((n_e, d), lambda i: (0, 0)), pl.BlockSpec((n_e, tm), lambda i: (0, 0)),
                  pl.BlockSpec((tm, tm), lambda i: (0, 0))],
        out_specs=[pl.BlockSpec((tm, d), lambda i: (i, 0)), kt_spec, kt_spec, kt_spec,
                   pl.BlockSpec((n_e, 128), lambda i: (0, 0))],
        out_shape=[jax.ShapeDtypeStruct((t, d), BF16), jax.ShapeDtypeStruct((TOP_K, t), jnp.int32),
                   jax.ShapeDtypeStruct((TOP_K, t), F32), jax.ShapeDtypeStruct((TOP_K, t), jnp.int32),
                   jax.ShapeDtypeStruct((n_e, 128), F32)],
        scratch_shapes=[pltpu.VMEM((n_e, 128), F32)],
        compiler_params=pltpu.CompilerParams(dimension_semantics=("arbitrary",),
                                             vmem_limit_bytes=VMEM_LIMIT),
        name="moe_route",
    )(tok, shift[:, None, :], scale[:, None, :], router_w.T, jnp.broadcast_to(router_b[:, None], (n_e, tm)),
      before)
    return h, top_e, gate, rank, cnt[:, 0].astype(jnp.int32)


def _moe(tok, shift, scale, rows_per_mod, router_w, router_b, w1_layers, w2_layers, layer, b1, b2):
    t, d = tok.shape
    h, top_e, gate, rank, counts = _route(tok, shift, scale, router_w, router_b, rows_per_mod)
    n_assign = t * TOP_K
    n_blocks = -(-n_assign // MOE_BLOCK) + N_EXPERTS
    n_slots = n_blocks * MOE_BLOCK
    padded = (counts + MOE_BLOCK - 1) // MOE_BLOCK * MOE_BLOCK
    start = jnp.cumsum(counts) - counts
    p_end = jnp.cumsum(padded)
    p_start = p_end - padded
    experts = jnp.arange(N_EXPERTS, dtype=jnp.int32)
    dest = jnp.sum(jnp.where(top_e[:, :, None] == experts, p_start, 0), axis=-1) + rank
    blk_first = jnp.arange(n_blocks, dtype=jnp.int32) * MOE_BLOCK
    blk_e = jnp.minimum(jnp.sum(p_end[None, :] <= blk_first[:, None], axis=1), N_EXPERTS - 1).astype(jnp.int32)
    n_used = (p_end[-1] // MOE_BLOCK).astype(jnp.int32).reshape(1)
    order = jnp.argsort(top_e.T.reshape(-1))
    per_slot = lambda per_expert: jnp.repeat(per_expert[blk_e], MOE_BLOCK)
    j = jnp.arange(n_slots, dtype=jnp.int32) - per_slot(p_start)
    src = jnp.clip(per_slot(start) + j, 0, n_assign - 1)
    slot_tok = jnp.where(j < per_slot(counts), order[src] // TOP_K, t).astype(jnp.int32)
    h_pad = jnp.concatenate([h, jnp.zeros((1, d), BF16)], axis=0)
    b1p = jnp.concatenate([b1[..., ::2], b1[..., 1::2]], axis=-1)
    y = None
    for first in range(0, n_blocks, n_blocks // MOE_CALLS):
        sl = slice(first * MOE_BLOCK, (first + n_blocks // MOE_CALLS) * MOE_BLOCK)
        y = _moe_experts(h_pad[slot_tok[sl]], first, n_blocks, y, blk_e, n_used, w1_layers, w2_layers, layer,
                         b1p, b2)
    return y[dest], gate


def _rms(x):
    return x * lax.rsqrt(jnp.mean(x * x, axis=-1, keepdims=True) + RMS_EPS)


def _conv3(u, w):
    up = jnp.pad(u, ((0, 0), (1, 1), (0, 0)))
    return up[:, :-2] * w[0] + up[:, 1:-1] * w[1] + up[:, 2:] * w[2]


def _seg_sum(x, ones_bd):
    x_hi, x_lo = _hi_lo(x)
    return (jnp.dot(x_hi, ones_bd, preferred_element_type=F32)
            + jnp.dot(x_lo, ones_bd, preferred_element_type=F32))


def _head_ones(width):
    h = np.arange(width) // HEAD_DIM
    return jnp.asarray(h[:, None] == h[None, :], BF16)


def _rwkv7_prep_kernel(first_ref, last_ref, u_ref, up_ref, un_ref, sw_ref, w0_ref, w2_ref, a0_ref, a2_ref,
                       g2_ref, kk_ref, ka_ref, bd_ref, r_ref, v_ref, kap_ref, g_ref, lw_ref, k_ref, al_ref,
                       *, d_a, lora_off):
    i = pl.program_id(0)
    keep_prev = (first_ref[i] == 0).astype(F32)
    keep_next = (last_ref[i] == 0).astype(F32)
    rows = u_ref.shape[0]

    def conv(lo, width, tap_lo):
        x = u_ref[:, lo:lo + width]
        ridx = lax.broadcasted_iota(jnp.int32, x.shape, 0)
        x_prev = jnp.where(ridx == 0, up_ref[7:8, lo:lo + width] * keep_prev, pltpu.roll(x, 1, axis=0))
        x_next = jnp.where(ridx == rows - 1, un_ref[0:1, lo:lo + width] * keep_next,
                           pltpu.roll(x, rows - 1, axis=0))
        w = sw_ref[:, tap_lo:tap_lo + width]
        return x_prev * w[0:1] + x * w[1:2] + x_next * w[2:3]

    r = conv(0, d_a, 0)
    k = conv(d_a, d_a, d_a)
    v = conv(2 * d_a, d_a, 2 * d_a)
    wd = conv(lora_off, 2 * DECAY_LORA, 3 * d_a)
    ad = conv(lora_off + 2 * DECAY_LORA, 2 * AAA_LORA, 3 * d_a + 2 * DECAY_LORA)
    gd = conv(lora_off + 2 * DECAY_LORA + 2 * AAA_LORA, GATE_LORA, 3 * d_a + 2 * DECAY_LORA + 2 * AAA_LORA)
    mm3 = lambda p, q: _mm(p, q, "nn", 3)
    kk = k * kk_ref[...]
    kap = kk / jnp.maximum(jnp.sqrt(_seg_sum(kk * kk, bd_ref[...])), 1e-12)
    lora_w = jnp.tanh(wd)
    for d in range(2):
        z = w0_ref[d:d + 1] + mm3(lora_w[:, d * DECAY_LORA:(d + 1) * DECAY_LORA], w2_ref[d])
        lw_ref[d] = -math.exp(-0.5) * jax.nn.sigmoid(z)
        a = jax.nn.sigmoid(a0_ref[d:d + 1] + mm3(ad[:, d * AAA_LORA:(d + 1) * AAA_LORA], a2_ref[d]))
        k_ref[d] = k * (1.0 + (a - 1.0) * ka_ref[...])
        al_ref[d] = kap * a
    g_ref[...] = mm3(jax.nn.sigmoid(gd), g2_ref[...])
    r_ref[...] = r
    v_ref[...] = v
    kap_ref[...] = kap


def _segment_flags(bsz, s, n_ctx, tile):
    lat, ctx = np.arange(bsz * s // tile), np.arange(bsz * n_ctx // tile)
    first = np.concatenate([lat % (s // tile) == 0, ctx % (n_ctx // tile) == 0])
    last = np.concatenate([lat % (s // tile) == s // tile - 1, ctx % (n_ctx // tile) == n_ctx // tile - 1])
    return jnp.asarray(first, jnp.int32), jnp.asarray(last, jnp.int32)


def _rwkv7_prep(u, lora_off, bsz, s, n_ctx, shift_w, w0, w2, a0, a2, g2, k_k, k_a):
    t, cols = u.shape
    d_a = w0.shape[-1]
    tile = ROW_TILE
    halo = 8
    first, last = _segment_flags(bsz, s, n_ctx, tile)
    full = lambda arr: pl.BlockSpec(arr.shape, lambda i, f, l: (0,) * arr.ndim)
    row = lambda arr: arr[None, :]
    params = [shift_w, w0, w2, a0, a2, g2, row(k_k), row(k_a), _head_ones(d_a)]
    one = pl.BlockSpec((tile, d_a), lambda i, f, l: (i, 0))
    two = pl.BlockSpec((2, tile, d_a), lambda i, f, l: (0, i, 0))
    flat = jax.ShapeDtypeStruct((t, d_a), F32)
    both = jax.ShapeDtypeStruct((2, t, d_a), F32)
    grid_spec = pltpu.PrefetchScalarGridSpec(
        num_scalar_prefetch=2,
        grid=(t // tile,),
        in_specs=[pl.BlockSpec((tile, cols), lambda i, f, l: (i, 0)),
                  pl.BlockSpec((halo, cols), lambda i, f, l: (jnp.maximum(i * (tile // halo) - 1, 0), 0)),
                  pl.BlockSpec((halo, cols),
                               lambda i, f, l: (jnp.minimum((i + 1) * (tile // halo), t // halo - 1), 0))]
        + [full(p) for p in params],
        out_specs=[one, one, one, one, two, two, two],
    )
    return pl.pallas_call(
        functools.partial(_rwkv7_prep_kernel, d_a=d_a, lora_off=lora_off),
        grid_spec=grid_spec,
        out_shape=[flat, flat, flat, flat, both, both, both],
        compiler_params=pltpu.CompilerParams(dimension_semantics=("arbitrary",),
                                             vmem_limit_bytes=VMEM_LIMIT),
        name="rwkv7_prep",
    )(first, last, u, u, u, *params)


def _even_out_kernel(yaf_ref, yab_ref, r_ref, v_ref, k_ref, g_ref, ybf_ref, ybb_ref, og_ref, rk_ref, lnw_ref,
                     lnb_ref, nw_ref, bd_ref, w_ref, res_ref, gate_ref, o_ref):
    bd = bd_ref[...]
    inv = 1.0 / HEAD_DIM
    y = yaf_ref[...] + yab_ref[...]
    yc = y - _seg_sum(y, bd) * inv
    var = _seg_sum(yc * yc, bd) * inv
    yn = yc * lax.rsqrt(var + LN_X_EPS) * lnw_ref[...] + lnb_ref[...]
    bonus = _seg_sum(r_ref[...] * (k_ref[0] + k_ref[1]) * rk_ref[...], bd) * v_ref[...]
    out_a = (yn + bonus) * g_ref[...]
    o = ybf_ref[...] + ybb_ref[...]
    out_b = o * lax.rsqrt(_seg_sum(o * o, bd) * inv + RMS_EPS) * nw_ref[...] * jax.nn.silu(og_ref[...])
    ycat = jnp.concatenate([out_a, out_b], axis=-1).astype(BF16)
    o_ref[...] = res_ref[...] + gate_ref[0] * jnp.dot(ycat, w_ref[...], preferred_element_type=F32)


def _even_out(ya, r, v, k, g, yb, u, col_og, r_k, ln_w, ln_b, norm_w, w_out, res, gate, rows_per_mod):
    t, width = r.shape
    d = res.shape[1]
    tile = ROW_TILE
    tiles_per_mod = rows_per_mod // tile
    one = pl.BlockSpec((tile, width), lambda i: (i, 0))
    two = pl.BlockSpec((2, tile, width), lambda i: (0, i, 0))
    vec = pl.BlockSpec((1, width), lambda i: (0, 0))
    n_heads = width // HEAD_DIM
    return pl.pallas_call(
        _even_out_kernel,
        grid=(t // tile,),
        in_specs=[one, one, one, one, two, one, one, one, pl.BlockSpec((tile, width), lambda i: (i, col_og)),
                  vec, vec, vec, vec, pl.BlockSpec((width, width), lambda i: (0, 0)),
                  pl.BlockSpec(w_out.shape, lambda i: (0, 0)), pl.BlockSpec((tile, d), lambda i: (i, 0)),
                  pl.BlockSpec((1, 1, d), lambda i: (i // tiles_per_mod, 0, 0))],
        out_specs=pl.BlockSpec((tile, d), lambda i: (i, 0)),
        out_shape=jax.ShapeDtypeStruct((t, d), F32),
        compiler_params=pltpu.CompilerParams(dimension_semantics=("arbitrary",),
                                             vmem_limit_bytes=VMEM_LIMIT),
        name="even_readout_out_proj",
    )(*ya, r, v, k, g, *yb, u, r_k.reshape(1, width), ln_w[None, :], ln_b[None, :],
      jnp.tile(norm_w, n_heads)[None, :], _head_ones(width), w_out.astype(BF16), res, gate[:, None, :])


def _even_layer(tok, sh1, sc1, g1, bsz, s, n_ctx, w_in, w_out, shift_w, w0, w2, a0, a2, g2, k_k, k_a, r_k, ln_w,
                ln_b, lb, norm_w):
    d_a = w0.shape[-1]
    d_b = lb.shape[0]
    rkv = 3 * d_a
    a_cols = rkv + 2 * DECAY_LORA + 2 * AAA_LORA + GATE_LORA
    w_perm = jnp.concatenate([w_in[:, :rkv], w_in[:, a_cols:], w_in[:, rkv:a_cols]], axis=1)
    u = _mod_linear(tok, sh1, sc1, w_perm, s)
    r, v, kap, g, lw, k, al = _rwkv7_prep(u, rkv + 5 * d_b, bsz, s, n_ctx, shift_w, w0, w2, a0, a2, g2, k_k, k_a)
    ya = _rwkv7_recurrence(r, v, kap, lw, k, al, bsz, s, n_ctx)
    col0 = rkv // d_b
    yb = _hgrn2_recurrence(u, lb, col0, col0 + 1, col0 + 3, bsz, s, n_ctx)
    return _even_out(ya, r, v, k, g, yb, u, col0 + 4, r_k, ln_w, ln_b, norm_w, w_out, tok, g1, s)


def _hyena_filters(length, d_d, w1, b1, fr1, w2, b2, fr2, w3):
    t = jnp.linspace(0.0, 1.0, length, dtype=F32)[:, None]
    bands = (HYENA_EMB - 1) // 2
    f = jnp.linspace(1e-4, bands - 1, bands, dtype=F32)
    ang = (2.0 * math.pi / length) * jnp.arange(length, dtype=F32)[:, None] * f
    z = jnp.concatenate([t, jnp.cos(ang), -jnp.sin(ang)], axis=-1)
    hid = jnp.sin(fr1 * (jnp.dot(z, w1, precision=_HI) + b1))
    hid = jnp.sin(fr2 * (jnp.dot(hid, w2, precision=_HI) + b2))
    h = jnp.dot(hid, w3, precision=_HI).reshape(length, HYENA_ORDER, 2, d_d)
    deltas = jnp.abs(jnp.linspace(math.log(HYENA_TARGET) / HYENA_SLOW_DECAY,
                                  math.log(HYENA_TARGET) / HYENA_FAST_DECAY, d_d, dtype=F32))
    return h * jnp.exp(-t * deltas)[:, None, None, :]


DFT_R = 128
DFT_COL_TILE = 4096
DFT_K1_TILE = 4


def _hi_lo(a):
    hi = a.astype(BF16)
    return hi, (a - hi.astype(F32)).astype(BF16)


def _mm3(f_hi, f_lo, x_hi, x_lo):
    d = lambda p, q: jnp.dot(p, q, preferred_element_type=F32)
    return d(f_hi, x_hi) + (d(f_hi, x_lo) + d(f_lo, x_hi))


def _dft_matrices(n1):
    k1 = np.arange(n1, dtype=np.float64)
    k2 = np.arange(DFT_R, dtype=np.float64)
    as_f32 = lambda a: jnp.asarray(a, F32)
    cs = lambda ang: (as_f32(np.cos(ang)), as_f32(-np.sin(ang)))
    f1 = cs(2.0 * np.pi * np.outer(k1, k1) / n1)
    f2 = cs(2.0 * np.pi * np.outer(k2, k2) / DFT_R)
    tw = cs(2.0 * np.pi * np.outer(k1, k2) / (n1 * DFT_R))
    tw = tuple(jnp.broadcast_to(t[:, :, None], (n1, DFT_R, 128)) for t in tw)
    return f1, f2, tw


def _dft_rows_kernel(x_ref, frh_ref, frl_ref, fih_ref, fil_ref, or_ref, oi_ref):
    frh, frl, fih, fil = frh_ref[...], frl_ref[...], fih_ref[...], fil_ref[...]
    x0 = _hi_lo(x_ref[0])
    if x_ref.shape[0] == 1:
        or_ref[0] = _mm3(frh, frl, *x0)
        oi_ref[0] = _mm3(fih, fil, *x0)
    else:
        x1 = _hi_lo(x_ref[1])
        or_ref[0] = _mm3(frh, frl, *x0) - _mm3(fih, fil, *x1)
        oi_ref[0] = _mm3(frh, frl, *x1) + _mm3(fih, fil, *x0)


def _dft_rows(x2d, f1, pack):
    bsz, rows, cols = x2d.shape
    n1 = f1[0].shape[0]
    mats = [m for f in f1 for m in _hi_lo(f[:, :rows])]
    mat_spec = pl.BlockSpec((n1, rows), lambda b, j: (0, 0))
    out_spec = pl.BlockSpec((1, n1, DFT_COL_TILE), lambda b, j: (b, 0, j))
    out = jax.ShapeDtypeStruct((bsz // pack, n1, cols), F32)
    return pl.pallas_call(
        _dft_rows_kernel,
        grid=(bsz // pack, cols // DFT_COL_TILE),
        in_specs=[pl.BlockSpec((pack, rows, DFT_COL_TILE), lambda b, j: (b, 0, j))] + [mat_spec] * 4,
        out_specs=[out_spec, out_spec],
        out_shape=[out, out],
        name="dft_rows",
    )(x2d, *mats)


def _dft_mid_kernel(*refs, conv):
    if conv:
        (ar_ref, ai_ref, tr_ref, ti_ref, kr_ref, ki_ref, frh_ref, frl_ref, fih_ref, fil_ref,
         or_ref, oi_ref) = refs
    else:
        ar_ref, ai_ref, tr_ref, ti_ref, frh_ref, frl_ref, fih_ref, fil_ref, or_ref, oi_ref = refs
    frh, frl, fih, fil = frh_ref[...], frl_ref[...], fih_ref[...], fil_ref[...]
    reps = ar_ref.shape[3] // tr_ref.shape[2]
    rows = range(ar_ref.shape[1])
    wide = lambda t: jnp.concatenate([t] * reps, axis=-1)
    tr = [wide(tr_ref[j]) for j in rows]
    ti = [wide(ti_ref[j]) for j in rows]
    ar = [ar_ref[0, j] for j in rows]
    ai = [ai_ref[0, j] for j in rows]
    pr = [_hi_lo(ar[j] * tr[j] - ai[j] * ti[j]) for j in rows]
    pi = [_hi_lo(ar[j] * ti[j] + ai[j] * tr[j]) for j in rows]
    xr = [_mm3(frh, frl, *pr[j]) - _mm3(fih, fil, *pi[j]) for j in rows]
    xi = [_mm3(frh, frl, *pi[j]) + _mm3(fih, fil, *pr[j]) for j in rows]
    if not conv:
        for j in rows:
            or_ref[0, j] = xr[j]
            oi_ref[0, j] = xi[j]
        return
    yr = [_hi_lo(xr[j] * kr_ref[0, j] - xi[j] * ki_ref[0, j]) for j in rows]
    yi = [_hi_lo(xr[j] * ki_ref[0, j] + xi[j] * kr_ref[0, j]) for j in rows]
    br = [_mm3(frh, frl, *yr[j]) + _mm3(fih, fil, *yi[j]) for j in rows]
    bi = [_mm3(frh, frl, *yi[j]) - _mm3(fih, fil, *yr[j]) for j in rows]
    for j in rows:
        or_ref[0, j] = br[j] * tr[j] + bi[j] * ti[j]
        oi_ref[0, j] = bi[j] * tr[j] - br[j] * ti[j]


def _dft_mid(a_r, a_i, f2, tw, kf=None):
    (f_r, f_i), (t_r, t_i) = f2, tw
    bsz, n1, _, ch = a_r.shape
    slab = pl.BlockSpec((1, DFT_K1_TILE, DFT_R, ch), lambda b, g: (b, g, 0, 0))
    tw = pl.BlockSpec((DFT_K1_TILE, DFT_R, t_r.shape[2]), lambda b, g: (g, 0, 0))
    mat = pl.BlockSpec((DFT_R, DFT_R), lambda b, g: (0, 0))
    args = [a_r, a_i, t_r, t_i]
    specs = [slab, slab, tw, tw]
    if kf is not None:
        kf_r, kf_i, o = kf
        kspec = pl.BlockSpec((1, DFT_K1_TILE, DFT_R, ch), lambda b, g: (o, g, 0, 0))
        args += [kf_r, kf_i]
        specs += [kspec, kspec]
    args += [m for f in (f_r, f_i) for m in _hi_lo(f)]
    specs += [mat] * 4
    out = jax.ShapeDtypeStruct(a_r.shape, F32)
    return pl.pallas_call(
        functools.partial(_dft_mid_kernel, conv=kf is not None),
        grid=(bsz, n1 // DFT_K1_TILE),
        in_specs=specs,
        out_specs=[slab, slab],
        out_shape=[out, out],
        compiler_params=pltpu.CompilerParams(dimension_semantics=("arbitrary", "arbitrary"),
                                             vmem_limit_bytes=VMEM_LIMIT),
        name="dft_mid_conv" if kf is not None else "dft_mid",
    )(*args)


def _idft_rows_kernel(br_ref, bi_ref, z_ref, g_ref, bias_ref, grh_ref, grl_ref, gih_ref, gil_ref, o_ref):
    grh, grl, gih, gil = grh_ref[...], grl_ref[...], gih_ref[...], gil_ref[...]
    b_r, b_i = _hi_lo(br_ref[0]), _hi_lo(bi_ref[0])
    y = (_mm3(grh, grl, *b_r) + _mm3(gih, gil, *b_i), _mm3(grh, grl, *b_i) - _mm3(gih, gil, *b_r))
    for e in range(2):
        o_ref[e] = g_ref[e] * (y[e] + z_ref[e] * bias_ref[...])


def _idft_rows(b_r, b_i, z2d, gate2d, bias_cols, f1):
    bsz, rows, cols = z2d.shape
    n1 = f1[0].shape[0]
    scale = 1.0 / (n1 * DFT_R)
    mats = [m for f in f1 for m in _hi_lo(f[:rows] * scale)]
    mat_spec = pl.BlockSpec((rows, n1), lambda b, j: (0, 0))
    in_spec = pl.BlockSpec((1, n1, DFT_COL_TILE), lambda b, j: (b, 0, j))
    io_spec = pl.BlockSpec((2, rows, DFT_COL_TILE), lambda b, j: (b, 0, j))
    return pl.pallas_call(
        _idft_rows_kernel,
        grid=(bsz // 2, cols // DFT_COL_TILE),
        in_specs=[in_spec, in_spec, io_spec, io_spec, pl.BlockSpec((1, DFT_COL_TILE), lambda b, j: (0, j))]
        + [mat_spec] * 4,
        out_specs=io_spec,
        out_shape=jax.ShapeDtypeStruct((bsz, rows, cols), F32),
        name="idft_rows",
    )(b_r, b_i, z2d, gate2d, bias_cols, *mats)


def _hyena(u, short_w, w1, b1, fr1, w2, b2, fr2, w3, bias):
    bsz, length, _ = u.shape
    u = _conv3(u, short_w)
    v, x1, x2 = jnp.split(u, 3, axis=-1)
    ch = v.shape[-1]
    rows = length // DFT_R
    h = _hyena_filters(length, ch, w1, b1, fr1, w2, b2, fr2, w3)
    n1 = 2 * rows
    f1, f2, tw = _dft_matrices(n1)
    four_d = lambda t: t.reshape(t.shape[0], n1, DFT_R, ch)
    hh = h.transpose(1, 2, 0, 3).reshape(HYENA_ORDER * 2, rows, DFT_R * ch)
    hf_r, hf_i = (t.reshape(HYENA_ORDER, 2, n1, DFT_R, ch)
                  for t in _dft_mid(*(four_d(t) for t in _dft_rows(hh, f1, 1)), f2, tw))
    kf_r = hf_r[:, 0] + hf_r[:, 1]
    kf_i = hf_i[:, 0] - hf_i[:, 1]
    assert bsz % 2 == 0, "batch entries are convolved in pairs (one complex sequence per pair)"
    z = v
    for o, gate in enumerate((x1, x2)):
        z2d = z.reshape(bsz, rows, DFT_R * ch)
        a_r, a_i = _dft_rows(z2d, f1, 2)
        b_r, b_i = _dft_mid(four_d(a_r), four_d(a_i), f2, tw, kf=(kf_r, kf_i, o))
        flat = lambda t: t.reshape(bsz // 2, n1, DFT_R * ch)
        y = _idft_rows(flat(b_r), flat(b_i), z2d, gate.reshape(z2d.shape), jnp.tile(bias[o], DFT_R)[None, :], f1)
        z = y.reshape(bsz, length, ch)
    return z


def _odd_layer(tok, sh1, sc1, g1, bsz, s, n_ctx, w_in, w_out, q_norm, k_norm, rpb, short_w, w1, b1, fr1, w2,
               b2, fr2, w3, bias):
    n_heads = rpb.shape[0]
    d_c = n_heads * HEAD_DIM
    n_lat = bsz * s
    u = _mod_linear(tok, sh1, sc1, w_in, s)
    q, k, v = _na_prep(u, bsz, s, n_heads, q_norm, k_norm)
    u_ctx = u[n_lat:, d_c:3 * d_c].reshape(bsz, n_ctx, 2, n_heads, HEAD_DIM)
    head_major = lambda t: t.transpose(0, 2, 1, 3).astype(BF16)
    kc = head_major(_rms(u_ctx[:, :, 0]) * k_norm)
    vc = head_major(u_ctx[:, :, 1])
    y_na = _neighbourhood_attention(q, k, v, kc, vc, rpb)
    y_hy = _hyena(u[:n_lat, 3 * d_c:].reshape(bsz, s, -1), short_w, w1, b1, fr1, w2, b2, fr2, w3, bias)
    return _res_linear2(y_na, y_hy.reshape(n_lat, -1), w_out, tok, g1, s)


def kernel(x, c, ctx, c_ctx, mod_w, mod_b, router_w, router_b, moe_w1, moe_b1, moe_w2, moe_b2, ab_w_in, ab_w_out, rwkv_shift, rwkv_w0, rwkv_w2, rwkv_a0, rwkv_a2, rwkv_g2, rwkv_k_k, rwkv_k_a, rwkv_r_k, rwkv_ln_w, rwkv_ln_b, hgrn_lb_logits, hgrn_norm_w, cd_w_in, cd_w_out, na_q_norm, na_k_norm, na_rpb, hy_short, hy_w1, hy_b1, hy_freq1, hy_w2, hy_b2, hy_freq2, hy_w3, hy_bias):
    bsz, s, d = x.shape
    n_ctx = ctx.shape[1]
    depth = mod_w.shape[0]
    n_lat = bsz * s
    lb_all = jnp.cumsum(jax.nn.softmax(hgrn_lb_logits, axis=0), axis=0)
    tok = jnp.concatenate([x.reshape(n_lat, d), ctx.reshape(bsz * n_ctx, d)], axis=0)
    cond = jnp.concatenate([jax.nn.silu(c), jax.nn.silu(c_ctx)[None, :],
                            jnp.zeros((8 - bsz - 1, d), F32)], axis=0)
    assert depth == 2, "layer 0 = RWKV-7 || HGRN2 with context outputs, layer 1 = attention || Hyena, latent only"
    for l in range(depth):
        j = l // 2
        mod = _small_linear(cond, mod_w[l], mod_b[l])[:bsz + 1]
        sh1, sc1, g1, sh2, sc2, g2 = jnp.split(mod, 6, axis=-1)
        if l % 2 == 0:
            tok = _even_layer(tok, sh1, sc1, g1, bsz, s, n_ctx, ab_w_in[j], ab_w_out[j], rwkv_shift[j],
                              rwkv_w0[j], rwkv_w2[j], rwkv_a0[j], rwkv_a2[j], rwkv_g2[j], rwkv_k_k[j],
                              rwkv_k_a[j], rwkv_r_k[j], rwkv_ln_w[j], rwkv_ln_b[j], lb_all[j], hgrn_norm_w[j])
        else:
            tok = _odd_layer(tok, sh1, sc1, g1, bsz, s, n_ctx, cd_w_in[j], cd_w_out[j], na_q_norm[j], na_k_norm[j],
                             na_rpb[j], hy_short[j], hy_w1[j], hy_b1[j], hy_freq1[j], hy_w2[j], hy_b2[j],
                             hy_freq2[j], hy_w3[j], hy_bias[j])
        yk, gate = _moe(tok, sh2, sc2, s, router_w[l], router_b[l], moe_w1, moe_w2, l, moe_b1[l], moe_b2[l])
        tok = _combine(tok, yk, gate, g2, s)
    return tok[:n_lat].reshape(bsz, s, d)


def _combine_kernel(x_ref, y_ref, w_ref, g_ref, o_ref):
    reps = x_ref.shape[1] // w_ref.shape[2]
    acc = y_ref[0] * jnp.concatenate([w_ref[0]] * reps, axis=-1)
    for k in range(1, y_ref.shape[0]):
        acc = acc + y_ref[k] * jnp.concatenate([w_ref[k]] * reps, axis=-1)
    o_ref[...] = x_ref[...] + g_ref[0] * acc


def _combine(x, yk, gate, res_gate, rows_per_mod):
    m, d = x.shape
    n_k = yk.shape[0]
    rows = COMBINE_TILE
    tiles_per_mod = rows_per_mod // rows
    tile = pl.BlockSpec((rows, d), lambda i: (i, 0))
    lanes = 128
    return pl.pallas_call(
        _combine_kernel,
        grid=(m // rows,),
        in_specs=[tile, pl.BlockSpec((n_k, rows, d), lambda i: (0, i, 0)),
                  pl.BlockSpec((n_k, rows, lanes), lambda i: (0, i, 0)),
                  pl.BlockSpec((1, 1, d), lambda i: (i // tiles_per_mod, 0, 0))],
        out_specs=tile,
        out_shape=jax.ShapeDtypeStruct((m, d), F32),
        compiler_params=pltpu.CompilerParams(dimension_semantics=("arbitrary",),
                                             vmem_limit_bytes=VMEM_LIMIT),
        name="moe_combine",
    )(x, yk, jnp.broadcast_to(gate[:, :, None], gate.shape + (lanes,)), res_gate[:, None, :])
```

```python
import functools
import math

import numpy as np
import jax
import jax.numpy as jnp
from jax import lax
from jax.experimental import pallas as pl
from jax.experimental.pallas import tpu as pltpu

F32 = jnp.float32
BF16 = jnp.bfloat16

HEAD_DIM = 64
GRID_W = 64
DECAY_LORA = 64
AAA_LORA = 64
GATE_LORA = 128
LN_X_EPS = 1e-5 * HEAD_DIM
NA_ROWS = 8
NA_COLS = 16
ROPE_THETA = 10000.0
HYENA_ORDER = 2
HYENA_EMB = 33
HYENA_FAST_DECAY = 0.3
HYENA_SLOW_DECAY = 1.5
HYENA_TARGET = 1e-2
N_EXPERTS = 32
TOP_K = 4
SWIGLU_ALPHA = 1.702
SWIGLU_LIMIT = 7.0
MOE_BLOCK = 256
MOE_CALLS = 4
RMS_EPS = 1e-6

ROW_TILE = 256
COMBINE_TILE = 512
REC_CHUNK = 64
REC_PASSES = 1
REC_CHAIN_PASSES = 1
VMEM_LIMIT = 56 * 1024 * 1024

_HI = lax.Precision.HIGHEST


_DIMS = {"nn": (((1,), (0,)), ((), ())), "nt": (((1,), (1,)), ((), ())), "tn": (((0,), (0,)), ((), ()))}


def _mm(a, b, form, passes):
    dims = _DIMS[form]
    if passes == 6:
        return lax.dot_general(a, b, dims, precision=_HI, preferred_element_type=F32)
    a_hi = a.astype(BF16)
    b_hi = b.astype(BF16)
    out = lax.dot_general(a_hi, b_hi, dims, preferred_element_type=F32)
    if passes == 3:
        a_lo = (a - a_hi.astype(F32)).astype(BF16)
        b_lo = (b - b_hi.astype(F32)).astype(BF16)
        out = out + (lax.dot_general(a_hi, b_lo, dims, preferred_element_type=F32)
                     + lax.dot_general(a_lo, b_hi, dims, preferred_element_type=F32))
    return out


def _dot(a, b):
    return _mm(a, b, "nn", 6)


def _mod_linear_kernel(x_ref, sh_ref, sc_ref, w_ref, o_ref):
    x = x_ref[...]
    ms = jnp.mean(x * x, axis=-1, keepdims=True)
    h = x * lax.rsqrt(ms + RMS_EPS) * (1.0 + sc_ref[0]) + sh_ref[0]
    o_ref[...] = jnp.dot(h.astype(BF16), w_ref[...], preferred_element_type=F32)


def _mod_linear(x, shift, scale, w, rows_per_mod):
    m, d = x.shape
    n = w.shape[1]
    tiles_per_mod = rows_per_mod // ROW_TILE
    mod_spec = pl.BlockSpec((1, 1, d), lambda i: (i // tiles_per_mod, 0, 0))
    return pl.pallas_call(
        _mod_linear_kernel,
        grid=(m // ROW_TILE,),
        in_specs=[pl.BlockSpec((ROW_TILE, d), lambda i: (i, 0)), mod_spec, mod_spec,
                  pl.BlockSpec((d, n), lambda i: (0, 0))],
        out_specs=pl.BlockSpec((ROW_TILE, n), lambda i: (i, 0)),
        out_shape=jax.ShapeDtypeStruct((m, n), F32),
        compiler_params=pltpu.CompilerParams(dimension_semantics=("arbitrary",),
                                             vmem_limit_bytes=VMEM_LIMIT),
        name="mod_linear",
    )(x, shift[:, None, :], scale[:, None, :], w.astype(BF16))


def _res_linear2_kernel(ya_ref, yb_ref, w_ref, res_ref, g_ref, o_ref):
    y = jnp.concatenate([ya_ref[...], yb_ref[...]], axis=-1).astype(BF16)
    o_ref[...] = res_ref[...] + g_ref[0] * jnp.dot(y, w_ref[...], preferred_element_type=F32)


def _res_linear2(ya, yb, w, res, gate, rows_per_mod):
    m, ka = ya.shape
    kb = yb.shape[1]
    n = w.shape[1]
    tiles_per_mod = rows_per_mod // ROW_TILE
    return pl.pallas_call(
        _res_linear2_kernel,
        grid=(m // ROW_TILE,),
        in_specs=[pl.BlockSpec((ROW_TILE, ka), lambda i: (i, 0)), pl.BlockSpec((ROW_TILE, kb), lambda i: (i, 0)),
                  pl.BlockSpec((ka + kb, n), lambda i: (0, 0)),
                  pl.BlockSpec((ROW_TILE, n), lambda i: (i, 0)),
                  pl.BlockSpec((1, 1, n), lambda i: (i // tiles_per_mod, 0, 0))],
        out_specs=pl.BlockSpec((ROW_TILE, n), lambda i: (i, 0)),
        out_shape=jax.ShapeDtypeStruct((m, n), F32),
        compiler_params=pltpu.CompilerParams(dimension_semantics=("arbitrary",),
                                             vmem_limit_bytes=VMEM_LIMIT),
        name="res_linear",
    )(ya, yb, w.astype(BF16), res, gate[:, None, :])


def _small_linear_kernel(x_ref, w_ref, b_ref, o_ref):
    o_ref[...] = _dot(x_ref[...], w_ref[...]) + b_ref[...]


def _small_linear(x, w, b):
    m, k = x.shape
    n = w.shape[1]
    tn = 1024 if n % 1024 == 0 else n
    return pl.pallas_call(
        _small_linear_kernel,
        grid=(n // tn,),
        in_specs=[pl.BlockSpec((m, k), lambda j: (0, 0)), pl.BlockSpec((k, tn), lambda j: (0, j)),
                  pl.BlockSpec((1, tn), lambda j: (0, j))],
        out_specs=pl.BlockSpec((m, tn), lambda j: (0, j)),
        out_shape=jax.ShapeDtypeStruct((m, n), F32),
        name="small_linear",
    )(x, w, b[None, :])


def _rec_kernel(*refs, delta, n_heads):
    n_in = 6 if delta else 3
    ins = (refs[:n_in], refs[n_in:2 * n_in])
    rest = refs[2 * n_in:]
    if delta:
        y_refs, s_ref = rest[:2], rest[2]
    else:
        lb_ref, y_refs, s_ref = rest[0], rest[1:3], rest[3]

    @pl.when(pl.program_id(1) == 0)
    def _():
        s_ref[...] = jnp.zeros_like(s_ref)

    c = ins[0][0].shape[0]
    row = lax.broadcasted_iota(jnp.int32, (c, c), 0)
    col = lax.broadcasted_iota(jnp.int32, (c, c), 1)
    mid = c // 2
    heads = range(n_heads)
    hs = lambda t: [t[:, h * HEAD_DIM:(h + 1) * HEAD_DIM] for h in heads]
    mm = functools.partial(_mm, passes=REC_PASSES)

    streams = []
    for d in range(2):
        ahead = row - col if d == 0 else col - row
        incl, strict = ahead >= 0, ahead > 0
        if delta:
            r_ref, v_ref, kap_ref, lw_ref, k_ref, al_ref = ins[d]
            r_all, v_all, lw, k = r_ref[...], v_ref[...], lw_ref[0], k_ref[0]
        else:
            q_ref, i_ref, f_ref = ins[d]
            fg = lb_ref[...] + (1.0 - lb_ref[...]) * jax.nn.sigmoid(f_ref[...])
            r_all, v_all, lw, k = jax.nn.silu(q_ref[...]), i_ref[...], jnp.log(fg), 1.0 - fg
        b = _mm(incl.astype(F32), lw, "nn", 6)
        bm = b[mid:mid + 1, :]
        tot = b[c - 1:c, :] if d == 0 else b[0:1, :]
        e_neg = jnp.exp(bm - b)
        e_end = jnp.exp(tot - b)
        st = dict(incl=incl, strict=strict, v=hs(v_all), rq=hs(r_all * jnp.exp(b - bm)), kd=hs(k * e_neg),
                  k_end=hs(k * e_end), g_mid=hs(jnp.exp(bm)), g_tot=hs(jnp.exp(tot)))
        if delta:
            al = al_ref[0]
            st.update(kq=hs(kap_ref[...] * jnp.exp(b - lw - bm)), ad=hs(al * e_neg), al_end=hs(al * e_end))
        streams.append(st)

    units = [(d, h) for d in range(2) for h in heads]
    n_u = range(len(units))
    per_head = lambda name: [streams[d][name][h] for d, h in units]
    per_dir = lambda name: [streams[d][name] for d, _ in units]
    incl, strict = per_dir("incl"), per_dir("strict")
    v, rq, kd, k_end, g_mid, g_tot = (per_head(n) for n in ("v", "rq", "kd", "k_end", "g_mid", "g_tot"))
    s0 = [s_ref[d, h] for d, h in units]
    s0m = [s0[n] * g_mid[n] for n in n_u]
    if delta:
        kq, ad, al_end = per_head("kq"), per_head("ad"), per_head("al_end")
        q2 = [jnp.concatenate([kq[n], rq[n]], axis=0) for n in n_u]
        k2 = [jnp.concatenate([kd[n], ad[n]], axis=0) for n in n_u]
        a = [mm(q2[n], k2[n], "nt") for n in n_u]
        p = [mm(q2[n], s0m[n], "nt") for n in n_u]
        z = [-(p[n][:c] + mm(jnp.where(strict[n], a[n][:c, :c], 0.0), v[n], "nn")) for n in n_u]
        mc = functools.partial(_mm, passes=REC_CHAIN_PASSES)
        m = [jnp.where(strict[n], a[n][:c, c:], 0.0) for n in n_u]
        pair = (row >> 1) == (col >> 1)
        eye = (row == col).astype(F32)
        t = [eye - jnp.where(pair, m[n], 0.0) for n in n_u]
        for lvl in range(1, int(math.log2(c))):
            off = ((row >> (lvl + 1)) == (col >> (lvl + 1))) & ((row >> lvl) != (col >> lvl))
            tc = [mc(t[n], jnp.where(off, m[n], 0.0), "nn") for n in n_u]
            t = [t[n] - mc(tc[n], t[n], "nn") for n in n_u]
        z = [mc(t[n], z[n], "nn") for n in n_u]
        y = [p[n][c:] + mm(jnp.where(incl[n], a[n][c:, :c], 0.0), v[n], "nn")
             + mm(jnp.where(incl[n], a[n][c:, c:], 0.0), z[n], "nn") for n in n_u]
        s_new = [s0[n] * g_tot[n] + mm(v[n], k_end[n], "tn") + mm(z[n], al_end[n], "tn") for n in n_u]
    else:
        a = [mm(rq[n], kd[n], "nt") for n in n_u]
        y = [mm(rq[n], s0m[n], "nt") + mm(jnp.where(incl[n], a[n], 0.0), v[n], "nn") for n in n_u]
        s_new = [s0[n] * g_tot[n] + mm(v[n], k_end[n], "tn") for n in n_u]
    for n, (d, h) in enumerate(units):
        s_ref[d, h] = s_new[n]
    for d in range(2):
        y_refs[d][...] = jnp.concatenate(y[d * n_heads:(d + 1) * n_heads], axis=-1)


def _seq_block_index(bsz, s, n_ctx, rows):
    n_c, n_l = n_ctx // rows, s // rows

    def index(d, b, i):
        back = jnp.where(i < n_c, n_c - 1 - i, n_l + 2 * n_c - 1 - i)
        pos = jnp.where(d == 0, i, back)
        return jnp.where(pos < n_c, bsz * n_l + b * n_c + pos, b * n_l + pos - n_c)

    return index


def _rec_call(args, specs, bsz, s, n_ctx, width, delta):
    c = REC_CHUNK
    n_heads = width // HEAD_DIM
    blk = _seq_block_index(bsz, s, n_ctx, c)
    out = jax.ShapeDtypeStruct((bsz * (s + n_ctx), width), F32)
    return pl.pallas_call(
        functools.partial(_rec_kernel, delta=delta, n_heads=n_heads),
        grid=(bsz, (s + n_ctx) // c),
        in_specs=specs,
        out_specs=[pl.BlockSpec((c, width), lambda b, i, d=d: (blk(d, b, i), 0)) for d in range(2)],
        out_shape=[out, out],
        scratch_shapes=[pltpu.VMEM((2, n_heads, HEAD_DIM, HEAD_DIM), F32)],
        compiler_params=pltpu.CompilerParams(dimension_semantics=("arbitrary", "arbitrary")),
        name="rwkv7_rec" if delta else "hgrn2_rec",
    )(*args)


def _rwkv7_recurrence(r, v, kap, lw, k, al, bsz, s, n_ctx):
    width = r.shape[1]
    blk = _seq_block_index(bsz, s, n_ctx, REC_CHUNK)
    specs = []
    for d in range(2):
        shared = pl.BlockSpec((REC_CHUNK, width), lambda b, i, d=d: (blk(d, b, i), 0))
        per_dir = pl.BlockSpec((1, REC_CHUNK, width), lambda b, i, d=d: (d, blk(d, b, i), 0))
        specs += [shared] * 3 + [per_dir] * 3
    return _rec_call([r, v, kap, lw, k, al] * 2, specs, bsz, s, n_ctx, width, True)


def _hgrn2_recurrence(u, lb, col_q, col_f, col_i, bsz, s, n_ctx):
    width = lb.shape[0]
    blk = _seq_block_index(bsz, s, n_ctx, REC_CHUNK)
    specs = []
    for d in range(2):
        specs += [pl.BlockSpec((REC_CHUNK, width), lambda b, i, d=d, j=j: (blk(d, b, i), j))
                  for j in (col_q, col_i, col_f + d)]
    specs.append(pl.BlockSpec((1, width), lambda b, i: (0, 0)))
    return _rec_call([u] * 6 + [lb[None, :]], specs, bsz, s, n_ctx, width, False)


def _na_kernel(q_ref, k_ref, v_ref, kc_ref, vc_ref, bias_ref, o_ref, *, rows_per_step, n_rows):
    g = pl.program_id(2)
    band = NA_ROWS * GRID_W
    n_h = q_ref.shape[1]
    units = [(h, j) for j in range(rows_per_step) for h in range(n_h)]
    nt = lambda p, q_: lax.dot_general(p, q_, (((1,), (1,)), ((), ())), preferred_element_type=F32)
    qr = [g * rows_per_step + j for j in range(rows_per_step)]
    start = [jnp.clip(r - NA_ROWS // 2, 0, n_rows - NA_ROWS) for r in qr]
    off = [pl.multiple_of(st * GRID_W, GRID_W) for st in start]
    q = [q_ref[0, h, j * GRID_W:(j + 1) * GRID_W, :] for h, j in units]
    s_win = [nt(q[n], k_ref[0, h, pl.ds(off[j], band), :]) + bias_ref[h, start[j] - qr[j] + NA_ROWS - 1]
             for n, (h, j) in enumerate(units)]
    s_ctx = [nt(q[n], kc_ref[0, h]) for n, (h, j) in enumerate(units)]
    ns = range(len(units))
    m = [jnp.maximum(jnp.max(s_win[n], axis=-1, keepdims=True), jnp.max(s_ctx[n], axis=-1, keepdims=True))
         for n in ns]
    p_win = [jnp.exp(s_win[n] - m[n]) for n in ns]
    p_ctx = [jnp.exp(s_ctx[n] - m[n]) for n in ns]
    den = [jnp.sum(p_win[n], axis=-1, keepdims=True) + jnp.sum(p_ctx[n], axis=-1, keepdims=True) for n in ns]
    o = [(jnp.dot(p_win[n].astype(BF16), v_ref[0, h, pl.ds(off[j], band), :], preferred_element_type=F32)
          + jnp.dot(p_ctx[n].astype(BF16), vc_ref[0, h], preferred_element_type=F32)) / den[n]
         for n, (h, j) in enumerate(units)]
    for j in range(rows_per_step):
        o_ref[j * GRID_W:(j + 1) * GRID_W, :] = jnp.concatenate(o[j * n_h:(j + 1) * n_h], axis=-1)


def _na_bias_table(rpb):
    cc = np.arange(GRID_W)
    col_start = np.clip(cc - NA_COLS // 2, 0, GRID_W - NA_COLS)
    kc = np.arange(GRID_W)
    inside = (kc[None, :] >= col_start[:, None]) & (kc[None, :] < col_start[:, None] + NA_COLS)
    col_off = np.clip(kc[None, :] - cc[:, None] + NA_COLS - 1, 0, 2 * NA_COLS - 2)
    tab = rpb[:, :, col_off]
    tab = jnp.where(jnp.asarray(inside)[None, None], tab, -jnp.inf)
    d0 = np.arange(NA_ROWS)[:, None] + np.arange(NA_ROWS)[None, :]
    band = tab[:, d0]
    return band.transpose(0, 1, 3, 2, 4).reshape(rpb.shape[0], NA_ROWS, GRID_W, NA_ROWS * GRID_W)


NA_HEADS_PER_STEP = 128 // HEAD_DIM


def _neighbourhood_attention(q, k, v, kc, vc, rpb):
    bsz, n_heads, s, dh = q.shape
    n_rows = s // GRID_W
    rows_per_step = 8
    hp = NA_HEADS_PER_STEP
    lc = kc.shape[2]
    bias = _na_bias_table(rpb)
    steps = n_rows // rows_per_step
    full = pl.BlockSpec((1, hp, s, dh), lambda b, h, g: (b, h, 0, 0))
    ctx = pl.BlockSpec((1, hp, lc, dh), lambda b, h, g: (b, h, 0, 0))
    return pl.pallas_call(
        functools.partial(_na_kernel, rows_per_step=rows_per_step, n_rows=n_rows),
        grid=(bsz, n_heads // hp, steps),
        in_specs=[pl.BlockSpec((1, hp, rows_per_step * GRID_W, dh), lambda b, h, g: (b, h, g, 0)),
                  full, full, ctx, ctx,
                  pl.BlockSpec((hp, NA_ROWS, GRID_W, NA_ROWS * GRID_W), lambda b, h, g: (h, 0, 0, 0))],
        out_specs=pl.BlockSpec((rows_per_step * GRID_W, hp * dh), lambda b, h, g: (b * steps + g, h)),
        out_shape=jax.ShapeDtypeStruct((bsz * s, n_heads * dh), F32),
        compiler_params=pltpu.CompilerParams(
            dimension_semantics=("arbitrary", "arbitrary", "arbitrary"), vmem_limit_bytes=VMEM_LIMIT),
        name="neighbourhood_attention",
    )(q, k, v, kc, vc, bias)


def _na_prep_kernel(u_ref, cos_ref, sin_ref, qn_ref, kn_ref, bd_ref, q_ref, k_ref, v_ref, *, n_heads):
    width = n_heads * HEAD_DIM
    bd = bd_ref[...]
    lane = lax.broadcasted_iota(jnp.int32, (u_ref.shape[0], 128), 1)
    first = (lane & (HEAD_DIM // 4)) == 0

    def rope(x):
        parts = []
        for j in range(width // 128):
            cols = slice(128 * j, 128 * (j + 1))
            xj = x[:, cols]
            partner = jnp.where(first, pltpu.roll(xj, 128 - HEAD_DIM // 4, axis=1),
                                pltpu.roll(xj, HEAD_DIM // 4, axis=1))
            parts.append(xj * cos_ref[:, cols] + partner * sin_ref[:, cols])
        return jnp.concatenate(parts, axis=-1)

    def normed(x, w_ref):
        return x * lax.rsqrt(_seg_sum(x * x, bd) * (1.0 / HEAD_DIM) + RMS_EPS) * w_ref[...]

    q = rope(normed(u_ref[:, :width], qn_ref)) * HEAD_DIM ** -0.5
    k = rope(normed(u_ref[:, width:2 * width], kn_ref))
    v = u_ref[:, 2 * width:3 * width]
    for h in range(n_heads):
        cols = slice(h * HEAD_DIM, (h + 1) * HEAD_DIM)
        q_ref[0, h] = q[:, cols].astype(BF16)
        k_ref[0, h] = k[:, cols].astype(BF16)
        v_ref[0, h] = v[:, cols].astype(BF16)


def _rope_tables(s, n_heads):
    t = jnp.arange(s)
    nf = HEAD_DIM // 4
    inv = ROPE_THETA ** (-jnp.arange(nf, dtype=F32) / nf)
    ang_r = (t // GRID_W).astype(F32)[:, None] * inv
    ang_c = (t % GRID_W).astype(F32)[:, None] * inv
    cos = jnp.concatenate([jnp.cos(ang_r)] * 2 + [jnp.cos(ang_c)] * 2, axis=-1)
    sin = jnp.concatenate([-jnp.sin(ang_r), jnp.sin(ang_r), -jnp.sin(ang_c), jnp.sin(ang_c)], axis=-1)
    return jnp.tile(cos, (1, n_heads)), jnp.tile(sin, (1, n_heads))


def _na_prep(u, bsz, s, n_heads, q_norm, k_norm):
    width = n_heads * HEAD_DIM
    tile = ROW_TILE
    per_batch = s // tile
    cos, sin = _rope_tables(s, n_heads)
    tab = pl.BlockSpec((tile, width), lambda i: (i % per_batch, 0))
    vec = pl.BlockSpec((1, width), lambda i: (0, 0))
    out_spec = pl.BlockSpec((1, n_heads, tile, HEAD_DIM), lambda i: (i // per_batch, 0, i % per_batch, 0))
    out = jax.ShapeDtypeStruct((bsz, n_heads, s, HEAD_DIM), BF16)
    return pl.pallas_call(
        functools.partial(_na_prep_kernel, n_heads=n_heads),
        grid=(bsz * per_batch,),
        in_specs=[pl.BlockSpec((tile, 3 * width), lambda i: (i, 0)), tab, tab, vec, vec,
                  pl.BlockSpec((width, width), lambda i: (0, 0))],
        out_specs=[out_spec, out_spec, out_spec],
        out_shape=[out, out, out],
        name="na_prep",
    )(u, cos, sin, jnp.tile(q_norm, n_heads)[None, :], jnp.tile(k_norm, n_heads)[None, :], _head_ones(width))


DEINT_GROUP = 256


def _moe_kernel(be_ref, nb_ref, x_ref, w1_ref, b1_ref, w2_ref, b2_ref, perm_ref, *rest, first_block):
    o_ref, w1_s, w2_s = rest[-3:]
    blk = first_block + pl.program_id(0)
    active = blk < nb_ref[0]
    fresh = jnp.logical_or(pl.program_id(0) == 0, be_ref[blk] != be_ref[jnp.maximum(blk - 1, 0)])

    @pl.when(jnp.logical_and(active, fresh))
    def _():
        half = w1_s.shape[1] // 2
        g_out = DEINT_GROUP // 2
        for g in range(w1_s.shape[1] // DEINT_GROUP):
            t = jnp.dot(w1_ref[0, 0, :, g * DEINT_GROUP:(g + 1) * DEINT_GROUP].astype(BF16), perm_ref[...],
                        preferred_element_type=F32).astype(BF16)
            w1_s[:, g * g_out:(g + 1) * g_out] = t[:, :g_out]
            w1_s[:, half + g * g_out:half + (g + 1) * g_out] = t[:, g_out:]
        w2_s[...] = w2_ref[0, 0].astype(BF16)

    @pl.when(active)
    def _():
        y = jnp.dot(x_ref[...], w1_s[...], preferred_element_type=F32) + b1_ref[0]
        half = y.shape[1] // 2
        glu = jnp.minimum(y[:, :half], SWIGLU_LIMIT)
        lin = jnp.clip(y[:, half:], -SWIGLU_LIMIT, SWIGLU_LIMIT)
        act = glu * jax.nn.sigmoid(SWIGLU_ALPHA * glu) * (lin + 1.0)
        o_ref[...] = jnp.dot(act.astype(BF16), w2_s[...], preferred_element_type=F32) + b2_ref[0]

    @pl.when(jnp.logical_not(active))
    def _():
        o_ref[...] = jnp.zeros_like(o_ref)


def _moe_experts(x_part, first_block, n_blocks_total, y_prev, blk_e, n_used, w1_layers, w2_layers, layer, b1, b2):
    d = x_part.shape[1]
    n_blocks = x_part.shape[0] // MOE_BLOCK
    n_slots = n_blocks_total * MOE_BLOCK
    off = first_block
    de2 = w1_layers.shape[3]
    perm = np.zeros((DEINT_GROUP, DEINT_GROUP), np.float32)
    idx = np.arange(DEINT_GROUP // 2)
    perm[2 * idx, idx] = 1.0
    perm[2 * idx + 1, DEINT_GROUP // 2 + idx] = 1.0
    grid_spec = pltpu.PrefetchScalarGridSpec(
        num_scalar_prefetch=2,
        grid=(n_blocks,),
        in_specs=[pl.BlockSpec((MOE_BLOCK, d), lambda i, be, nb: (i, 0)),
                  pl.BlockSpec((1, 1, d, de2), lambda i, be, nb: (layer, be[off + i], 0, 0)),
                  pl.BlockSpec((1, 1, de2), lambda i, be, nb: (be[off + i], 0, 0)),
                  pl.BlockSpec((1, 1, de2 // 2, d), lambda i, be, nb: (layer, be[off + i], 0, 0)),
                  pl.BlockSpec((1, 1, d), lambda i, be, nb: (be[off + i], 0, 0)),
                  pl.BlockSpec((DEINT_GROUP, DEINT_GROUP), lambda i, be, nb: (0, 0))]
        + ([] if y_prev is None else [pl.BlockSpec(memory_space=pl.ANY)]),
        out_specs=pl.BlockSpec((MOE_BLOCK, d), lambda i, be, nb: (off + i, 0)),
        scratch_shapes=[pltpu.VMEM((d, de2), BF16), pltpu.VMEM((de2 // 2, d), BF16)],
    )
    args = [blk_e, n_used, x_part, w1_layers, b1[:, None, :], w2_layers, b2[:, None, :], jnp.asarray(perm, BF16)]
    return pl.pallas_call(
        functools.partial(_moe_kernel, first_block=first_block),
        grid_spec=grid_spec,
        out_shape=jax.ShapeDtypeStruct((n_slots, d), F32),
        input_output_aliases={} if y_prev is None else {len(args): 0},
        compiler_params=pltpu.CompilerParams(dimension_semantics=("arbitrary",),
                                             vmem_limit_bytes=VMEM_LIMIT),
        name="moe_experts",
    )(*args, *([] if y_prev is None else [y_prev]))


ROUTE_TILE = 1024


def _route_kernel(x_ref, sh_ref, sc_ref, wt_ref, b_ref, before_ref, h_ref, e_ref, g_ref, r_ref, cnt_ref,
                  carry_ref):
    i = pl.program_id(0)

    @pl.when(i == 0)
    def _():
        carry_ref[...] = jnp.zeros_like(carry_ref)

    x = x_ref[...]
    ms = jnp.mean(x * x, axis=-1, keepdims=True)
    h = x * lax.rsqrt(ms + RMS_EPS) * (1.0 + sc_ref[0]) + sh_ref[0]
    h_ref[...] = h.astype(BF16)
    logits = _mm(wt_ref[...], h, "nt", 6) + b_ref[...]
    n_e, tm = logits.shape
    eidx = lax.broadcasted_iota(jnp.int32, (n_e, tm), 0)
    work = logits
    top_v, top_e = [], []
    for _ in range(TOP_K):
        m = jnp.max(work, axis=0, keepdims=True)
        sel = jnp.min(jnp.where(work == m, eidx, n_e), axis=0, keepdims=True)
        top_v.append(m)
        top_e.append(sel)
        work = jnp.where(eidx == sel, -jnp.inf, work)
    ex = [jnp.exp(v - top_v[0]) for v in top_v]
    den = ex[0] + ex[1] + ex[2] + ex[3]
    g_ref[...] = jnp.concatenate([e_ / den for e_ in ex], axis=0)
    e_ref[...] = jnp.concatenate(top_e, axis=0)
    chosen = [eidx == sel for sel in top_e]
    ind = sum(c.astype(F32) for c in chosen)
    carry = carry_ref[...]
    cnt = (jnp.dot(ind.astype(BF16), before_ref[...], preferred_element_type=F32)
           + jnp.concatenate([carry] * (tm // carry.shape[1]), axis=1))
    r_ref[...] = jnp.concatenate(
        [jnp.sum(jnp.where(c, cnt, 0.0), axis=0, keepdims=True) for c in chosen], axis=0).astype(jnp.int32)
    carry = carry + jnp.sum(ind, axis=1, keepdims=True)
    carry_ref[...] = carry
    cnt_ref[...] = carry


def _route(tok, shift, scale, router_w, router_b, rows_per_mod):
    t, d = tok.shape
    n_e = router_w.shape[1]
    tm = ROUTE_TILE
    tiles_per_mod = rows_per_mod // tm
    mod_spec = pl.BlockSpec((1, 1, d), lambda i: (i // tiles_per_mod, 0, 0))
    before = jnp.asarray(np.triu(np.ones((tm, tm), np.float32), 1), BF16)
    kt_spec = pl.BlockSpec((TOP_K, tm), lambda i: (0, i))
    h, top_e, gate, rank, cnt = pl.pallas_call(
        _route_kernel,
        grid=(t // tm,),
        in_specs=[pl.BlockSpec((tm, d), lambda i: (i, 0)), mod_spec, mod_spec,
                  pl.BlockSpec((n_e, d), lambda i: (0, 0)), pl.BlockSpec((n_e, tm), lambda i: (0, 0)),
                  pl.BlockSpec((tm, tm), lambda i: (0, 0))],
        out_specs=[pl.BlockSpec((tm, d), lambda i: (i, 0)), kt_spec, kt_spec, kt_spec,
                   pl.BlockSpec((n_e, 128), lambda i: (0, 0))],
        out_shape=[jax.ShapeDtypeStruct((t, d), BF16), jax.ShapeDtypeStruct((TOP_K, t), jnp.int32),
                   jax.ShapeDtypeStruct((TOP_K, t), F32), jax.ShapeDtypeStruct((TOP_K, t), jnp.int32),
                   jax.ShapeDtypeStruct((n_e, 128), F32)],
        scratch_shapes=[pltpu.VMEM((n_e, 128), F32)],
        compiler_params=pltpu.CompilerParams(dimension_semantics=("arbitrary",),
                                             vmem_limit_bytes=VMEM_LIMIT),
        name="moe_route",
    )(tok, shift[:, None, :], scale[:, None, :], router_w.T, jnp.broadcast_to(router_b[:, None], (n_e, tm)),
      before)
    return h, top_e, gate, rank, cnt[:, 0].astype(jnp.int32)


def _moe(tok, shift, scale, rows_per_mod, router_w, router_b, w1_layers, w2_layers, layer, b1, b2):
    t, d = tok.shape
    h, top_e, gate, rank, counts = _route(tok, shift, scale, router_w, router_b, rows_per_mod)
    n_assign = t * TOP_K
    n_blocks = -(-n_assign // MOE_BLOCK) + N_EXPERTS
    n_slots = n_blocks * MOE_BLOCK
    padded = (counts + MOE_BLOCK - 1) // MOE_BLOCK * MOE_BLOCK
    start = jnp.cumsum(counts) - counts
    p_end = jnp.cumsum(padded)
    p_start = p_end - padded
    experts = jnp.arange(N_EXPERTS, dtype=jnp.int32)
    dest = jnp.sum(jnp.where(top_e[:, :, None] == experts, p_start, 0), axis=-1) + rank
    blk_first = jnp.arange(n_blocks, dtype=jnp.int32) * MOE_BLOCK
    blk_e = jnp.minimum(jnp.sum(p_end[None, :] <= blk_first[:, None], axis=1), N_EXPERTS - 1).astype(jnp.int32)
    n_used = (p_end[-1] // MOE_BLOCK).astype(jnp.int32).reshape(1)
    order = jnp.argsort(top_e.T.reshape(-1))
    per_slot = lambda per_expert: jnp.repeat(per_expert[blk_e], MOE_BLOCK)
    j = jnp.arange(n_slots, dtype=jnp.int32) - per_slot(p_start)
    src = jnp.clip(per_slot(start) + j, 0, n_assign - 1)
    slot_tok = jnp.where(j < per_slot(counts), order[src] // TOP_K, t).astype(jnp.int32)
    h_pad = jnp.concatenate([h, jnp.zeros((1, d), BF16)], axis=0)
    b1p = jnp.concatenate([b1[..., ::2], b1[..., 1::2]], axis=-1)
    y = None
    for first in range(0, n_blocks, n_blocks // MOE_CALLS):
        sl = slice(first * MOE_BLOCK, (first + n_blocks // MOE_CALLS) * MOE_BLOCK)
        y = _moe_experts(h_pad[slot_tok[sl]], first, n_blocks, y, blk_e, n_used, w1_layers, w2_layers, layer,
                         b1p, b2)
    return y[dest], gate


def _rms(x):
    return x * lax.rsqrt(jnp.mean(x * x, axis=-1, keepdims=True) + RMS_EPS)


def _seg_sum(x, ones_bd):
    x_hi, x_lo = _hi_lo(x)
    return (jnp.dot(x_hi, ones_bd, preferred_element_type=F32)
            + jnp.dot(x_lo, ones_bd, preferred_element_type=F32))


def _head_ones(width):
    h = np.arange(width) // HEAD_DIM
    return jnp.asarray(h[:, None] == h[None, :], BF16)


def _shift_conv(i, first_ref, last_ref, u_ref, up_ref, un_ref, sw_ref):
    keep_prev = (first_ref[i] == 0).astype(F32)
    keep_next = (last_ref[i] == 0).astype(F32)
    rows = u_ref.shape[0]

    def conv(lo, width, tap_lo):
        x = u_ref[:, lo:lo + width]
        ridx = lax.broadcasted_iota(jnp.int32, x.shape, 0)
        x_prev = jnp.where(ridx == 0, up_ref[7:8, lo:lo + width] * keep_prev, pltpu.roll(x, 1, axis=0))
        x_next = jnp.where(ridx == rows - 1, un_ref[0:1, lo:lo + width] * keep_next,
                           pltpu.roll(x, rows - 1, axis=0))
        w = sw_ref[:, tap_lo:tap_lo + width]
        return x_prev * w[0:1] + x * w[1:2] + x_next * w[2:3]

    return conv


def _rwkv7_prep_kernel(first_ref, last_ref, u_ref, up_ref, un_ref, sw_ref, w0_ref, w2_ref, a0_ref, a2_ref,
                       g2_ref, kk_ref, ka_ref, bd_ref, r_ref, v_ref, kap_ref, g_ref, lw_ref, k_ref, al_ref,
                       *, d_a, lora_off):
    conv = _shift_conv(pl.program_id(0), first_ref, last_ref, u_ref, up_ref, un_ref, sw_ref)
    r = conv(0, d_a, 0)
    k = conv(d_a, d_a, d_a)
    v = conv(2 * d_a, d_a, 2 * d_a)
    wd = conv(lora_off, 2 * DECAY_LORA, 3 * d_a)
    ad = conv(lora_off + 2 * DECAY_LORA, 2 * AAA_LORA, 3 * d_a + 2 * DECAY_LORA)
    gd = conv(lora_off + 2 * DECAY_LORA + 2 * AAA_LORA, GATE_LORA, 3 * d_a + 2 * DECAY_LORA + 2 * AAA_LORA)
    mm3 = lambda p, q: _mm(p, q, "nn", 3)
    kk = k * kk_ref[...]
    kap = kk / jnp.maximum(jnp.sqrt(_seg_sum(kk * kk, bd_ref[...])), 1e-12)
    lora_w = jnp.tanh(wd)
    for d in range(2):
        z = w0_ref[d:d + 1] + mm3(lora_w[:, d * DECAY_LORA:(d + 1) * DECAY_LORA], w2_ref[d])
        lw_ref[d] = -math.exp(-0.5) * jax.nn.sigmoid(z)
        a = jax.nn.sigmoid(a0_ref[d:d + 1] + mm3(ad[:, d * AAA_LORA:(d + 1) * AAA_LORA], a2_ref[d]))
        k_ref[d] = k * (1.0 + (a - 1.0) * ka_ref[...])
        al_ref[d] = kap * a
    g_ref[...] = mm3(jax.nn.sigmoid(gd), g2_ref[...])
    r_ref[...] = r
    v_ref[...] = v
    kap_ref[...] = kap


def _segment_flags(bsz, s, n_ctx, tile):
    lat, ctx = np.arange(bsz * s // tile), np.arange(bsz * n_ctx // tile)
    first = np.concatenate([lat % (s // tile) == 0, ctx % (n_ctx // tile) == 0])
    last = np.concatenate([lat % (s // tile) == s // tile - 1, ctx % (n_ctx // tile) == n_ctx // tile - 1])
    return jnp.asarray(first, jnp.int32), jnp.asarray(last, jnp.int32)


def _rwkv7_prep(u, lora_off, bsz, s, n_ctx, shift_w, w0, w2, a0, a2, g2, k_k, k_a):
    t, cols = u.shape
    d_a = w0.shape[-1]
    tile = ROW_TILE
    halo = 8
    first, last = _segment_flags(bsz, s, n_ctx, tile)
    full = lambda arr: pl.BlockSpec(arr.shape, lambda i, f, l: (0,) * arr.ndim)
    row = lambda arr: arr[None, :]
    params = [shift_w, w0, w2, a0, a2, g2, row(k_k), row(k_a), _head_ones(d_a)]
    one = pl.BlockSpec((tile, d_a), lambda i, f, l: (i, 0))
    two = pl.BlockSpec((2, tile, d_a), lambda i, f, l: (0, i, 0))
    flat = jax.ShapeDtypeStruct((t, d_a), F32)
    both = jax.ShapeDtypeStruct((2, t, d_a), F32)
    grid_spec = pltpu.PrefetchScalarGridSpec(
        num_scalar_prefetch=2,
        grid=(t // tile,),
        in_specs=[pl.BlockSpec((tile, cols), lambda i, f, l: (i, 0)),
                  pl.BlockSpec((halo, cols), lambda i, f, l: (jnp.maximum(i * (tile // halo) - 1, 0), 0)),
                  pl.BlockSpec((halo, cols),
                               lambda i, f, l: (jnp.minimum((i + 1) * (tile // halo), t // halo - 1), 0))]
        + [full(p) for p in params],
        out_specs=[one, one, one, one, two, two, two],
    )
    return pl.pallas_call(
        functools.partial(_rwkv7_prep_kernel, d_a=d_a, lora_off=lora_off),
        grid_spec=grid_spec,
        out_shape=[flat, flat, flat, flat, both, both, both],
        compiler_params=pltpu.CompilerParams(dimension_semantics=("arbitrary",),
                                             vmem_limit_bytes=VMEM_LIMIT),
        name="rwkv7_prep",
    )(first, last, u, u, u, *params)


def _even_out_kernel(yaf_ref, yab_ref, r_ref, v_ref, k_ref, g_ref, ybf_ref, ybb_ref, og_ref, rk_ref, lnw_ref,
                     lnb_ref, nw_ref, bd_ref, w_ref, res_ref, gate_ref, o_ref):
    bd = bd_ref[...]
    inv = 1.0 / HEAD_DIM
    y = yaf_ref[...] + yab_ref[...]
    yc = y - _seg_sum(y, bd) * inv
    var = _seg_sum(yc * yc, bd) * inv
    yn = yc * lax.rsqrt(var + LN_X_EPS) * lnw_ref[...] + lnb_ref[...]
    bonus = _seg_sum(r_ref[...] * (k_ref[0] + k_ref[1]) * rk_ref[...], bd) * v_ref[...]
    out_a = (yn + bonus) * g_ref[...]
    o = ybf_ref[...] + ybb_ref[...]
    out_b = o * lax.rsqrt(_seg_sum(o * o, bd) * inv + RMS_EPS) * nw_ref[...] * jax.nn.silu(og_ref[...])
    ycat = jnp.concatenate([out_a, out_b], axis=-1).astype(BF16)
    o_ref[...] = res_ref[...] + gate_ref[0] * jnp.dot(ycat, w_ref[...], preferred_element_type=F32)


def _even_out(ya, r, v, k, g, yb, u, col_og, r_k, ln_w, ln_b, norm_w, w_out, res, gate, rows_per_mod):
    t, width = r.shape
    d = res.shape[1]
    tile = ROW_TILE
    tiles_per_mod = rows_per_mod // tile
    one = pl.BlockSpec((tile, width), lambda i: (i, 0))
    two = pl.BlockSpec((2, tile, width), lambda i: (0, i, 0))
    vec = pl.BlockSpec((1, width), lambda i: (0, 0))
    n_heads = width // HEAD_DIM
    return pl.pallas_call(
        _even_out_kernel,
        grid=(t // tile,),
        in_specs=[one, one, one, one, two, one, one, one, pl.BlockSpec((tile, width), lambda i: (i, col_og)),
                  vec, vec, vec, vec, pl.BlockSpec((width, width), lambda i: (0, 0)),
                  pl.BlockSpec(w_out.shape, lambda i: (0, 0)), pl.BlockSpec((tile, d), lambda i: (i, 0)),
                  pl.BlockSpec((1, 1, d), lambda i: (i // tiles_per_mod, 0, 0))],
        out_specs=pl.BlockSpec((tile, d), lambda i: (i, 0)),
        out_shape=jax.ShapeDtypeStruct((t, d), F32),
        compiler_params=pltpu.CompilerParams(dimension_semantics=("arbitrary",),
                                             vmem_limit_bytes=VMEM_LIMIT),
        name="even_readout_out_proj",
    )(*ya, r, v, k, g, *yb, u, r_k.reshape(1, width), ln_w[None, :], ln_b[None, :],
      jnp.tile(norm_w, n_heads)[None, :], _head_ones(width), w_out.astype(BF16), res, gate[:, None, :])


def _even_layer(tok, sh1, sc1, g1, bsz, s, n_ctx, w_in, w_out, shift_w, w0, w2, a0, a2, g2, k_k, k_a, r_k, ln_w,
                ln_b, lb, norm_w):
    d_a = w0.shape[-1]
    d_b = lb.shape[0]
    rkv = 3 * d_a
    a_cols = rkv + 2 * DECAY_LORA + 2 * AAA_LORA + GATE_LORA
    w_perm = jnp.concatenate([w_in[:, :rkv], w_in[:, a_cols:], w_in[:, rkv:a_cols]], axis=1)
    u = _mod_linear(tok, sh1, sc1, w_perm, s)
    r, v, kap, g, lw, k, al = _rwkv7_prep(u, rkv + 5 * d_b, bsz, s, n_ctx, shift_w, w0, w2, a0, a2, g2, k_k, k_a)
    ya = _rwkv7_recurrence(r, v, kap, lw, k, al, bsz, s, n_ctx)
    col0 = rkv // d_b
    yb = _hgrn2_recurrence(u, lb, col0, col0 + 1, col0 + 3, bsz, s, n_ctx)
    return _even_out(ya, r, v, k, g, yb, u, col0 + 4, r_k, ln_w, ln_b, norm_w, w_out, tok, g1, s)


def _hyena_filters(length, d_d, w1, b1, fr1, w2, b2, fr2, w3):
    t = jnp.linspace(0.0, 1.0, length, dtype=F32)[:, None]
    bands = (HYENA_EMB - 1) // 2
    f = jnp.linspace(1e-4, bands - 1, bands, dtype=F32)
    ang = (2.0 * math.pi / length) * jnp.arange(length, dtype=F32)[:, None] * f
    z = jnp.concatenate([t, jnp.cos(ang), -jnp.sin(ang)], axis=-1)
    hid = jnp.sin(fr1 * (jnp.dot(z, w1, precision=_HI) + b1))
    hid = jnp.sin(fr2 * (jnp.dot(hid, w2, precision=_HI) + b2))
    h = jnp.dot(hid, w3, precision=_HI).reshape(length, HYENA_ORDER, 2, d_d)
    deltas = jnp.abs(jnp.linspace(math.log(HYENA_TARGET) / HYENA_SLOW_DECAY,
                                  math.log(HYENA_TARGET) / HYENA_FAST_DECAY, d_d, dtype=F32))
    return h * jnp.exp(-t * deltas)[:, None, None, :]


DFT_R = 128
DFT_COL_TILE = 4096
DFT_K1_TILE = 4


def _hi_lo(a):
    hi = a.astype(BF16)
    return hi, (a - hi.astype(F32)).astype(BF16)


def _mm3(f_hi, f_lo, x_hi, x_lo):
    d = lambda p, q: jnp.dot(p, q, preferred_element_type=F32)
    return d(f_hi, x_hi) + (d(f_hi, x_lo) + d(f_lo, x_hi))


def _dft_matrices(n1):
    k1 = np.arange(n1, dtype=np.float64)
    k2 = np.arange(DFT_R, dtype=np.float64)
    as_f32 = lambda a: jnp.asarray(a, F32)
    cs = lambda ang: (as_f32(np.cos(ang)), as_f32(-np.sin(ang)))
    f1 = cs(2.0 * np.pi * np.outer(k1, k1) / n1)
    f2 = cs(2.0 * np.pi * np.outer(k2, k2) / DFT_R)
    tw = cs(2.0 * np.pi * np.outer(k1, k2) / (n1 * DFT_R))
    tw = tuple(jnp.broadcast_to(t[:, :, None], (n1, DFT_R, 128)) for t in tw)
    return f1, f2, tw


def _dft_rows_kernel(x_ref, frh_ref, frl_ref, fih_ref, fil_ref, or_ref, oi_ref):
    frh, frl, fih, fil = frh_ref[...], frl_ref[...], fih_ref[...], fil_ref[...]
    x0 = _hi_lo(x_ref[0])
    if x_ref.shape[0] == 1:
        or_ref[0] = _mm3(frh, frl, *x0)
        oi_ref[0] = _mm3(fih, fil, *x0)
    else:
        x1 = _hi_lo(x_ref[1])
        or_ref[0] = _mm3(frh, frl, *x0) - _mm3(fih, fil, *x1)
        oi_ref[0] = _mm3(frh, frl, *x1) + _mm3(fih, fil, *x0)


def _dft_rows(x2d, f1, pack):
    bsz, rows, cols = x2d.shape
    n1 = f1[0].shape[0]
    mats = [m for f in f1 for m in _hi_lo(f[:, :rows])]
    mat_spec = pl.BlockSpec((n1, rows), lambda b, j: (0, 0))
    out_spec = pl.BlockSpec((1, n1, DFT_COL_TILE), lambda b, j: (b, 0, j))
    out = jax.ShapeDtypeStruct((bsz // pack, n1, cols), F32)
    return pl.pallas_call(
        _dft_rows_kernel,
        grid=(bsz // pack, cols // DFT_COL_TILE),
        in_specs=[pl.BlockSpec((pack, rows, DFT_COL_TILE), lambda b, j: (b, 0, j))] + [mat_spec] * 4,
        out_specs=[out_spec, out_spec],
        out_shape=[out, out],
        name="dft_rows",
    )(x2d, *mats)


def _dft_mid_kernel(*refs, conv):
    if conv:
        (ar_ref, ai_ref, tr_ref, ti_ref, kr_ref, ki_ref, frh_ref, frl_ref, fih_ref, fil_ref,
         or_ref, oi_ref) = refs
    else:
        ar_ref, ai_ref, tr_ref, ti_ref, frh_ref, frl_ref, fih_ref, fil_ref, or_ref, oi_ref = refs
    frh, frl, fih, fil = frh_ref[...], frl_ref[...], fih_ref[...], fil_ref[...]
    reps = ar_ref.shape[3] // tr_ref.shape[2]
    rows = range(ar_ref.shape[1])
    wide = lambda t: jnp.concatenate([t] * reps, axis=-1)
    tr = [wide(tr_ref[j]) for j in rows]
    ti = [wide(ti_ref[j]) for j in rows]
    ar = [ar_ref[0, j] for j in rows]
    ai = [ai_ref[0, j] for j in rows]
    pr = [_hi_lo(ar[j] * tr[j] - ai[j] * ti[j]) for j in rows]
    pi = [_hi_lo(ar[j] * ti[j] + ai[j] * tr[j]) for j in rows]
    xr = [_mm3(frh, frl, *pr[j]) - _mm3(fih, fil, *pi[j]) for j in rows]
    xi = [_mm3(frh, frl, *pi[j]) + _mm3(fih, fil, *pr[j]) for j in rows]
    if not conv:
        for j in rows:
            or_ref[0, j] = xr[j]
            oi_ref[0, j] = xi[j]
        return
    yr = [_hi_lo(xr[j] * kr_ref[0, j] - xi[j] * ki_ref[0, j]) for j in rows]
    yi = [_hi_lo(xr[j] * ki_ref[0, j] + xi[j] * kr_ref[0, j]) for j in rows]
    br = [_mm3(frh, frl, *yr[j]) + _mm3(fih, fil, *yi[j]) for j in rows]
    bi = [_mm3(frh, frl, *yi[j]) - _mm3(fih, fil, *yr[j]) for j in rows]
    for j in rows:
        or_ref[0, j] = br[j] * tr[j] + bi[j] * ti[j]
        oi_ref[0, j] = bi[j] * tr[j] - br[j] * ti[j]


def _dft_mid(a_r, a_i, f2, tw, kf=None):
    (f_r, f_i), (t_r, t_i) = f2, tw
    bsz, n1, _, ch = a_r.shape
    slab = pl.BlockSpec((1, DFT_K1_TILE, DFT_R, ch), lambda b, g: (b, g, 0, 0))
    tw = pl.BlockSpec((DFT_K1_TILE, DFT_R, t_r.shape[2]), lambda b, g: (g, 0, 0))
    mat = pl.BlockSpec((DFT_R, DFT_R), lambda b, g: (0, 0))
    args = [a_r, a_i, t_r, t_i]
    specs = [slab, slab, tw, tw]
    if kf is not None:
        kf_r, kf_i, o = kf
        kspec = pl.BlockSpec((1, DFT_K1_TILE, DFT_R, ch), lambda b, g: (o, g, 0, 0))
        args += [kf_r, kf_i]
        specs += [kspec, kspec]
    args += [m for f in (f_r, f_i) for m in _hi_lo(f)]
    specs += [mat] * 4
    out = jax.ShapeDtypeStruct(a_r.shape, F32)
    return pl.pallas_call(
        functools.partial(_dft_mid_kernel, conv=kf is not None),
        grid=(bsz, n1 // DFT_K1_TILE),
        in_specs=specs,
        out_specs=[slab, slab],
        out_shape=[out, out],
        compiler_params=pltpu.CompilerParams(dimension_semantics=("arbitrary", "arbitrary"),
                                             vmem_limit_bytes=VMEM_LIMIT),
        name="dft_mid_conv" if kf is not None else "dft_mid",
    )(*args)


def _idft_rows_kernel(br_ref, bi_ref, z_ref, g_ref, bias_ref, grh_ref, grl_ref, gih_ref, gil_ref, o_ref):
    grh, grl, gih, gil = grh_ref[...], grl_ref[...], gih_ref[...], gil_ref[...]
    b_r, b_i = _hi_lo(br_ref[0]), _hi_lo(bi_ref[0])
    y = (_mm3(grh, grl, *b_r) + _mm3(gih, gil, *b_i), _mm3(grh, grl, *b_i) - _mm3(gih, gil, *b_r))
    for e in range(2):
        o_ref[e] = g_ref[e] * (y[e] + z_ref[e] * bias_ref[...])


def _idft_rows(b_r, b_i, z2d, gate2d, bias_cols, f1):
    bsz, rows, cols = z2d.shape
    n1 = f1[0].shape[0]
    scale = 1.0 / (n1 * DFT_R)
    mats = [m for f in f1 for m in _hi_lo(f[:rows] * scale)]
    mat_spec = pl.BlockSpec((rows, n1), lambda b, j: (0, 0))
    in_spec = pl.BlockSpec((1, n1, DFT_COL_TILE), lambda b, j: (b, 0, j))
    io_spec = pl.BlockSpec((2, rows, DFT_COL_TILE), lambda b, j: (b, 0, j))
    return pl.pallas_call(
        _idft_rows_kernel,
        grid=(bsz // 2, cols // DFT_COL_TILE),
        in_specs=[in_spec, in_spec, io_spec, io_spec, pl.BlockSpec((1, DFT_COL_TILE), lambda b, j: (0, j))]
        + [mat_spec] * 4,
        out_specs=io_spec,
        out_shape=jax.ShapeDtypeStruct((bsz, rows, cols), F32),
        name="idft_rows",
    )(b_r, b_i, z2d, gate2d, bias_cols, *mats)


def _hyena_prep_kernel(first_ref, last_ref, u_ref, up_ref, un_ref, sw_ref, v_ref, x1_ref, x2_ref):
    conv = _shift_conv(pl.program_id(0), first_ref, last_ref, u_ref, up_ref, un_ref, sw_ref)
    ch = v_ref.shape[1]
    for n, o_ref in enumerate((v_ref, x1_ref, x2_ref)):
        o_ref[...] = conv(n * ch, ch, n * ch)


def _hyena_prep(u, col_block, bsz, s, short_w):
    width = short_w.shape[1]
    ch = width // 3
    tile = ROW_TILE
    halo = 8
    n_tiles = bsz * s // tile
    in_seq = np.arange(n_tiles) % (s // tile)
    first = jnp.asarray(in_seq == 0, jnp.int32)
    last = jnp.asarray(in_seq == s // tile - 1, jnp.int32)
    out = jax.ShapeDtypeStruct((bsz * s, ch), F32)
    out_spec = pl.BlockSpec((tile, ch), lambda i, f, l: (i, 0))
    grid_spec = pltpu.PrefetchScalarGridSpec(
        num_scalar_prefetch=2,
        grid=(n_tiles,),
        in_specs=[pl.BlockSpec((tile, width), lambda i, f, l: (i, col_block)),
                  pl.BlockSpec((halo, width), lambda i, f, l: (jnp.maximum(i * (tile // halo) - 1, 0), col_block)),
                  pl.BlockSpec((halo, width), lambda i, f, l: ((i + 1) * (tile // halo), col_block)),
                  pl.BlockSpec(short_w.shape, lambda i, f, l: (0, 0))],
        out_specs=[out_spec, out_spec, out_spec],
    )
    return pl.pallas_call(
        _hyena_prep_kernel,
        grid_spec=grid_spec,
        out_shape=[out, out, out],
        name="hyena_prep",
    )(first, last, u, u, u, short_w)


def _hyena(v, x1, x2, bsz, length, w1, b1, fr1, w2, b2, fr2, w3, bias):
    ch = v.shape[-1]
    rows = length // DFT_R
    h = _hyena_filters(length, ch, w1, b1, fr1, w2, b2, fr2, w3)
    n1 = 2 * rows
    f1, f2, tw = _dft_matrices(n1)
    four_d = lambda t: t.reshape(t.shape[0], n1, DFT_R, ch)
    hh = h.transpose(1, 2, 0, 3).reshape(HYENA_ORDER * 2, rows, DFT_R * ch)
    hf_r, hf_i = (t.reshape(HYENA_ORDER, 2, n1, DFT_R, ch)
                  for t in _dft_mid(*(four_d(t) for t in _dft_rows(hh, f1, 1)), f2, tw))
    kf_r = hf_r[:, 0] + hf_r[:, 1]
    kf_i = hf_i[:, 0] - hf_i[:, 1]
    assert bsz % 2 == 0, "batch entries are convolved in pairs (one complex sequence per pair)"
    z = v
    for o, gate in enumerate((x1, x2)):
        z2d = z.reshape(bsz, rows, DFT_R * ch)
        a_r, a_i = _dft_rows(z2d, f1, 2)
        b_r, b_i = _dft_mid(four_d(a_r), four_d(a_i), f2, tw, kf=(kf_r, kf_i, o))
        flat = lambda t: t.reshape(bsz // 2, n1, DFT_R * ch)
        y = _idft_rows(flat(b_r), flat(b_i), z2d, gate.reshape(z2d.shape), jnp.tile(bias[o], DFT_R)[None, :], f1)
        z = y
    return z.reshape(bsz * length, ch)


def _odd_layer(tok, sh1, sc1, g1, bsz, s, n_ctx, w_in, w_out, q_norm, k_norm, rpb, short_w, w1, b1, fr1, w2,
               b2, fr2, w3, bias):
    n_heads = rpb.shape[0]
    d_c = n_heads * HEAD_DIM
    n_lat = bsz * s
    u = _mod_linear(tok, sh1, sc1, w_in, s)
    q, k, v = _na_prep(u, bsz, s, n_heads, q_norm, k_norm)
    u_ctx = u[n_lat:, d_c:3 * d_c].reshape(bsz, n_ctx, 2, n_heads, HEAD_DIM)
    head_major = lambda t: t.transpose(0, 2, 1, 3).astype(BF16)
    kc = head_major(_rms(u_ctx[:, :, 0]) * k_norm)
    vc = head_major(u_ctx[:, :, 1])
    y_na = _neighbourhood_attention(q, k, v, kc, vc, rpb)
    assert (3 * d_c) % short_w.shape[1] == 0
    y_hy = _hyena(*_hyena_prep(u, 3 * d_c // short_w.shape[1], bsz, s, short_w), bsz, s, w1, b1, fr1, w2, b2,
                  fr2, w3, bias)
    return _res_linear2(y_na, y_hy, w_out, tok, g1, s)


def kernel(x, c, ctx, c_ctx, mod_w, mod_b, router_w, router_b, moe_w1, moe_b1, moe_w2, moe_b2, ab_w_in, ab_w_out, rwkv_shift, rwkv_w0, rwkv_w2, rwkv_a0, rwkv_a2, rwkv_g2, rwkv_k_k, rwkv_k_a, rwkv_r_k, rwkv_ln_w, rwkv_ln_b, hgrn_lb_logits, hgrn_norm_w, cd_w_in, cd_w_out, na_q_norm, na_k_norm, na_rpb, hy_short, hy_w1, hy_b1, hy_freq1, hy_w2, hy_b2, hy_freq2, hy_w3, hy_bias):
    bsz, s, d = x.shape
    n_ctx = ctx.shape[1]
    depth = mod_w.shape[0]
    n_lat = bsz * s
    lb_all = jnp.cumsum(jax.nn.softmax(hgrn_lb_logits, axis=0), axis=0)
    tok = jnp.concatenate([x.reshape(n_lat, d), ctx.reshape(bsz * n_ctx, d)], axis=0)
    cond = jnp.concatenate([jax.nn.silu(c), jax.nn.silu(c_ctx)[None, :],
                            jnp.zeros((8 - bsz - 1, d), F32)], axis=0)
    assert depth == 2, "layer 0 = RWKV-7 || HGRN2 with context outputs, layer 1 = attention || Hyena, latent only"
    for l in range(depth):
        j = l // 2
        mod = _small_linear(cond, mod_w[l], mod_b[l])[:bsz + 1]
        sh1, sc1, g1, sh2, sc2, g2 = jnp.split(mod, 6, axis=-1)
        if l % 2 == 0:
            tok = _even_layer(tok, sh1, sc1, g1, bsz, s, n_ctx, ab_w_in[j], ab_w_out[j], rwkv_shift[j],
                              rwkv_w0[j], rwkv_w2[j], rwkv_a0[j], rwkv_a2[j], rwkv_g2[j], rwkv_k_k[j],
                              rwkv_k_a[j], rwkv_r_k[j], rwkv_ln_w[j], rwkv_ln_b[j], lb_all[j], hgrn_norm_w[j])
        else:
            tok = _odd_layer(tok, sh1, sc1, g1, bsz, s, n_ctx, cd_w_in[j], cd_w_out[j], na_q_norm[j], na_k_norm[j],
                             na_rpb[j], hy_short[j], hy_w1[j], hy_b1[j], hy_freq1[j], hy_w2[j], hy_b2[j],
                             hy_freq2[j], hy_w3[j], hy_bias[j])
        yk, gate = _moe(tok, sh2, sc2, s, router_w[l], router_b[l], moe_w1, moe_w2, l, moe_b1[l], moe_b2[l])
        tok = _combine(tok, yk, gate, g2, s)
    return tok[:n_lat].reshape(bsz, s, d)


def _combine_kernel(x_ref, y_ref, w_ref, g_ref, o_ref):
    reps = x_ref.shape[1] // w_ref.shape[2]
    acc = y_ref[0] * jnp.concatenate([w_ref[0]] * reps, axis=-1)
    for k in range(1, y_ref.shape[0]):
        acc = acc + y_ref[k] * jnp.concatenate([w_ref[k]] * reps, axis=-1)
    o_ref[...] = x_ref[...] + g_ref[0] * acc


def _combine(x, yk, gate, res_gate, rows_per_mod):
    m, d = x.shape
    n_k = yk.shape[0]
    rows = COMBINE_TILE
    tiles_per_mod = rows_per_mod // rows
    tile = pl.BlockSpec((rows, d), lambda i: (i, 0))
    lanes = 128
    return pl.pallas_call(
        _combine_kernel,
        grid=(m // rows,),
        in_specs=[tile, pl.BlockSpec((n_k, rows, d), lambda i: (0, i, 0)),
                  pl.BlockSpec((n_k, rows, lanes), lambda i: (0, i, 0)),
                  pl.BlockSpec((1, 1, d), lambda i: (i // tiles_per_mod, 0, 0))],
        out_specs=tile,
        out_shape=jax.ShapeDtypeStruct((m, d), F32),
        compiler_params=pltpu.CompilerParams(dimension_semantics=("arbitrary",),
                                             vmem_limit_bytes=VMEM_LIMIT),
        name="moe_combine",
    )(x, yk, jnp.broadcast_to(gate[:, :, None], gate.shape + (lanes,)), res_gate[:, None, :])
```

```python
import functools
import math

import numpy as np
import jax
import jax.numpy as jnp
from jax import lax
from jax.experimental import pallas as pl
from jax.experimental.pallas import tpu as pltpu

F32 = jnp.float32
BF16 = jnp.bfloat16

HEAD_DIM = 64
GRID_W = 64
DECAY_LORA = 64
AAA_LORA = 64
GATE_LORA = 128
LN_X_EPS = 1e-5 * HEAD_DIM
NA_ROWS = 8
NA_COLS = 16
ROPE_THETA = 10000.0
HYENA_ORDER = 2
HYENA_EMB = 33
HYENA_FAST_DECAY = 0.3
HYENA_SLOW_DECAY = 1.5
HYENA_TARGET = 1e-2
N_EXPERTS = 32
TOP_K = 4
SWIGLU_ALPHA = 1.702
SWIGLU_LIMIT = 7.0
MOE_BLOCK = 256
MOE_CALLS = 4
RMS_EPS = 1e-6

ROW_TILE = 256
COMBINE_TILE = 512
COMBINE_CALLS = 2
REC_CHUNK = 64
REC_PASSES = 1
REC_CHAIN_PASSES = 1
VMEM_LIMIT = 56 * 1024 * 1024

_HI = lax.Precision.HIGHEST


_DIMS = {"nn": (((1,), (0,)), ((), ())), "nt": (((1,), (1,)), ((), ())), "tn": (((0,), (0,)), ((), ()))}


def _mm(a, b, form, passes):
    dims = _DIMS[form]
    if passes == 6:
        return lax.dot_general(a, b, dims, precision=_HI, preferred_element_type=F32)
    a_hi = a.astype(BF16)
    b_hi = b.astype(BF16)
    out = lax.dot_general(a_hi, b_hi, dims, preferred_element_type=F32)
    if passes == 3:
        a_lo = (a - a_hi.astype(F32)).astype(BF16)
        b_lo = (b - b_hi.astype(F32)).astype(BF16)
        out = out + (lax.dot_general(a_hi, b_lo, dims, preferred_element_type=F32)
                     + lax.dot_general(a_lo, b_hi, dims, preferred_element_type=F32))
    return out


def _dot(a, b):
    return _mm(a, b, "nn", 6)


def _mod_linear_kernel(x_ref, sh_ref, sc_ref, w_ref, o_ref):
    x = x_ref[...]
    ms = jnp.mean(x * x, axis=-1, keepdims=True)
    h = x * lax.rsqrt(ms + RMS_EPS) * (1.0 + sc_ref[0]) + sh_ref[0]
    o_ref[...] = jnp.dot(h.astype(BF16), w_ref[...], preferred_element_type=F32)


def _mod_linear(x, shift, scale, w, rows_per_mod):
    m, d = x.shape
    n = w.shape[1]
    tiles_per_mod = rows_per_mod // ROW_TILE
    mod_spec = pl.BlockSpec((1, 1, d), lambda i: (i // tiles_per_mod, 0, 0))
    return pl.pallas_call(
        _mod_linear_kernel,
        grid=(m // ROW_TILE,),
        in_specs=[pl.BlockSpec((ROW_TILE, d), lambda i: (i, 0)), mod_spec, mod_spec,
                  pl.BlockSpec((d, n), lambda i: (0, 0))],
        out_specs=pl.BlockSpec((ROW_TILE, n), lambda i: (i, 0)),
        out_shape=jax.ShapeDtypeStruct((m, n), F32),
        compiler_params=pltpu.CompilerParams(dimension_semantics=("arbitrary",),
                                             vmem_limit_bytes=VMEM_LIMIT),
        name="mod_linear",
    )(x, shift[:, None, :], scale[:, None, :], w.astype(BF16))


def _res_linear2_kernel(ya_ref, yb_ref, w_ref, res_ref, g_ref, o_ref):
    y = jnp.concatenate([ya_ref[...], yb_ref[...]], axis=-1).astype(BF16)
    o_ref[...] = res_ref[...] + g_ref[0] * jnp.dot(y, w_ref[...], preferred_element_type=F32)


def _res_linear2(ya, yb, w, res, gate, rows_per_mod):
    m, ka = ya.shape
    kb = yb.shape[1]
    n = w.shape[1]
    tiles_per_mod = rows_per_mod // ROW_TILE
    return pl.pallas_call(
        _res_linear2_kernel,
        grid=(m // ROW_TILE,),
        in_specs=[pl.BlockSpec((ROW_TILE, ka), lambda i: (i, 0)), pl.BlockSpec((ROW_TILE, kb), lambda i: (i, 0)),
                  pl.BlockSpec((ka + kb, n), lambda i: (0, 0)),
                  pl.BlockSpec((ROW_TILE, n), lambda i: (i, 0)),
                  pl.BlockSpec((1, 1, n), lambda i: (i // tiles_per_mod, 0, 0))],
        out_specs=pl.BlockSpec((ROW_TILE, n), lambda i: (i, 0)),
        out_shape=jax.ShapeDtypeStruct((m, n), F32),
        compiler_params=pltpu.CompilerParams(dimension_semantics=("arbitrary",),
                                             vmem_limit_bytes=VMEM_LIMIT),
        name="res_linear",
    )(ya, yb, w.astype(BF16), res, gate[:, None, :])


def _small_linear_kernel(x_ref, w_ref, b_ref, o_ref):
    o_ref[...] = _dot(x_ref[...], w_ref[...]) + b_ref[...]


def _small_linear(x, w, b):
    m, k = x.shape
    n = w.shape[1]
    tn = 1024 if n % 1024 == 0 else n
    return pl.pallas_call(
        _small_linear_kernel,
        grid=(n // tn,),
        in_specs=[pl.BlockSpec((m, k), lambda j: (0, 0)), pl.BlockSpec((k, tn), lambda j: (0, j)),
                  pl.BlockSpec((1, tn), lambda j: (0, j))],
        out_specs=pl.BlockSpec((m, tn), lambda j: (0, j)),
        out_shape=jax.ShapeDtypeStruct((m, n), F32),
        name="small_linear",
    )(x, w, b[None, :])


def _rec_kernel(*refs, delta, n_heads):
    n_in = 6 if delta else 3
    ins = (refs[:n_in], refs[n_in:2 * n_in])
    rest = refs[2 * n_in:]
    if delta:
        y_refs, s_ref = rest[:2], rest[2]
    else:
        lb_ref, y_refs, s_ref = rest[0], rest[1:3], rest[3]

    @pl.when(pl.program_id(1) == 0)
    def _():
        s_ref[...] = jnp.zeros_like(s_ref)

    c = ins[0][0].shape[0]
    row = lax.broadcasted_iota(jnp.int32, (c, c), 0)
    col = lax.broadcasted_iota(jnp.int32, (c, c), 1)
    mid = c // 2
    heads = range(n_heads)
    hs = lambda t: [t[:, h * HEAD_DIM:(h + 1) * HEAD_DIM] for h in heads]
    mm = functools.partial(_mm, passes=REC_PASSES)

    streams = []
    for d in range(2):
        ahead = row - col if d == 0 else col - row
        incl, strict = ahead >= 0, ahead > 0
        if delta:
            r_ref, v_ref, kap_ref, lw_ref, k_ref, al_ref = ins[d]
            r_all, v_all, lw, k = r_ref[...], v_ref[...], lw_ref[0], k_ref[0]
        else:
            q_ref, i_ref, f_ref = ins[d]
            fg = lb_ref[...] + (1.0 - lb_ref[...]) * jax.nn.sigmoid(f_ref[...])
            r_all, v_all, lw, k = jax.nn.silu(q_ref[...]), i_ref[...], jnp.log(fg), 1.0 - fg
        b = _mm(incl.astype(F32), lw, "nn", 6)
        bm = b[mid:mid + 1, :]
        tot = b[c - 1:c, :] if d == 0 else b[0:1, :]
        e_neg = jnp.exp(bm - b)
        e_end = jnp.exp(tot - b)
        st = dict(incl=incl, strict=strict, v=hs(v_all), rq=hs(r_all * jnp.exp(b - bm)), kd=hs(k * e_neg),
                  k_end=hs(k * e_end), g_mid=hs(jnp.exp(bm)), g_tot=hs(jnp.exp(tot)))
        if delta:
            al = al_ref[0]
            st.update(kq=hs(kap_ref[...] * jnp.exp(b - lw - bm)), ad=hs(al * e_neg), al_end=hs(al * e_end))
        streams.append(st)

    units = [(d, h) for d in range(2) for h in heads]
    n_u = range(len(units))
    per_head = lambda name: [streams[d][name][h] for d, h in units]
    per_dir = lambda name: [streams[d][name] for d, _ in units]
    incl, strict = per_dir("incl"), per_dir("strict")
    v, rq, kd, k_end, g_mid, g_tot = (per_head(n) for n in ("v", "rq", "kd", "k_end", "g_mid", "g_tot"))
    s0 = [s_ref[d, h] for d, h in units]
    s0m = [s0[n] * g_mid[n] for n in n_u]
    if delta:
        kq, ad, al_end = per_head("kq"), per_head("ad"), per_head("al_end")
        q2 = [jnp.concatenate([kq[n], rq[n]], axis=0) for n in n_u]
        k2 = [jnp.concatenate([kd[n], ad[n]], axis=0) for n in n_u]
        a = [mm(q2[n], k2[n], "nt") for n in n_u]
        p = [mm(q2[n], s0m[n], "nt") for n in n_u]
        z = [-(p[n][:c] + mm(jnp.where(strict[n], a[n][:c, :c], 0.0), v[n], "nn")) for n in n_u]
        mc = functools.partial(_mm, passes=REC_CHAIN_PASSES)
        m = [jnp.where(strict[n], a[n][:c, c:], 0.0) for n in n_u]
        pair = (row >> 1) == (col >> 1)
        eye = (row == col).astype(F32)
        t = [eye - jnp.where(pair, m[n], 0.0) for n in n_u]
        for lvl in range(1, int(math.log2(c))):
            off = ((row >> (lvl + 1)) == (col >> (lvl + 1))) & ((row >> lvl) != (col >> lvl))
            tc = [mc(t[n], jnp.where(off, m[n], 0.0), "nn") for n in n_u]
            t = [t[n] - mc(tc[n], t[n], "nn") for n in n_u]
        z = [mc(t[n], z[n], "nn") for n in n_u]
        y = [p[n][c:] + mm(jnp.where(incl[n], a[n][c:, :c], 0.0), v[n], "nn")
             + mm(jnp.where(incl[n], a[n][c:, c:], 0.0), z[n], "nn") for n in n_u]
        s_new = [s0[n] * g_tot[n] + mm(v[n], k_end[n], "tn") + mm(z[n], al_end[n], "tn") for n in n_u]
    else:
        a = [mm(rq[n], kd[n], "nt") for n in n_u]
        y = [mm(rq[n], s0m[n], "nt") + mm(jnp.where(incl[n], a[n], 0.0), v[n], "nn") for n in n_u]
        s_new = [s0[n] * g_tot[n] + mm(v[n], k_end[n], "tn") for n in n_u]
    for n, (d, h) in enumerate(units):
        s_ref[d, h] = s_new[n]
    for d in range(2):
        y_refs[d][...] = jnp.concatenate(y[d * n_heads:(d + 1) * n_heads], axis=-1)


def _seq_block_index(bsz, s, n_ctx, rows):
    n_c, n_l = n_ctx // rows, s // rows

    def index(d, b, i):
        back = jnp.where(i < n_c, n_c - 1 - i, n_l + 2 * n_c - 1 - i)
        pos = jnp.where(d == 0, i, back)
        return jnp.where(pos < n_c, bsz * n_l + b * n_c + pos, b * n_l + pos - n_c)

    return index


def _rec_call(args, specs, bsz, s, n_ctx, width, delta):
    c = REC_CHUNK
    n_heads = width // HEAD_DIM
    blk = _seq_block_index(bsz, s, n_ctx, c)
    out = jax.ShapeDtypeStruct((bsz * (s + n_ctx), width), F32)
    return pl.pallas_call(
        functools.partial(_rec_kernel, delta=delta, n_heads=n_heads),
        grid=(bsz, (s + n_ctx) // c),
        in_specs=specs,
        out_specs=[pl.BlockSpec((c, width), lambda b, i, d=d: (blk(d, b, i), 0)) for d in range(2)],
        out_shape=[out, out],
        scratch_shapes=[pltpu.VMEM((2, n_heads, HEAD_DIM, HEAD_DIM), F32)],
        compiler_params=pltpu.CompilerParams(dimension_semantics=("arbitrary", "arbitrary")),
        name="rwkv7_rec" if delta else "hgrn2_rec",
    )(*args)


def _rwkv7_recurrence(r, v, kap, lw, k, al, bsz, s, n_ctx):
    width = r.shape[1]
    blk = _seq_block_index(bsz, s, n_ctx, REC_CHUNK)
    specs = []
    for d in range(2):
        shared = pl.BlockSpec((REC_CHUNK, width), lambda b, i, d=d: (blk(d, b, i), 0))
        per_dir = pl.BlockSpec((1, REC_CHUNK, width), lambda b, i, d=d: (d, blk(d, b, i), 0))
        specs += [shared] * 3 + [per_dir] * 3
    return _rec_call([r, v, kap, lw, k, al] * 2, specs, bsz, s, n_ctx, width, True)


def _hgrn2_recurrence(u, lb, col_q, col_f, col_i, bsz, s, n_ctx):
    width = lb.shape[0]
    blk = _seq_block_index(bsz, s, n_ctx, REC_CHUNK)
    specs = []
    for d in range(2):
        specs += [pl.BlockSpec((REC_CHUNK, width), lambda b, i, d=d, j=j: (blk(d, b, i), j))
                  for j in (col_q, col_i, col_f + d)]
    specs.append(pl.BlockSpec((1, width), lambda b, i: (0, 0)))
    return _rec_call([u] * 6 + [lb[None, :]], specs, bsz, s, n_ctx, width, False)


def _na_kernel(q_ref, k_ref, v_ref, kc_ref, vc_ref, bias_ref, o_ref, *, rows_per_step, n_rows):
    g = pl.program_id(2)
    band = NA_ROWS * GRID_W
    n_h = q_ref.shape[1]
    units = [(h, j) for j in range(rows_per_step) for h in range(n_h)]
    nt = lambda p, q_: lax.dot_general(p, q_, (((1,), (1,)), ((), ())), preferred_element_type=F32)
    qr = [g * rows_per_step + j for j in range(rows_per_step)]
    start = [jnp.clip(r - NA_ROWS // 2, 0, n_rows - NA_ROWS) for r in qr]
    off = [pl.multiple_of(st * GRID_W, GRID_W) for st in start]
    q = [q_ref[0, h, j * GRID_W:(j + 1) * GRID_W, :] for h, j in units]
    s_win = [nt(q[n], k_ref[0, h, pl.ds(off[j], band), :]) + bias_ref[h, start[j] - qr[j] + NA_ROWS - 1]
             for n, (h, j) in enumerate(units)]
    s_ctx = [nt(q[n], kc_ref[0, h]) for n, (h, j) in enumerate(units)]
    ns = range(len(units))
    m = [jnp.maximum(jnp.max(s_win[n], axis=-1, keepdims=True), jnp.max(s_ctx[n], axis=-1, keepdims=True))
         for n in ns]
    p_win = [jnp.exp(s_win[n] - m[n]) for n in ns]
    p_ctx = [jnp.exp(s_ctx[n] - m[n]) for n in ns]
    den = [jnp.sum(p_win[n], axis=-1, keepdims=True) + jnp.sum(p_ctx[n], axis=-1, keepdims=True) for n in ns]
    o = [(jnp.dot(p_win[n].astype(BF16), v_ref[0, h, pl.ds(off[j], band), :], preferred_element_type=F32)
          + jnp.dot(p_ctx[n].astype(BF16), vc_ref[0, h], preferred_element_type=F32)) / den[n]
         for n, (h, j) in enumerate(units)]
    for j in range(rows_per_step):
        o_ref[j * GRID_W:(j + 1) * GRID_W, :] = jnp.concatenate(o[j * n_h:(j + 1) * n_h], axis=-1)


def _na_bias_table(rpb):
    cc = np.arange(GRID_W)
    col_start = np.clip(cc - NA_COLS // 2, 0, GRID_W - NA_COLS)
    kc = np.arange(GRID_W)
    inside = (kc[None, :] >= col_start[:, None]) & (kc[None, :] < col_start[:, None] + NA_COLS)
    col_off = np.clip(kc[None, :] - cc[:, None] + NA_COLS - 1, 0, 2 * NA_COLS - 2)
    tab = rpb[:, :, col_off]
    tab = jnp.where(jnp.asarray(inside)[None, None], tab, -jnp.inf)
    d0 = np.arange(NA_ROWS)[:, None] + np.arange(NA_ROWS)[None, :]
    band = tab[:, d0]
    return band.transpose(0, 1, 3, 2, 4).reshape(rpb.shape[0], NA_ROWS, GRID_W, NA_ROWS * GRID_W)


NA_HEADS_PER_STEP = 128 // HEAD_DIM


def _neighbourhood_attention(q, k, v, kc, vc, rpb):
    bsz, n_heads, s, dh = q.shape
    n_rows = s // GRID_W
    rows_per_step = 8
    hp = NA_HEADS_PER_STEP
    lc = kc.shape[2]
    bias = _na_bias_table(rpb)
    steps = n_rows // rows_per_step
    full = pl.BlockSpec((1, hp, s, dh), lambda b, h, g: (b, h, 0, 0))
    ctx = pl.BlockSpec((1, hp, lc, dh), lambda b, h, g: (b, h, 0, 0))
    return pl.pallas_call(
        functools.partial(_na_kernel, rows_per_step=rows_per_step, n_rows=n_rows),
        grid=(bsz, n_heads // hp, steps),
        in_specs=[pl.BlockSpec((1, hp, rows_per_step * GRID_W, dh), lambda b, h, g: (b, h, g, 0)),
                  full, full, ctx, ctx,
                  pl.BlockSpec((hp, NA_ROWS, GRID_W, NA_ROWS * GRID_W), lambda b, h, g: (h, 0, 0, 0))],
        out_specs=pl.BlockSpec((rows_per_step * GRID_W, hp * dh), lambda b, h, g: (b * steps + g, h)),
        out_shape=jax.ShapeDtypeStruct((bsz * s, n_heads * dh), F32),
        compiler_params=pltpu.CompilerParams(
            dimension_semantics=("arbitrary", "arbitrary", "arbitrary"), vmem_limit_bytes=VMEM_LIMIT),
        name="neighbourhood_attention",
    )(q, k, v, kc, vc, bias)


def _na_prep_kernel(u_ref, cos_ref, sin_ref, qn_ref, kn_ref, bd_ref, q_ref, k_ref, v_ref, *, n_heads):
    width = n_heads * HEAD_DIM
    bd = bd_ref[...]
    lane = lax.broadcasted_iota(jnp.int32, (u_ref.shape[0], 128), 1)
    first = (lane & (HEAD_DIM // 4)) == 0

    def rope(x):
        parts = []
        for j in range(width // 128):
            cols = slice(128 * j, 128 * (j + 1))
            xj = x[:, cols]
            partner = jnp.where(first, pltpu.roll(xj, 128 - HEAD_DIM // 4, axis=1),
                                pltpu.roll(xj, HEAD_DIM // 4, axis=1))
            parts.append(xj * cos_ref[:, cols] + partner * sin_ref[:, cols])
        return jnp.concatenate(parts, axis=-1)

    def normed(x, w_ref):
        return x * lax.rsqrt(_seg_sum(x * x, bd) * (1.0 / HEAD_DIM) + RMS_EPS) * w_ref[...]

    q = rope(normed(u_ref[:, :width], qn_ref)) * HEAD_DIM ** -0.5
    k = rope(normed(u_ref[:, width:2 * width], kn_ref))
    v = u_ref[:, 2 * width:3 * width]
    for h in range(n_heads):
        cols = slice(h * HEAD_DIM, (h + 1) * HEAD_DIM)
        q_ref[0, h] = q[:, cols].astype(BF16)
        k_ref[0, h] = k[:, cols].astype(BF16)
        v_ref[0, h] = v[:, cols].astype(BF16)


def _rope_tables(s, n_heads):
    t = jnp.arange(s)
    nf = HEAD_DIM // 4
    inv = ROPE_THETA ** (-jnp.arange(nf, dtype=F32) / nf)
    ang_r = (t // GRID_W).astype(F32)[:, None] * inv
    ang_c = (t % GRID_W).astype(F32)[:, None] * inv
    cos = jnp.concatenate([jnp.cos(ang_r)] * 2 + [jnp.cos(ang_c)] * 2, axis=-1)
    sin = jnp.concatenate([-jnp.sin(ang_r), jnp.sin(ang_r), -jnp.sin(ang_c), jnp.sin(ang_c)], axis=-1)
    return jnp.tile(cos, (1, n_heads)), jnp.tile(sin, (1, n_heads))


def _na_prep(u, bsz, s, n_heads, q_norm, k_norm):
    width = n_heads * HEAD_DIM
    tile = ROW_TILE
    per_batch = s // tile
    cos, sin = _rope_tables(s, n_heads)
    tab = pl.BlockSpec((tile, width), lambda i: (i % per_batch, 0))
    vec = pl.BlockSpec((1, width), lambda i: (0, 0))
    out_spec = pl.BlockSpec((1, n_heads, tile, HEAD_DIM), lambda i: (i // per_batch, 0, i % per_batch, 0))
    out = jax.ShapeDtypeStruct((bsz, n_heads, s, HEAD_DIM), BF16)
    return pl.pallas_call(
        functools.partial(_na_prep_kernel, n_heads=n_heads),
        grid=(bsz * per_batch,),
        in_specs=[pl.BlockSpec((tile, 3 * width), lambda i: (i, 0)), tab, tab, vec, vec,
                  pl.BlockSpec((width, width), lambda i: (0, 0))],
        out_specs=[out_spec, out_spec, out_spec],
        out_shape=[out, out, out],
        name="na_prep",
    )(u, cos, sin, jnp.tile(q_norm, n_heads)[None, :], jnp.tile(k_norm, n_heads)[None, :], _head_ones(width))


DEINT_GROUP = 256


def _moe_kernel(be_ref, nb_ref, x_ref, w1_ref, b1_ref, w2_ref, b2_ref, perm_ref, *rest, first_block):
    o_ref, w1_s, w2_s = rest[-3:]
    blk = first_block + pl.program_id(0)
    active = blk < nb_ref[0]
    fresh = jnp.logical_or(pl.program_id(0) == 0, be_ref[blk] != be_ref[jnp.maximum(blk - 1, 0)])

    @pl.when(jnp.logical_and(active, fresh))
    def _():
        half = w1_s.shape[1] // 2
        g_out = DEINT_GROUP // 2
        for g in range(w1_s.shape[1] // DEINT_GROUP):
            t = jnp.dot(w1_ref[0, 0, :, g * DEINT_GROUP:(g + 1) * DEINT_GROUP].astype(BF16), perm_ref[...],
                        preferred_element_type=F32).astype(BF16)
            w1_s[:, g * g_out:(g + 1) * g_out] = t[:, :g_out]
            w1_s[:, half + g * g_out:half + (g + 1) * g_out] = t[:, g_out:]
        w2_s[...] = w2_ref[0, 0].astype(BF16)

    @pl.when(active)
    def _():
        y = jnp.dot(x_ref[...], w1_s[...], preferred_element_type=F32) + b1_ref[0]
        half = y.shape[1] // 2
        glu = jnp.minimum(y[:, :half], SWIGLU_LIMIT)
        lin = jnp.clip(y[:, half:], -SWIGLU_LIMIT, SWIGLU_LIMIT)
        act = glu * jax.nn.sigmoid(SWIGLU_ALPHA * glu) * (lin + 1.0)
        o_ref[...] = jnp.dot(act.astype(BF16), w2_s[...], preferred_element_type=F32) + b2_ref[0]

    @pl.when(jnp.logical_not(active))
    def _():
        o_ref[...] = jnp.zeros_like(o_ref)


def _moe_experts(x_part, first_block, n_blocks_total, y_prev, blk_e, n_used, w1_layers, w2_layers, layer, b1, b2):
    d = x_part.shape[1]
    n_blocks = x_part.shape[0] // MOE_BLOCK
    n_slots = n_blocks_total * MOE_BLOCK
    off = first_block
    de2 = w1_layers.shape[3]
    perm = np.zeros((DEINT_GROUP, DEINT_GROUP), np.float32)
    idx = np.arange(DEINT_GROUP // 2)
    perm[2 * idx, idx] = 1.0
    perm[2 * idx + 1, DEINT_GROUP // 2 + idx] = 1.0
    grid_spec = pltpu.PrefetchScalarGridSpec(
        num_scalar_prefetch=2,
        grid=(n_blocks,),
        in_specs=[pl.BlockSpec((MOE_BLOCK, d), lambda i, be, nb: (i, 0)),
                  pl.BlockSpec((1, 1, d, de2), lambda i, be, nb: (layer, be[off + i], 0, 0)),
                  pl.BlockSpec((1, 1, de2), lambda i, be, nb: (be[off + i], 0, 0)),
                  pl.BlockSpec((1, 1, de2 // 2, d), lambda i, be, nb: (layer, be[off + i], 0, 0)),
                  pl.BlockSpec((1, 1, d), lambda i, be, nb: (be[off + i], 0, 0)),
                  pl.BlockSpec((DEINT_GROUP, DEINT_GROUP), lambda i, be, nb: (0, 0))]
        + ([] if y_prev is None else [pl.BlockSpec(memory_space=pl.ANY)]),
        out_specs=pl.BlockSpec((MOE_BLOCK, d), lambda i, be, nb: (off + i, 0)),
        scratch_shapes=[pltpu.VMEM((d, de2), BF16), pltpu.VMEM((de2 // 2, d), BF16)],
    )
    args = [blk_e, n_used, x_part, w1_layers, b1[:, None, :], w2_layers, b2[:, None, :], jnp.asarray(perm, BF16)]
    return pl.pallas_call(
        functools.partial(_moe_kernel, first_block=first_block),
        grid_spec=grid_spec,
        out_shape=jax.ShapeDtypeStruct((n_slots, d), F32),
        input_output_aliases={} if y_prev is None else {len(args): 0},
        compiler_params=pltpu.CompilerParams(dimension_semantics=("arbitrary",),
                                             vmem_limit_bytes=VMEM_LIMIT),
        name="moe_experts",
    )(*args, *([] if y_prev is None else [y_prev]))


ROUTE_TILE = 1024


def _route_kernel(x_ref, sh_ref, sc_ref, wt_ref, b_ref, before_ref, h_ref, e_ref, g_ref, r_ref, cnt_ref,
                  carry_ref):
    i = pl.program_id(0)

    @pl.when(i == 0)
    def _():
        carry_ref[...] = jnp.zeros_like(carry_ref)

    x = x_ref[...]
    ms = jnp.mean(x * x, axis=-1, keepdims=True)
    h = x * lax.rsqrt(ms + RMS_EPS) * (1.0 + sc_ref[0]) + sh_ref[0]
    h_ref[...] = h.astype(BF16)
    logits = _mm(wt_ref[...], h, "nt", 6) + b_ref[...]
    n_e, tm = logits.shape
    eidx = lax.broadcasted_iota(jnp.int32, (n_e, tm), 0)
    work = logits
    top_v, top_e = [], []
    for _ in range(TOP_K):
        m = jnp.max(work, axis=0, keepdims=True)
        sel = jnp.min(jnp.where(work == m, eidx, n_e), axis=0, keepdims=True)
        top_v.append(m)
        top_e.append(sel)
        work = jnp.where(eidx == sel, -jnp.inf, work)
    ex = [jnp.exp(v - top_v[0]) for v in top_v]
    den = ex[0] + ex[1] + ex[2] + ex[3]
    g_ref[...] = jnp.concatenate([e_ / den for e_ in ex], axis=0)
    e_ref[...] = jnp.concatenate(top_e, axis=0)
    chosen = [eidx == sel for sel in top_e]
    ind = sum(c.astype(F32) for c in chosen)
    carry = carry_ref[...]
    cnt = (jnp.dot(ind.astype(BF16), before_ref[...], preferred_element_type=F32)
           + jnp.concatenate([carry] * (tm // carry.shape[1]), axis=1))
    r_ref[...] = jnp.concatenate(
        [jnp.sum(jnp.where(c, cnt, 0.0), axis=0, keepdims=True) for c in chosen], axis=0).astype(jnp.int32)
    carry = carry + jnp.sum(ind, axis=1, keepdims=True)
    carry_ref[...] = carry
    cnt_ref[...] = carry


def _route(tok, shift, scale, router_w, router_b, rows_per_mod):
    t, d = tok.shape
    n_e = router_w.shape[1]
    tm = ROUTE_TILE
    tiles_per_mod = rows_per_mod // tm
    mod_spec = pl.BlockSpec((1, 1, d), lambda i: (i // tiles_per_mod, 0, 0))
    before = jnp.asarray(np.triu(np.ones((tm, tm), np.float32), 1), BF16)
    kt_spec = pl.BlockSpec((TOP_K, tm), lambda i: (0, i))
    h, top_e, gate, rank, cnt = pl.pallas_call(
        _route_kernel,
        grid=(t // tm,),
        in_specs=[pl.BlockSpec((tm, d), lambda i: (i, 0)), mod_spec, mod_spec,
                  pl.BlockSpec((n_e, d), lambda i: (0, 0)), pl.BlockSpec((n_e, tm), lambda i: (0, 0)),
                  pl.BlockSpec((tm, tm), lambda i: (0, 0))],
        out_specs=[pl.BlockSpec((tm, d), lambda i: (i, 0)), kt_spec, kt_spec, kt_spec,
                   pl.BlockSpec((n_e, 128), lambda i: (0, 0))],
        out_shape=[jax.ShapeDtypeStruct((t, d), BF16), jax.ShapeDtypeStruct((TOP_K, t), jnp.int32),
                   jax.ShapeDtypeStruct((TOP_K, t), F32), jax.ShapeDtypeStruct((TOP_K, t), jnp.int32),
                   jax.ShapeDtypeStruct((n_e, 128), F32)],
        scratch_shapes=[pltpu.VMEM((n_e, 128), F32)],
        compiler_params=pltpu.CompilerParams(dimension_semantics=("arbitrary",),
                                             vmem_limit_bytes=VMEM_LIMIT),
        name="moe_route",
    )(tok, shift[:, None, :], scale[:, None, :], router_w.T, jnp.broadcast_to(router_b[:, None], (n_e, tm)),
      before)
    return h, top_e, gate, rank, cnt[:, 0].astype(jnp.int32)


def _moe(tok, shift, scale, rows_per_mod, router_w, router_b, w1_layers, w2_layers, layer, b1, b2):
    t, d = tok.shape
    h, top_e, gate, rank, counts = _route(tok, shift, scale, router_w, router_b, rows_per_mod)
    n_assign = t * TOP_K
    n_blocks = -(-n_assign // MOE_BLOCK) + N_EXPERTS
    n_slots = n_blocks * MOE_BLOCK
    padded = (counts + MOE_BLOCK - 1) // MOE_BLOCK * MOE_BLOCK
    start = jnp.cumsum(counts) - counts
    p_end = jnp.cumsum(padded)
    p_start = p_end - padded
    experts = jnp.arange(N_EXPERTS, dtype=jnp.int32)
    dest = jnp.sum(jnp.where(top_e[:, :, None] == experts, p_start, 0), axis=-1) + rank
    blk_first = jnp.arange(n_blocks, dtype=jnp.int32) * MOE_BLOCK
    blk_e = jnp.minimum(jnp.sum(p_end[None, :] <= blk_first[:, None], axis=1), N_EXPERTS - 1).astype(jnp.int32)
    n_used = (p_end[-1] // MOE_BLOCK).astype(jnp.int32).reshape(1)
    order = jnp.argsort(top_e.T.reshape(-1))
    per_slot = lambda per_expert: jnp.repeat(per_expert[blk_e], MOE_BLOCK)
    j = jnp.arange(n_slots, dtype=jnp.int32) - per_slot(p_start)
    src = jnp.clip(per_slot(start) + j, 0, n_assign - 1)
    slot_tok = jnp.where(j < per_slot(counts), order[src] // TOP_K, t).astype(jnp.int32)
    h_pad = jnp.concatenate([h, jnp.zeros((1, d), BF16)], axis=0)
    b1p = jnp.concatenate([b1[..., ::2], b1[..., 1::2]], axis=-1)
    y = None
    for first in range(0, n_blocks, n_blocks // MOE_CALLS):
        sl = slice(first * MOE_BLOCK, (first + n_blocks // MOE_CALLS) * MOE_BLOCK)
        y = _moe_experts(h_pad[slot_tok[sl]], first, n_blocks, y, blk_e, n_used, w1_layers, w2_layers, layer,
                         b1p, b2)
    return y, dest, gate


def _rms(x):
    return x * lax.rsqrt(jnp.mean(x * x, axis=-1, keepdims=True) + RMS_EPS)


def _seg_sum(x, ones_bd):
    x_hi, x_lo = _hi_lo(x)
    return (jnp.dot(x_hi, ones_bd, preferred_element_type=F32)
            + jnp.dot(x_lo, ones_bd, preferred_element_type=F32))


def _head_ones(width):
    h = np.arange(width) // HEAD_DIM
    return jnp.asarray(h[:, None] == h[None, :], BF16)


def _shift_conv(i, first_ref, last_ref, u_ref, up_ref, un_ref, sw_ref):
    keep_prev = (first_ref[i] == 0).astype(F32)
    keep_next = (last_ref[i] == 0).astype(F32)
    rows = u_ref.shape[0]

    def conv(lo, width, tap_lo):
        x = u_ref[:, lo:lo + width]
        ridx = lax.broadcasted_iota(jnp.int32, x.shape, 0)
        x_prev = jnp.where(ridx == 0, up_ref[7:8, lo:lo + width] * keep_prev, pltpu.roll(x, 1, axis=0))
        x_next = jnp.where(ridx == rows - 1, un_ref[0:1, lo:lo + width] * keep_next,
                           pltpu.roll(x, rows - 1, axis=0))
        w = sw_ref[:, tap_lo:tap_lo + width]
        return x_prev * w[0:1] + x * w[1:2] + x_next * w[2:3]

    return conv


def _rwkv7_prep_kernel(first_ref, last_ref, u_ref, up_ref, un_ref, sw_ref, w0_ref, w2_ref, a0_ref, a2_ref,
                       g2_ref, kk_ref, ka_ref, bd_ref, r_ref, v_ref, kap_ref, g_ref, lw_ref, k_ref, al_ref,
                       *, d_a, lora_off):
    conv = _shift_conv(pl.program_id(0), first_ref, last_ref, u_ref, up_ref, un_ref, sw_ref)
    r = conv(0, d_a, 0)
    k = conv(d_a, d_a, d_a)
    v = conv(2 * d_a, d_a, 2 * d_a)
    wd = conv(lora_off, 2 * DECAY_LORA, 3 * d_a)
    ad = conv(lora_off + 2 * DECAY_LORA, 2 * AAA_LORA, 3 * d_a + 2 * DECAY_LORA)
    gd = conv(lora_off + 2 * DECAY_LORA + 2 * AAA_LORA, GATE_LORA, 3 * d_a + 2 * DECAY_LORA + 2 * AAA_LORA)
    mm3 = lambda p, q: _mm(p, q, "nn", 3)
    kk = k * kk_ref[...]
    kap = kk / jnp.maximum(jnp.sqrt(_seg_sum(kk * kk, bd_ref[...])), 1e-12)
    lora_w = jnp.tanh(wd)
    for d in range(2):
        z = w0_ref[d:d + 1] + mm3(lora_w[:, d * DECAY_LORA:(d + 1) * DECAY_LORA], w2_ref[d])
        lw_ref[d] = -math.exp(-0.5) * jax.nn.sigmoid(z)
        a = jax.nn.sigmoid(a0_ref[d:d + 1] + mm3(ad[:, d * AAA_LORA:(d + 1) * AAA_LORA], a2_ref[d]))
        k_ref[d] = k * (1.0 + (a - 1.0) * ka_ref[...])
        al_ref[d] = kap * a
    g_ref[...] = mm3(jax.nn.sigmoid(gd), g2_ref[...])
    r_ref[...] = r
    v_ref[...] = v
    kap_ref[...] = kap


def _segment_flags(bsz, s, n_ctx, tile):
    lat, ctx = np.arange(bsz * s // tile), np.arange(bsz * n_ctx // tile)
    first = np.concatenate([lat % (s // tile) == 0, ctx % (n_ctx // tile) == 0])
    last = np.concatenate([lat % (s // tile) == s // tile - 1, ctx % (n_ctx // tile) == n_ctx // tile - 1])
    return jnp.asarray(first, jnp.int32), jnp.asarray(last, jnp.int32)


def _rwkv7_prep(u, lora_off, bsz, s, n_ctx, shift_w, w0, w2, a0, a2, g2, k_k, k_a):
    t, cols = u.shape
    d_a = w0.shape[-1]
    tile = ROW_TILE
    halo = 8
    first, last = _segment_flags(bsz, s, n_ctx, tile)
    full = lambda arr: pl.BlockSpec(arr.shape, lambda i, f, l: (0,) * arr.ndim)
    row = lambda arr: arr[None, :]
    params = [shift_w, w0, w2, a0, a2, g2, row(k_k), row(k_a), _head_ones(d_a)]
    one = pl.BlockSpec((tile, d_a), lambda i, f, l: (i, 0))
    two = pl.BlockSpec((2, tile, d_a), lambda i, f, l: (0, i, 0))
    flat = jax.ShapeDtypeStruct((t, d_a), F32)
    both = jax.ShapeDtypeStruct((2, t, d_a), F32)
    grid_spec = pltpu.PrefetchScalarGridSpec(
        num_scalar_prefetch=2,
        grid=(t // tile,),
        in_specs=[pl.BlockSpec((tile, cols), lambda i, f, l: (i, 0)),
                  pl.BlockSpec((halo, cols), lambda i, f, l: (jnp.maximum(i * (tile // halo) - 1, 0), 0)),
                  pl.BlockSpec((halo, cols),
                               lambda i, f, l: (jnp.minimum((i + 1) * (tile // halo), t // halo - 1), 0))]
        + [full(p) for p in params],
        out_specs=[one, one, one, one, two, two, two],
    )
    return pl.pallas_call(
        functools.partial(_rwkv7_prep_kernel, d_a=d_a, lora_off=lora_off),
        grid_spec=grid_spec,
        out_shape=[flat, flat, flat, flat, both, both, both],
        compiler_params=pltpu.CompilerParams(dimension_semantics=("arbitrary",),
                                             vmem_limit_bytes=VMEM_LIMIT),
        name="rwkv7_prep",
    )(first, last, u, u, u, *params)


def _even_out_kernel(yaf_ref, yab_ref, r_ref, v_ref, k_ref, g_ref, ybf_ref, ybb_ref, og_ref, rk_ref, lnw_ref,
                     lnb_ref, nw_ref, bd_ref, w_ref, res_ref, gate_ref, o_ref):
    bd = bd_ref[...]
    inv = 1.0 / HEAD_DIM
    y = yaf_ref[...] + yab_ref[...]
    yc = y - _seg_sum(y, bd) * inv
    var = _seg_sum(yc * yc, bd) * inv
    yn = yc * lax.rsqrt(var + LN_X_EPS) * lnw_ref[...] + lnb_ref[...]
    bonus = _seg_sum(r_ref[...] * (k_ref[0] + k_ref[1]) * rk_ref[...], bd) * v_ref[...]
    out_a = (yn + bonus) * g_ref[...]
    o = ybf_ref[...] + ybb_ref[...]
    out_b = o * lax.rsqrt(_seg_sum(o * o, bd) * inv + RMS_EPS) * nw_ref[...] * jax.nn.silu(og_ref[...])
    ycat = jnp.concatenate([out_a, out_b], axis=-1).astype(BF16)
    o_ref[...] = res_ref[...] + gate_ref[0] * jnp.dot(ycat, w_ref[...], preferred_element_type=F32)


def _even_out(ya, r, v, k, g, yb, u, col_og, r_k, ln_w, ln_b, norm_w, w_out, res, gate, rows_per_mod):
    t, width = r.shape
    d = res.shape[1]
    tile = ROW_TILE
    tiles_per_mod = rows_per_mod // tile
    one = pl.BlockSpec((tile, width), lambda i: (i, 0))
    two = pl.BlockSpec((2, tile, width), lambda i: (0, i, 0))
    vec = pl.BlockSpec((1, width), lambda i: (0, 0))
    n_heads = width // HEAD_DIM
    return pl.pallas_call(
        _even_out_kernel,
        grid=(t // tile,),
        in_specs=[one, one, one, one, two, one, one, one, pl.BlockSpec((tile, width), lambda i: (i, col_og)),
                  vec, vec, vec, vec, pl.BlockSpec((width, width), lambda i: (0, 0)),
                  pl.BlockSpec(w_out.shape, lambda i: (0, 0)), pl.BlockSpec((tile, d), lambda i: (i, 0)),
                  pl.BlockSpec((1, 1, d), lambda i: (i // tiles_per_mod, 0, 0))],
        out_specs=pl.BlockSpec((tile, d), lambda i: (i, 0)),
        out_shape=jax.ShapeDtypeStruct((t, d), F32),
        compiler_params=pltpu.CompilerParams(dimension_semantics=("arbitrary",),
                                             vmem_limit_bytes=VMEM_LIMIT),
        name="even_readout_out_proj",
    )(*ya, r, v, k, g, *yb, u, r_k.reshape(1, width), ln_w[None, :], ln_b[None, :],
      jnp.tile(norm_w, n_heads)[None, :], _head_ones(width), w_out.astype(BF16), res, gate[:, None, :])


def _even_layer(tok, sh1, sc1, g1, bsz, s, n_ctx, w_in, w_out, shift_w, w0, w2, a0, a2, g2, k_k, k_a, r_k, ln_w,
                ln_b, lb, norm_w):
    d_a = w0.shape[-1]
    d_b = lb.shape[0]
    rkv = 3 * d_a
    a_cols = rkv + 2 * DECAY_LORA + 2 * AAA_LORA + GATE_LORA
    w_perm = jnp.concatenate([w_in[:, :rkv], w_in[:, a_cols:], w_in[:, rkv:a_cols]], axis=1)
    u = _mod_linear(tok, sh1, sc1, w_perm, s)
    r, v, kap, g, lw, k, al = _rwkv7_prep(u, rkv + 5 * d_b, bsz, s, n_ctx, shift_w, w0, w2, a0, a2, g2, k_k, k_a)
    ya = _rwkv7_recurrence(r, v, kap, lw, k, al, bsz, s, n_ctx)
    col0 = rkv // d_b
    yb = _hgrn2_recurrence(u, lb, col0, col0 + 1, col0 + 3, bsz, s, n_ctx)
    return _even_out(ya, r, v, k, g, yb, u, col0 + 4, r_k, ln_w, ln_b, norm_w, w_out, tok, g1, s)


def _hyena_filters(length, d_d, w1, b1, fr1, w2, b2, fr2, w3):
    t = jnp.linspace(0.0, 1.0, length, dtype=F32)[:, None]
    bands = (HYENA_EMB - 1) // 2
    f = jnp.linspace(1e-4, bands - 1, bands, dtype=F32)
    ang = (2.0 * math.pi / length) * jnp.arange(length, dtype=F32)[:, None] * f
    z = jnp.concatenate([t, jnp.cos(ang), -jnp.sin(ang)], axis=-1)
    hid = jnp.sin(fr1 * (jnp.dot(z, w1, precision=_HI) + b1))
    hid = jnp.sin(fr2 * (jnp.dot(hid, w2, precision=_HI) + b2))
    h = jnp.dot(hid, w3, precision=_HI).reshape(length, HYENA_ORDER, 2, d_d)
    deltas = jnp.abs(jnp.linspace(math.log(HYENA_TARGET) / HYENA_SLOW_DECAY,
                                  math.log(HYENA_TARGET) / HYENA_FAST_DECAY, d_d, dtype=F32))
    return h * jnp.exp(-t * deltas)[:, None, None, :]


DFT_R = 128
DFT_COL_TILE = 4096
DFT_K1_TILE = 4


def _hi_lo(a):
    hi = a.astype(BF16)
    return hi, (a - hi.astype(F32)).astype(BF16)


def _mm3(f_hi, f_lo, x_hi, x_lo):
    d = lambda p, q: jnp.dot(p, q, preferred_element_type=F32)
    return d(f_hi, x_hi) + (d(f_hi, x_lo) + d(f_lo, x_hi))


def _dft_matrices(n1):
    k1 = np.arange(n1, dtype=np.float64)
    k2 = np.arange(DFT_R, dtype=np.float64)
    as_f32 = lambda a: jnp.asarray(a, F32)
    cs = lambda ang: (as_f32(np.cos(ang)), as_f32(-np.sin(ang)))
    f1 = cs(2.0 * np.pi * np.outer(k1, k1) / n1)
    f2 = cs(2.0 * np.pi * np.outer(k2, k2) / DFT_R)
    tw = cs(2.0 * np.pi * np.outer(k1, k2) / (n1 * DFT_R))
    tw = tuple(jnp.broadcast_to(t[:, :, None], (n1, DFT_R, 128)) for t in tw)
    return f1, f2, tw


def _dft_rows_kernel(x_ref, frh_ref, frl_ref, fih_ref, fil_ref, or_ref, oi_ref):
    frh, frl, fih, fil = frh_ref[...], frl_ref[...], fih_ref[...], fil_ref[...]
    x0 = _hi_lo(x_ref[0])
    if x_ref.shape[0] == 1:
        or_ref[0] = _mm3(frh, frl, *x0)
        oi_ref[0] = _mm3(fih, fil, *x0)
    else:
        x1 = _hi_lo(x_ref[1])
        or_ref[0] = _mm3(frh, frl, *x0) - _mm3(fih, fil, *x1)
        oi_ref[0] = _mm3(frh, frl, *x1) + _mm3(fih, fil, *x0)


def _dft_rows(x2d, f1, pack):
    bsz, rows, cols = x2d.shape
    n1 = f1[0].shape[0]
    mats = [m for f in f1 for m in _hi_lo(f[:, :rows])]
    mat_spec = pl.BlockSpec((n1, rows), lambda b, j: (0, 0))
    out_spec = pl.BlockSpec((1, n1, DFT_COL_TILE), lambda b, j: (b, 0, j))
    out = jax.ShapeDtypeStruct((bsz // pack, n1, cols), F32)
    return pl.pallas_call(
        _dft_rows_kernel,
        grid=(bsz // pack, cols // DFT_COL_TILE),
        in_specs=[pl.BlockSpec((pack, rows, DFT_COL_TILE), lambda b, j: (b, 0, j))] + [mat_spec] * 4,
        out_specs=[out_spec, out_spec],
        out_shape=[out, out],
        name="dft_rows",
    )(x2d, *mats)


def _dft_mid_kernel(*refs, conv):
    if conv:
        (ar_ref, ai_ref, tr_ref, ti_ref, kr_ref, ki_ref, frh_ref, frl_ref, fih_ref, fil_ref,
         or_ref, oi_ref) = refs
    else:
        ar_ref, ai_ref, tr_ref, ti_ref, frh_ref, frl_ref, fih_ref, fil_ref, or_ref, oi_ref = refs
    frh, frl, fih, fil = frh_ref[...], frl_ref[...], fih_ref[...], fil_ref[...]
    reps = ar_ref.shape[3] // tr_ref.shape[2]
    rows = range(ar_ref.shape[1])
    wide = lambda t: jnp.concatenate([t] * reps, axis=-1)
    tr = [wide(tr_ref[j]) for j in rows]
    ti = [wide(ti_ref[j]) for j in rows]
    ar = [ar_ref[0, j] for j in rows]
    ai = [ai_ref[0, j] for j in rows]
    pr = [_hi_lo(ar[j] * tr[j] - ai[j] * ti[j]) for j in rows]
    pi = [_hi_lo(ar[j] * ti[j] + ai[j] * tr[j]) for j in rows]
    xr = [_mm3(frh, frl, *pr[j]) - _mm3(fih, fil, *pi[j]) for j in rows]
    xi = [_mm3(frh, frl, *pi[j]) + _mm3(fih, fil, *pr[j]) for j in rows]
    if not conv:
        for j in rows:
            or_ref[0, j] = xr[j]
            oi_ref[0, j] = xi[j]
        return
    yr = [_hi_lo(xr[j] * kr_ref[0, j] - xi[j] * ki_ref[0, j]) for j in rows]
    yi = [_hi_lo(xr[j] * ki_ref[0, j] + xi[j] * kr_ref[0, j]) for j in rows]
    br = [_mm3(frh, frl, *yr[j]) + _mm3(fih, fil, *yi[j]) for j in rows]
    bi = [_mm3(frh, frl, *yi[j]) - _mm3(fih, fil, *yr[j]) for j in rows]
    for j in rows:
        or_ref[0, j] = br[j] * tr[j] + bi[j] * ti[j]
        oi_ref[0, j] = bi[j] * tr[j] - br[j] * ti[j]


def _dft_mid(a_r, a_i, f2, tw, kf=None):
    (f_r, f_i), (t_r, t_i) = f2, tw
    bsz, n1, _, ch = a_r.shape
    slab = pl.BlockSpec((1, DFT_K1_TILE, DFT_R, ch), lambda b, g: (b, g, 0, 0))
    tw = pl.BlockSpec((DFT_K1_TILE, DFT_R, t_r.shape[2]), lambda b, g: (g, 0, 0))
    mat = pl.BlockSpec((DFT_R, DFT_R), lambda b, g: (0, 0))
    args = [a_r, a_i, t_r, t_i]
    specs = [slab, slab, tw, tw]
    if kf is not None:
        kf_r, kf_i, o = kf
        kspec = pl.BlockSpec((1, DFT_K1_TILE, DFT_R, ch), lambda b, g: (o, g, 0, 0))
        args += [kf_r, kf_i]
        specs += [kspec, kspec]
    args += [m for f in (f_r, f_i) for m in _hi_lo(f)]
    specs += [mat] * 4
    out = jax.ShapeDtypeStruct(a_r.shape, F32)
    return pl.pallas_call(
        functools.partial(_dft_mid_kernel, conv=kf is not None),
        grid=(bsz, n1 // DFT_K1_TILE),
        in_specs=specs,
        out_specs=[slab, slab],
        out_shape=[out, out],
        compiler_params=pltpu.CompilerParams(dimension_semantics=("arbitrary", "arbitrary"),
                                             vmem_limit_bytes=VMEM_LIMIT),
        name="dft_mid_conv" if kf is not None else "dft_mid",
    )(*args)


def _idft_rows_kernel(br_ref, bi_ref, z_ref, g_ref, bias_ref, grh_ref, grl_ref, gih_ref, gil_ref, o_ref):
    grh, grl, gih, gil = grh_ref[...], grl_ref[...], gih_ref[...], gil_ref[...]
    b_r, b_i = _hi_lo(br_ref[0]), _hi_lo(bi_ref[0])
    y = (_mm3(grh, grl, *b_r) + _mm3(gih, gil, *b_i), _mm3(grh, grl, *b_i) - _mm3(gih, gil, *b_r))
    for e in range(2):
        o_ref[e] = g_ref[e] * (y[e] + z_ref[e] * bias_ref[...])


def _idft_rows(b_r, b_i, z2d, gate2d, bias_cols, f1):
    bsz, rows, cols = z2d.shape
    n1 = f1[0].shape[0]
    scale = 1.0 / (n1 * DFT_R)
    mats = [m for f in f1 for m in _hi_lo(f[:rows] * scale)]
    mat_spec = pl.BlockSpec((rows, n1), lambda b, j: (0, 0))
    in_spec = pl.BlockSpec((1, n1, DFT_COL_TILE), lambda b, j: (b, 0, j))
    io_spec = pl.BlockSpec((2, rows, DFT_COL_TILE), lambda b, j: (b, 0, j))
    return pl.pallas_call(
        _idft_rows_kernel,
        grid=(bsz // 2, cols // DFT_COL_TILE),
        in_specs=[in_spec, in_spec, io_spec, io_spec, pl.BlockSpec((1, DFT_COL_TILE), lambda b, j: (0, j))]
        + [mat_spec] * 4,
        out_specs=io_spec,
        out_shape=jax.ShapeDtypeStruct((bsz, rows, cols), F32),
        name="idft_rows",
    )(b_r, b_i, z2d, gate2d, bias_cols, *mats)


def _hyena_prep_kernel(first_ref, last_ref, u_ref, up_ref, un_ref, sw_ref, v_ref, x1_ref, x2_ref):
    conv = _shift_conv(pl.program_id(0), first_ref, last_ref, u_ref, up_ref, un_ref, sw_ref)
    ch = v_ref.shape[1]
    for n, o_ref in enumerate((v_ref, x1_ref, x2_ref)):
        o_ref[...] = conv(n * ch, ch, n * ch)


def _hyena_prep(u, col_block, bsz, s, short_w):
    width = short_w.shape[1]
    ch = width // 3
    tile = ROW_TILE
    halo = 8
    n_tiles = bsz * s // tile
    in_seq = np.arange(n_tiles) % (s // tile)
    first = jnp.asarray(in_seq == 0, jnp.int32)
    last = jnp.asarray(in_seq == s // tile - 1, jnp.int32)
    out = jax.ShapeDtypeStruct((bsz * s, ch), F32)
    out_spec = pl.BlockSpec((tile, ch), lambda i, f, l: (i, 0))
    grid_spec = pltpu.PrefetchScalarGridSpec(
        num_scalar_prefetch=2,
        grid=(n_tiles,),
        in_specs=[pl.BlockSpec((tile, width), lambda i, f, l: (i, col_block)),
                  pl.BlockSpec((halo, width), lambda i, f, l: (jnp.maximum(i * (tile // halo) - 1, 0), col_block)),
                  pl.BlockSpec((halo, width), lambda i, f, l: ((i + 1) * (tile // halo), col_block)),
                  pl.BlockSpec(short_w.shape, lambda i, f, l: (0, 0))],
        out_specs=[out_spec, out_spec, out_spec],
    )
    return pl.pallas_call(
        _hyena_prep_kernel,
        grid_spec=grid_spec,
        out_shape=[out, out, out],
        name="hyena_prep",
    )(first, last, u, u, u, short_w)


def _hyena(v, x1, x2, bsz, length, w1, b1, fr1, w2, b2, fr2, w3, bias):
    ch = v.shape[-1]
    rows = length // DFT_R
    h = _hyena_filters(length, ch, w1, b1, fr1, w2, b2, fr2, w3)
    n1 = 2 * rows
    f1, f2, tw = _dft_matrices(n1)
    four_d = lambda t: t.reshape(t.shape[0], n1, DFT_R, ch)
    hh = h.transpose(1, 2, 0, 3).reshape(HYENA_ORDER * 2, rows, DFT_R * ch)
    hf_r, hf_i = (t.reshape(HYENA_ORDER, 2, n1, DFT_R, ch)
                  for t in _dft_mid(*(four_d(t) for t in _dft_rows(hh, f1, 1)), f2, tw))
    kf_r = hf_r[:, 0] + hf_r[:, 1]
    kf_i = hf_i[:, 0] - hf_i[:, 1]
    assert bsz % 2 == 0, "batch entries are convolved in pairs (one complex sequence per pair)"
    z = v
    for o, gate in enumerate((x1, x2)):
        z2d = z.reshape(bsz, rows, DFT_R * ch)
        a_r, a_i = _dft_rows(z2d, f1, 2)
        b_r, b_i = _dft_mid(four_d(a_r), four_d(a_i), f2, tw, kf=(kf_r, kf_i, o))
        flat = lambda t: t.reshape(bsz // 2, n1, DFT_R * ch)
        y = _idft_rows(flat(b_r), flat(b_i), z2d, gate.reshape(z2d.shape), jnp.tile(bias[o], DFT_R)[None, :], f1)
        z = y
    return z.reshape(bsz * length, ch)


def _odd_layer(tok, sh1, sc1, g1, bsz, s, n_ctx, w_in, w_out, q_norm, k_norm, rpb, short_w, w1, b1, fr1, w2,
               b2, fr2, w3, bias):
    n_heads = rpb.shape[0]
    d_c = n_heads * HEAD_DIM
    n_lat = bsz * s
    u = _mod_linear(tok, sh1, sc1, w_in, s)
    q, k, v = _na_prep(u, bsz, s, n_heads, q_norm, k_norm)
    u_ctx = u[n_lat:, d_c:3 * d_c].reshape(bsz, n_ctx, 2, n_heads, HEAD_DIM)
    head_major = lambda t: t.transpose(0, 2, 1, 3).astype(BF16)
    kc = head_major(_rms(u_ctx[:, :, 0]) * k_norm)
    vc = head_major(u_ctx[:, :, 1])
    y_na = _neighbourhood_attention(q, k, v, kc, vc, rpb)
    assert (3 * d_c) % short_w.shape[1] == 0
    y_hy = _hyena(*_hyena_prep(u, 3 * d_c // short_w.shape[1], bsz, s, short_w), bsz, s, w1, b1, fr1, w2, b2,
                  fr2, w3, bias)
    return _res_linear2(y_na, y_hy, w_out, tok, g1, s)


def kernel(x, c, ctx, c_ctx, mod_w, mod_b, router_w, router_b, moe_w1, moe_b1, moe_w2, moe_b2, ab_w_in, ab_w_out, rwkv_shift, rwkv_w0, rwkv_w2, rwkv_a0, rwkv_a2, rwkv_g2, rwkv_k_k, rwkv_k_a, rwkv_r_k, rwkv_ln_w, rwkv_ln_b, hgrn_lb_logits, hgrn_norm_w, cd_w_in, cd_w_out, na_q_norm, na_k_norm, na_rpb, hy_short, hy_w1, hy_b1, hy_freq1, hy_w2, hy_b2, hy_freq2, hy_w3, hy_bias):
    bsz, s, d = x.shape
    n_ctx = ctx.shape[1]
    depth = mod_w.shape[0]
    n_lat = bsz * s
    lb_all = jnp.cumsum(jax.nn.softmax(hgrn_lb_logits, axis=0), axis=0)
    tok = jnp.concatenate([x.reshape(n_lat, d), ctx.reshape(bsz * n_ctx, d)], axis=0)
    cond = jnp.concatenate([jax.nn.silu(c), jax.nn.silu(c_ctx)[None, :],
                            jnp.zeros((8 - bsz - 1, d), F32)], axis=0)
    assert depth == 2, "layer 0 = RWKV-7 || HGRN2 with context outputs, layer 1 = attention || Hyena, latent only"
    for l in range(depth):
        j = l // 2
        mod = _small_linear(cond, mod_w[l], mod_b[l])[:bsz + 1]
        sh1, sc1, g1, sh2, sc2, g2 = jnp.split(mod, 6, axis=-1)
        if l % 2 == 0:
            tok = _even_layer(tok, sh1, sc1, g1, bsz, s, n_ctx, ab_w_in[j], ab_w_out[j], rwkv_shift[j],
                              rwkv_w0[j], rwkv_w2[j], rwkv_a0[j], rwkv_a2[j], rwkv_g2[j], rwkv_k_k[j],
                              rwkv_k_a[j], rwkv_r_k[j], rwkv_ln_w[j], rwkv_ln_b[j], lb_all[j], hgrn_norm_w[j])
        else:
            tok = _odd_layer(tok, sh1, sc1, g1, bsz, s, n_ctx, cd_w_in[j], cd_w_out[j], na_q_norm[j], na_k_norm[j],
                             na_rpb[j], hy_short[j], hy_w1[j], hy_b1[j], hy_freq1[j], hy_w2[j], hy_b2[j],
                             hy_freq2[j], hy_w3[j], hy_bias[j])
        y, dest, gate = _moe(tok, sh2, sc2, s, router_w[l], router_b[l], moe_w1, moe_w2, l, moe_b1[l], moe_b2[l])
        part = tok.shape[0] // COMBINE_CALLS
        out = None
        for first in range(0, tok.shape[0], part):
            rows = slice(first, first + part)
            out = _combine(tok, first, y[dest[:, rows]], gate[:, rows], g2, s, out)
        tok = out
    return tok[:n_lat].reshape(bsz, s, d)


def _combine_kernel(x_ref, y_ref, w_ref, g_ref, *rest):
    o_ref = rest[-1]
    reps = x_ref.shape[1] // w_ref.shape[2]
    acc = y_ref[0] * jnp.concatenate([w_ref[0]] * reps, axis=-1)
    for k in range(1, y_ref.shape[0]):
        acc = acc + y_ref[k] * jnp.concatenate([w_ref[k]] * reps, axis=-1)
    o_ref[...] = x_ref[...] + g_ref[0] * acc


def _combine(x, first_row, yk, gate, res_gate, rows_per_mod, out_prev):
    m, d = x.shape
    n_k, part, _ = yk.shape
    rows = COMBINE_TILE
    tiles_per_mod = rows_per_mod // rows
    off = first_row // rows
    tile = pl.BlockSpec((rows, d), lambda i: (off + i, 0))
    lanes = 128
    args = [x, yk, jnp.broadcast_to(gate[:, :, None], gate.shape + (lanes,)), res_gate[:, None, :]]
    return pl.pallas_call(
        _combine_kernel,
        grid=(part // rows,),
        in_specs=[tile, pl.BlockSpec((n_k, rows, d), lambda i: (0, i, 0)),
                  pl.BlockSpec((n_k, rows, lanes), lambda i: (0, i, 0)),
                  pl.BlockSpec((1, 1, d), lambda i: ((off + i) // tiles_per_mod, 0, 0))]
        + ([] if out_prev is None else [pl.BlockSpec(memory_space=pl.ANY)]),
        out_specs=tile,
        out_shape=jax.ShapeDtypeStruct((m, d), F32),
        input_output_aliases={} if out_prev is None else {len(args): 0},
        compiler_params=pltpu.CompilerParams(dimension_semantics=("arbitrary",),
                                             vmem_limit_bytes=VMEM_LIMIT),
        name="moe_combine",
    )(*args, *([] if out_prev is None else [out_prev]))
```

```python
import functools
import math

import numpy as np
import jax
import jax.numpy as jnp
from jax import lax
from jax.experimental import pallas as pl
from jax.experimental.pallas import tpu as pltpu

F32 = jnp.float32
BF16 = jnp.bfloat16

HEAD_DIM = 64
GRID_W = 64
DECAY_LORA = 64
AAA_LORA = 64
GATE_LORA = 128
LN_X_EPS = 1e-5 * HEAD_DIM
NA_ROWS = 8
NA_COLS = 16
ROPE_THETA = 10000.0
HYENA_ORDER = 2
HYENA_EMB = 33
HYENA_FAST_DECAY = 0.3
HYENA_SLOW_DECAY = 1.5
HYENA_TARGET = 1e-2
N_EXPERTS = 32
TOP_K = 4
SWIGLU_ALPHA = 1.702
SWIGLU_LIMIT = 7.0
MOE_BLOCK = 256
MOE_CALLS = 4
RMS_EPS = 1e-6

ROW_TILE = 256
COMBINE_TILE = 512
REC_CHUNK = 64
REC_PASSES = 1
REC_CHAIN_PASSES = 1
VMEM_LIMIT = 56 * 1024 * 1024

_HI = lax.Precision.HIGHEST


_DIMS = {"nn": (((1,), (0,)), ((), ())), "nt": (((1,), (1,)), ((), ())), "tn": (((0,), (0,)), ((), ()))}


def _mm(a, b, form, passes):
    dims = _DIMS[form]
    if passes == 6:
        return lax.dot_general(a, b, dims, precision=_HI, preferred_element_type=F32)
    a_hi = a.astype(BF16)
    b_hi = b.astype(BF16)
    out = lax.dot_general(a_hi, b_hi, dims, preferred_element_type=F32)
    if passes == 3:
        a_lo = (a - a_hi.astype(F32)).astype(BF16)
        b_lo = (b - b_hi.astype(F32)).astype(BF16)
        out = out + (lax.dot_general(a_hi, b_lo, dims, preferred_element_type=F32)
                     + lax.dot_general(a_lo, b_hi, dims, preferred_element_type=F32))
    return out


def _dot(a, b):
    return _mm(a, b, "nn", 6)


def _mod_linear_kernel(x_ref, sh_ref, sc_ref, w_ref, o_ref):
    x = x_ref[...]
    ms = jnp.mean(x * x, axis=-1, keepdims=True)
    h = x * lax.rsqrt(ms + RMS_EPS) * (1.0 + sc_ref[0]) + sh_ref[0]
    o_ref[...] = jnp.dot(h.astype(BF16), w_ref[...], preferred_element_type=F32)


def _mod_linear(x, shift, scale, w, rows_per_mod):
    m, d = x.shape
    n = w.shape[1]
    tiles_per_mod = rows_per_mod // ROW_TILE
    mod_spec = pl.BlockSpec((1, 1, d), lambda i: (i // tiles_per_mod, 0, 0))
    return pl.pallas_call(
        _mod_linear_kernel,
        grid=(m // ROW_TILE,),
        in_specs=[pl.BlockSpec((ROW_TILE, d), lambda i: (i, 0)), mod_spec, mod_spec,
                  pl.BlockSpec((d, n), lambda i: (0, 0))],
        out_specs=pl.BlockSpec((ROW_TILE, n), lambda i: (i, 0)),
        out_shape=jax.ShapeDtypeStruct((m, n), F32),
        compiler_params=pltpu.CompilerParams(dimension_semantics=("arbitrary",),
                                             vmem_limit_bytes=VMEM_LIMIT),
        name="mod_linear",
    )(x, shift[:, None, :], scale[:, None, :], w.astype(BF16))


def _res_linear2_kernel(ya_ref, yb_ref, w_ref, res_ref, g_ref, o_ref):
    y = jnp.concatenate([ya_ref[...], yb_ref[...]], axis=-1).astype(BF16)
    o_ref[...] = res_ref[...] + g_ref[0] * jnp.dot(y, w_ref[...], preferred_element_type=F32)


def _res_linear2(ya, yb, w, res, gate, rows_per_mod):
    m, ka = ya.shape
    kb = yb.shape[1]
    n = w.shape[1]
    tiles_per_mod = rows_per_mod // ROW_TILE
    return pl.pallas_call(
        _res_linear2_kernel,
        grid=(m // ROW_TILE,),
        in_specs=[pl.BlockSpec((ROW_TILE, ka), lambda i: (i, 0)), pl.BlockSpec((ROW_TILE, kb), lambda i: (i, 0)),
                  pl.BlockSpec((ka + kb, n), lambda i: (0, 0)),
                  pl.BlockSpec((ROW_TILE, n), lambda i: (i, 0)),
                  pl.BlockSpec((1, 1, n), lambda i: (i // tiles_per_mod, 0, 0))],
        out_specs=pl.BlockSpec((ROW_TILE, n), lambda i: (i, 0)),
        out_shape=jax.ShapeDtypeStruct((m, n), F32),
        compiler_params=pltpu.CompilerParams(dimension_semantics=("arbitrary",),
                                             vmem_limit_bytes=VMEM_LIMIT),
        name="res_linear",
    )(ya, yb, w.astype(BF16), res, gate[:, None, :])


def _small_linear_kernel(x_ref, w_ref, b_ref, o_ref):
    o_ref[...] = _dot(x_ref[...], w_ref[...]) + b_ref[...]


def _small_linear(x, w, b):
    m, k = x.shape
    n = w.shape[1]
    tn = 1024 if n % 1024 == 0 else n
    return pl.pallas_call(
        _small_linear_kernel,
        grid=(n // tn,),
        in_specs=[pl.BlockSpec((m, k), lambda j: (0, 0)), pl.BlockSpec((k, tn), lambda j: (0, j)),
                  pl.BlockSpec((1, tn), lambda j: (0, j))],
        out_specs=pl.BlockSpec((m, tn), lambda j: (0, j)),
        out_shape=jax.ShapeDtypeStruct((m, n), F32),
        name="small_linear",
    )(x, w, b[None, :])


def _rec_kernel(*refs, delta, n_heads):
    n_in = 6 if delta else 3
    ins = (refs[:n_in], refs[n_in:2 * n_in])
    rest = refs[2 * n_in:]
    if delta:
        y_refs, s_ref = rest[:2], rest[2]
    else:
        lb_ref, y_refs, s_ref = rest[0], rest[1:3], rest[3]

    @pl.when(pl.program_id(1) == 0)
    def _():
        s_ref[...] = jnp.zeros_like(s_ref)

    c = ins[0][0].shape[0]
    row = lax.broadcasted_iota(jnp.int32, (c, c), 0)
    col = lax.broadcasted_iota(jnp.int32, (c, c), 1)
    mid = c // 2
    heads = range(n_heads)
    hs = lambda t: [t[:, h * HEAD_DIM:(h + 1) * HEAD_DIM] for h in heads]
    mm = functools.partial(_mm, passes=REC_PASSES)

    streams = []
    for d in range(2):
        ahead = row - col if d == 0 else col - row
        incl, strict = ahead >= 0, ahead > 0
        if delta:
            r_ref, v_ref, kap_ref, lw_ref, k_ref, al_ref = ins[d]
            r_all, v_all, lw, k = r_ref[...], v_ref[...], lw_ref[0], k_ref[0]
        else:
            q_ref, i_ref, f_ref = ins[d]
            fg = lb_ref[...] + (1.0 - lb_ref[...]) * jax.nn.sigmoid(f_ref[...])
            r_all, v_all, lw, k = jax.nn.silu(q_ref[...]), i_ref[...], jnp.log(fg), 1.0 - fg
        b = _mm(incl.astype(F32), lw, "nn", 6)
        bm = b[mid:mid + 1, :]
        tot = b[c - 1:c, :] if d == 0 else b[0:1, :]
        e_neg = jnp.exp(bm - b)
        e_end = jnp.exp(tot - b)
        st = dict(incl=incl, strict=strict, v=hs(v_all), rq=hs(r_all * jnp.exp(b - bm)), kd=hs(k * e_neg),
                  k_end=hs(k * e_end), g_mid=hs(jnp.exp(bm)), g_tot=hs(jnp.exp(tot)))
        if delta:
            al = al_ref[0]
            st.update(kq=hs(kap_ref[...] * jnp.exp(b - lw - bm)), ad=hs(al * e_neg), al_end=hs(al * e_end))
        streams.append(st)

    units = [(d, h) for d in range(2) for h in heads]
    n_u = range(len(units))
    per_head = lambda name: [streams[d][name][h] for d, h in units]
    per_dir = lambda name: [streams[d][name] for d, _ in units]
    incl, strict = per_dir("incl"), per_dir("strict")
    v, rq, kd, k_end, g_mid, g_tot = (per_head(n) for n in ("v", "rq", "kd", "k_end", "g_mid", "g_tot"))
    s0 = [s_ref[d, h] for d, h in units]
    s0m = [s0[n] * g_mid[n] for n in n_u]
    if delta:
        kq, ad, al_end = per_head("kq"), per_head("ad"), per_head("al_end")
        q2 = [jnp.concatenate([kq[n], rq[n]], axis=0) for n in n_u]
        k2 = [jnp.concatenate([kd[n], ad[n]], axis=0) for n in n_u]
        a = [mm(q2[n], k2[n], "nt") for n in n_u]
        p = [mm(q2[n], s0m[n], "nt") for n in n_u]
        z = [-(p[n][:c] + mm(jnp.where(strict[n], a[n][:c, :c], 0.0), v[n], "nn")) for n in n_u]
        mc = functools.partial(_mm, passes=REC_CHAIN_PASSES)
        m = [jnp.where(strict[n], a[n][:c, c:], 0.0) for n in n_u]
        pair = (row >> 1) == (col >> 1)
        eye = (row == col).astype(F32)
        t = [eye - jnp.where(pair, m[n], 0.0) for n in n_u]
        for lvl in range(1, int(math.log2(c))):
            off = ((row >> (lvl + 1)) == (col >> (lvl + 1))) & ((row >> lvl) != (col >> lvl))
            tc = [mc(t[n], jnp.where(off, m[n], 0.0), "nn") for n in n_u]
            t = [t[n] - mc(tc[n], t[n], "nn") for n in n_u]
        z = [mc(t[n], z[n], "nn") for n in n_u]
        y = [p[n][c:] + mm(jnp.where(incl[n], a[n][c:, :c], 0.0), v[n], "nn")
             + mm(jnp.where(incl[n], a[n][c:, c:], 0.0), z[n], "nn") for n in n_u]
        s_new = [s0[n] * g_tot[n] + mm(v[n], k_end[n], "tn") + mm(z[n], al_end[n], "tn") for n in n_u]
    else:
        a = [mm(rq[n], kd[n], "nt") for n in n_u]
        y = [mm(rq[n], s0m[n], "nt") + mm(jnp.where(incl[n], a[n], 0.0), v[n], "nn") for n in n_u]
        s_new = [s0[n] * g_tot[n] + mm(v[n], k_end[n], "tn") for n in n_u]
    for n, (d, h) in enumerate(units):
        s_ref[d, h] = s_new[n]
    for d in range(2):
        y_refs[d][...] = jnp.concatenate(y[d * n_heads:(d + 1) * n_heads], axis=-1)


def _seq_block_index(bsz, s, n_ctx, rows):
    n_c, n_l = n_ctx // rows, s // rows

    def index(d, b, i):
        back = jnp.where(i < n_c, n_c - 1 - i, n_l + 2 * n_c - 1 - i)
        pos = jnp.where(d == 0, i, back)
        return jnp.where(pos < n_c, bsz * n_l + b * n_c + pos, b * n_l + pos - n_c)

    return index


def _rec_call(args, specs, bsz, s, n_ctx, width, delta):
    c = REC_CHUNK
    n_heads = width // HEAD_DIM
    blk = _seq_block_index(bsz, s, n_ctx, c)
    out = jax.ShapeDtypeStruct((bsz * (s + n_ctx), width), F32)
    return pl.pallas_call(
        functools.partial(_rec_kernel, delta=delta, n_heads=n_heads),
        grid=(bsz, (s + n_ctx) // c),
        in_specs=specs,
        out_specs=[pl.BlockSpec((c, width), lambda b, i, d=d: (blk(d, b, i), 0)) for d in range(2)],
        out_shape=[out, out],
        scratch_shapes=[pltpu.VMEM((2, n_heads, HEAD_DIM, HEAD_DIM), F32)],
        compiler_params=pltpu.CompilerParams(dimension_semantics=("arbitrary", "arbitrary")),
        name="rwkv7_rec" if delta else "hgrn2_rec",
    )(*args)


def _rwkv7_recurrence(r, v, kap, lw, k, al, bsz, s, n_ctx):
    width = r.shape[1]
    blk = _seq_block_index(bsz, s, n_ctx, REC_CHUNK)
    specs = []
    for d in range(2):
        shared = pl.BlockSpec((REC_CHUNK, width), lambda b, i, d=d: (blk(d, b, i), 0))
        per_dir = pl.BlockSpec((1, REC_CHUNK, width), lambda b, i, d=d: (d, blk(d, b, i), 0))
        specs += [shared] * 3 + [per_dir] * 3
    return _rec_call([r, v, kap, lw, k, al] * 2, specs, bsz, s, n_ctx, width, True)


def _hgrn2_recurrence(u, lb, col_q, col_f, col_i, bsz, s, n_ctx):
    width = lb.shape[0]
    blk = _seq_block_index(bsz, s, n_ctx, REC_CHUNK)
    specs = []
    for d in range(2):
        specs += [pl.BlockSpec((REC_CHUNK, width), lambda b, i, d=d, j=j: (blk(d, b, i), j))
                  for j in (col_q, col_i, col_f + d)]
    specs.append(pl.BlockSpec((1, width), lambda b, i: (0, 0)))
    return _rec_call([u] * 6 + [lb[None, :]], specs, bsz, s, n_ctx, width, False)


def _na_kernel(q_ref, k_ref, v_ref, kc_ref, vc_ref, bias_ref, o_ref, *, rows_per_step, n_rows):
    g = pl.program_id(2)
    band = NA_ROWS * GRID_W
    n_h = q_ref.shape[1]
    units = [(h, j) for j in range(rows_per_step) for h in range(n_h)]
    nt = lambda p, q_: lax.dot_general(p, q_, (((1,), (1,)), ((), ())), preferred_element_type=F32)
    qr = [g * rows_per_step + j for j in range(rows_per_step)]
    start = [jnp.clip(r - NA_ROWS // 2, 0, n_rows - NA_ROWS) for r in qr]
    off = [pl.multiple_of(st * GRID_W, GRID_W) for st in start]
    q = [q_ref[0, h, j * GRID_W:(j + 1) * GRID_W, :] for h, j in units]
    s_win = [nt(q[n], k_ref[0, h, pl.ds(off[j], band), :]) + bias_ref[h, start[j] - qr[j] + NA_ROWS - 1]
             for n, (h, j) in enumerate(units)]
    s_ctx = [nt(q[n], kc_ref[0, h]) for n, (h, j) in enumerate(units)]
    ns = range(len(units))
    m = [jnp.maximum(jnp.max(s_win[n], axis=-1, keepdims=True), jnp.max(s_ctx[n], axis=-1, keepdims=True))
         for n in ns]
    p_win = [jnp.exp(s_win[n] - m[n]) for n in ns]
    p_ctx = [jnp.exp(s_ctx[n] - m[n]) for n in ns]
    den = [jnp.sum(p_win[n], axis=-1, keepdims=True) + jnp.sum(p_ctx[n], axis=-1, keepdims=True) for n in ns]
    o = [(jnp.dot(p_win[n].astype(BF16), v_ref[0, h, pl.ds(off[j], band), :], preferred_element_type=F32)
          + jnp.dot(p_ctx[n].astype(BF16), vc_ref[0, h], preferred_element_type=F32)) / den[n]
         for n, (h, j) in enumerate(units)]
    for j in range(rows_per_step):
        o_ref[j * GRID_W:(j + 1) * GRID_W, :] = jnp.concatenate(o[j * n_h:(j + 1) * n_h], axis=-1)


def _na_bias_table(rpb):
    cc = np.arange(GRID_W)
    col_start = np.clip(cc - NA_COLS // 2, 0, GRID_W - NA_COLS)
    kc = np.arange(GRID_W)
    inside = (kc[None, :] >= col_start[:, None]) & (kc[None, :] < col_start[:, None] + NA_COLS)
    col_off = np.clip(kc[None, :] - cc[:, None] + NA_COLS - 1, 0, 2 * NA_COLS - 2)
    tab = rpb[:, :, col_off]
    tab = jnp.where(jnp.asarray(inside)[None, None], tab, -jnp.inf)
    d0 = np.arange(NA_ROWS)[:, None] + np.arange(NA_ROWS)[None, :]
    band = tab[:, d0]
    return band.transpose(0, 1, 3, 2, 4).reshape(rpb.shape[0], NA_ROWS, GRID_W, NA_ROWS * GRID_W)


NA_HEADS_PER_STEP = 128 // HEAD_DIM


def _neighbourhood_attention(q, k, v, kc, vc, rpb):
    bsz, n_heads, s, dh = q.shape
    n_rows = s // GRID_W
    rows_per_step = 8
    hp = NA_HEADS_PER_STEP
    lc = kc.shape[2]
    bias = _na_bias_table(rpb)
    steps = n_rows // rows_per_step
    full = pl.BlockSpec((1, hp, s, dh), lambda b, h, g: (b, h, 0, 0))
    ctx = pl.BlockSpec((1, hp, lc, dh), lambda b, h, g: (b, h, 0, 0))
    return pl.pallas_call(
        functools.partial(_na_kernel, rows_per_step=rows_per_step, n_rows=n_rows),
        grid=(bsz, n_heads // hp, steps),
        in_specs=[pl.BlockSpec((1, hp, rows_per_step * GRID_W, dh), lambda b, h, g: (b, h, g, 0)),
                  full, full, ctx, ctx,
                  pl.BlockSpec((hp, NA_ROWS, GRID_W, NA_ROWS * GRID_W), lambda b, h, g: (h, 0, 0, 0))],
        out_specs=pl.BlockSpec((rows_per_step * GRID_W, hp * dh), lambda b, h, g: (b * steps + g, h)),
        out_shape=jax.ShapeDtypeStruct((bsz * s, n_heads * dh), F32),
        compiler_params=pltpu.CompilerParams(
            dimension_semantics=("arbitrary", "arbitrary", "arbitrary"), vmem_limit_bytes=VMEM_LIMIT),
        name="neighbourhood_attention",
    )(q, k, v, kc, vc, bias)


def _na_prep_kernel(u_ref, cos_ref, sin_ref, qn_ref, kn_ref, bd_ref, q_ref, k_ref, v_ref, *, n_heads):
    width = n_heads * HEAD_DIM
    bd = bd_ref[...]
    lane = lax.broadcasted_iota(jnp.int32, (u_ref.shape[0], 128), 1)
    first = (lane & (HEAD_DIM // 4)) == 0

    def rope(x):
        parts = []
        for j in range(width // 128):
            cols = slice(128 * j, 128 * (j + 1))
            xj = x[:, cols]
            partner = jnp.where(first, pltpu.roll(xj, 128 - HEAD_DIM // 4, axis=1),
                                pltpu.roll(xj, HEAD_DIM // 4, axis=1))
            parts.append(xj * cos_ref[:, cols] + partner * sin_ref[:, cols])
        return jnp.concatenate(parts, axis=-1)

    def normed(x, w_ref):
        return x * lax.rsqrt(_seg_sum(x * x, bd) * (1.0 / HEAD_DIM) + RMS_EPS) * w_ref[...]

    q = rope(normed(u_ref[:, :width], qn_ref)) * HEAD_DIM ** -0.5
    k = rope(normed(u_ref[:, width:2 * width], kn_ref))
    v = u_ref[:, 2 * width:3 * width]
    for h in range(n_heads):
        cols = slice(h * HEAD_DIM, (h + 1) * HEAD_DIM)
        q_ref[0, h] = q[:, cols].astype(BF16)
        k_ref[0, h] = k[:, cols].astype(BF16)
        v_ref[0, h] = v[:, cols].astype(BF16)


def _rope_tables(s, n_heads):
    t = jnp.arange(s)
    nf = HEAD_DIM // 4
    inv = ROPE_THETA ** (-jnp.arange(nf, dtype=F32) / nf)
    ang_r = (t // GRID_W).astype(F32)[:, None] * inv
    ang_c = (t % GRID_W).astype(F32)[:, None] * inv
    cos = jnp.concatenate([jnp.cos(ang_r)] * 2 + [jnp.cos(ang_c)] * 2, axis=-1)
    sin = jnp.concatenate([-jnp.sin(ang_r), jnp.sin(ang_r), -jnp.sin(ang_c), jnp.sin(ang_c)], axis=-1)
    return jnp.tile(cos, (1, n_heads)), jnp.tile(sin, (1, n_heads))


def _na_prep(u, bsz, s, n_heads, q_norm, k_norm):
    width = n_heads * HEAD_DIM
    tile = ROW_TILE
    per_batch = s // tile
    cos, sin = _rope_tables(s, n_heads)
    tab = pl.BlockSpec((tile, width), lambda i: (i % per_batch, 0))
    vec = pl.BlockSpec((1, width), lambda i: (0, 0))
    out_spec = pl.BlockSpec((1, n_heads, tile, HEAD_DIM), lambda i: (i // per_batch, 0, i % per_batch, 0))
    out = jax.ShapeDtypeStruct((bsz, n_heads, s, HEAD_DIM), BF16)
    return pl.pallas_call(
        functools.partial(_na_prep_kernel, n_heads=n_heads),
        grid=(bsz * per_batch,),
        in_specs=[pl.BlockSpec((tile, 3 * width), lambda i: (i, 0)), tab, tab, vec, vec,
                  pl.BlockSpec((width, width), lambda i: (0, 0))],
        out_specs=[out_spec, out_spec, out_spec],
        out_shape=[out, out, out],
        name="na_prep",
    )(u, cos, sin, jnp.tile(q_norm, n_heads)[None, :], jnp.tile(k_norm, n_heads)[None, :], _head_ones(width))


DEINT_GROUP = 256


def _moe_kernel(be_ref, nb_ref, x_ref, w1_ref, b1_ref, w2_ref, b2_ref, perm_ref, *rest, first_block):
    o_ref, w1_s, w2_s = rest[-3:]
    blk = first_block + pl.program_id(0)
    active = blk < nb_ref[0]
    fresh = jnp.logical_or(pl.program_id(0) == 0, be_ref[blk] != be_ref[jnp.maximum(blk - 1, 0)])

    @pl.when(jnp.logical_and(active, fresh))
    def _():
        half = w1_s.shape[1] // 2
        g_out = DEINT_GROUP // 2
        for g in range(w1_s.shape[1] // DEINT_GROUP):
            t = jnp.dot(w1_ref[0, 0, :, g * DEINT_GROUP:(g + 1) * DEINT_GROUP].astype(BF16), perm_ref[...],
                        preferred_element_type=F32).astype(BF16)
            w1_s[:, g * g_out:(g + 1) * g_out] = t[:, :g_out]
            w1_s[:, half + g * g_out:half + (g + 1) * g_out] = t[:, g_out:]
        w2_s[...] = w2_ref[0, 0].astype(BF16)

    @pl.when(active)
    def _():
        y = jnp.dot(x_ref[...], w1_s[...], preferred_element_type=F32) + b1_ref[0]
        half = y.shape[1] // 2
        glu = jnp.minimum(y[:, :half], SWIGLU_LIMIT)
        lin = jnp.clip(y[:, half:], -SWIGLU_LIMIT, SWIGLU_LIMIT)
        act = glu * jax.nn.sigmoid(SWIGLU_ALPHA * glu) * (lin + 1.0)
        o_ref[...] = jnp.dot(act.astype(BF16), w2_s[...], preferred_element_type=F32) + b2_ref[0]

    @pl.when(jnp.logical_not(active))
    def _():
        o_ref[...] = jnp.zeros_like(o_ref)


def _moe_experts(x_part, first_block, n_blocks_total, y_prev, blk_e, n_used, w1_layers, w2_layers, layer, b1, b2):
    d = x_part.shape[1]
    n_blocks = x_part.shape[0] // MOE_BLOCK
    n_slots = n_blocks_total * MOE_BLOCK
    off = first_block
    de2 = w1_layers.shape[3]
    perm = np.zeros((DEINT_GROUP, DEINT_GROUP), np.float32)
    idx = np.arange(DEINT_GROUP // 2)
    perm[2 * idx, idx] = 1.0
    perm[2 * idx + 1, DEINT_GROUP // 2 + idx] = 1.0
    grid_spec = pltpu.PrefetchScalarGridSpec(
        num_scalar_prefetch=2,
        grid=(n_blocks,),
        in_specs=[pl.BlockSpec((MOE_BLOCK, d), lambda i, be, nb: (i, 0)),
                  pl.BlockSpec((1, 1, d, de2), lambda i, be, nb: (layer, be[off + i], 0, 0)),
                  pl.BlockSpec((1, 1, de2), lambda i, be, nb: (be[off + i], 0, 0)),
                  pl.BlockSpec((1, 1, de2 // 2, d), lambda i, be, nb: (layer, be[off + i], 0, 0)),
                  pl.BlockSpec((1, 1, d), lambda i, be, nb: (be[off + i], 0, 0)),
                  pl.BlockSpec((DEINT_GROUP, DEINT_GROUP), lambda i, be, nb: (0, 0))]
        + ([] if y_prev is None else [pl.BlockSpec(memory_space=pl.ANY)]),
        out_specs=pl.BlockSpec((MOE_BLOCK, d), lambda i, be, nb: (off + i, 0)),
        scratch_shapes=[pltpu.VMEM((d, de2), BF16), pltpu.VMEM((de2 // 2, d), BF16)],
    )
    args = [blk_e, n_used, x_part, w1_layers, b1[:, None, :], w2_layers, b2[:, None, :], jnp.asarray(perm, BF16)]
    return pl.pallas_call(
        functools.partial(_moe_kernel, first_block=first_block),
        grid_spec=grid_spec,
        out_shape=jax.ShapeDtypeStruct((n_slots, d), F32),
        input_output_aliases={} if y_prev is None else {len(args): 0},
        compiler_params=pltpu.CompilerParams(dimension_semantics=("arbitrary",),
                                             vmem_limit_bytes=VMEM_LIMIT),
        name="moe_experts",
    )(*args, *([] if y_prev is None else [y_prev]))


ROUTE_TILE = 1024


def _route_kernel(x_ref, sh_ref, sc_ref, wt_ref, b_ref, before_ref, h_ref, e_ref, g_ref, r_ref, cnt_ref,
                  carry_ref):
    i = pl.program_id(0)

    @pl.when(i == 0)
    def _():
        carry_ref[...] = jnp.zeros_like(carry_ref)

    x = x_ref[...]
    ms = jnp.mean(x * x, axis=-1, keepdims=True)
    h = x * lax.rsqrt(ms + RMS_EPS) * (1.0 + sc_ref[0]) + sh_ref[0]
    h_ref[...] = h.astype(BF16)
    logits = _mm(wt_ref[...], h, "nt", 6) + b_ref[...]
    n_e, tm = logits.shape
    eidx = lax.broadcasted_iota(jnp.int32, (n_e, tm), 0)
    work = logits
    top_v, top_e = [], []
    for _ in range(TOP_K):
        m = jnp.max(work, axis=0, keepdims=True)
        sel = jnp.min(jnp.where(work == m, eidx, n_e), axis=0, keepdims=True)
        top_v.append(m)
        top_e.append(sel)
        work = jnp.where(eidx == sel, -jnp.inf, work)
    ex = [jnp.exp(v - top_v[0]) for v in top_v]
    den = ex[0] + ex[1] + ex[2] + ex[3]
    g_ref[...] = jnp.concatenate([e_ / den for e_ in ex], axis=0)
    e_ref[...] = jnp.concatenate(top_e, axis=0)
    chosen = [eidx == sel for sel in top_e]
    ind = sum(c.astype(F32) for c in chosen)
    carry = carry_ref[...]
    cnt = (jnp.dot(ind.astype(BF16), before_ref[...], preferred_element_type=F32)
           + jnp.concatenate([carry] * (tm // carry.shape[1]), axis=1))
    r_ref[...] = jnp.concatenate(
        [jnp.sum(jnp.where(c, cnt, 0.0), axis=0, keepdims=True) for c in chosen], axis=0).astype(jnp.int32)
    carry = carry + jnp.sum(ind, axis=1, keepdims=True)
    carry_ref[...] = carry
    cnt_ref[...] = carry


def _route(tok, shift, scale, router_w, router_b, rows_per_mod):
    t, d = tok.shape
    n_e = router_w.shape[1]
    tm = ROUTE_TILE
    tiles_per_mod = rows_per_mod // tm
    mod_spec = pl.BlockSpec((1, 1, d), lambda i: (i // tiles_per_mod, 0, 0))
    before = jnp.asarray(np.triu(np.ones((tm, tm), np.float32), 1), BF16)
    kt_spec = pl.BlockSpec((TOP_K, tm), lambda i: (0, i))
    h, top_e, gate, rank, cnt = pl.pallas_call(
        _route_kernel,
        grid=(t // tm,),
        in_specs=[pl.BlockSpec((tm, d), lambda i: (i, 0)), mod_spec, mod_spec,
                  pl.BlockSpec((n_e, d), lambda i: (0, 0)), pl.BlockSpec((n_e, tm), lambda i: (0, 0)),
                  pl.BlockSpec((tm, tm), lambda i: (0, 0))],
        out_specs=[pl.BlockSpec((tm, d), lambda i: (i, 0)), kt_spec, kt_spec, kt_spec,
                   pl.BlockSpec((n_e, 128), lambda i: (0, 0))],
        out_shape=[jax.ShapeDtypeStruct((t, d), BF16), jax.ShapeDtypeStruct((TOP_K, t), jnp.int32),
                   jax.ShapeDtypeStruct((TOP_K, t), F32), jax.ShapeDtypeStruct((TOP_K, t), jnp.int32),
                   jax.ShapeDtypeStruct((n_e, 128), F32)],
        scratch_shapes=[pltpu.VMEM((n_e, 128), F32)],
        compiler_params=pltpu.CompilerParams(dimension_semantics=("arbitrary",),
                                             vmem_limit_bytes=VMEM_LIMIT),
        name="moe_route",
    )(tok, shift[:, None, :], scale[:, None, :], router_w.T, jnp.broadcast_to(router_b[:, None], (n_e, tm)),
      before)
    return h, top_e, gate, rank, cnt[:, 0].astype(jnp.int32)


def _moe(tok, shift, scale, rows_per_mod, router_w, router_b, w1_layers, w2_layers, layer, b1, b2):
    t, d = tok.shape
    h, top_e, gate, rank, counts = _route(tok, shift, scale, router_w, router_b, rows_per_mod)
    n_assign = t * TOP_K
    n_blocks = -(-n_assign // MOE_BLOCK) + N_EXPERTS
    n_slots = n_blocks * MOE_BLOCK
    padded = (counts + MOE_BLOCK - 1) // MOE_BLOCK * MOE_BLOCK
    start = jnp.cumsum(counts) - counts
    p_end = jnp.cumsum(padded)
    p_start = p_end - padded
    experts = jnp.arange(N_EXPERTS, dtype=jnp.int32)
    dest = jnp.sum(jnp.where(top_e[:, :, None] == experts, p_start, 0), axis=-1) + rank
    blk_first = jnp.arange(n_blocks, dtype=jnp.int32) * MOE_BLOCK
    blk_e = jnp.minimum(jnp.sum(p_end[None, :] <= blk_first[:, None], axis=1), N_EXPERTS - 1).astype(jnp.int32)
    n_used = (p_end[-1] // MOE_BLOCK).astype(jnp.int32).reshape(1)
    order = jnp.argsort(top_e.T.reshape(-1))
    per_slot = lambda per_expert: jnp.repeat(per_expert[blk_e], MOE_BLOCK)
    j = jnp.arange(n_slots, dtype=jnp.int32) - per_slot(p_start)
    src = jnp.clip(per_slot(start) + j, 0, n_assign - 1)
    slot_tok = jnp.where(j < per_slot(counts), order[src] // TOP_K, t).astype(jnp.int32)
    h_pad = jnp.concatenate([h, jnp.zeros((1, d), BF16)], axis=0)
    b1p = jnp.concatenate([b1[..., ::2], b1[..., 1::2]], axis=-1)
    y = jnp.zeros((n_slots, d), F32)
    for first in range(0, n_blocks, n_blocks // MOE_CALLS):
        sl = slice(first * MOE_BLOCK, (first + n_blocks // MOE_CALLS) * MOE_BLOCK)
        y = _moe_experts(h_pad[slot_tok[sl]], first, n_blocks, y, blk_e, n_used, w1_layers, w2_layers, layer,
                         b1p, b2)
    return y[dest], gate


def _rms(x):
    return x * lax.rsqrt(jnp.mean(x * x, axis=-1, keepdims=True) + RMS_EPS)


def _seg_sum(x, ones_bd):
    x_hi, x_lo = _hi_lo(x)
    return (jnp.dot(x_hi, ones_bd, preferred_element_type=F32)
            + jnp.dot(x_lo, ones_bd, preferred_element_type=F32))


def _head_ones(width):
    h = np.arange(width) // HEAD_DIM
    return jnp.asarray(h[:, None] == h[None, :], BF16)


def _shift_conv(i, first_ref, last_ref, u_ref, up_ref, un_ref, sw_ref):
    keep_prev = (first_ref[i] == 0).astype(F32)
    keep_next = (last_ref[i] == 0).astype(F32)
    rows = u_ref.shape[0]

    def conv(lo, width, tap_lo):
        x = u_ref[:, lo:lo + width]
        ridx = lax.broadcasted_iota(jnp.int32, x.shape, 0)
        x_prev = jnp.where(ridx == 0, up_ref[7:8, lo:lo + width] * keep_prev, pltpu.roll(x, 1, axis=0))
        x_next = jnp.where(ridx == rows - 1, un_ref[0:1, lo:lo + width] * keep_next,
                           pltpu.roll(x, rows - 1, axis=0))
        w = sw_ref[:, tap_lo:tap_lo + width]
        return x_prev * w[0:1] + x * w[1:2] + x_next * w[2:3]

    return conv


def _rwkv7_prep_kernel(first_ref, last_ref, u_ref, up_ref, un_ref, sw_ref, w0_ref, w2_ref, a0_ref, a2_ref,
                       g2_ref, kk_ref, ka_ref, bd_ref, r_ref, v_ref, kap_ref, g_ref, lw_ref, k_ref, al_ref,
                       *, d_a, lora_off):
    conv = _shift_conv(pl.program_id(0), first_ref, last_ref, u_ref, up_ref, un_ref, sw_ref)
    r = conv(0, d_a, 0)
    k = conv(d_a, d_a, d_a)
    v = conv(2 * d_a, d_a, 2 * d_a)
    wd = conv(lora_off, 2 * DECAY_LORA, 3 * d_a)
    ad = conv(lora_off + 2 * DECAY_LORA, 2 * AAA_LORA, 3 * d_a + 2 * DECAY_LORA)
    gd = conv(lora_off + 2 * DECAY_LORA + 2 * AAA_LORA, GATE_LORA, 3 * d_a + 2 * DECAY_LORA + 2 * AAA_LORA)
    mm3 = lambda p, q: _mm(p, q, "nn", 3)
    kk = k * kk_ref[...]
    kap = kk / jnp.maximum(jnp.sqrt(_seg_sum(kk * kk, bd_ref[...])), 1e-12)
    lora_w = jnp.tanh(wd)
    for d in range(2):
        z = w0_ref[d:d + 1] + mm3(lora_w[:, d * DECAY_LORA:(d + 1) * DECAY_LORA], w2_ref[d])
        lw_ref[d] = -math.exp(-0.5) * jax.nn.sigmoid(z)
        a = jax.nn.sigmoid(a0_ref[d:d + 1] + mm3(ad[:, d * AAA_LORA:(d + 1) * AAA_LORA], a2_ref[d]))
        k_ref[d] = k * (1.0 + (a - 1.0) * ka_ref[...])
        al_ref[d] = kap * a
    g_ref[...] = mm3(jax.nn.sigmoid(gd), g2_ref[...])
    r_ref[...] = r
    v_ref[...] = v
    kap_ref[...] = kap


def _segment_flags(bsz, s, n_ctx, tile):
    lat, ctx = np.arange(bsz * s // tile), np.arange(bsz * n_ctx // tile)
    first = np.concatenate([lat % (s // tile) == 0, ctx % (n_ctx // tile) == 0])
    last = np.concatenate([lat % (s // tile) == s // tile - 1, ctx % (n_ctx // tile) == n_ctx // tile - 1])
    return jnp.asarray(first, jnp.int32), jnp.asarray(last, jnp.int32)


def _rwkv7_prep(u, lora_off, bsz, s, n_ctx, shift_w, w0, w2, a0, a2, g2, k_k, k_a):
    t, cols = u.shape
    d_a = w0.shape[-1]
    tile = ROW_TILE
    halo = 8
    first, last = _segment_flags(bsz, s, n_ctx, tile)
    full = lambda arr: pl.BlockSpec(arr.shape, lambda i, f, l: (0,) * arr.ndim)
    row = lambda arr: arr[None, :]
    params = [shift_w, w0, w2, a0, a2, g2, row(k_k), row(k_a), _head_ones(d_a)]
    one = pl.BlockSpec((tile, d_a), lambda i, f, l: (i, 0))
    two = pl.BlockSpec((2, tile, d_a), lambda i, f, l: (0, i, 0))
    flat = jax.ShapeDtypeStruct((t, d_a), F32)
    both = jax.ShapeDtypeStruct((2, t, d_a), F32)
    grid_spec = pltpu.PrefetchScalarGridSpec(
        num_scalar_prefetch=2,
        grid=(t // tile,),
        in_specs=[pl.BlockSpec((tile, cols), lambda i, f, l: (i, 0)),
                  pl.BlockSpec((halo, cols), lambda i, f, l: (jnp.maximum(i * (tile // halo) - 1, 0), 0)),
                  pl.BlockSpec((halo, cols),
                               lambda i, f, l: (jnp.minimum((i + 1) * (tile // halo), t // halo - 1), 0))]
        + [full(p) for p in params],
        out_specs=[one, one, one, one, two, two, two],
    )
    return pl.pallas_call(
        functools.partial(_rwkv7_prep_kernel, d_a=d_a, lora_off=lora_off),
        grid_spec=grid_spec,
        out_shape=[flat, flat, flat, flat, both, both, both],
        compiler_params=pltpu.CompilerParams(dimension_semantics=("arbitrary",),
                                             vmem_limit_bytes=VMEM_LIMIT),
        name="rwkv7_prep",
    )(first, last, u, u, u, *params)


def _even_out_kernel(yaf_ref, yab_ref, r_ref, v_ref, k_ref, g_ref, ybf_ref, ybb_ref, og_ref, rk_ref, lnw_ref,
                     lnb_ref, nw_ref, bd_ref, w_ref, res_ref, gate_ref, o_ref):
    bd = bd_ref[...]
    inv = 1.0 / HEAD_DIM
    y = yaf_ref[...] + yab_ref[...]
    yc = y - _seg_sum(y, bd) * inv
    var = _seg_sum(yc * yc, bd) * inv
    yn = yc * lax.rsqrt(var + LN_X_EPS) * lnw_ref[...] + lnb_ref[...]
    bonus = _seg_sum(r_ref[...] * (k_ref[0] + k_ref[1]) * rk_ref[...], bd) * v_ref[...]
    out_a = (yn + bonus) * g_ref[...]
    o = ybf_ref[...] + ybb_ref[...]
    out_b = o * lax.rsqrt(_seg_sum(o * o, bd) * inv + RMS_EPS) * nw_ref[...] * jax.nn.silu(og_ref[...])
    ycat = jnp.concatenate([out_a, out_b], axis=-1).astype(BF16)
    o_ref[...] = res_ref[...] + gate_ref[0] * jnp.dot(ycat, w_ref[...], preferred_element_type=F32)


def _even_out(ya, r, v, k, g, yb, u, col_og, r_k, ln_w, ln_b, norm_w, w_out, res, gate, rows_per_mod):
    t, width = r.shape
    d = res.shape[1]
    tile = ROW_TILE
    tiles_per_mod = rows_per_mod // tile
    one = pl.BlockSpec((tile, width), lambda i: (i, 0))
    two = pl.BlockSpec((2, tile, width), lambda i: (0, i, 0))
    vec = pl.BlockSpec((1, width), lambda i: (0, 0))
    n_heads = width // HEAD_DIM
    return pl.pallas_call(
        _even_out_kernel,
        grid=(t // tile,),
        in_specs=[one, one, one, one, two, one, one, one, pl.BlockSpec((tile, width), lambda i: (i, col_og)),
                  vec, vec, vec, vec, pl.BlockSpec((width, width), lambda i: (0, 0)),
                  pl.BlockSpec(w_out.shape, lambda i: (0, 0)), pl.BlockSpec((tile, d), lambda i: (i, 0)),
                  pl.BlockSpec((1, 1, d), lambda i: (i // tiles_per_mod, 0, 0))],
        out_specs=pl.BlockSpec((tile, d), lambda i: (i, 0)),
        out_shape=jax.ShapeDtypeStruct((t, d), F32),
        compiler_params=pltpu.CompilerParams(dimension_semantics=("arbitrary",),
                                             vmem_limit_bytes=VMEM_LIMIT),
        name="even_readout_out_proj",
    )(*ya, r, v, k, g, *yb, u, r_k.reshape(1, width), ln_w[None, :], ln_b[None, :],
      jnp.tile(norm_w, n_heads)[None, :], _head_ones(width), w_out.astype(BF16), res, gate[:, None, :])


def _even_layer(tok, sh1, sc1, g1, bsz, s, n_ctx, w_in, w_out, shift_w, w0, w2, a0, a2, g2, k_k, k_a, r_k, ln_w,
                ln_b, lb, norm_w):
    d_a = w0.shape[-1]
    d_b = lb.shape[0]
    rkv = 3 * d_a
    a_cols = rkv + 2 * DECAY_LORA + 2 * AAA_LORA + GATE_LORA
    w_perm = jnp.concatenate([w_in[:, :rkv], w_in[:, a_cols:], w_in[:, rkv:a_cols]], axis=1)
    u = _mod_linear(tok, sh1, sc1, w_perm, s)
    r, v, kap, g, lw, k, al = _rwkv7_prep(u, rkv + 5 * d_b, bsz, s, n_ctx, shift_w, w0, w2, a0, a2, g2, k_k, k_a)
    ya = _rwkv7_recurrence(r, v, kap, lw, k, al, bsz, s, n_ctx)
    col0 = rkv // d_b
    yb = _hgrn2_recurrence(u, lb, col0, col0 + 1, col0 + 3, bsz, s, n_ctx)
    return _even_out(ya, r, v, k, g, yb, u, col0 + 4, r_k, ln_w, ln_b, norm_w, w_out, tok, g1, s)


def _hyena_filters(length, d_d, w1, b1, fr1, w2, b2, fr2, w3):
    t = jnp.linspace(0.0, 1.0, length, dtype=F32)[:, None]
    bands = (HYENA_EMB - 1) // 2
    f = jnp.linspace(1e-4, bands - 1, bands, dtype=F32)
    ang = (2.0 * math.pi / length) * jnp.arange(length, dtype=F32)[:, None] * f
    z = jnp.concatenate([t, jnp.cos(ang), -jnp.sin(ang)], axis=-1)
    hid = jnp.sin(fr1 * (jnp.dot(z, w1, precision=_HI) + b1))
    hid = jnp.sin(fr2 * (jnp.dot(hid, w2, precision=_HI) + b2))
    h = jnp.dot(hid, w3, precision=_HI).reshape(length, HYENA_ORDER, 2, d_d)
    deltas = jnp.abs(jnp.linspace(math.log(HYENA_TARGET) / HYENA_SLOW_DECAY,
                                  math.log(HYENA_TARGET) / HYENA_FAST_DECAY, d_d, dtype=F32))
    return h * jnp.exp(-t * deltas)[:, None, None, :]


DFT_R = 128
DFT_COL_TILE = 4096
DFT_K1_TILE = 4


def _hi_lo(a):
    hi = a.astype(BF16)
    return hi, (a - hi.astype(F32)).astype(BF16)


def _mm3(f_hi, f_lo, x_hi, x_lo):
    d = lambda p, q: jnp.dot(p, q, preferred_element_type=F32)
    return d(f_hi, x_hi) + (d(f_hi, x_lo) + d(f_lo, x_hi))


def _dft_matrices(n1):
    k1 = np.arange(n1, dtype=np.float64)
    k2 = np.arange(DFT_R, dtype=np.float64)
    as_f32 = lambda a: jnp.asarray(a, F32)
    cs = lambda ang: (as_f32(np.cos(ang)), as_f32(-np.sin(ang)))
    f1 = cs(2.0 * np.pi * np.outer(k1, k1) / n1)
    f2 = cs(2.0 * np.pi * np.outer(k2, k2) / DFT_R)
    tw = cs(2.0 * np.pi * np.outer(k1, k2) / (n1 * DFT_R))
    tw = tuple(jnp.broadcast_to(t[:, :, None], (n1, DFT_R, 128)) for t in tw)
    return f1, f2, tw


def _dft_rows_kernel(x_ref, frh_ref, frl_ref, fih_ref, fil_ref, or_ref, oi_ref):
    frh, frl, fih, fil = frh_ref[...], frl_ref[...], fih_ref[...], fil_ref[...]
    x0 = _hi_lo(x_ref[0])
    if x_ref.shape[0] == 1:
        or_ref[0] = _mm3(frh, frl, *x0)
        oi_ref[0] = _mm3(fih, fil, *x0)
    else:
        x1 = _hi_lo(x_ref[1])
        or_ref[0] = _mm3(frh, frl, *x0) - _mm3(fih, fil, *x1)
        oi_ref[0] = _mm3(frh, frl, *x1) + _mm3(fih, fil, *x0)


def _dft_rows(x2d, f1, pack):
    bsz, rows, cols = x2d.shape
    n1 = f1[0].shape[0]
    mats = [m for f in f1 for m in _hi_lo(f[:, :rows])]
    mat_spec = pl.BlockSpec((n1, rows), lambda b, j: (0, 0))
    out_spec = pl.BlockSpec((1, n1, DFT_COL_TILE), lambda b, j: (b, 0, j))
    out = jax.ShapeDtypeStruct((bsz // pack, n1, cols), F32)
    return pl.pallas_call(
        _dft_rows_kernel,
        grid=(bsz // pack, cols // DFT_COL_TILE),
        in_specs=[pl.BlockSpec((pack, rows, DFT_COL_TILE), lambda b, j: (b, 0, j))] + [mat_spec] * 4,
        out_specs=[out_spec, out_spec],
        out_shape=[out, out],
        name="dft_rows",
    )(x2d, *mats)


def _dft_mid_kernel(*refs, conv):
    if conv:
        (ar_ref, ai_ref, tr_ref, ti_ref, kr_ref, ki_ref, frh_ref, frl_ref, fih_ref, fil_ref,
         or_ref, oi_ref) = refs
    else:
        ar_ref, ai_ref, tr_ref, ti_ref, frh_ref, frl_ref, fih_ref, fil_ref, or_ref, oi_ref = refs
    frh, frl, fih, fil = frh_ref[...], frl_ref[...], fih_ref[...], fil_ref[...]
    reps = ar_ref.shape[3] // tr_ref.shape[2]
    rows = range(ar_ref.shape[1])
    wide = lambda t: jnp.concatenate([t] * reps, axis=-1)
    tr = [wide(tr_ref[j]) for j in rows]
    ti = [wide(ti_ref[j]) for j in rows]
    ar = [ar_ref[0, j] for j in rows]
    ai = [ai_ref[0, j] for j in rows]
    pr = [_hi_lo(ar[j] * tr[j] - ai[j] * ti[j]) for j in rows]
    pi = [_hi_lo(ar[j] * ti[j] + ai[j] * tr[j]) for j in rows]
    xr = [_mm3(frh, frl, *pr[j]) - _mm3(fih, fil, *pi[j]) for j in rows]
    xi = [_mm3(frh, frl, *pi[j]) + _mm3(fih, fil, *pr[j]) for j in rows]
    if not conv:
        for j in rows:
            or_ref[0, j] = xr[j]
            oi_ref[0, j] = xi[j]
        return
    yr = [_hi_lo(xr[j] * kr_ref[0, j] - xi[j] * ki_ref[0, j]) for j in rows]
    yi = [_hi_lo(xr[j] * ki_ref[0, j] + xi[j] * kr_ref[0, j]) for j in rows]
    br = [_mm3(frh, frl, *yr[j]) + _mm3(fih, fil, *yi[j]) for j in rows]
    bi = [_mm3(frh, frl, *yi[j]) - _mm3(fih, fil, *yr[j]) for j in rows]
    for j in rows:
        or_ref[0, j] = br[j] * tr[j] + bi[j] * ti[j]
        oi_ref[0, j] = bi[j] * tr[j] - br[j] * ti[j]


def _dft_mid(a_r, a_i, f2, tw, kf=None):
    (f_r, f_i), (t_r, t_i) = f2, tw
    bsz, n1, _, ch = a_r.shape
    slab = pl.BlockSpec((1, DFT_K1_TILE, DFT_R, ch), lambda b, g: (b, g, 0, 0))
    tw = pl.BlockSpec((DFT_K1_TILE, DFT_R, t_r.shape[2]), lambda b, g: (g, 0, 0))
    mat = pl.BlockSpec((DFT_R, DFT_R), lambda b, g: (0, 0))
    args = [a_r, a_i, t_r, t_i]
    specs = [slab, slab, tw, tw]
    if kf is not None:
        kf_r, kf_i, o = kf
        kspec = pl.BlockSpec((1, DFT_K1_TILE, DFT_R, ch), lambda b, g: (o, g, 0, 0))
        args += [kf_r, kf_i]
        specs += [kspec, kspec]
    args += [m for f in (f_r, f_i) for m in _hi_lo(f)]
    specs += [mat] * 4
    out = jax.ShapeDtypeStruct(a_r.shape, F32)
    return pl.pallas_call(
        functools.partial(_dft_mid_kernel, conv=kf is not None),
        grid=(bsz, n1 // DFT_K1_TILE),
        in_specs=specs,
        out_specs=[slab, slab],
        out_shape=[out, out],
        compiler_params=pltpu.CompilerParams(dimension_semantics=("arbitrary", "arbitrary"),
                                             vmem_limit_bytes=VMEM_LIMIT),
        name="dft_mid_conv" if kf is not None else "dft_mid",
    )(*args)


def _idft_rows_kernel(br_ref, bi_ref, z_ref, g_ref, bias_ref, grh_ref, grl_ref, gih_ref, gil_ref, o_ref):
    grh, grl, gih, gil = grh_ref[...], grl_ref[...], gih_ref[...], gil_ref[...]
    b_r, b_i = _hi_lo(br_ref[0]), _hi_lo(bi_ref[0])
    y = (_mm3(grh, grl, *b_r) + _mm3(gih, gil, *b_i), _mm3(grh, grl, *b_i) - _mm3(gih, gil, *b_r))
    for e in range(2):
        o_ref[e] = g_ref[e] * (y[e] + z_ref[e] * bias_ref[...])


def _idft_rows(b_r, b_i, z2d, gate2d, bias_cols, f1):
    bsz, rows, cols = z2d.shape
    n1 = f1[0].shape[0]
    scale = 1.0 / (n1 * DFT_R)
    mats = [m for f in f1 for m in _hi_lo(f[:rows] * scale)]
    mat_spec = pl.BlockSpec((rows, n1), lambda b, j: (0, 0))
    in_spec = pl.BlockSpec((1, n1, DFT_COL_TILE), lambda b, j: (b, 0, j))
    io_spec = pl.BlockSpec((2, rows, DFT_COL_TILE), lambda b, j: (b, 0, j))
    return pl.pallas_call(
        _idft_rows_kernel,
        grid=(bsz // 2, cols // DFT_COL_TILE),
        in_specs=[in_spec, in_spec, io_spec, io_spec, pl.BlockSpec((1, DFT_COL_TILE), lambda b, j: (0, j))]
        + [mat_spec] * 4,
        out_specs=io_spec,
        out_shape=jax.ShapeDtypeStruct((bsz, rows, cols), F32),
        name="idft_rows",
    )(b_r, b_i, z2d, gate2d, bias_cols, *mats)


def _hyena_prep_kernel(first_ref, last_ref, u_ref, up_ref, un_ref, sw_ref, v_ref, x1_ref, x2_ref):
    conv = _shift_conv(pl.program_id(0), first_ref, last_ref, u_ref, up_ref, un_ref, sw_ref)
    ch = v_ref.shape[1]
    for n, o_ref in enumerate((v_ref, x1_ref, x2_ref)):
        o_ref[...] = conv(n * ch, ch, n * ch)


def _hyena_prep(u, col_block, bsz, s, short_w):
    width = short_w.shape[1]
    ch = width // 3
    tile = ROW_TILE
    halo = 8
    n_tiles = bsz * s // tile
    in_seq = np.arange(n_tiles) % (s // tile)
    first = jnp.asarray(in_seq == 0, jnp.int32)
    last = jnp.asarray(in_seq == s // tile - 1, jnp.int32)
    out = jax.ShapeDtypeStruct((bsz * s, ch), F32)
    out_spec = pl.BlockSpec((tile, ch), lambda i, f, l: (i, 0))
    grid_spec = pltpu.PrefetchScalarGridSpec(
        num_scalar_prefetch=2,
        grid=(n_tiles,),
        in_specs=[pl.BlockSpec((tile, width), lambda i, f, l: (i, col_block)),
                  pl.BlockSpec((halo, width), lambda i, f, l: (jnp.maximum(i * (tile // halo) - 1, 0), col_block)),
                  pl.BlockSpec((halo, width), lambda i, f, l: ((i + 1) * (tile // halo), col_block)),
                  pl.BlockSpec(short_w.shape, lambda i, f, l: (0, 0))],
        out_specs=[out_spec, out_spec, out_spec],
    )
    return pl.pallas_call(
        _hyena_prep_kernel,
        grid_spec=grid_spec,
        out_shape=[out, out, out],
        name="hyena_prep",
    )(first, last, u, u, u, short_w)


def _hyena(v, x1, x2, bsz, length, w1, b1, fr1, w2, b2, fr2, w3, bias):
    ch = v.shape[-1]
    rows = length // DFT_R
    h = _hyena_filters(length, ch, w1, b1, fr1, w2, b2, fr2, w3)
    n1 = 2 * rows
    f1, f2, tw = _dft_matrices(n1)
    four_d = lambda t: t.reshape(t.shape[0], n1, DFT_R, ch)
    hh = h.transpose(1, 2, 0, 3).reshape(HYENA_ORDER * 2, rows, DFT_R * ch)
    hf_r, hf_i = (t.reshape(HYENA_ORDER, 2, n1, DFT_R, ch)
                  for t in _dft_mid(*(four_d(t) for t in _dft_rows(hh, f1, 1)), f2, tw))
    kf_r = hf_r[:, 0] + hf_r[:, 1]
    kf_i = hf_i[:, 0] - hf_i[:, 1]
    assert bsz % 2 == 0, "batch entries are convolved in pairs (one complex sequence per pair)"
    z = v
    for o, gate in enumerate((x1, x2)):
        z2d = z.reshape(bsz, rows, DFT_R * ch)
        a_r, a_i = _dft_rows(z2d, f1, 2)
        b_r, b_i = _dft_mid(four_d(a_r), four_d(a_i), f2, tw, kf=(kf_r, kf_i, o))
        flat = lambda t: t.reshape(bsz // 2, n1, DFT_R * ch)
        y = _idft_rows(flat(b_r), flat(b_i), z2d, gate.reshape(z2d.shape), jnp.tile(bias[o], DFT_R)[None, :], f1)
        z = y
    return z.reshape(bsz * length, ch)


def _odd_layer(tok, sh1, sc1, g1, bsz, s, n_ctx, w_in, w_out, q_norm, k_norm, rpb, short_w, w1, b1, fr1, w2,
               b2, fr2, w3, bias):
    n_heads = rpb.shape[0]
    d_c = n_heads * HEAD_DIM
    n_lat = bsz * s
    u = _mod_linear(tok, sh1, sc1, w_in, s)
    q, k, v = _na_prep(u, bsz, s, n_heads, q_norm, k_norm)
    u_ctx = u[n_lat:, d_c:3 * d_c].reshape(bsz, n_ctx, 2, n_heads, HEAD_DIM)
    head_major = lambda t: t.transpose(0, 2, 1, 3).astype(BF16)
    kc = head_major(_rms(u_ctx[:, :, 0]) * k_norm)
    vc = head_major(u_ctx[:, :, 1])
    y_na = _neighbourhood_attention(q, k, v, kc, vc, rpb)
    assert (3 * d_c) % short_w.shape[1] == 0
    y_hy = _hyena(*_hyena_prep(u, 3 * d_c // short_w.shape[1], bsz, s, short_w), bsz, s, w1, b1, fr1, w2, b2,
                  fr2, w3, bias)
    return _res_linear2(y_na, y_hy, w_out, tok, g1, s)


def kernel(x, c, ctx, c_ctx, mod_w, mod_b, router_w, router_b, moe_w1, moe_b1, moe_w2, moe_b2, ab_w_in, ab_w_out, rwkv_shift, rwkv_w0, rwkv_w2, rwkv_a0, rwkv_a2, rwkv_g2, rwkv_k_k, rwkv_k_a, rwkv_r_k, rwkv_ln_w, rwkv_ln_b, hgrn_lb_logits, hgrn_norm_w, cd_w_in, cd_w_out, na_q_norm, na_k_norm, na_rpb, hy_short, hy_w1, hy_b1, hy_freq1, hy_w2, hy_b2, hy_freq2, hy_w3, hy_bias):
    bsz, s, d = x.shape
    n_ctx = ctx.shape[1]
    depth = mod_w.shape[0]
    n_lat = bsz * s
    lb_all = jnp.cumsum(jax.nn.softmax(hgrn_lb_logits, axis=0), axis=0)
    tok = jnp.concatenate([x.reshape(n_lat, d), ctx.reshape(bsz * n_ctx, d)], axis=0)
    cond = jnp.concatenate([jax.nn.silu(c), jax.nn.silu(c_ctx)[None, :],
                            jnp.zeros((8 - bsz - 1, d), F32)], axis=0)
    assert depth == 2, "layer 0 = RWKV-7 || HGRN2 with context outputs, layer 1 = attention || Hyena, latent only"
    for l in range(depth):
        j = l // 2
        mod = _small_linear(cond, mod_w[l], mod_b[l])[:bsz + 1]
        sh1, sc1, g1, sh2, sc2, g2 = jnp.split(mod, 6, axis=-1)
        if l % 2 == 0:
            tok = _even_layer(tok, sh1, sc1, g1, bsz, s, n_ctx, ab_w_in[j], ab_w_out[j], rwkv_shift[j],
                              rwkv_w0[j], rwkv_w2[j], rwkv_a0[j], rwkv_a2[j], rwkv_g2[j], rwkv_k_k[j],
                              rwkv_k_a[j], rwkv_r_k[j], rwkv_ln_w[j], rwkv_ln_b[j], lb_all[j], hgrn_norm_w[j])
        else:
            tok = _odd_layer(tok, sh1, sc1, g1, bsz, s, n_ctx, cd_w_in[j], cd_w_out[j], na_q_norm[j], na_k_norm[j],
                             na_rpb[j], hy_short[j], hy_w1[j], hy_b1[j], hy_freq1[j], hy_w2[j], hy_b2[j],
                             hy_freq2[j], hy_w3[j], hy_bias[j])
        yk, gate = _moe(tok, sh2, sc2, s, router_w[l], router_b[l], moe_w1, moe_w2, l, moe_b1[l], moe_b2[l])
        tok = _combine(tok, yk, gate, g2, s)
    return tok[:n_lat].reshape(bsz, s, d)


def _combine_kernel(x_ref, y_ref, w_ref, g_ref, o_ref):
    reps = x_ref.shape[1] // w_ref.shape[2]
    acc = y_ref[0] * jnp.concatenate([w_ref[0]] * reps, axis=-1)
    for k in range(1, y_ref.shape[0]):
        acc = acc + y_ref[k] * jnp.concatenate([w_ref[k]] * reps, axis=-1)
    o_ref[...] = x_ref[...] + g_ref[0] * acc


def _combine(x, yk, gate, res_gate, rows_per_mod):
    m, d = x.shape
    n_k = yk.shape[0]
    rows = COMBINE_TILE
    tiles_per_mod = rows_per_mod // rows
    tile = pl.BlockSpec((rows, d), lambda i: (i, 0))
    lanes = 128
    return pl.pallas_call(
        _combine_kernel,
        grid=(m // rows,),
        in_specs=[tile, pl.BlockSpec((n_k, rows, d), lambda i: (0, i, 0)),
                  pl.BlockSpec((n_k, rows, lanes), lambda i: (0, i, 0)),
                  pl.BlockSpec((1, 1, d), lambda i: (i // tiles_per_mod, 0, 0))],
        out_specs=tile,
        out_shape=jax.ShapeDtypeStruct((m, d), F32),
        compiler_params=pltpu.CompilerParams(dimension_semantics=("arbitrary",),
                                             vmem_limit_bytes=VMEM_LIMIT),
        name="moe_combine",
    )(x, yk, jnp.broadcast_to(gate[:, :, None], gate.shape + (lanes,)), res_gate[:, None, :])
```
